```python
import math
import jax, jax.numpy as jnp
from jax import lax
import numpy as np

D_MODEL = 1024
BATCH = 8
SEQ = 8192
DEPTH = 1
DEC_BATCH = 32
DEC_SEQ = 16
PAST_LEN = 2048

CHUNK = 64
N_META = 16
EPS = 1e-6
D_INNER = 2 * D_MODEL
SSD_HEAD_DIM = 64
SSD_HEADS = D_INNER // SSD_HEAD_DIM
SSD_GROUPS = 8
SSD_HPG = SSD_HEADS // SSD_GROUPS
SSD_STATE = 128
SSD_GN = SSD_GROUPS * SSD_STATE
CONV_W = 4
CONV_DIM = D_INNER + 2 * SSD_GN
SSD_BLOCK = 64
ATT_HEADS = 8
ATT_HEAD_DIM = 64
ATT_V_DIM = 2 * ATT_HEAD_DIM
ATT_QK_WIDTH = ATT_HEADS * 2 * ATT_HEAD_DIM
ATT_V_WIDTH = ATT_HEADS * ATT_V_DIM
ROPE_THETA = 10000.0
Q_BLOCK = 128
IN_SPLITS = (D_INNER, CONV_DIM, SSD_HEADS, ATT_QK_WIDTH, ATT_QK_WIDTH, ATT_V_WIDTH, 2 * D_MODEL)
D_IN_PROJ = D_INNER + CONV_DIM + SSD_HEADS + 2 * ATT_QK_WIDTH + ATT_V_WIDTH + 2 * D_MODEL
N_EXPERTS = 32
TOP_K = 4
D_FF = D_MODEL
SWIGLU_LIMIT = 7.0
SWIGLU_ALPHA = 1.702
MOE_BLOCK = 128

kernel_name = "hybrid_ssd_diffattn_moe_streaming_step"

F32 = jnp.float32


def rmsnorm(x, g):
    xf = x.astype(F32)
    y = xf * lax.rsqrt(jnp.mean(xf * xf, axis=-1, keepdims=True) + EPS)
    return (y * g.astype(F32)).astype(x.dtype)


def split_in_proj(proj):
    outs = []
    start = 0
    for w in IN_SPLITS:
        outs.append(proj[..., start:start + w])
        start += w
    return outs


def rope(x, pos):
    d = x.shape[-1]
    inv = ROPE_THETA ** (-jnp.arange(0, d, 2, dtype=F32) / d)
    ang = pos.astype(F32)[:, None] * inv[None, :]
    cos = jnp.cos(ang)[None, :, None, :]
    sin = jnp.sin(ang)[None, :, None, :]
    x1 = x[..., : d // 2].astype(F32)
    x2 = x[..., d // 2:].astype(F32)
    return jnp.concatenate([x1 * cos - x2 * sin, x2 * cos + x1 * sin], axis=-1).astype(x.dtype)


def causal_conv(xbc, prev, w, b):
    t = xbc.shape[1]
    full = jnp.concatenate([prev.astype(xbc.dtype), xbc], axis=1)
    y = b + sum(w[i] * full[:, i:i + t] for i in range(CONV_W))
    return jax.nn.silu(y), full[:, -(CONV_W - 1):]


def ssd_scan(xh, dt, a_log, bmat, cmat, h0):
    bsz, t = xh.shape[:2]
    nc = -(-t // SSD_BLOCK)
    pad = nc * SSD_BLOCK - t

    def prep(a):
        a = jnp.pad(a.astype(F32), [(0, 0), (0, pad)] + [(0, 0)] * (a.ndim - 2))
        a = a.reshape((bsz, nc, SSD_BLOCK) + a.shape[2:])
        return jnp.moveaxis(a, 1, 0)

    a_neg = -jnp.exp(a_log.astype(F32))
    da = (dt * a_neg).reshape(bsz, t, SSD_GROUPS, SSD_HPG)
    xdt = (xh.astype(F32) * dt[..., None]).reshape(bsz, t, SSD_GROUPS, SSD_HPG, SSD_HEAD_DIM)
    xs = (prep(xdt), prep(da), prep(bmat), prep(cmat))
    tril = jnp.tril(jnp.ones((SSD_BLOCK, SSD_BLOCK), dtype=bool))[None, :, :, None, None]

    def step(h, blk):
        xc, dac, bc, cc = blk
        cum = jnp.cumsum(dac, axis=1)
        seg = cum[:, :, None] - cum[:, None, :]
        decay = jnp.exp(jnp.where(tril, seg, -jnp.inf))
        cb = jnp.einsum('blgn,bsgn->blsg', cc, bc)
        y = jnp.einsum('blsg,blsgr,bsgrp->blgrp', cb, decay, xc)
        y = y + jnp.einsum('blgn,bgrpn,blgr->blgrp', cc, h, jnp.exp(cum))
        total = cum[:, -1]
        w_end = jnp.exp(total[:, None] - cum)
        h = h * jnp.exp(total)[..., None, None] + jnp.einsum('bsgn,bsgr,bsgrp->bgrpn', bc, w_end, xc)
        return h, y

    h_init = h0.astype(F32).reshape(bsz, SSD_GROUPS, SSD_HPG, SSD_HEAD_DIM, SSD_STATE)
    h_fin, ys = lax.scan(step, h_init, xs)
    y = jnp.moveaxis(ys, 0, 1).reshape(bsz, nc * SSD_BLOCK, SSD_HEADS, SSD_HEAD_DIM)[:, :t]
    return y, h_fin.reshape(bsz, SSD_HEADS, SSD_HEAD_DIM, SSD_STATE)


def diff_combine(p, v, lam):
    b, _, q, kk = p.shape
    p = p.reshape(b, ATT_HEADS, 2, q, kk)
    a = (p[:, :, 0] - lam * p[:, :, 1]).astype(v.dtype)
    return jnp.einsum('bhqk,bkhe->bqhe', a, v)


def diff_attn_blocked(q, k, v, chunk_id, lam):
    bsz, t = q.shape[:2]
    nb = -(-t // Q_BLOCK)
    pad = nb * Q_BLOCK - t
    scale = ATT_HEAD_DIM ** -0.5
    qb = jnp.pad(q, [(0, 0), (0, pad), (0, 0), (0, 0)]).reshape(bsz, nb, Q_BLOCK, 2 * ATT_HEADS, ATT_HEAD_DIM)
    qb = jnp.moveaxis(qb, 1, 0)
    q_chunk = jnp.pad(chunk_id, (0, pad), constant_values=jnp.iinfo(jnp.int32).max).reshape(nb, Q_BLOCK)

    def one_block(args):
        qblk, qc = args
        s = jnp.einsum('bqhd,bkhd->bhqk', qblk, k, preferred_element_type=F32) * scale
        mask = chunk_id[None, :] <= qc[:, None]
        p = jax.nn.softmax(jnp.where(mask[None, None], s, -jnp.inf), axis=-1)
        return diff_combine(p, v, lam)

    out = lax.map(one_block, (qb, q_chunk))
    return jnp.moveaxis(out, 0, 1).reshape(bsz, nb * Q_BLOCK, ATT_HEADS, ATT_V_DIM)[:, :t]


def diff_attn_cached(q, k_all, v_all, lam):
    s = jnp.einsum('bqhd,bkhd->bhqk', q, k_all, preferred_element_type=F32) * (ATT_HEAD_DIM ** -0.5)
    return diff_combine(jax.nn.softmax(s, axis=-1), v_all, lam)


def moe_ffn(u, w_router, b_router, w_gu, b_gu, w_down, b_down):
    t = u.shape[0]
    logits = jnp.einsum('td,de->te', u, w_router, preferred_element_type=F32) + b_router.astype(F32)
    top_val, top_idx = lax.top_k(logits, TOP_K)
    gates = jax.nn.softmax(top_val, axis=-1).astype(u.dtype)
    m = t * TOP_K
    e_flat = top_idx.reshape(m).astype(jnp.int32)
    g_flat = gates.reshape(m)
    tok_flat = jnp.arange(m, dtype=jnp.int32) // TOP_K
    order = jnp.argsort(e_flat)
    e_sorted = e_flat[order]
    counts = jnp.zeros((N_EXPERTS,), jnp.int32).at[e_flat].add(1)
    padded = (counts + MOE_BLOCK - 1) // MOE_BLOCK * MOE_BLOCK
    start = jnp.cumsum(counts) - counts
    pend = jnp.cumsum(padded)
    pstart = pend - padded
    dest = pstart[e_sorted] + (jnp.arange(m, dtype=jnp.int32) - start[e_sorted])
    n_blocks = -(-m // MOE_BLOCK) + N_EXPERTS
    rows = n_blocks * MOE_BLOCK
    row_tok = jnp.zeros((rows,), jnp.int32).at[dest].set(tok_flat[order])
    row_gate = jnp.zeros((rows,), u.dtype).at[dest].set(g_flat[order])
    block_start = jnp.arange(n_blocks, dtype=jnp.int32) * MOE_BLOCK
    block_exp = jnp.clip(jnp.searchsorted(pend, block_start, side='right'), 0, N_EXPERTS - 1)

    def expert_block(args):
        tok, e = args
        xb = u[tok]
        gu = xb @ w_gu[e] + b_gu[e]
        gate = jnp.minimum(gu[:, :D_FF], SWIGLU_LIMIT)
        up = jnp.clip(gu[:, D_FF:], -SWIGLU_LIMIT, SWIGLU_LIMIT)
        hdn = (up + 1.0) * gate * jax.nn.sigmoid(SWIGLU_ALPHA * gate)
        return hdn @ w_down[e] + b_down[e]

    out = lax.map(expert_block, (row_tok.reshape(n_blocks, MOE_BLOCK), block_exp))
    out = out.reshape(rows, D_MODEL) * row_gate[:, None]
    return jax.ops.segment_sum(out, row_tok, num_segments=t)


def trunk_layer(h, pos, chunk_id, conv_prev, ssm_prev, k_past, v_past, lam_init, lw):
    bsz, t, _ = h.shape
    u = rmsnorm(h, lw['norm1_g'])
    z, xbc, dt_raw, q, k, v, gate_raw = split_in_proj(u @ lw['w_in'])
    xbc, conv_new = causal_conv(xbc, conv_prev, lw['conv_w'], lw['conv_b'])
    xh = xbc[..., :D_INNER].reshape(bsz, t, SSD_HEADS, SSD_HEAD_DIM)
    bm = xbc[..., D_INNER:D_INNER + SSD_GN].reshape(bsz, t, SSD_GROUPS, SSD_STATE)
    cm = xbc[..., D_INNER + SSD_GN:].reshape(bsz, t, SSD_GROUPS, SSD_STATE)
    dt = jax.nn.softplus(dt_raw.astype(F32) + lw['dt_bias'].astype(F32))
    y, ssm_new = ssd_scan(xh, dt, lw['a_log'], bm, cm, ssm_prev)
    y = y + lw['d_skip'].astype(F32)[:, None] * xh.astype(F32)
    y = y.reshape(bsz, t, D_INNER) * jax.nn.silu(z.astype(F32))
    yg = y.reshape(bsz, t, SSD_GROUPS, D_INNER // SSD_GROUPS)
    yg = yg * lax.rsqrt(jnp.mean(yg * yg, axis=-1, keepdims=True) + EPS)
    y = (yg.reshape(bsz, t, D_INNER) * lw['ssd_norm_g'].astype(F32)).astype(h.dtype)
    y_ssd = y @ lw['w_ssd_out']
    q = rope(q.reshape(bsz, t, 2 * ATT_HEADS, ATT_HEAD_DIM), pos)
    k = rope(k.reshape(bsz, t, 2 * ATT_HEADS, ATT_HEAD_DIM), pos)
    v = v.reshape(bsz, t, ATT_HEADS, ATT_V_DIM)
    lam = (jnp.exp(jnp.sum(lw['lambda_q1'].astype(F32) * lw['lambda_k1'].astype(F32)))
           - jnp.exp(jnp.sum(lw['lambda_q2'].astype(F32) * lw['lambda_k2'].astype(F32))) + lam_init)
    if k_past is None:
        o = diff_attn_blocked(q, k, v, chunk_id, lam)
    else:
        o = diff_attn_cached(q, jnp.concatenate([k_past, k], axis=1), jnp.concatenate([v_past, v], axis=1), lam)
    o = rmsnorm(o, lw['subln_g']) * (1.0 - lam_init)
    y_att = o.reshape(bsz, t, ATT_V_WIDTH) @ lw['w_att_out']
    gates = jax.nn.sigmoid(gate_raw + lw['b_gate'])
    mix = (gates[..., :D_MODEL] * y_ssd + gates[..., D_MODEL:] * y_att) @ lw['w_o']
    h = h + mix
    u2 = rmsnorm(h, lw['norm2_g'])
    h = h + moe_ffn(u2.reshape(bsz * t, D_MODEL), lw['w_router'], lw['b_router'], lw['w_gu'],
                    lw['b_gu'], lw['w_down'], lw['b_down']).reshape(bsz, t, D_MODEL)
    return h, k, v, ssm_new.astype(ssm_prev.dtype), conv_new


def setup_inputs(seed: int = 0) -> dict:
    key = jax.random.key(seed)
    ks = iter(jax.random.split(key, 40))

    def nrm(shape, scale):
        return jax.random.normal(next(ks), shape, F32) * scale

    def gain(shape):
        return 1.0 + nrm(shape, 0.02)

    dt0 = jnp.exp(jax.random.uniform(next(ks), (DEPTH, SSD_HEADS), F32, math.log(1e-3), math.log(1e-1)))
    dt_bias = dt0 + jnp.log(-jnp.expm1(-dt0))
    a_log = jnp.log(jax.random.uniform(next(ks), (DEPTH, SSD_HEADS), F32, 1.0, 16.0))
    return {
        'x_prompt': nrm((BATCH, SEQ, D_MODEL), 1.0),
        'x_sample': nrm((DEC_BATCH, DEC_SEQ, D_MODEL), 1.0),
        'cache_k': nrm((DEPTH, DEC_BATCH, PAST_LEN, 2 * ATT_HEADS, ATT_HEAD_DIM), 1.0),
        'cache_v': nrm((DEPTH, DEC_BATCH, PAST_LEN, ATT_HEADS, ATT_V_DIM), 1.0),
        'state_ssm': nrm((DEPTH, DEC_BATCH, SSD_HEADS, SSD_HEAD_DIM, SSD_STATE), 0.1),
        'state_conv': nrm((DEPTH, DEC_BATCH, CONV_W - 1, CONV_DIM), 1.0),
        'meta_tokens': nrm((N_META, D_MODEL), 1.0),
        'norm1_g': gain((DEPTH, D_MODEL)),
        'w_in': nrm((DEPTH, D_MODEL, D_IN_PROJ), D_MODEL ** -0.5),
        'conv_w': nrm((DEPTH, CONV_W, CONV_DIM), CONV_W ** -0.5),
        'conv_b': nrm((DEPTH, CONV_DIM), 0.01),
        'dt_bias': dt_bias,
        'a_log': a_log,
        'd_skip': gain((DEPTH, SSD_HEADS)),
        'ssd_norm_g': gain((DEPTH, D_INNER)),
        'lambda_q1': nrm((DEPTH, ATT_HEAD_DIM), 0.1),
        'lambda_k1': nrm((DEPTH, ATT_HEAD_DIM), 0.1),
        'lambda_q2': nrm((DEPTH, ATT_HEAD_DIM), 0.1),
        'lambda_k2': nrm((DEPTH, ATT_HEAD_DIM), 0.1),
        'subln_g': gain((DEPTH, ATT_V_DIM)),
        'w_ssd_out': nrm((DEPTH, D_INNER, D_MODEL), D_INNER ** -0.5),
        'w_att_out': nrm((DEPTH, ATT_V_WIDTH, D_MODEL), ATT_V_WIDTH ** -0.5),
        'b_gate': nrm((DEPTH, 2 * D_MODEL), 0.01),
        'w_o': nrm((DEPTH, D_MODEL, D_MODEL), D_MODEL ** -0.5),
        'norm2_g': gain((DEPTH, D_MODEL)),
        'w_router': nrm((DEPTH, D_MODEL, N_EXPERTS), D_MODEL ** -0.5),
        'b_router': nrm((DEPTH, N_EXPERTS), 0.01),
        'w_gu': nrm((DEPTH, N_EXPERTS, D_MODEL, 2 * D_FF), D_MODEL ** -0.5),
        'b_gu': nrm((DEPTH, N_EXPERTS, 2 * D_FF), 0.01),
        'w_down': nrm((DEPTH, N_EXPERTS, D_FF, D_MODEL), D_FF ** -0.5),
        'b_down': nrm((DEPTH, N_EXPERTS, D_MODEL), 0.01),
        'final_norm_g': gain((D_MODEL,)),
    }


def reference(x_prompt, x_sample, cache_k, cache_v, state_ssm, state_conv, meta_tokens, norm1_g, w_in,
              conv_w, conv_b, dt_bias, a_log, d_skip, ssd_norm_g, lambda_q1, lambda_k1, lambda_q2,
              lambda_k2, subln_g, w_ssd_out, w_att_out, b_gate, w_o, norm2_g, w_router, b_router, w_gu,
              b_gu, w_down, b_down, final_norm_g):
    bsz_p, seq_p, _ = x_prompt.shape
    bsz_s, seq_s, _ = x_sample.shape
    t_p = N_META + seq_p
    pos_p = jnp.arange(t_p, dtype=jnp.int32)
    chunk_p = jnp.where(pos_p < N_META, 0, 1 + (pos_p - N_META) // CHUNK).astype(jnp.int32)
    pos_s = PAST_LEN + jnp.arange(seq_s, dtype=jnp.int32)
    meta = jnp.broadcast_to(meta_tokens.astype(x_prompt.dtype)[None], (bsz_p, N_META, D_MODEL))
    h_p = jnp.concatenate([meta, x_prompt], axis=1)
    h_s = x_sample
    conv0 = jnp.zeros((bsz_p, CONV_W - 1, CONV_DIM), x_prompt.dtype)
    ssm0 = jnp.zeros((bsz_p, SSD_HEADS, SSD_HEAD_DIM, SSD_STATE), state_ssm.dtype)
    kp_l, vp_l, sp_l, cp_l, ks_l, vs_l, ss_l, cs_l = [], [], [], [], [], [], [], []
    for l in range(DEPTH):
        lam_init = 0.8 - 0.6 * math.exp(-0.3 * l)
        lw = {'norm1_g': norm1_g[l], 'w_in': w_in[l], 'conv_w': conv_w[l], 'conv_b': conv_b[l],
              'dt_bias': dt_bias[l], 'a_log': a_log[l], 'd_skip': d_skip[l], 'ssd_norm_g': ssd_norm_g[l],
              'lambda_q1': lambda_q1[l], 'lambda_k1': lambda_k1[l], 'lambda_q2': lambda_q2[l],
              'lambda_k2': lambda_k2[l], 'subln_g': subln_g[l], 'w_ssd_out': w_ssd_out[l],
              'w_att_out': w_att_out[l], 'b_gate': b_gate[l], 'w_o': w_o[l], 'norm2_g': norm2_g[l],
              'w_router': w_router[l], 'b_router': b_router[l], 'w_gu': w_gu[l], 'b_gu': b_gu[l],
              'w_down': w_down[l], 'b_down': b_down[l]}
        h_p, kp, vp, sp, cp = trunk_layer(h_p, pos_p, chunk_p, conv0, ssm0, None, None, lam_init, lw)
        h_s, ks_, vs_, ss_, cs_ = trunk_layer(h_s, pos_s, None, state_conv[l], state_ssm[l],
                                              cache_k[l], cache_v[l], lam_init, lw)
        kp_l.append(kp); vp_l.append(vp); sp_l.append(sp); cp_l.append(cp)
        ks_l.append(ks_); vs_l.append(vs_); ss_l.append(ss_); cs_l.append(cs_)
    y_prompt = rmsnorm(h_p, final_norm_g)[:, N_META:]
    y_sample = rmsnorm(h_s, final_norm_g)
    return (y_prompt, y_sample,
            jnp.stack(kp_l), jnp.stack(vp_l), jnp.stack(sp_l), jnp.stack(cp_l),
            jnp.stack(ks_l), jnp.stack(vs_l), jnp.stack(ss_l), jnp.stack(cs_l))
```

```python
import functools
import math

import jax
import jax.numpy as jnp
from jax import lax
from jax.experimental import pallas as pl
from jax.experimental.pallas import tpu as pltpu

F32 = jnp.float32
BF16 = jnp.bfloat16
I32 = jnp.int32

D_MODEL = 1024
D_INNER = 2048
SSD_HEADS = 32
SSD_HEAD_DIM = 64
SSD_GROUPS = 8
SSD_HPG = SSD_HEADS // SSD_GROUPS
SSD_STATE = 128
SSD_GN = SSD_GROUPS * SSD_STATE
CONV_W = 4
CONV_DIM = D_INNER + 2 * SSD_GN
ATT_HEADS = 8
ATT_HEAD_DIM = 64
ATT_V_DIM = 128
ATT_WIDTH = 1024
CHUNK = 64
N_META = 16
EPS = 1e-6
ROPE_THETA = 10000.0
N_EXPERTS = 32
TOP_K = 4
D_FF = 1024
SWIGLU_LIMIT = 7.0
SWIGLU_ALPHA = 1.702

COL_Z = 0
COL_X = COL_Z + D_INNER
COL_B = COL_X + D_INNER
COL_C = COL_B + SSD_GN
COL_Q = COL_C + SSD_GN
COL_K = COL_Q + ATT_WIDTH
COL_V = COL_K + ATT_WIDTH
COL_G = COL_V + ATT_WIDTH
N_MAIN = COL_G + 2 * D_MODEL

LANES = 128
SUBLANES = 8
GROUP_W = D_INNER // SSD_GROUPS
CONV_GW = GROUP_W + 2 * SSD_STATE
PROJ_TN = 1024
TOKEN_TM = 512
SSD_L = 256
ATT_TQ = 256
MOE_BLK = 256
VMEM_LIMIT = 56 * 1024 * 1024
NEG_BIG = -1e30


def _cparams(sem):
    return pltpu.CompilerParams(dimension_semantics=sem, vmem_limit_bytes=VMEM_LIMIT)


def _sigmoid(x):
    return 1.0 / (1.0 + jnp.exp(-x))


def _inproj_kernel(x_ref, g_ref, w_ref, wdt_ref, cos_ref, sin_ref, o_ref, dt_ref, u_scr):
    j = pl.program_id(1)

    @pl.when(j == 0)
    def _():
        x = x_ref[...]
        ms = jnp.mean(x * x, axis=-1, keepdims=True)
        u = (x * lax.rsqrt(ms + EPS) * g_ref[...]).astype(BF16)
        u_scr[...] = u
        dt_ref[...] = jnp.dot(u, wdt_ref[...], preferred_element_type=F32)

    acc = jnp.dot(u_scr[...], w_ref[...], preferred_element_type=F32)
    is_q = j == COL_Q // PROJ_TN
    is_k = j == COL_K // PROJ_TN

    @pl.when(is_q | is_k)
    def _():
        reps = PROJ_TN // LANES
        cosf = jnp.tile(cos_ref[...], (1, reps))
        sinf = jnp.tile(sin_ref[...], (1, reps))
        lane = lax.broadcasted_iota(I32, acc.shape, 1)
        first_half = (lane % ATT_HEAD_DIM) < (ATT_HEAD_DIM // 2)
        half = ATT_HEAD_DIM // 2
        swapped = jnp.where(first_half, pltpu.roll(acc, PROJ_TN - half, 1), pltpu.roll(acc, half, 1))
        scale = jnp.where(is_q, ATT_HEAD_DIM ** -0.5, 1.0).astype(F32)
        o_ref[...] = ((acc * cosf + swapped * sinf) * scale).astype(o_ref.dtype)

    @pl.when(jnp.logical_not(is_q | is_k))
    def _():
        o_ref[...] = acc.astype(o_ref.dtype)


def _in_proj(x, g1, w_main, w_dt, cos_t, sin_t, tm, rope_blocks):
    rows = x.shape[0]
    grid = (rows // tm, N_MAIN // PROJ_TN)
    return pl.pallas_call(
        _inproj_kernel,
        grid=grid,
        in_specs=[
            pl.BlockSpec((tm, D_MODEL), lambda i, j: (i, 0)),
            pl.BlockSpec((1, D_MODEL), lambda i, j: (0, 0)),
            pl.BlockSpec((D_MODEL, PROJ_TN), lambda i, j: (0, j)),
            pl.BlockSpec((D_MODEL, SSD_GROUPS * LANES), lambda i, j: (0, 0)),
            pl.BlockSpec((tm, LANES), lambda i, j: (i % rope_blocks, 0)),
            pl.BlockSpec((tm, LANES), lambda i, j: (i % rope_blocks, 0)),
        ],
        out_specs=[
            pl.BlockSpec((tm, PROJ_TN), lambda i, j: (i, j)),
            pl.BlockSpec((tm, SSD_GROUPS * LANES), lambda i, j: (i, 0)),
        ],
        out_shape=[
            jax.ShapeDtypeStruct((rows, N_MAIN), BF16),
            jax.ShapeDtypeStruct((rows, SSD_GROUPS * LANES), F32),
        ],
        scratch_shapes=[pltpu.VMEM((tm, D_MODEL), BF16)],
        compiler_params=_cparams(("parallel", "arbitrary")),
        name="in_proj",
    )(x, g1, w_main, w_dt, cos_t, sin_t)


def _ssd_kernel(x_ref, b_ref, c_ref, z_ref, dt_ref, h0_ref, cp_ref, cw_ref, cb_ref, dtb_ref,
                aneg_ref, dsk_ref, ng_ref, y_ref, hf_ref, ct_ref, h_scr, f_scr):
    c = pl.program_id(2)
    L = x_ref.shape[0]

    @pl.when(c == 0)
    def _():
        h_scr[...] = h0_ref[...]
        f_scr[0:SUBLANES, :] = cp_ref[...]

    f_scr[SUBLANES:SUBLANES + L, 0:GROUP_W] = x_ref[...].astype(F32)
    f_scr[SUBLANES:SUBLANES + L, GROUP_W:GROUP_W + SSD_STATE] = b_ref[...].astype(F32)
    f_scr[SUBLANES:SUBLANES + L, GROUP_W + SSD_STATE:CONV_GW] = c_ref[...].astype(F32)
    w = cw_ref[...]
    acc = cb_ref[...]
    for i in range(CONV_W):
        lo = SUBLANES - (CONV_W - 1) + i
        acc = acc + w[i:i + 1, :] * f_scr[lo:lo + L, :]
    xc = acc * _sigmoid(acc)
    tail = f_scr[L:L + SUBLANES, :]
    f_scr[0:SUBLANES, :] = tail
    ct_ref[...] = tail

    xg = xc[:, 0:GROUP_W]
    bm = xc[:, GROUP_W:GROUP_W + SSD_STATE].astype(BF16)
    cm = xc[:, GROUP_W + SSD_STATE:CONV_GW].astype(BF16)

    dtr = dt_ref[...] + dtb_ref[...]
    dt = jnp.maximum(dtr, 0.0) + jnp.log(1.0 + jnp.exp(-jnp.abs(dtr)))
    da = dt * aneg_ref[...]
    ti = lax.broadcasted_iota(I32, (L, L), 0)
    si = lax.broadcasted_iota(I32, (L, L), 1)
    causal = si <= ti
    tril = causal.astype(F32)
    cum = jnp.dot(tril, da, preferred_element_type=F32, precision=lax.Precision.HIGHEST)
    sel = (lax.broadcasted_iota(I32, (SUBLANES, LANES), 0)
           == lax.broadcasted_iota(I32, (SUBLANES, LANES), 1)).astype(F32)
    cum_t = lax.dot_general(sel, cum, (((1,), (1,)), ((), ())), preferred_element_type=F32,
                            precision=lax.Precision.HIGHEST)

    cb = lax.dot_general(cm, bm, (((1,), (1,)), ((), ())), preferred_element_type=F32)
    dsk = dsk_ref[...]
    ys = []
    for r in range(SSD_HPG):
        col = cum[:, r:r + 1]
        row = cum_t[r:r + 1, :]
        dec = jnp.exp(jnp.where(causal, col - row, NEG_BIG))
        m = (cb * dec).astype(BF16)
        xh = xg[:, r * SSD_HEAD_DIM:(r + 1) * SSD_HEAD_DIM]
        xdt = xh * dt[:, r:r + 1]
        h_prev = h_scr[r]
        y = jnp.dot(m, xdt.astype(BF16), preferred_element_type=F32)
        y = y + jnp.exp(col) * lax.dot_general(cm, h_prev.astype(BF16), (((1,), (1,)), ((), ())),
                                               preferred_element_type=F32)
        y = y + dsk[:, r:r + 1] * xh
        ys.append(y)
        tot = cum[L - 1:L, r:r + 1]
        xw = (xdt * jnp.exp(tot - col)).astype(BF16)
        upd = lax.dot_general(xw, bm, (((0,), (0,)), ((), ())), preferred_element_type=F32)
        h_scr[r] = h_prev * jnp.exp(tot) + upd
    yg = jnp.concatenate(ys, axis=1)
    z = z_ref[...].astype(F32)
    yz = yg * (z * _sigmoid(z))
    ms = jnp.mean(yz * yz, axis=-1, keepdims=True)
    y_ref[...] = (yz * lax.rsqrt(ms + EPS) * ng_ref[...]).astype(y_ref.dtype)
    hf_ref[...] = h_scr[...]


def _ssd(proj, dt_raw, h0, conv_prev, cw_g, cb_g, dtb_g, aneg_g, dsk_g, norm_g, *, n_seq, t, l, row0,
         shared_state):
    nc = t // l
    rb0 = row0 // l

    def rows(s, g, c):
        return rb0 + s * nc + c

    def sidx(s):
        return 0 if shared_state else s

    return pl.pallas_call(
        _ssd_kernel,
        grid=(n_seq, SSD_GROUPS, nc),
        in_specs=[
            pl.BlockSpec((l, GROUP_W), lambda s, g, c: (rows(s, g, c), COL_X // GROUP_W + g)),
            pl.BlockSpec((l, SSD_STATE), lambda s, g, c: (rows(s, g, c), COL_B // SSD_STATE + g)),
            pl.BlockSpec((l, SSD_STATE), lambda s, g, c: (rows(s, g, c), COL_C // SSD_STATE + g)),
            pl.BlockSpec((l, GROUP_W), lambda s, g, c: (rows(s, g, c), COL_Z // GROUP_W + g)),
            pl.BlockSpec((l, LANES), lambda s, g, c: (rows(s, g, c), g)),
            pl.BlockSpec((None, SSD_HPG, SSD_HEAD_DIM, SSD_STATE), lambda s, g, c: (sidx(s), g, 0, 0)),
            pl.BlockSpec((None, None, SUBLANES, CONV_GW), lambda s, g, c: (sidx(s), g, 0, 0)),
            pl.BlockSpec((None, CONV_W, CONV_GW), lambda s, g, c: (g, 0, 0)),
            pl.BlockSpec((None, 1, CONV_GW), lambda s, g, c: (g, 0, 0)),
            pl.BlockSpec((None, 1, LANES), lambda s, g, c: (g, 0, 0)),
            pl.BlockSpec((None, 1, LANES), lambda s, g, c: (g, 0, 0)),
            pl.BlockSpec((None, 1, LANES), lambda s, g, c: (g, 0, 0)),
            pl.BlockSpec((1, GROUP_W), lambda s, g, c: (0, g)),
        ],
        out_specs=[
            pl.BlockSpec((l, GROUP_W), lambda s, g, c: (s * nc + c, g)),
            pl.BlockSpec((None, SSD_HPG, SSD_HEAD_DIM, SSD_STATE), lambda s, g, c: (s, g, 0, 0)),
            pl.BlockSpec((None, None, SUBLANES, CONV_GW), lambda s, g, c: (s, g, 0, 0)),
        ],
        out_shape=[
            jax.ShapeDtypeStruct((n_seq * t, D_INNER), BF16),
            jax.ShapeDtypeStruct((n_seq, SSD_HEADS, SSD_HEAD_DIM, SSD_STATE), F32),
            jax.ShapeDtypeStruct((n_seq, SSD_GROUPS, SUBLANES, CONV_GW), F32),
        ],
        scratch_shapes=[
            pltpu.VMEM((SSD_HPG, SSD_HEAD_DIM, SSD_STATE), F32),
            pltpu.VMEM((l + SUBLANES, CONV_GW), F32),
        ],
        compiler_params=_cparams(("parallel", "parallel", "arbitrary")),
        name="ssd",
    )(proj, proj, proj, proj, dt_raw, h0, conv_prev, cw_g, cb_g, dtb_g, aneg_g, dsk_g, norm_g)


def _softmax_step(s, m_prev, l_prev, acc_prev, v):
    m_new = jnp.maximum(m_prev, jnp.max(s, axis=-1, keepdims=True))
    alpha = jnp.exp(m_prev - m_new)
    p = jnp.exp(s - m_new)
    l_new = alpha * l_prev + jnp.sum(p, axis=-1, keepdims=True)
    acc_new = alpha * acc_prev + jnp.dot(p.astype(BF16), v, preferred_element_type=F32)
    return m_new, l_new, acc_new


def _qk(q, k):
    return lax.dot_general(q, k, (((1,), (1,)), ((), ())), preferred_element_type=F32)


def _attn_prompt_kernel(lam_ref, q_ref, k_ref, v_ref, mk_ref, mv_ref, o_ref):
    i = pl.program_id(2)
    tq = q_ref.shape[0]
    lam = lam_ref[0]
    mv = mv_ref[...]
    qpos = lax.broadcasted_iota(I32, (tq, tq), 0) // CHUNK
    kpos = lax.broadcasted_iota(I32, (tq, tq), 1) // CHUNK
    diag_mask = kpos <= qpos
    outs = []
    for r in range(2):
        sl = slice(r * ATT_HEAD_DIM, (r + 1) * ATT_HEAD_DIM)
        q = q_ref[:, sl]
        s0 = _qk(q, mk_ref[:, sl])
        m0 = jnp.max(s0, axis=-1, keepdims=True)
        p0 = jnp.exp(s0 - m0)
        l0 = jnp.sum(p0, axis=-1, keepdims=True)
        a0 = jnp.dot(p0.astype(BF16), mv, preferred_element_type=F32)

        def body(j, carry, q=q, sl=sl):
            m_p, l_p, a_p = carry
            off = pl.multiple_of(j * tq, tq)
            s = _qk(q, k_ref[pl.ds(off, tq), sl])
            return _softmax_step(s, m_p, l_p, a_p, v_ref[pl.ds(off, tq), :])

        m1, l1, a1 = lax.fori_loop(0, i, body, (m0, l0, a0))
        off = pl.multiple_of(i * tq, tq)
        s = jnp.where(diag_mask, _qk(q, k_ref[pl.ds(off, tq), sl]), NEG_BIG)
        _, l2, a2 = _softmax_step(s, m1, l1, a1, v_ref[pl.ds(off, tq), :])
        outs.append(a2 / l2)
    o_ref[...] = (outs[0] - lam * outs[1]).astype(o_ref.dtype)


def _attn_prompt(lam, proj_p, proj_s, *, batch, seq, meta_row0):
    nq = seq // ATT_TQ
    return pl.pallas_call(
        _attn_prompt_kernel,
        grid=(batch, ATT_HEADS, nq),
        in_specs=[
            pl.BlockSpec(memory_space=pltpu.SMEM),
            pl.BlockSpec((ATT_TQ, LANES), lambda b, h, i: (b * nq + i, COL_Q // LANES + h)),
            pl.BlockSpec((seq, LANES), lambda b, h, i: (b, COL_K // LANES + h)),
            pl.BlockSpec((seq, LANES), lambda b, h, i: (b, COL_V // LANES + h)),
            pl.BlockSpec((N_META, LANES), lambda b, h, i: (meta_row0 // N_META, COL_K // LANES + h)),
            pl.BlockSpec((N_META, LANES), lambda b, h, i: (meta_row0 // N_META, COL_V // LANES + h)),
        ],
        out_specs=pl.BlockSpec((ATT_TQ, LANES), lambda b, h, i: (b * nq + i, h)),
        out_shape=jax.ShapeDtypeStruct((batch * seq, ATT_WIDTH), BF16),
        compiler_params=_cparams(("parallel", "parallel", "arbitrary")),
        name="attn_prompt",
    )(lam, proj_p, proj_p, proj_p, proj_s, proj_s)


def _attn_short_kernel(lam_ref, q_ref, kn_ref, vn_ref, *rest, has_cache):
    if has_cache:
        kc_ref, vc_ref, o_ref = rest
    else:
        (o_ref,) = rest
    t = q_ref.shape[0]
    nh = 2 * ATT_HEADS
    lam = lam_ref[0]
    q = q_ref[...].astype(F32)
    qb = jnp.broadcast_to(q[None], (nh, t, ATT_WIDTH)).reshape(nh * t, ATT_WIDTH)
    row_head = lax.broadcasted_iota(I32, (nh * t, ATT_WIDTH), 0) // t
    col_head = lax.broadcasted_iota(I32, (nh * t, ATT_WIDTH), 1) // ATT_HEAD_DIM
    qbd = jnp.where(row_head == col_head, qb, 0.0).astype(BF16)
    s_new = _qk(qbd, kn_ref[...])
    m = jnp.max(s_new, axis=-1, keepdims=True)
    if has_cache:
        s_old = _qk(qbd, kc_ref[...].astype(BF16))
        m = jnp.maximum(m, jnp.max(s_old, axis=-1, keepdims=True))
    p_new = jnp.exp(s_new - m)
    den = jnp.sum(p_new, axis=-1, keepdims=True)
    acc = jnp.dot(p_new.astype(BF16), vn_ref[...], preferred_element_type=F32)
    if has_cache:
        p_old = jnp.exp(s_old - m)
        den = den + jnp.sum(p_old, axis=-1, keepdims=True)
        acc = acc + jnp.dot(p_old.astype(BF16), vc_ref[...].astype(BF16), preferred_element_type=F32)
    acc = acc / den
    for h in range(ATT_HEADS):
        cols = slice(h * ATT_V_DIM, (h + 1) * ATT_V_DIM)
        a1 = acc[(2 * h) * t:(2 * h + 1) * t, cols]
        a2 = acc[(2 * h + 1) * t:(2 * h + 2) * t, cols]
        o_ref[:, cols] = (a1 - lam * a2).astype(o_ref.dtype)


def _attn_short(lam, proj_s, cache_k, cache_v, *, n_seq, t, row0):
    has_cache = cache_k is not None
    rb0 = row0 // t
    in_specs = [
        pl.BlockSpec(memory_space=pltpu.SMEM),
        pl.BlockSpec((t, ATT_WIDTH), lambda s: (rb0 + s, COL_Q // ATT_WIDTH)),
        pl.BlockSpec((t, ATT_WIDTH), lambda s: (rb0 + s, COL_K // ATT_WIDTH)),
        pl.BlockSpec((t, ATT_WIDTH), lambda s: (rb0 + s, COL_V // ATT_WIDTH)),
    ]
    args = [lam, proj_s, proj_s, proj_s]
    if has_cache:
        past = cache_k.shape[1]
        in_specs += [pl.BlockSpec((None, past, ATT_WIDTH), lambda s: (s, 0, 0)),
                     pl.BlockSpec((None, past, ATT_WIDTH), lambda s: (s, 0, 0))]
        args += [cache_k, cache_v]
    return pl.pallas_call(
        functools.partial(_attn_short_kernel, has_cache=has_cache),
        grid=(n_seq,),
        in_specs=in_specs,
        out_specs=pl.BlockSpec((t, ATT_WIDTH), lambda s: (s, 0)),
        out_shape=jax.ShapeDtypeStruct((n_seq * t, ATT_WIDTH), BF16),
        compiler_params=_cparams(("parallel",)),
        name="attn_cached" if has_cache else "attn_meta",
    )(*args)


def _merge_kernel(x_ref, yn_ref, o_ref, g1_ref, g2_ref, wso_ref, wao_ref, wo_ref, bg1_ref, bg2_ref,
                  sub_ref, n2_ref, wr_ref, br_ref, base_ref,
                  h1_ref, u2_ref, eidx_ref, gate_ref, rank_ref, cnt_ref, cnt_scr, *, sub_scale):
    i = pl.program_id(0)
    tm = x_ref.shape[0]

    @pl.when(i == 0)
    def _():
        cnt_scr[...] = base_ref[...]

    y_ssd = jnp.dot(yn_ref[...], wso_ref[...], preferred_element_type=F32)
    o = o_ref[...].astype(F32)
    parts = []
    for h in range(ATT_HEADS):
        oh = o[:, h * ATT_V_DIM:(h + 1) * ATT_V_DIM]
        ms = jnp.mean(oh * oh, axis=-1, keepdims=True)
        parts.append(oh * lax.rsqrt(ms + EPS) * sub_ref[...] * sub_scale)
    on = jnp.concatenate(parts, axis=1).astype(BF16)
    y_att = jnp.dot(on, wao_ref[...], preferred_element_type=F32)
    gs = _sigmoid(g1_ref[...].astype(F32) + bg1_ref[...])
    ga = _sigmoid(g2_ref[...].astype(F32) + bg2_ref[...])
    mix_in = (gs * y_ssd + ga * y_att).astype(BF16)
    h1 = x_ref[...] + jnp.dot(mix_in, wo_ref[...], preferred_element_type=F32)
    h1_ref[...] = h1
    ms = jnp.mean(h1 * h1, axis=-1, keepdims=True)
    u2 = h1 * lax.rsqrt(ms + EPS) * n2_ref[...]
    _store_rows8(u2_ref, u2)

    logits = jnp.dot(u2, wr_ref[...], preferred_element_type=F32, precision=lax.Precision.HIGHEST)
    logits = logits + br_ref[...]
    lane = lax.broadcasted_iota(I32, (tm, LANES), 1).astype(F32)
    work = jnp.where(lane < N_EXPERTS, logits, NEG_BIG)
    vals, idxs, hots = [], [], []
    for _ in range(TOP_K):
        mx = jnp.max(work, axis=-1, keepdims=True)
        ix = jnp.min(jnp.where(work == mx, lane, float(LANES)), axis=-1, keepdims=True)
        hot = lane == ix
        vals.append(mx)
        idxs.append(ix)
        hots.append(hot)
        work = jnp.where(hot, NEG_BIG, work)
    es = [jnp.exp(v - vals[0]) for v in vals]
    den = es[0] + es[1] + es[2] + es[3]
    hot_all = (hots[0] | hots[1] | hots[2] | hots[3])
    ti = lax.broadcasted_iota(I32, (tm, tm), 0)
    si = lax.broadcasted_iota(I32, (tm, tm), 1)
    strict = (si < ti).astype(BF16)
    prefix = jnp.dot(strict, hot_all.astype(BF16), preferred_element_type=F32) + cnt_scr[...]
    eidx = jnp.zeros((tm, LANES), F32)
    gate = jnp.zeros((tm, LANES), F32)
    rank = jnp.zeros((tm, LANES), F32)
    for k in range(TOP_K):
        rk = jnp.sum(jnp.where(hots[k], prefix, 0.0), axis=-1, keepdims=True)
        eidx = jnp.where(lane == k, idxs[k], eidx)
        gate = jnp.where(lane == k, es[k] / den, gate)
        rank = jnp.where(lane == k, rk, rank)
    eidx_ref[...] = eidx.astype(I32)
    gate_ref[...] = gate
    rank_ref[...] = rank.astype(I32)
    cnt_scr[...] = cnt_scr[...] + jnp.sum(hot_all.astype(F32), axis=0, keepdims=True)
    cnt_ref[...] = cnt_scr[...]


def _merge(x, yn, o, proj, wso, wao, wo, bg1, bg2, sub_g, n2_g, wr, br, base_cnt, *, tm, sub_scale):
    rows = x.shape[0]
    full = lambda shape: pl.BlockSpec(shape, lambda i: (0,) * len(shape))
    tok = lambda w: pl.BlockSpec((tm, w), lambda i: (i, 0))
    rows8 = pl.BlockSpec((tm * SUBLANES, LANES), lambda i: (i, 0))
    return pl.pallas_call(
        functools.partial(_merge_kernel, sub_scale=sub_scale),
        grid=(rows // tm,),
        in_specs=[
            tok(D_MODEL), tok(D_INNER), tok(ATT_WIDTH),
            pl.BlockSpec((tm, D_MODEL), lambda i: (i, COL_G // D_MODEL)),
            pl.BlockSpec((tm, D_MODEL), lambda i: (i, COL_G // D_MODEL + 1)),
            full((D_INNER, D_MODEL)), full((ATT_WIDTH, D_MODEL)), full((D_MODEL, D_MODEL)),
            full((1, D_MODEL)), full((1, D_MODEL)), full((1, ATT_V_DIM)), full((1, D_MODEL)),
            full((D_MODEL, LANES)), full((1, LANES)), full((1, LANES)),
        ],
        out_specs=[tok(D_MODEL), rows8, tok(LANES), tok(LANES), tok(LANES), full((1, LANES))],
        out_shape=[
            jax.ShapeDtypeStruct((rows, D_MODEL), F32),
            jax.ShapeDtypeStruct((rows * SUBLANES, LANES), F32),
            jax.ShapeDtypeStruct((rows, LANES), I32),
            jax.ShapeDtypeStruct((rows, LANES), F32),
            jax.ShapeDtypeStruct((rows, LANES), I32),
            jax.ShapeDtypeStruct((1, LANES), F32),
        ],
        scratch_shapes=[pltpu.VMEM((1, LANES), F32)],
        compiler_params=_cparams(("arbitrary",)),
        name="merge_router",
    )(x, yn, o, proj, proj, wso, wao, wo, bg1, bg2, sub_g, n2_g, wr, br, base_cnt)


def _store_rows8(ref, val):
    n = val.shape[0]
    for c in range(D_MODEL // LANES):
        ref[pl.ds(c, n, stride=SUBLANES), :] = val[:, c * LANES:(c + 1) * LANES]


def _load_rows8(ref, n):
    return jnp.concatenate([ref[pl.ds(c, n, stride=SUBLANES), :] for c in range(D_MODEL // LANES)], axis=1)


def _row_copy(src_ref, src_row, dst_ref, dst_row, sem):
    src = src_ref.at[pl.ds(pl.multiple_of(src_row * SUBLANES, SUBLANES), SUBLANES), :]
    dst = dst_ref.at[pl.ds(pl.multiple_of(dst_row * SUBLANES, SUBLANES), SUBLANES), :]
    return pltpu.make_async_copy(src, dst, sem)


def _dispatch_kernel(dest_ref, u_ref, xs_in_ref, xs_ref, sem):
    del xs_in_ref
    n = dest_ref.shape[1]

    def issue(p, _):
        _row_copy(u_ref, p // TOP_K, xs_ref, dest_ref[0, p], sem).start()
        return 0

    lax.fori_loop(0, n, issue, 0)

    def drain(p, _):
        _row_copy(u_ref, p // TOP_K, xs_ref, dest_ref[0, p], sem).wait()
        return 0

    lax.fori_loop(0, n, drain, 0)


def _dispatch(dest, u2r, xs, *, tm):
    rows = u2r.shape[0] // SUBLANES
    return pl.pallas_call(
        _dispatch_kernel,
        grid=(rows // tm,),
        in_specs=[
            pl.BlockSpec((None, 1, tm * TOP_K), lambda i: (i, 0, 0), memory_space=pltpu.SMEM),
            pl.BlockSpec((tm * SUBLANES, LANES), lambda i: (i, 0)),
            pl.BlockSpec(memory_space=pl.ANY),
        ],
        out_specs=pl.BlockSpec(memory_space=pl.ANY),
        out_shape=jax.ShapeDtypeStruct(xs.shape, xs.dtype),
        scratch_shapes=[pltpu.SemaphoreType.DMA(())],
        input_output_aliases={2: 0},
        compiler_params=_cparams(("arbitrary",)),
        name="moe_dispatch",
    )(dest, u2r, xs)


def _ffn_kernel(be_ref, x_ref, wgu_ref, bgu_ref, wd_ref, bd_ref, y_ref):
    del be_ref
    x = _load_rows8(x_ref, MOE_BLK).astype(BF16)
    gu = jnp.dot(x, wgu_ref[...], preferred_element_type=F32) + bgu_ref[...]
    gate = jnp.minimum(gu[:, :D_FF], SWIGLU_LIMIT)
    up = jnp.clip(gu[:, D_FF:], -SWIGLU_LIMIT, SWIGLU_LIMIT)
    hdn = (up + 1.0) * gate * _sigmoid(SWIGLU_ALPHA * gate)
    y = jnp.dot(hdn.astype(BF16), wd_ref[...], preferred_element_type=F32) + bd_ref[...]
    _store_rows8(y_ref, y)


def _ffn(block_exp, xs, wgu, bgu, wd, bd):
    n_blocks = xs.shape[0] // (MOE_BLK * SUBLANES)
    return pl.pallas_call(
        _ffn_kernel,
        grid_spec=pltpu.PrefetchScalarGridSpec(
            num_scalar_prefetch=1,
            grid=(n_blocks,),
            in_specs=[
                pl.BlockSpec((MOE_BLK * SUBLANES, LANES), lambda i, be: (i, 0)),
                pl.BlockSpec((None, D_MODEL, 2 * D_FF), lambda i, be: (be[i], 0, 0)),
                pl.BlockSpec((None, 1, 2 * D_FF), lambda i, be: (be[i], 0, 0)),
                pl.BlockSpec((None, D_FF, D_MODEL), lambda i, be: (be[i], 0, 0)),
                pl.BlockSpec((None, 1, D_MODEL), lambda i, be: (be[i], 0, 0)),
            ],
            out_specs=pl.BlockSpec((MOE_BLK * SUBLANES, LANES), lambda i, be: (i, 0)),
        ),
        out_shape=jax.ShapeDtypeStruct(xs.shape, F32),
        compiler_params=_cparams(("arbitrary",)),
        name="moe_ffn",
    )(block_exp, xs, wgu, bgu, wd, bd)


def _combine_kernel(dest_ref, h1_ref, gate_ref, fg_ref, ys_ref, y_ref, buf, sem):
    tm = h1_ref.shape[0]
    n = dest_ref.shape[1]

    def issue(p, _):
        _row_copy(ys_ref, dest_ref[0, p], buf.at[p % TOP_K], p // TOP_K, sem).start()
        return 0

    lax.fori_loop(0, n, issue, 0)

    def drain(p, _):
        _row_copy(ys_ref, dest_ref[0, p], buf.at[p % TOP_K], p // TOP_K, sem).wait()
        return 0

    lax.fori_loop(0, n, drain, 0)
    h = h1_ref[...]
    gate = gate_ref[...]
    for k in range(TOP_K):
        h = h + gate[:, k:k + 1] * _load_rows8(buf.at[k], tm)
    ms = jnp.mean(h * h, axis=-1, keepdims=True)
    y_ref[...] = h * lax.rsqrt(ms + EPS) * fg_ref[...]


def _combine(dest, h1, gate, fg, ys, *, tm):
    rows = h1.shape[0]
    return pl.pallas_call(
        _combine_kernel,
        grid=(rows // tm,),
        in_specs=[
            pl.BlockSpec((None, 1, tm * TOP_K), lambda i: (i, 0, 0), memory_space=pltpu.SMEM),
            pl.BlockSpec((tm, D_MODEL), lambda i: (i, 0)),
            pl.BlockSpec((tm, LANES), lambda i: (i, 0)),
            pl.BlockSpec((1, D_MODEL), lambda i: (0, 0)),
            pl.BlockSpec(memory_space=pl.ANY),
        ],
        out_specs=pl.BlockSpec((tm, D_MODEL), lambda i: (i, 0)),
        out_shape=jax.ShapeDtypeStruct((rows, D_MODEL), F32),
        scratch_shapes=[pltpu.VMEM((TOP_K, tm * SUBLANES, LANES), F32), pltpu.SemaphoreType.DMA(())],
        compiler_params=_cparams(("arbitrary",)),
        name="moe_combine",
    )(dest, h1, gate, fg, ys)


def _rope_tables(pos):
    d = ATT_HEAD_DIM
    inv = ROPE_THETA ** (-jnp.arange(0, d, 2, dtype=F32) / d)
    ang = pos.astype(F32)[:, None] * inv[None, :]
    cos = jnp.cos(ang)
    sin = jnp.sin(ang)
    cos_h = jnp.concatenate([cos, cos], axis=1)
    sin_h = jnp.concatenate([-sin, sin], axis=1)
    return jnp.tile(cos_h, (1, LANES // d)), jnp.tile(sin_h, (1, LANES // d))


def _per_group_lanes(v):
    out = jnp.zeros((SSD_GROUPS, 1, LANES), F32)
    return out.at[:, 0, :SSD_HPG].set(v.astype(F32).reshape(SSD_GROUPS, SSD_HPG))


def _conv_by_group(a):
    lead = a.shape[:-1]
    x = a[..., :D_INNER].reshape(lead + (SSD_GROUPS, GROUP_W))
    b = a[..., D_INNER:D_INNER + SSD_GN].reshape(lead + (SSD_GROUPS, SSD_STATE))
    c = a[..., D_INNER + SSD_GN:].reshape(lead + (SSD_GROUPS, SSD_STATE))
    return jnp.concatenate([x, b, c], axis=-1)


def _conv_from_group(a):
    lead = a.shape[:-2]
    x = a[..., :GROUP_W].reshape(lead + (D_INNER,))
    b = a[..., GROUP_W:GROUP_W + SSD_STATE].reshape(lead + (SSD_GN,))
    c = a[..., GROUP_W + SSD_STATE:].reshape(lead + (SSD_GN,))
    return jnp.concatenate([x, b, c], axis=-1)


def _conv_prev_blocks(prev):
    g = jnp.moveaxis(_conv_by_group(prev.astype(F32)), 1, 2)
    return jnp.pad(g, ((0, 0), (0, 0), (SUBLANES - (CONV_W - 1), 0), (0, 0)))


def _conv_tail_rows(ct):
    return _conv_from_group(jnp.moveaxis(ct[:, :, SUBLANES - (CONV_W - 1):, :], 1, 2))


def kernel(x_prompt, x_sample, cache_k, cache_v, state_ssm, state_conv, meta_tokens, norm1_g, w_in, conv_w, conv_b, dt_bias, a_log, d_skip, ssd_norm_g, lambda_q1, lambda_k1, lambda_q2, lambda_k2, subln_g, w_ssd_out, w_att_out, b_gate, w_o, norm2_g, w_router, b_router, w_gu, b_gu, w_down, b_down, final_norm_g):
    batch, seq, _ = x_prompt.shape
    dbatch, dseq, _ = x_sample.shape
    past = cache_k.shape[2]
    depth = norm1_g.shape[0]
    assert depth == 1 and dseq == N_META
    assert seq % SSD_L == 0 and seq % ATT_TQ == 0 and seq % TOKEN_TM == 0 and ATT_TQ % CHUNK == 0
    n_p = batch * seq
    n_dec = dbatch * dseq
    n_s = n_dec + N_META
    lam_init = 0.8 - 0.6 * math.exp(-0.3 * 0)
    l = 0

    wi = w_in[l]
    o_z, o_xbc, o_dt = 0, D_INNER, D_INNER + CONV_DIM
    o_q = o_dt + SSD_HEADS
    w_main = jnp.concatenate([wi[:, o_z:o_xbc], wi[:, o_xbc:o_dt], wi[:, o_q:]], axis=1).astype(BF16)
    w_dt = jnp.zeros((D_MODEL, SSD_GROUPS, LANES), F32).at[:, :, :SSD_HPG].set(
        wi[:, o_dt:o_q].reshape(D_MODEL, SSD_GROUPS, SSD_HPG)).reshape(D_MODEL, SSD_GROUPS * LANES).astype(BF16)
    g1 = norm1_g[l].reshape(1, D_MODEL)
    cw_g = jnp.moveaxis(_conv_by_group(conv_w[l]), 0, 1)
    cb_g = _conv_by_group(conv_b[l])[:, None, :]
    dtb_g = _per_group_lanes(dt_bias[l])
    aneg_g = _per_group_lanes(-jnp.exp(a_log[l].astype(F32)))
    dsk_g = _per_group_lanes(d_skip[l])
    norm_g = ssd_norm_g[l].reshape(1, D_INNER)
    lam = (jnp.exp(jnp.sum(lambda_q1[l].astype(F32) * lambda_k1[l].astype(F32)))
           - jnp.exp(jnp.sum(lambda_q2[l].astype(F32) * lambda_k2[l].astype(F32))) + lam_init).reshape(1)
    wso = w_ssd_out[l].astype(BF16)
    wao = w_att_out[l].astype(BF16)
    wo = w_o[l].astype(BF16)
    bg1 = b_gate[l][:D_MODEL].reshape(1, D_MODEL)
    bg2 = b_gate[l][D_MODEL:].reshape(1, D_MODEL)
    sub_g = subln_g[l].reshape(1, ATT_V_DIM)
    n2_g = norm2_g[l].reshape(1, D_MODEL)
    wr = jnp.zeros((D_MODEL, LANES), F32).at[:, :N_EXPERTS].set(w_router[l])
    br = jnp.zeros((1, LANES), F32).at[0, :N_EXPERTS].set(b_router[l])
    wgu = w_gu[l].astype(BF16)
    bgu = b_gu[l][:, None, :]
    wd = w_down[l].astype(BF16)
    bd = b_down[l][:, None, :]
    fg = final_norm_g.reshape(1, D_MODEL)

    xp = x_prompt.reshape(n_p, D_MODEL)
    xs_rows = jnp.concatenate([x_sample.reshape(n_dec, D_MODEL), meta_tokens.astype(x_prompt.dtype)], axis=0)
    cos_p, sin_p = _rope_tables(N_META + jnp.arange(seq, dtype=I32))
    pos_s = jnp.concatenate([jnp.tile(past + jnp.arange(dseq, dtype=I32), dbatch), jnp.arange(N_META, dtype=I32)])
    cos_s, sin_s = _rope_tables(pos_s)

    proj_p, dt_p = _in_proj(xp, g1, w_main, w_dt, cos_p, sin_p, TOKEN_TM, seq // TOKEN_TM)
    proj_s, dt_s = _in_proj(xs_rows, g1, w_main, w_dt, cos_s, sin_s, n_s, 1)

    ssd_args = (cw_g, cb_g, dtb_g, aneg_g, dsk_g, norm_g)
    zero_h = jnp.zeros((1, SSD_HEADS, SSD_HEAD_DIM, SSD_STATE), F32)
    zero_c = jnp.zeros((1, SSD_GROUPS, SUBLANES, CONV_GW), F32)
    yn_m, h_m, ct_m = _ssd(proj_s, dt_s, zero_h, zero_c, *ssd_args, n_seq=1, t=N_META, l=N_META,
                           row0=n_dec, shared_state=True)
    yn_p, h_p, ct_p = _ssd(proj_p, dt_p, h_m, ct_m, *ssd_args, n_seq=batch, t=seq, l=SSD_L, row0=0,
                           shared_state=True)
    yn_d, h_d, ct_d = _ssd(proj_s, dt_s, state_ssm[l].astype(F32), _conv_prev_blocks(state_conv[l]), *ssd_args,
                           n_seq=dbatch, t=dseq, l=dseq, row0=0, shared_state=False)
    yn_s = jnp.concatenate([yn_d, yn_m], axis=0)

    o_p = _attn_prompt(lam, proj_p, proj_s, batch=batch, seq=seq, meta_row0=n_dec)
    o_d = _attn_short(lam, proj_s, cache_k[l].reshape(dbatch, past, ATT_WIDTH),
                      cache_v[l].reshape(dbatch, past, ATT_WIDTH), n_seq=dbatch, t=dseq, row0=0)
    o_m = _attn_short(lam, proj_s, None, None, n_seq=1, t=N_META, row0=n_dec)
    o_s = jnp.concatenate([o_d, o_m], axis=0)

    merge_w = (wso, wao, wo, bg1, bg2, sub_g, n2_g, wr, br)
    zero_cnt = jnp.zeros((1, LANES), F32)
    h1_p, u2_p, e_p, gt_p, rk_p, cnt_p = _merge(xp, yn_p, o_p, proj_p, *merge_w, zero_cnt, tm=TOKEN_TM,
                                                sub_scale=1.0 - lam_init)
    h1_s, u2_s, e_s, gt_s, rk_s, cnt = _merge(xs_rows, yn_s, o_s, proj_s, *merge_w, cnt_p, tm=n_s,
                                              sub_scale=1.0 - lam_init)

    counts = cnt[0, :N_EXPERTS].astype(I32)
    padded = (counts + MOE_BLK - 1) // MOE_BLK * MOE_BLK
    pend = jnp.cumsum(padded)
    pstart = pend - padded
    n_blocks = -(-((n_p + n_s) * TOP_K) // MOE_BLK) + N_EXPERTS
    block_exp = jnp.clip(jnp.searchsorted(pend, jnp.arange(n_blocks, dtype=I32) * MOE_BLK, side='right'),
                         0, N_EXPERTS - 1).astype(I32)

    def dest_of(e, rk):
        return (pstart[e[:, :TOP_K]] + rk[:, :TOP_K]).astype(I32)

    dest_p = dest_of(e_p, rk_p).reshape(n_p // TOKEN_TM, 1, TOKEN_TM * TOP_K)
    dest_s = dest_of(e_s, rk_s).reshape(1, 1, n_s * TOP_K)
    xs = jnp.zeros((n_blocks * MOE_BLK * SUBLANES, LANES), F32)
    xs = _dispatch(dest_p, u2_p, xs, tm=TOKEN_TM)
    xs = _dispatch(dest_s, u2_s, xs, tm=n_s)
    ys = _ffn(block_exp, xs, wgu, bgu, wd, bd)
    y_p = _combine(dest_p, h1_p, gt_p, fg, ys, tm=TOKEN_TM)
    y_s = _combine(dest_s, h1_s, gt_s, fg, ys, tm=n_s)

    def kv_rows(proj, col, lo, hi):
        return proj[lo:hi, col:col + ATT_WIDTH].astype(F32)

    k_meta = kv_rows(proj_s, COL_K, n_dec, n_s).reshape(1, N_META, 2 * ATT_HEADS, ATT_HEAD_DIM)
    v_meta = kv_rows(proj_s, COL_V, n_dec, n_s).reshape(1, N_META, ATT_HEADS, ATT_V_DIM)
    k_fr = kv_rows(proj_p, COL_K, 0, n_p).reshape(batch, seq, 2 * ATT_HEADS, ATT_HEAD_DIM)
    v_fr = kv_rows(proj_p, COL_V, 0, n_p).reshape(batch, seq, ATT_HEADS, ATT_V_DIM)
    new_k_p = jnp.concatenate([jnp.broadcast_to(k_meta, (batch,) + k_meta.shape[1:]), k_fr], axis=1)[None]
    new_v_p = jnp.concatenate([jnp.broadcast_to(v_meta, (batch,) + v_meta.shape[1:]), v_fr], axis=1)[None]
    new_k_s = kv_rows(proj_s, COL_K, 0, n_dec).reshape(1, dbatch, dseq, 2 * ATT_HEADS, ATT_HEAD_DIM)
    new_v_s = kv_rows(proj_s, COL_V, 0, n_dec).reshape(1, dbatch, dseq, ATT_HEADS, ATT_V_DIM)
    return (y_p.reshape(batch, seq, D_MODEL),
            y_s[:n_dec].reshape(dbatch, dseq, D_MODEL),
            new_k_p, new_v_p,
            h_p.astype(state_ssm.dtype)[None],
            _conv_tail_rows(ct_p).astype(x_prompt.dtype)[None],
            new_k_s, new_v_s,
            h_d.astype(state_ssm.dtype)[None],
            _conv_tail_rows(ct_d).astype(x_sample.dtype)[None])
```

```python
import functools
import math

import jax
import jax.numpy as jnp
from jax import lax
from jax.experimental import pallas as pl
from jax.experimental.pallas import tpu as pltpu

F32 = jnp.float32
BF16 = jnp.bfloat16
I32 = jnp.int32

D_MODEL = 1024
D_INNER = 2048
SSD_HEADS = 32
SSD_HEAD_DIM = 64
SSD_GROUPS = 8
SSD_HPG = SSD_HEADS // SSD_GROUPS
SSD_STATE = 128
SSD_GN = SSD_GROUPS * SSD_STATE
CONV_W = 4
CONV_DIM = D_INNER + 2 * SSD_GN
ATT_HEADS = 8
ATT_HEAD_DIM = 64
ATT_V_DIM = 128
ATT_WIDTH = 1024
CHUNK = 64
N_META = 16
EPS = 1e-6
ROPE_THETA = 10000.0
N_EXPERTS = 32
TOP_K = 4
D_FF = 1024
SWIGLU_LIMIT = 7.0
SWIGLU_ALPHA = 1.702

COL_Z = 0
COL_X = COL_Z + D_INNER
COL_B = COL_X + D_INNER
COL_C = COL_B + SSD_GN
COL_Q = COL_C + SSD_GN
COL_K = COL_Q + ATT_WIDTH
COL_V = COL_K + ATT_WIDTH
COL_G = COL_V + ATT_WIDTH
N_MAIN = COL_G + 2 * D_MODEL

LANES = 128
SUBLANES = 8
GROUP_W = D_INNER // SSD_GROUPS
CONV_GW = GROUP_W + 2 * SSD_STATE
PROJ_TN = 1024
PROJ_TM = 1024
TOKEN_TM = 512
SSD_L = 256
ATT_TQ = 512
MOE_BLK = 256
DMA_UNROLL = 8
VMEM_LIMIT = 56 * 1024 * 1024
NEG_BIG = -1e30


def _cparams(sem):
    return pltpu.CompilerParams(dimension_semantics=sem, vmem_limit_bytes=VMEM_LIMIT)


def _sigmoid(x):
    return 1.0 / (1.0 + jnp.exp(-x))


def _inproj_kernel(x_ref, g_ref, w_ref, wdt_ref, cos_ref, sin_ref, o_ref, dt_ref, u_scr):
    j = pl.program_id(1)

    @pl.when(j == 0)
    def _():
        x = x_ref[...]
        ms = jnp.mean(x * x, axis=-1, keepdims=True)
        u = (x * lax.rsqrt(ms + EPS) * g_ref[...]).astype(BF16)
        u_scr[...] = u
        dt_ref[...] = jnp.dot(u, wdt_ref[...], preferred_element_type=F32)

    acc = jnp.dot(u_scr[...], w_ref[...], preferred_element_type=F32)
    is_q = j == COL_Q // PROJ_TN
    is_k = j == COL_K // PROJ_TN

    @pl.when(is_q | is_k)
    def _():
        reps = PROJ_TN // LANES
        cosf = jnp.tile(cos_ref[...], (1, reps))
        sinf = jnp.tile(sin_ref[...], (1, reps))
        lane = lax.broadcasted_iota(I32, acc.shape, 1)
        first_half = (lane % ATT_HEAD_DIM) < (ATT_HEAD_DIM // 2)
        half = ATT_HEAD_DIM // 2
        swapped = jnp.where(first_half, pltpu.roll(acc, PROJ_TN - half, 1), pltpu.roll(acc, half, 1))
        scale = jnp.where(is_q, ATT_HEAD_DIM ** -0.5, 1.0).astype(F32)
        o_ref[...] = ((acc * cosf + swapped * sinf) * scale).astype(o_ref.dtype)

    @pl.when(jnp.logical_not(is_q | is_k))
    def _():
        o_ref[...] = acc.astype(o_ref.dtype)


def _in_proj(x, g1, w_main, w_dt, cos_t, sin_t, tm, rope_blocks):
    rows = x.shape[0]
    grid = (rows // tm, N_MAIN // PROJ_TN)
    return pl.pallas_call(
        _inproj_kernel,
        grid=grid,
        in_specs=[
            pl.BlockSpec((tm, D_MODEL), lambda i, j: (i, 0)),
            pl.BlockSpec((1, D_MODEL), lambda i, j: (0, 0)),
            pl.BlockSpec((D_MODEL, PROJ_TN), lambda i, j: (0, j)),
            pl.BlockSpec((D_MODEL, SSD_GROUPS * LANES), lambda i, j: (0, 0)),
            pl.BlockSpec((tm, LANES), lambda i, j: (i % rope_blocks, 0)),
            pl.BlockSpec((tm, LANES), lambda i, j: (i % rope_blocks, 0)),
        ],
        out_specs=[
            pl.BlockSpec((tm, PROJ_TN), lambda i, j: (i, j)),
            pl.BlockSpec((tm, SSD_GROUPS * LANES), lambda i, j: (i, 0)),
        ],
        out_shape=[
            jax.ShapeDtypeStruct((rows, N_MAIN), BF16),
            jax.ShapeDtypeStruct((rows, SSD_GROUPS * LANES), F32),
        ],
        scratch_shapes=[pltpu.VMEM((tm, D_MODEL), BF16)],
        compiler_params=_cparams(("parallel", "arbitrary")),
        name="in_proj",
    )(x, g1, w_main, w_dt, cos_t, sin_t)


def _ssd_kernel(x_ref, b_ref, c_ref, z_ref, dt_ref, h0_ref, cp_ref, cw_ref, cb_ref, dtb_ref,
                aneg_ref, dsk_ref, ng_ref, y_ref, hf_ref, ct_ref, h_scr, f_scr):
    c = pl.program_id(2)
    L = x_ref.shape[0]

    @pl.when(c == 0)
    def _():
        h_scr[...] = h0_ref[...]
        f_scr[0:SUBLANES, :] = cp_ref[...]

    f_scr[SUBLANES:SUBLANES + L, 0:GROUP_W] = x_ref[...].astype(F32)
    f_scr[SUBLANES:SUBLANES + L, GROUP_W:GROUP_W + SSD_STATE] = b_ref[...].astype(F32)
    f_scr[SUBLANES:SUBLANES + L, GROUP_W + SSD_STATE:CONV_GW] = c_ref[...].astype(F32)
    w = cw_ref[...]
    acc = cb_ref[...]
    for i in range(CONV_W):
        lo = SUBLANES - (CONV_W - 1) + i
        acc = acc + w[i:i + 1, :] * f_scr[lo:lo + L, :]
    xc = acc * _sigmoid(acc)
    tail = f_scr[L:L + SUBLANES, :]
    f_scr[0:SUBLANES, :] = tail
    ct_ref[...] = tail

    xg = xc[:, 0:GROUP_W]
    bm = xc[:, GROUP_W:GROUP_W + SSD_STATE].astype(BF16)
    cm = xc[:, GROUP_W + SSD_STATE:CONV_GW].astype(BF16)

    dtr = dt_ref[...] + dtb_ref[...]
    dt = jnp.maximum(dtr, 0.0) + jnp.log(1.0 + jnp.exp(-jnp.abs(dtr)))
    da = dt * aneg_ref[...]
    ti = lax.broadcasted_iota(I32, (L, L), 0)
    si = lax.broadcasted_iota(I32, (L, L), 1)
    causal = si <= ti
    tril = causal.astype(F32)
    cum = jnp.dot(tril, da, preferred_element_type=F32, precision=lax.Precision.HIGHEST)
    sel = (lax.broadcasted_iota(I32, (SUBLANES, LANES), 0)
           == lax.broadcasted_iota(I32, (SUBLANES, LANES), 1)).astype(F32)
    cum_t = lax.dot_general(sel, cum, (((1,), (1,)), ((), ())), preferred_element_type=F32,
                            precision=lax.Precision.HIGHEST)

    cb = lax.dot_general(cm, bm, (((1,), (1,)), ((), ())), preferred_element_type=F32)
    dsk = dsk_ref[...]
    ys = []
    for r in range(SSD_HPG):
        col = cum[:, r:r + 1]
        row = cum_t[r:r + 1, :]
        dec = jnp.exp(jnp.where(causal, col - row, NEG_BIG))
        m = (cb * dec).astype(BF16)
        xh = xg[:, r * SSD_HEAD_DIM:(r + 1) * SSD_HEAD_DIM]
        xdt = xh * dt[:, r:r + 1]
        h_prev = h_scr[r]
        y = jnp.dot(m, xdt.astype(BF16), preferred_element_type=F32)
        y = y + jnp.exp(col) * lax.dot_general(cm, h_prev.astype(BF16), (((1,), (1,)), ((), ())),
                                               preferred_element_type=F32)
        y = y + dsk[:, r:r + 1] * xh
        ys.append(y)
        tot = cum[L - 1:L, r:r + 1]
        xw = (xdt * jnp.exp(tot - col)).astype(BF16)
        upd = lax.dot_general(xw, bm, (((0,), (0,)), ((), ())), preferred_element_type=F32)
        h_scr[r] = h_prev * jnp.exp(tot) + upd
    yg = jnp.concatenate(ys, axis=1)
    z = z_ref[...].astype(F32)
    yz = yg * (z * _sigmoid(z))
    ms = jnp.mean(yz * yz, axis=-1, keepdims=True)
    y_ref[...] = (yz * lax.rsqrt(ms + EPS) * ng_ref[...]).astype(y_ref.dtype)
    hf_ref[...] = h_scr[...]


def _ssd(proj, dt_raw, h0, conv_prev, cw_g, cb_g, dtb_g, aneg_g, dsk_g, norm_g, *, n_seq, t, l, row0,
         shared_state):
    nc = t // l
    rb0 = row0 // l

    def rows(s, g, c):
        return rb0 + s * nc + c

    def sidx(s):
        return 0 if shared_state else s

    return pl.pallas_call(
        _ssd_kernel,
        grid=(n_seq, SSD_GROUPS, nc),
        in_specs=[
            pl.BlockSpec((l, GROUP_W), lambda s, g, c: (rows(s, g, c), COL_X // GROUP_W + g)),
            pl.BlockSpec((l, SSD_STATE), lambda s, g, c: (rows(s, g, c), COL_B // SSD_STATE + g)),
            pl.BlockSpec((l, SSD_STATE), lambda s, g, c: (rows(s, g, c), COL_C // SSD_STATE + g)),
            pl.BlockSpec((l, GROUP_W), lambda s, g, c: (rows(s, g, c), COL_Z // GROUP_W + g)),
            pl.BlockSpec((l, LANES), lambda s, g, c: (rows(s, g, c), g)),
            pl.BlockSpec((None, SSD_HPG, SSD_HEAD_DIM, SSD_STATE), lambda s, g, c: (sidx(s), g, 0, 0)),
            pl.BlockSpec((None, None, SUBLANES, CONV_GW), lambda s, g, c: (sidx(s), g, 0, 0)),
            pl.BlockSpec((None, CONV_W, CONV_GW), lambda s, g, c: (g, 0, 0)),
            pl.BlockSpec((None, 1, CONV_GW), lambda s, g, c: (g, 0, 0)),
            pl.BlockSpec((None, 1, LANES), lambda s, g, c: (g, 0, 0)),
            pl.BlockSpec((None, 1, LANES), lambda s, g, c: (g, 0, 0)),
            pl.BlockSpec((None, 1, LANES), lambda s, g, c: (g, 0, 0)),
            pl.BlockSpec((1, GROUP_W), lambda s, g, c: (0, g)),
        ],
        out_specs=[
            pl.BlockSpec((l, GROUP_W), lambda s, g, c: (s * nc + c, g)),
            pl.BlockSpec((None, SSD_HPG, SSD_HEAD_DIM, SSD_STATE), lambda s, g, c: (s, g, 0, 0)),
            pl.BlockSpec((None, None, SUBLANES, CONV_GW), lambda s, g, c: (s, g, 0, 0)),
        ],
        out_shape=[
            jax.ShapeDtypeStruct((n_seq * t, D_INNER), BF16),
            jax.ShapeDtypeStruct((n_seq, SSD_HEADS, SSD_HEAD_DIM, SSD_STATE), F32),
            jax.ShapeDtypeStruct((n_seq, SSD_GROUPS, SUBLANES, CONV_GW), F32),
        ],
        scratch_shapes=[
            pltpu.VMEM((SSD_HPG, SSD_HEAD_DIM, SSD_STATE), F32),
            pltpu.VMEM((l + SUBLANES, CONV_GW), F32),
        ],
        compiler_params=_cparams(("parallel", "parallel", "arbitrary")),
        name="ssd",
    )(proj, proj, proj, proj, dt_raw, h0, conv_prev, cw_g, cb_g, dtb_g, aneg_g, dsk_g, norm_g)


def _qk(q, k):
    return lax.dot_general(q, k, (((1,), (1,)), ((), ())), preferred_element_type=F32)


def _attn_prompt_kernel(lam_ref, q_ref, k_ref, vt_ref, mk_ref, mvt_ref, o_ref):
    i = pl.program_id(2)
    tq = q_ref.shape[0]
    lam = lam_ref[0]
    q = q_ref[...]
    lane = lax.broadcasted_iota(I32, q.shape, 1)
    zero = jnp.zeros_like(q)
    qm = [jnp.where(lane < ATT_HEAD_DIM, q, zero), jnp.where(lane >= ATT_HEAD_DIM, q, zero)]

    def scores(kblk):
        return tuple(_qk(kblk, qm[r]) for r in range(2))

    def update(s_pair, vt, state, mask=None):
        new = []
        for r in range(2):
            m_p, l_p, a_p = state[r]
            s = s_pair[r] if mask is None else jnp.where(mask, s_pair[r], NEG_BIG)
            m_n = jnp.maximum(m_p, jnp.max(s, axis=0, keepdims=True))
            alpha = jnp.exp(m_p - m_n)
            p = jnp.exp(s - m_n)
            l_n = alpha * l_p + jnp.sum(p, axis=0, keepdims=True)
            a_n = alpha * a_p + jnp.dot(vt, p.astype(BF16), preferred_element_type=F32)
            new.append((m_n, l_n, a_n))
        return tuple(new)

    s0 = scores(mk_ref[...])
    mvt = mvt_ref[...]
    state = []
    for r in range(2):
        m0 = jnp.max(s0[r], axis=0, keepdims=True)
        p0 = jnp.exp(s0[r] - m0)
        state.append((m0, jnp.sum(p0, axis=0, keepdims=True),
                      jnp.dot(mvt, p0.astype(BF16), preferred_element_type=F32)))
    state = tuple(state)

    def kblock(j):
        return k_ref[pl.ds(pl.multiple_of(j * tq, tq), tq), :]

    def body(j, carry):
        s_cur, st = carry
        s_next = scores(kblock(j + 1))
        return s_next, update(s_cur, vt_ref[j], st)

    s_last, state = lax.fori_loop(0, i, body, (scores(kblock(0)), state))
    kpos = lax.broadcasted_iota(I32, (tq, tq), 0) // CHUNK
    qpos = lax.broadcasted_iota(I32, (tq, tq), 1) // CHUNK
    state = update(s_last, vt_ref[i], state, mask=kpos <= qpos)
    outs = [state[r][2] / state[r][1] for r in range(2)]
    o_ref[...] = (outs[0] - lam * outs[1]).T.astype(o_ref.dtype)


def _attn_prompt(lam, proj_p, vt_p, proj_s, mvt, *, batch, seq, meta_row0):
    nq = seq // ATT_TQ
    return pl.pallas_call(
        _attn_prompt_kernel,
        grid=(batch, ATT_HEADS, nq),
        in_specs=[
            pl.BlockSpec(memory_space=pltpu.SMEM),
            pl.BlockSpec((ATT_TQ, LANES), lambda b, h, i: (b * nq + i, COL_Q // LANES + h)),
            pl.BlockSpec((seq, LANES), lambda b, h, i: (b, COL_K // LANES + h)),
            pl.BlockSpec((None, None, nq, ATT_V_DIM, ATT_TQ), lambda b, h, i: (b, h, 0, 0, 0)),
            pl.BlockSpec((N_META, LANES), lambda b, h, i: (meta_row0 // N_META, COL_K // LANES + h)),
            pl.BlockSpec((None, ATT_V_DIM, N_META), lambda b, h, i: (h, 0, 0)),
        ],
        out_specs=pl.BlockSpec((ATT_TQ, LANES), lambda b, h, i: (b * nq + i, h)),
        out_shape=jax.ShapeDtypeStruct((batch * seq, ATT_WIDTH), BF16),
        compiler_params=_cparams(("parallel", "parallel", "arbitrary")),
        name="attn_prompt",
    )(lam, proj_p, proj_p, vt_p, proj_s, mvt)


def _attn_short_kernel(lam_ref, q_ref, kn_ref, vn_ref, *rest, has_cache):
    if has_cache:
        kc_ref, vc_ref, o_ref = rest
    else:
        (o_ref,) = rest
    t = q_ref.shape[0]
    nh = 2 * ATT_HEADS
    lam = lam_ref[0]
    q = q_ref[...].astype(F32)
    qb = jnp.broadcast_to(q[None], (nh, t, ATT_WIDTH)).reshape(nh * t, ATT_WIDTH)
    row_head = lax.broadcasted_iota(I32, (nh * t, ATT_WIDTH), 0) // t
    col_head = lax.broadcasted_iota(I32, (nh * t, ATT_WIDTH), 1) // ATT_HEAD_DIM
    qbd = jnp.where(row_head == col_head, qb, 0.0).astype(BF16)
    s_new = _qk(qbd, kn_ref[...])
    m = jnp.max(s_new, axis=-1, keepdims=True)
    if has_cache:
        s_old = _qk(qbd, kc_ref[...].astype(BF16))
        m = jnp.maximum(m, jnp.max(s_old, axis=-1, keepdims=True))
    p_new = jnp.exp(s_new - m)
    den = jnp.sum(p_new, axis=-1, keepdims=True)
    acc = jnp.dot(p_new.astype(BF16), vn_ref[...], preferred_element_type=F32)
    if has_cache:
        p_old = jnp.exp(s_old - m)
        den = den + jnp.sum(p_old, axis=-1, keepdims=True)
        acc = acc + jnp.dot(p_old.astype(BF16), vc_ref[...].astype(BF16), preferred_element_type=F32)
    acc = acc / den
    for h in range(ATT_HEADS):
        cols = slice(h * ATT_V_DIM, (h + 1) * ATT_V_DIM)
        a1 = acc[(2 * h) * t:(2 * h + 1) * t, cols]
        a2 = acc[(2 * h + 1) * t:(2 * h + 2) * t, cols]
        o_ref[:, cols] = (a1 - lam * a2).astype(o_ref.dtype)


def _attn_short(lam, proj_s, cache_k, cache_v, *, n_seq, t, row0):
    has_cache = cache_k is not None
    rb0 = row0 // t
    in_specs = [
        pl.BlockSpec(memory_space=pltpu.SMEM),
        pl.BlockSpec((t, ATT_WIDTH), lambda s: (rb0 + s, COL_Q // ATT_WIDTH)),
        pl.BlockSpec((t, ATT_WIDTH), lambda s: (rb0 + s, COL_K // ATT_WIDTH)),
        pl.BlockSpec((t, ATT_WIDTH), lambda s: (rb0 + s, COL_V // ATT_WIDTH)),
    ]
    args = [lam, proj_s, proj_s, proj_s]
    if has_cache:
        past = cache_k.shape[1]
        in_specs += [pl.BlockSpec((None, past, ATT_WIDTH), lambda s: (s, 0, 0)),
                     pl.BlockSpec((None, past, ATT_WIDTH), lambda s: (s, 0, 0))]
        args += [cache_k, cache_v]
    return pl.pallas_call(
        functools.partial(_attn_short_kernel, has_cache=has_cache),
        grid=(n_seq,),
        in_specs=in_specs,
        out_specs=pl.BlockSpec((t, ATT_WIDTH), lambda s: (s, 0)),
        out_shape=jax.ShapeDtypeStruct((n_seq * t, ATT_WIDTH), BF16),
        compiler_params=_cparams(("parallel",)),
        name="attn_cached" if has_cache else "attn_meta",
    )(*args)


def _merge_kernel(x_ref, yn_ref, o_ref, g1_ref, g2_ref, wso_ref, wao_ref, wo_ref, bg1_ref, bg2_ref,
                  sub_ref, n2_ref, wr_ref, br_ref, base_ref,
                  h1_ref, u2_ref, eidx_ref, gate_ref, rank_ref, cnt_ref, cnt_scr, *, sub_scale):
    i = pl.program_id(0)
    tm = x_ref.shape[0]

    @pl.when(i == 0)
    def _():
        cnt_scr[...] = base_ref[...]

    y_ssd = jnp.dot(yn_ref[...], wso_ref[...], preferred_element_type=F32)
    o = o_ref[...].astype(F32)
    parts = []
    for h in range(ATT_HEADS):
        oh = o[:, h * ATT_V_DIM:(h + 1) * ATT_V_DIM]
        ms = jnp.mean(oh * oh, axis=-1, keepdims=True)
        parts.append(oh * lax.rsqrt(ms + EPS) * sub_ref[...] * sub_scale)
    on = jnp.concatenate(parts, axis=1).astype(BF16)
    y_att = jnp.dot(on, wao_ref[...], preferred_element_type=F32)
    gs = _sigmoid(g1_ref[...].astype(F32) + bg1_ref[...])
    ga = _sigmoid(g2_ref[...].astype(F32) + bg2_ref[...])
    mix_in = (gs * y_ssd + ga * y_att).astype(BF16)
    h1 = x_ref[...] + jnp.dot(mix_in, wo_ref[...], preferred_element_type=F32)
    h1_ref[...] = h1
    ms = jnp.mean(h1 * h1, axis=-1, keepdims=True)
    u2 = h1 * lax.rsqrt(ms + EPS) * n2_ref[...]
    _store_rows8(u2_ref, u2)

    logits = jnp.dot(u2, wr_ref[...], preferred_element_type=F32, precision=lax.Precision.HIGHEST)
    logits = logits + br_ref[...]
    lane = lax.broadcasted_iota(I32, (tm, LANES), 1).astype(F32)
    work = jnp.where(lane < N_EXPERTS, logits, NEG_BIG)
    vals, idxs, hots = [], [], []
    for _ in range(TOP_K):
        mx = jnp.max(work, axis=-1, keepdims=True)
        ix = jnp.min(jnp.where(work == mx, lane, float(LANES)), axis=-1, keepdims=True)
        hot = lane == ix
        vals.append(mx)
        idxs.append(ix)
        hots.append(hot)
        work = jnp.where(hot, NEG_BIG, work)
    es = [jnp.exp(v - vals[0]) for v in vals]
    den = es[0] + es[1] + es[2] + es[3]
    hot_all = (hots[0] | hots[1] | hots[2] | hots[3])
    ti = lax.broadcasted_iota(I32, (tm, tm), 0)
    si = lax.broadcasted_iota(I32, (tm, tm), 1)
    strict = (si < ti).astype(BF16)
    prefix = jnp.dot(strict, hot_all.astype(BF16), preferred_element_type=F32) + cnt_scr[...]
    eidx = jnp.zeros((tm, LANES), F32)
    gate = jnp.zeros((tm, LANES), F32)
    rank = jnp.zeros((tm, LANES), F32)
    for k in range(TOP_K):
        rk = jnp.sum(jnp.where(hots[k], prefix, 0.0), axis=-1, keepdims=True)
        eidx = jnp.where(lane == k, idxs[k], eidx)
        gate = jnp.where(lane == k, es[k] / den, gate)
        rank = jnp.where(lane == k, rk, rank)
    eidx_ref[...] = eidx.astype(I32)
    gate_ref[...] = gate
    rank_ref[...] = rank.astype(I32)
    cnt_scr[...] = cnt_scr[...] + jnp.sum(hot_all.astype(F32), axis=0, keepdims=True)
    cnt_ref[...] = cnt_scr[...]


def _merge(x, yn, o, proj, wso, wao, wo, bg1, bg2, sub_g, n2_g, wr, br, base_cnt, *, tm, sub_scale):
    rows = x.shape[0]
    full = lambda shape: pl.BlockSpec(shape, lambda i: (0,) * len(shape))
    tok = lambda w: pl.BlockSpec((tm, w), lambda i: (i, 0))
    rows8 = pl.BlockSpec((tm * SUBLANES, LANES), lambda i: (i, 0))
    return pl.pallas_call(
        functools.partial(_merge_kernel, sub_scale=sub_scale),
        grid=(rows // tm,),
        in_specs=[
            tok(D_MODEL), tok(D_INNER), tok(ATT_WIDTH),
            pl.BlockSpec((tm, D_MODEL), lambda i: (i, COL_G // D_MODEL)),
            pl.BlockSpec((tm, D_MODEL), lambda i: (i, COL_G // D_MODEL + 1)),
            full((D_INNER, D_MODEL)), full((ATT_WIDTH, D_MODEL)), full((D_MODEL, D_MODEL)),
            full((1, D_MODEL)), full((1, D_MODEL)), full((1, ATT_V_DIM)), full((1, D_MODEL)),
            full((D_MODEL, LANES)), full((1, LANES)), full((1, LANES)),
        ],
        out_specs=[tok(D_MODEL), rows8, tok(LANES), tok(LANES), tok(LANES), full((1, LANES))],
        out_shape=[
            jax.ShapeDtypeStruct((rows, D_MODEL), F32),
            jax.ShapeDtypeStruct((rows * SUBLANES, LANES), F32),
            jax.ShapeDtypeStruct((rows, LANES), I32),
            jax.ShapeDtypeStruct((rows, LANES), F32),
            jax.ShapeDtypeStruct((rows, LANES), I32),
            jax.ShapeDtypeStruct((1, LANES), F32),
        ],
        scratch_shapes=[pltpu.VMEM((1, LANES), F32)],
        compiler_params=_cparams(("arbitrary",)),
        name="merge_router",
    )(x, yn, o, proj, proj, wso, wao, wo, bg1, bg2, sub_g, n2_g, wr, br, base_cnt)


def _store_rows8(ref, val):
    n = val.shape[0]
    for c in range(D_MODEL // LANES):
        ref[pl.ds(c, n, stride=SUBLANES), :] = val[:, c * LANES:(c + 1) * LANES]


def _load_rows8(ref, n):
    return jnp.concatenate([ref[pl.ds(c, n, stride=SUBLANES), :] for c in range(D_MODEL // LANES)], axis=1)


def _row_copy(src_ref, src_row, dst_ref, dst_row, sem):
    src = src_ref.at[pl.ds(pl.multiple_of(src_row * SUBLANES, SUBLANES), SUBLANES), :]
    dst = dst_ref.at[pl.ds(pl.multiple_of(dst_row * SUBLANES, SUBLANES), SUBLANES), :]
    return pltpu.make_async_copy(src, dst, sem)


def _dispatch_kernel(dest_ref, u_ref, xs_in_ref, xs_ref, sem):
    del xs_in_ref
    n = dest_ref.shape[1]

    def issue(p, _):
        _row_copy(u_ref, p // TOP_K, xs_ref, dest_ref[0, p], sem).start()
        return 0

    lax.fori_loop(0, n, issue, 0, unroll=DMA_UNROLL)
    for _ in range(TOP_K):
        pltpu.make_async_copy(u_ref, u_ref, sem).wait()


def _dispatch(dest, u2r, xs, *, tm):
    rows = u2r.shape[0] // SUBLANES
    return pl.pallas_call(
        _dispatch_kernel,
        grid=(rows // tm,),
        in_specs=[
            pl.BlockSpec((None, 1, tm * TOP_K), lambda i: (i, 0, 0), memory_space=pltpu.SMEM),
            pl.BlockSpec((tm * SUBLANES, LANES), lambda i: (i, 0)),
            pl.BlockSpec(memory_space=pl.ANY),
        ],
        out_specs=pl.BlockSpec(memory_space=pl.ANY),
        out_shape=jax.ShapeDtypeStruct(xs.shape, xs.dtype),
        scratch_shapes=[pltpu.SemaphoreType.DMA(())],
        input_output_aliases={2: 0},
        compiler_params=_cparams(("arbitrary",)),
        name="moe_dispatch",
    )(dest, u2r, xs)


def _ffn_kernel(be_ref, x_ref, wgu_ref, bgu_ref, wd_ref, bd_ref, y_ref):
    del be_ref
    x = _load_rows8(x_ref, MOE_BLK).astype(BF16)
    gu = jnp.dot(x, wgu_ref[...], preferred_element_type=F32) + bgu_ref[...]
    gate = jnp.minimum(gu[:, :D_FF], SWIGLU_LIMIT)
    up = jnp.clip(gu[:, D_FF:], -SWIGLU_LIMIT, SWIGLU_LIMIT)
    hdn = (up + 1.0) * gate * _sigmoid(SWIGLU_ALPHA * gate)
    y = jnp.dot(hdn.astype(BF16), wd_ref[...], preferred_element_type=F32) + bd_ref[...]
    _store_rows8(y_ref, y)


def _ffn(block_exp, xs, wgu, bgu, wd, bd):
    n_blocks = xs.shape[0] // (MOE_BLK * SUBLANES)
    return pl.pallas_call(
        _ffn_kernel,
        grid_spec=pltpu.PrefetchScalarGridSpec(
            num_scalar_prefetch=1,
            grid=(n_blocks,),
            in_specs=[
                pl.BlockSpec((MOE_BLK * SUBLANES, LANES), lambda i, be: (i, 0)),
                pl.BlockSpec((None, D_MODEL, 2 * D_FF), lambda i, be: (be[i], 0, 0)),
                pl.BlockSpec((None, 1, 2 * D_FF), lambda i, be: (be[i], 0, 0)),
                pl.BlockSpec((None, D_FF, D_MODEL), lambda i, be: (be[i], 0, 0)),
                pl.BlockSpec((None, 1, D_MODEL), lambda i, be: (be[i], 0, 0)),
            ],
            out_specs=pl.BlockSpec((MOE_BLK * SUBLANES, LANES), lambda i, be: (i, 0)),
        ),
        out_shape=jax.ShapeDtypeStruct(xs.shape, F32),
        compiler_params=_cparams(("arbitrary",)),
        name="moe_ffn",
    )(block_exp, xs, wgu, bgu, wd, bd)


def _combine_kernel(dest_ref, h1_ref, gate_ref, fg_ref, ys_ref, y_ref, buf, sem):
    tm = h1_ref.shape[0]
    n = dest_ref.shape[1]

    def issue(p, _):
        _row_copy(ys_ref, dest_ref[0, p], buf.at[p % TOP_K], p // TOP_K, sem).start()
        return 0

    lax.fori_loop(0, n, issue, 0, unroll=DMA_UNROLL)
    pltpu.make_async_copy(buf, buf, sem).wait()
    h = h1_ref[...]
    gate = gate_ref[...]
    for k in range(TOP_K):
        h = h + gate[:, k:k + 1] * _load_rows8(buf.at[k], tm)
    ms = jnp.mean(h * h, axis=-1, keepdims=True)
    y_ref[...] = h * lax.rsqrt(ms + EPS) * fg_ref[...]


def _combine(dest, h1, gate, fg, ys, *, tm):
    rows = h1.shape[0]
    return pl.pallas_call(
        _combine_kernel,
        grid=(rows // tm,),
        in_specs=[
            pl.BlockSpec((None, 1, tm * TOP_K), lambda i: (i, 0, 0), memory_space=pltpu.SMEM),
            pl.BlockSpec((tm, D_MODEL), lambda i: (i, 0)),
            pl.BlockSpec((tm, LANES), lambda i: (i, 0)),
            pl.BlockSpec((1, D_MODEL), lambda i: (0, 0)),
            pl.BlockSpec(memory_space=pl.ANY),
        ],
        out_specs=pl.BlockSpec((tm, D_MODEL), lambda i: (i, 0)),
        out_shape=jax.ShapeDtypeStruct((rows, D_MODEL), F32),
        scratch_shapes=[pltpu.VMEM((TOP_K, tm * SUBLANES, LANES), F32), pltpu.SemaphoreType.DMA(())],
        compiler_params=_cparams(("arbitrary",)),
        name="moe_combine",
    )(dest, h1, gate, fg, ys)


def _rope_tables(pos):
    d = ATT_HEAD_DIM
    inv = ROPE_THETA ** (-jnp.arange(0, d, 2, dtype=F32) / d)
    ang = pos.astype(F32)[:, None] * inv[None, :]
    cos = jnp.cos(ang)
    sin = jnp.sin(ang)
    cos_h = jnp.concatenate([cos, cos], axis=1)
    sin_h = jnp.concatenate([-sin, sin], axis=1)
    return jnp.tile(cos_h, (1, LANES // d)), jnp.tile(sin_h, (1, LANES // d))


def _per_group_lanes(v):
    out = jnp.zeros((SSD_GROUPS, 1, LANES), F32)
    return out.at[:, 0, :SSD_HPG].set(v.astype(F32).reshape(SSD_GROUPS, SSD_HPG))


def _conv_by_group(a):
    lead = a.shape[:-1]
    x = a[..., :D_INNER].reshape(lead + (SSD_GROUPS, GROUP_W))
    b = a[..., D_INNER:D_INNER + SSD_GN].reshape(lead + (SSD_GROUPS, SSD_STATE))
    c = a[..., D_INNER + SSD_GN:].reshape(lead + (SSD_GROUPS, SSD_STATE))
    return jnp.concatenate([x, b, c], axis=-1)


def _conv_from_group(a):
    lead = a.shape[:-2]
    x = a[..., :GROUP_W].reshape(lead + (D_INNER,))
    b = a[..., GROUP_W:GROUP_W + SSD_STATE].reshape(lead + (SSD_GN,))
    c = a[..., GROUP_W + SSD_STATE:].reshape(lead + (SSD_GN,))
    return jnp.concatenate([x, b, c], axis=-1)


def _conv_prev_blocks(prev):
    g = jnp.moveaxis(_conv_by_group(prev.astype(F32)), 1, 2)
    return jnp.pad(g, ((0, 0), (0, 0), (SUBLANES - (CONV_W - 1), 0), (0, 0)))


def _conv_tail_rows(ct):
    return _conv_from_group(jnp.moveaxis(ct[:, :, SUBLANES - (CONV_W - 1):, :], 1, 2))


def kernel(x_prompt, x_sample, cache_k, cache_v, state_ssm, state_conv, meta_tokens, norm1_g, w_in, conv_w, conv_b, dt_bias, a_log, d_skip, ssd_norm_g, lambda_q1, lambda_k1, lambda_q2, lambda_k2, subln_g, w_ssd_out, w_att_out, b_gate, w_o, norm2_g, w_router, b_router, w_gu, b_gu, w_down, b_down, final_norm_g):
    batch, seq, _ = x_prompt.shape
    dbatch, dseq, _ = x_sample.shape
    past = cache_k.shape[2]
    depth = norm1_g.shape[0]
    assert depth == 1 and dseq == N_META
    assert seq % SSD_L == 0 and seq % ATT_TQ == 0 and seq % TOKEN_TM == 0 and ATT_TQ % CHUNK == 0
    n_p = batch * seq
    n_dec = dbatch * dseq
    n_s = n_dec + N_META
    lam_init = 0.8 - 0.6 * math.exp(-0.3 * 0)
    l = 0

    wi = w_in[l]
    o_z, o_xbc, o_dt = 0, D_INNER, D_INNER + CONV_DIM
    o_q = o_dt + SSD_HEADS
    w_main = jnp.concatenate([wi[:, o_z:o_xbc], wi[:, o_xbc:o_dt], wi[:, o_q:]], axis=1).astype(BF16)
    w_dt = jnp.zeros((D_MODEL, SSD_GROUPS, LANES), F32).at[:, :, :SSD_HPG].set(
        wi[:, o_dt:o_q].reshape(D_MODEL, SSD_GROUPS, SSD_HPG)).reshape(D_MODEL, SSD_GROUPS * LANES).astype(BF16)
    g1 = norm1_g[l].reshape(1, D_MODEL)
    cw_g = jnp.moveaxis(_conv_by_group(conv_w[l]), 0, 1)
    cb_g = _conv_by_group(conv_b[l])[:, None, :]
    dtb_g = _per_group_lanes(dt_bias[l])
    aneg_g = _per_group_lanes(-jnp.exp(a_log[l].astype(F32)))
    dsk_g = _per_group_lanes(d_skip[l])
    norm_g = ssd_norm_g[l].reshape(1, D_INNER)
    lam = (jnp.exp(jnp.sum(lambda_q1[l].astype(F32) * lambda_k1[l].astype(F32)))
           - jnp.exp(jnp.sum(lambda_q2[l].astype(F32) * lambda_k2[l].astype(F32))) + lam_init).reshape(1)
    wso = w_ssd_out[l].astype(BF16)
    wao = w_att_out[l].astype(BF16)
    wo = w_o[l].astype(BF16)
    bg1 = b_gate[l][:D_MODEL].reshape(1, D_MODEL)
    bg2 = b_gate[l][D_MODEL:].reshape(1, D_MODEL)
    sub_g = subln_g[l].reshape(1, ATT_V_DIM)
    n2_g = norm2_g[l].reshape(1, D_MODEL)
    wr = jnp.zeros((D_MODEL, LANES), F32).at[:, :N_EXPERTS].set(w_router[l])
    br = jnp.zeros((1, LANES), F32).at[0, :N_EXPERTS].set(b_router[l])
    wgu = w_gu[l].astype(BF16)
    bgu = b_gu[l][:, None, :]
    wd = w_down[l].astype(BF16)
    bd = b_down[l][:, None, :]
    fg = final_norm_g.reshape(1, D_MODEL)

    xp = x_prompt.reshape(n_p, D_MODEL)
    xs_rows = jnp.concatenate([x_sample.reshape(n_dec, D_MODEL), meta_tokens.astype(x_prompt.dtype)], axis=0)
    cos_p, sin_p = _rope_tables(N_META + jnp.arange(seq, dtype=I32))
    pos_s = jnp.concatenate([jnp.tile(past + jnp.arange(dseq, dtype=I32), dbatch), jnp.arange(N_META, dtype=I32)])
    cos_s, sin_s = _rope_tables(pos_s)

    tm_p = PROJ_TM if seq % PROJ_TM == 0 else TOKEN_TM
    proj_p, dt_p = _in_proj(xp, g1, w_main, w_dt, cos_p, sin_p, tm_p, seq // tm_p)
    proj_s, dt_s = _in_proj(xs_rows, g1, w_main, w_dt, cos_s, sin_s, n_s, 1)

    ssd_args = (cw_g, cb_g, dtb_g, aneg_g, dsk_g, norm_g)
    zero_h = jnp.zeros((1, SSD_HEADS, SSD_HEAD_DIM, SSD_STATE), F32)
    zero_c = jnp.zeros((1, SSD_GROUPS, SUBLANES, CONV_GW), F32)
    yn_m, h_m, ct_m = _ssd(proj_s, dt_s, zero_h, zero_c, *ssd_args, n_seq=1, t=N_META, l=N_META,
                           row0=n_dec, shared_state=True)
    yn_p, h_p, ct_p = _ssd(proj_p, dt_p, h_m, ct_m, *ssd_args, n_seq=batch, t=seq, l=SSD_L, row0=0,
                           shared_state=True)
    yn_d, h_d, ct_d = _ssd(proj_s, dt_s, state_ssm[l].astype(F32), _conv_prev_blocks(state_conv[l]), *ssd_args,
                           n_seq=dbatch, t=dseq, l=dseq, row0=0, shared_state=False)
    yn_s = jnp.concatenate([yn_d, yn_m], axis=0)

    nq = seq // ATT_TQ
    vt_p = proj_p[:, COL_V:COL_V + ATT_WIDTH].reshape(batch, nq, ATT_TQ, ATT_HEADS, ATT_V_DIM).transpose(0, 3, 1, 4, 2)
    mvt = proj_s[n_dec:n_s, COL_V:COL_V + ATT_WIDTH].reshape(N_META, ATT_HEADS, ATT_V_DIM).transpose(1, 2, 0)
    o_p = _attn_prompt(lam, proj_p, vt_p, proj_s, mvt, batch=batch, seq=seq, meta_row0=n_dec)
    o_d = _attn_short(lam, proj_s, cache_k[l].reshape(dbatch, past, ATT_WIDTH),
                      cache_v[l].reshape(dbatch, past, ATT_WIDTH), n_seq=dbatch, t=dseq, row0=0)
    o_m = _attn_short(lam, proj_s, None, None, n_seq=1, t=N_META, row0=n_dec)
    o_s = jnp.concatenate([o_d, o_m], axis=0)

    merge_w = (wso, wao, wo, bg1, bg2, sub_g, n2_g, wr, br)
    zero_cnt = jnp.zeros((1, LANES), F32)
    h1_p, u2_p, e_p, gt_p, rk_p, cnt_p = _merge(xp, yn_p, o_p, proj_p, *merge_w, zero_cnt, tm=TOKEN_TM,
                                                sub_scale=1.0 - lam_init)
    h1_s, u2_s, e_s, gt_s, rk_s, cnt = _merge(xs_rows, yn_s, o_s, proj_s, *merge_w, cnt_p, tm=n_s,
                                              sub_scale=1.0 - lam_init)

    counts = cnt[0, :N_EXPERTS].astype(I32)
    padded = (counts + MOE_BLK - 1) // MOE_BLK * MOE_BLK
    pend = jnp.cumsum(padded)
    pstart = pend - padded
    n_blocks = -(-((n_p + n_s) * TOP_K) // MOE_BLK) + N_EXPERTS
    block_exp = jnp.clip(jnp.searchsorted(pend, jnp.arange(n_blocks, dtype=I32) * MOE_BLK, side='right'),
                         0, N_EXPERTS - 1).astype(I32)

    def dest_of(e, rk):
        return (pstart[e[:, :TOP_K]] + rk[:, :TOP_K]).astype(I32)

    dest_p = dest_of(e_p, rk_p).reshape(n_p // TOKEN_TM, 1, TOKEN_TM * TOP_K)
    dest_s = dest_of(e_s, rk_s).reshape(1, 1, n_s * TOP_K)
    xs = jnp.zeros((n_blocks * MOE_BLK * SUBLANES, LANES), F32)
    xs = _dispatch(dest_p, u2_p, xs, tm=TOKEN_TM)
    xs = _dispatch(dest_s, u2_s, xs, tm=n_s)
    ys = _ffn(block_exp, xs, wgu, bgu, wd, bd)
    y_p = _combine(dest_p, h1_p, gt_p, fg, ys, tm=TOKEN_TM)
    y_s = _combine(dest_s, h1_s, gt_s, fg, ys, tm=n_s)

    def kv_rows(proj, col, lo, hi):
        return proj[lo:hi, col:col + ATT_WIDTH].astype(F32)

    k_meta = kv_rows(proj_s, COL_K, n_dec, n_s).reshape(1, N_META, 2 * ATT_HEADS, ATT_HEAD_DIM)
    v_meta = kv_rows(proj_s, COL_V, n_dec, n_s).reshape(1, N_META, ATT_HEADS, ATT_V_DIM)
    k_fr = kv_rows(proj_p, COL_K, 0, n_p).reshape(batch, seq, 2 * ATT_HEADS, ATT_HEAD_DIM)
    v_fr = kv_rows(proj_p, COL_V, 0, n_p).reshape(batch, seq, ATT_HEADS, ATT_V_DIM)
    new_k_p = jnp.concatenate([jnp.broadcast_to(k_meta, (batch,) + k_meta.shape[1:]), k_fr], axis=1)[None]
    new_v_p = jnp.concatenate([jnp.broadcast_to(v_meta, (batch,) + v_meta.shape[1:]), v_fr], axis=1)[None]
    new_k_s = kv_rows(proj_s, COL_K, 0, n_dec).reshape(1, dbatch, dseq, 2 * ATT_HEADS, ATT_HEAD_DIM)
    new_v_s = kv_rows(proj_s, COL_V, 0, n_dec).reshape(1, dbatch, dseq, ATT_HEADS, ATT_V_DIM)
    return (y_p.reshape(batch, seq, D_MODEL),
            y_s[:n_dec].reshape(dbatch, dseq, D_MODEL),
            new_k_p, new_v_p,
            h_p.astype(state_ssm.dtype)[None],
            _conv_tail_rows(ct_p).astype(x_prompt.dtype)[None],
            new_k_s, new_v_s,
            h_d.astype(state_ssm.dtype)[None],
            _conv_tail_rows(ct_d).astype(x_sample.dtype)[None])
```

```python
import functools
import math

import jax
import jax.numpy as jnp
from jax import lax
from jax.experimental import pallas as pl
from jax.experimental.pallas import tpu as pltpu

F32 = jnp.float32
BF16 = jnp.bfloat16
I32 = jnp.int32

D_MODEL = 1024
D_INNER = 2048
SSD_HEADS = 32
SSD_HEAD_DIM = 64
SSD_GROUPS = 8
SSD_HPG = SSD_HEADS // SSD_GROUPS
SSD_STATE = 128
SSD_GN = SSD_GROUPS * SSD_STATE
CONV_W = 4
CONV_DIM = D_INNER + 2 * SSD_GN
ATT_HEADS = 8
ATT_HEAD_DIM = 64
ATT_V_DIM = 128
ATT_WIDTH = 1024
CHUNK = 64
N_META = 16
EPS = 1e-6
ROPE_THETA = 10000.0
N_EXPERTS = 32
TOP_K = 4
D_FF = 1024
SWIGLU_LIMIT = 7.0
SWIGLU_ALPHA = 1.702

COL_Z = 0
COL_X = COL_Z + D_INNER
COL_B = COL_X + D_INNER
COL_C = COL_B + SSD_GN
COL_Q = COL_C + SSD_GN
COL_K = COL_Q + ATT_WIDTH
COL_V = COL_K + ATT_WIDTH
COL_G = COL_V + ATT_WIDTH
N_MAIN = COL_G + 2 * D_MODEL

LANES = 128
SUBLANES = 8
GROUP_W = D_INNER // SSD_GROUPS
CONV_GW = GROUP_W + 2 * SSD_STATE
PROJ_TN = 1024
PROJ_TM = 1024
TOKEN_TM = 512
SSD_L = 256
ATT_TQ = 512
MOE_BLK = 256
DMA_UNROLL = 8
ONES_ROWS = 16
VMEM_LIMIT = 56 * 1024 * 1024
NEG_BIG = -1e30


def _cparams(sem):
    return pltpu.CompilerParams(dimension_semantics=sem, vmem_limit_bytes=VMEM_LIMIT)


def _sigmoid(x):
    return 1.0 / (1.0 + jnp.exp(-x))


def _inproj_kernel(x_ref, g_ref, w_ref, wdt_ref, cos_ref, sin_ref, o_ref, dt_ref, u_scr):
    j = pl.program_id(1)

    @pl.when(j == 0)
    def _():
        x = x_ref[...]
        ms = jnp.mean(x * x, axis=-1, keepdims=True)
        u = (x * lax.rsqrt(ms + EPS) * g_ref[...]).astype(BF16)
        u_scr[...] = u
        dt_ref[...] = jnp.dot(u, wdt_ref[...], preferred_element_type=F32)

    acc = jnp.dot(u_scr[...], w_ref[...], preferred_element_type=F32)
    is_q = j == COL_Q // PROJ_TN
    is_k = j == COL_K // PROJ_TN

    @pl.when(is_q | is_k)
    def _():
        cos = cos_ref[...]
        sin = sin_ref[...]
        lane = lax.broadcasted_iota(I32, cos.shape, 1)
        half = ATT_HEAD_DIM // 2
        first_half = (lane % ATT_HEAD_DIM) < half
        scale = jnp.where(is_q, ATT_HEAD_DIM ** -0.5, 1.0).astype(F32)
        for c in range(PROJ_TN // LANES):
            a = acc[:, c * LANES:(c + 1) * LANES]
            swapped = jnp.where(first_half, pltpu.roll(a, LANES - half, 1), pltpu.roll(a, half, 1))
            o_ref[:, c * LANES:(c + 1) * LANES] = ((a * cos + swapped * sin) * scale).astype(o_ref.dtype)

    @pl.when(jnp.logical_not(is_q | is_k))
    def _():
        o_ref[...] = acc.astype(o_ref.dtype)


def _in_proj(x, g1, w_main, w_dt, cos_t, sin_t, tm, rope_blocks):
    rows = x.shape[0]
    grid = (rows // tm, N_MAIN // PROJ_TN)
    return pl.pallas_call(
        _inproj_kernel,
        grid=grid,
        in_specs=[
            pl.BlockSpec((tm, D_MODEL), lambda i, j: (i, 0)),
            pl.BlockSpec((1, D_MODEL), lambda i, j: (0, 0)),
            pl.BlockSpec((D_MODEL, PROJ_TN), lambda i, j: (0, j)),
            pl.BlockSpec((D_MODEL, SSD_GROUPS * LANES), lambda i, j: (0, 0)),
            pl.BlockSpec((tm, LANES), lambda i, j: (i % rope_blocks, 0)),
            pl.BlockSpec((tm, LANES), lambda i, j: (i % rope_blocks, 0)),
        ],
        out_specs=[
            pl.BlockSpec((tm, PROJ_TN), lambda i, j: (i, j)),
            pl.BlockSpec((tm, SSD_GROUPS * LANES), lambda i, j: (i, 0)),
        ],
        out_shape=[
            jax.ShapeDtypeStruct((rows, N_MAIN), BF16),
            jax.ShapeDtypeStruct((rows, SSD_GROUPS * LANES), F32),
        ],
        scratch_shapes=[pltpu.VMEM((tm, D_MODEL), BF16)],
        compiler_params=_cparams(("parallel", "arbitrary")),
        name="in_proj",
    )(x, g1, w_main, w_dt, cos_t, sin_t)


def _ssd_kernel(x_ref, b_ref, c_ref, z_ref, dt_ref, h0_ref, cp_ref, cw_ref, cb_ref, dtb_ref,
                aneg_ref, dsk_ref, ng_ref, y_ref, hf_ref, ct_ref, h_scr, f_scr):
    c = pl.program_id(2)
    L = x_ref.shape[0]

    @pl.when(c == 0)
    def _():
        h_scr[...] = h0_ref[...]
        f_scr[0:SUBLANES, :] = cp_ref[...]

    f_scr[SUBLANES:SUBLANES + L, 0:GROUP_W] = x_ref[...].astype(F32)
    f_scr[SUBLANES:SUBLANES + L, GROUP_W:GROUP_W + SSD_STATE] = b_ref[...].astype(F32)
    f_scr[SUBLANES:SUBLANES + L, GROUP_W + SSD_STATE:CONV_GW] = c_ref[...].astype(F32)
    w = cw_ref[...]
    acc = cb_ref[...]
    for i in range(CONV_W):
        lo = SUBLANES - (CONV_W - 1) + i
        acc = acc + w[i:i + 1, :] * f_scr[lo:lo + L, :]
    xc = acc * _sigmoid(acc)
    tail = f_scr[L:L + SUBLANES, :]
    f_scr[0:SUBLANES, :] = tail
    ct_ref[...] = tail

    xg = xc[:, 0:GROUP_W]
    bm = xc[:, GROUP_W:GROUP_W + SSD_STATE].astype(BF16)
    cm = xc[:, GROUP_W + SSD_STATE:CONV_GW].astype(BF16)

    dtr = dt_ref[...] + dtb_ref[...]
    dt = jnp.maximum(dtr, 0.0) + jnp.log(1.0 + jnp.exp(-jnp.abs(dtr)))
    da = dt * aneg_ref[...]
    ti = lax.broadcasted_iota(I32, (L, L), 0)
    si = lax.broadcasted_iota(I32, (L, L), 1)
    causal = si <= ti
    tril = causal.astype(F32)
    cum = jnp.dot(tril, da, preferred_element_type=F32, precision=lax.Precision.HIGHEST)
    sel = (lax.broadcasted_iota(I32, (SUBLANES, LANES), 0)
           == lax.broadcasted_iota(I32, (SUBLANES, LANES), 1)).astype(F32)
    cum_t = lax.dot_general(sel, cum, (((1,), (1,)), ((), ())), preferred_element_type=F32,
                            precision=lax.Precision.HIGHEST)

    cb = lax.dot_general(cm, bm, (((1,), (1,)), ((), ())), preferred_element_type=F32)
    dsk = dsk_ref[...]
    ys = []
    for r in range(SSD_HPG):
        col = cum[:, r:r + 1]
        row = cum_t[r:r + 1, :]
        dec = jnp.exp(jnp.where(causal, col - row, NEG_BIG))
        m = (cb * dec).astype(BF16)
        xh = xg[:, r * SSD_HEAD_DIM:(r + 1) * SSD_HEAD_DIM]
        xdt = xh * dt[:, r:r + 1]
        h_prev = h_scr[r]
        y = jnp.dot(m, xdt.astype(BF16), preferred_element_type=F32)
        y = y + jnp.exp(col) * lax.dot_general(cm, h_prev.astype(BF16), (((1,), (1,)), ((), ())),
                                               preferred_element_type=F32)
        y = y + dsk[:, r:r + 1] * xh
        ys.append(y)
        tot = cum[L - 1:L, r:r + 1]
        xw = (xdt * jnp.exp(tot - col)).astype(BF16)
        upd = lax.dot_general(xw, bm, (((0,), (0,)), ((), ())), preferred_element_type=F32)
        h_scr[r] = h_prev * jnp.exp(tot) + upd
    yg = jnp.concatenate(ys, axis=1)
    z = z_ref[...].astype(F32)
    yz = yg * (z * _sigmoid(z))
    ms = jnp.mean(yz * yz, axis=-1, keepdims=True)
    y_ref[...] = (yz * lax.rsqrt(ms + EPS) * ng_ref[...]).astype(y_ref.dtype)
    hf_ref[...] = h_scr[...]


def _ssd(proj, dt_raw, h0, conv_prev, cw_g, cb_g, dtb_g, aneg_g, dsk_g, norm_g, *, n_seq, t, l, row0,
         shared_state):
    nc = t // l
    rb0 = row0 // l

    def rows(s, g, c):
        return rb0 + s * nc + c

    def sidx(s):
        return 0 if shared_state else s

    return pl.pallas_call(
        _ssd_kernel,
        grid=(n_seq, SSD_GROUPS, nc),
        in_specs=[
            pl.BlockSpec((l, GROUP_W), lambda s, g, c: (rows(s, g, c), COL_X // GROUP_W + g)),
            pl.BlockSpec((l, SSD_STATE), lambda s, g, c: (rows(s, g, c), COL_B // SSD_STATE + g)),
            pl.BlockSpec((l, SSD_STATE), lambda s, g, c: (rows(s, g, c), COL_C // SSD_STATE + g)),
            pl.BlockSpec((l, GROUP_W), lambda s, g, c: (rows(s, g, c), COL_Z // GROUP_W + g)),
            pl.BlockSpec((l, LANES), lambda s, g, c: (rows(s, g, c), g)),
            pl.BlockSpec((None, SSD_HPG, SSD_HEAD_DIM, SSD_STATE), lambda s, g, c: (sidx(s), g, 0, 0)),
            pl.BlockSpec((None, None, SUBLANES, CONV_GW), lambda s, g, c: (sidx(s), g, 0, 0)),
            pl.BlockSpec((None, CONV_W, CONV_GW), lambda s, g, c: (g, 0, 0)),
            pl.BlockSpec((None, 1, CONV_GW), lambda s, g, c: (g, 0, 0)),
            pl.BlockSpec((None, 1, LANES), lambda s, g, c: (g, 0, 0)),
            pl.BlockSpec((None, 1, LANES), lambda s, g, c: (g, 0, 0)),
            pl.BlockSpec((None, 1, LANES), lambda s, g, c: (g, 0, 0)),
            pl.BlockSpec((1, GROUP_W), lambda s, g, c: (0, g)),
        ],
        out_specs=[
            pl.BlockSpec((l, GROUP_W), lambda s, g, c: (s * nc + c, g)),
            pl.BlockSpec((None, SSD_HPG, SSD_HEAD_DIM, SSD_STATE), lambda s, g, c: (s, g, 0, 0)),
            pl.BlockSpec((None, None, SUBLANES, CONV_GW), lambda s, g, c: (s, g, 0, 0)),
        ],
        out_shape=[
            jax.ShapeDtypeStruct((n_seq * t, D_INNER), BF16),
            jax.ShapeDtypeStruct((n_seq, SSD_HEADS, SSD_HEAD_DIM, SSD_STATE), F32),
            jax.ShapeDtypeStruct((n_seq, SSD_GROUPS, SUBLANES, CONV_GW), F32),
        ],
        scratch_shapes=[
            pltpu.VMEM((SSD_HPG, SSD_HEAD_DIM, SSD_STATE), F32),
            pltpu.VMEM((l + SUBLANES, CONV_GW), F32),
        ],
        compiler_params=_cparams(("parallel", "parallel", "arbitrary")),
        name="ssd",
    )(proj, proj, proj, proj, dt_raw, h0, conv_prev, cw_g, cb_g, dtb_g, aneg_g, dsk_g, norm_g)


def _qk(q, k):
    return lax.dot_general(q, k, (((1,), (1,)), ((), ())), preferred_element_type=F32)


def _attn_prompt_kernel(lam_ref, q_ref, k_ref, vt_ref, mk_ref, mvt_ref, o_ref, sa_scr, sb_scr, acc_scr, m_scr):
    i = pl.program_id(2)
    tq = q_ref.shape[0]
    lam = lam_ref[0]
    q = q_ref[...]
    lane = lax.broadcasted_iota(I32, q.shape, 1)
    zero = jnp.zeros_like(q)
    qm = [jnp.where(lane < ATT_HEAD_DIM, q, zero), jnp.where(lane >= ATT_HEAD_DIM, q, zero)]

    def put_scores(s_ref, j):
        kblk = k_ref[pl.ds(pl.multiple_of(j * tq, tq), tq), :]
        for r in range(2):
            s_ref[r] = _qk(kblk, qm[r])

    def with_ones(vt):
        return jnp.concatenate([vt, jnp.ones((ONES_ROWS, vt.shape[1]), BF16)], axis=0)

    def update(s_ref, j, masked=False):
        vt1 = with_ones(vt_ref[j])
        for r in range(2):
            s = s_ref[r]
            if masked:
                kpos = lax.broadcasted_iota(I32, (tq, tq), 0) // CHUNK
                qpos = lax.broadcasted_iota(I32, (tq, tq), 1) // CHUNK
                s = jnp.where(kpos <= qpos, s, NEG_BIG)
            m_p = m_scr[r]
            m_n = jnp.maximum(m_p, jnp.max(s, axis=0, keepdims=True))
            alpha = jnp.exp(m_p - m_n)
            p = jnp.exp((s - m_n).astype(BF16))
            acc_scr[r] = alpha * acc_scr[r] + jnp.dot(vt1, p, preferred_element_type=F32)
            m_scr[r] = m_n

    mk = mk_ref[...]
    mvt1 = with_ones(mvt_ref[...])
    for r in range(2):
        s0 = _qk(mk, qm[r])
        m0 = jnp.max(s0, axis=0, keepdims=True)
        acc_scr[r] = jnp.dot(mvt1, jnp.exp((s0 - m0).astype(BF16)), preferred_element_type=F32)
        m_scr[r] = m0

    put_scores(sa_scr, 0)

    def body(jj, _):
        j = 2 * jj
        put_scores(sb_scr, j + 1)
        update(sa_scr, j)
        put_scores(sa_scr, j + 2)
        update(sb_scr, j + 1)
        return 0

    lax.fori_loop(0, i // 2, body, 0)

    @pl.when(i % 2 == 0)
    def _():
        update(sa_scr, i, masked=True)

    @pl.when(i % 2 == 1)
    def _():
        put_scores(sb_scr, i)
        update(sa_scr, i - 1)
        update(sb_scr, i, masked=True)

    outs = [acc_scr[r, :ATT_V_DIM, :] / acc_scr[r, ATT_V_DIM:ATT_V_DIM + 1, :] for r in range(2)]
    o_ref[...] = (outs[0] - lam * outs[1]).T.astype(o_ref.dtype)


def _attn_prompt(lam, proj_p, vt_p, proj_s, mvt, *, batch, seq, meta_row0):
    nq = seq // ATT_TQ
    return pl.pallas_call(
        _attn_prompt_kernel,
        grid=(batch, ATT_HEADS, nq),
        in_specs=[
            pl.BlockSpec(memory_space=pltpu.SMEM),
            pl.BlockSpec((ATT_TQ, LANES), lambda b, h, i: (b * nq + i, COL_Q // LANES + h)),
            pl.BlockSpec((seq, LANES), lambda b, h, i: (b, COL_K // LANES + h)),
            pl.BlockSpec((None, None, nq, ATT_V_DIM, ATT_TQ), lambda b, h, i: (b, h, 0, 0, 0)),
            pl.BlockSpec((N_META, LANES), lambda b, h, i: (meta_row0 // N_META, COL_K // LANES + h)),
            pl.BlockSpec((None, ATT_V_DIM, N_META), lambda b, h, i: (h, 0, 0)),
        ],
        out_specs=pl.BlockSpec((ATT_TQ, LANES), lambda b, h, i: (b * nq + i, h)),
        out_shape=jax.ShapeDtypeStruct((batch * seq, ATT_WIDTH), BF16),
        scratch_shapes=[pltpu.VMEM((2, ATT_TQ, ATT_TQ), F32),
                        pltpu.VMEM((2, ATT_TQ, ATT_TQ), F32),
                        pltpu.VMEM((2, ATT_V_DIM + ONES_ROWS, ATT_TQ), F32),
                        pltpu.VMEM((2, 1, ATT_TQ), F32)],
        compiler_params=_cparams(("parallel", "parallel", "arbitrary")),
        name="attn_prompt",
    )(lam, proj_p, proj_p, vt_p, proj_s, mvt)


def _attn_short_kernel(lam_ref, q_ref, kn_ref, vn_ref, *rest, has_cache):
    if has_cache:
        kc_ref, vc_ref, o_ref = rest
    else:
        (o_ref,) = rest
    t = q_ref.shape[0]
    nh = 2 * ATT_HEADS
    lam = lam_ref[0]
    q = q_ref[...].astype(F32)
    qb = jnp.broadcast_to(q[None], (nh, t, ATT_WIDTH)).reshape(nh * t, ATT_WIDTH)
    row_head = lax.broadcasted_iota(I32, (nh * t, ATT_WIDTH), 0) // t
    col_head = lax.broadcasted_iota(I32, (nh * t, ATT_WIDTH), 1) // ATT_HEAD_DIM
    qbd = jnp.where(row_head == col_head, qb, 0.0).astype(BF16)
    s_new = _qk(qbd, kn_ref[...])
    m = jnp.max(s_new, axis=-1, keepdims=True)
    if has_cache:
        s_old = _qk(qbd, kc_ref[...].astype(BF16))
        m = jnp.maximum(m, jnp.max(s_old, axis=-1, keepdims=True))
    p_new = jnp.exp(s_new - m)
    den = jnp.sum(p_new, axis=-1, keepdims=True)
    acc = jnp.dot(p_new.astype(BF16), vn_ref[...], preferred_element_type=F32)
    if has_cache:
        p_old = jnp.exp(s_old - m)
        den = den + jnp.sum(p_old, axis=-1, keepdims=True)
        acc = acc + jnp.dot(p_old.astype(BF16), vc_ref[...].astype(BF16), preferred_element_type=F32)
    acc = acc / den
    for h in range(ATT_HEADS):
        cols = slice(h * ATT_V_DIM, (h + 1) * ATT_V_DIM)
        a1 = acc[(2 * h) * t:(2 * h + 1) * t, cols]
        a2 = acc[(2 * h + 1) * t:(2 * h + 2) * t, cols]
        o_ref[:, cols] = (a1 - lam * a2).astype(o_ref.dtype)


def _attn_short(lam, proj_s, cache_k, cache_v, *, n_seq, t, row0):
    has_cache = cache_k is not None
    rb0 = row0 // t
    in_specs = [
        pl.BlockSpec(memory_space=pltpu.SMEM),
        pl.BlockSpec((t, ATT_WIDTH), lambda s: (rb0 + s, COL_Q // ATT_WIDTH)),
        pl.BlockSpec((t, ATT_WIDTH), lambda s: (rb0 + s, COL_K // ATT_WIDTH)),
        pl.BlockSpec((t, ATT_WIDTH), lambda s: (rb0 + s, COL_V // ATT_WIDTH)),
    ]
    args = [lam, proj_s, proj_s, proj_s]
    if has_cache:
        past = cache_k.shape[1]
        in_specs += [pl.BlockSpec((None, past, ATT_WIDTH), lambda s: (s, 0, 0)),
                     pl.BlockSpec((None, past, ATT_WIDTH), lambda s: (s, 0, 0))]
        args += [cache_k, cache_v]
    return pl.pallas_call(
        functools.partial(_attn_short_kernel, has_cache=has_cache),
        grid=(n_seq,),
        in_specs=in_specs,
        out_specs=pl.BlockSpec((t, ATT_WIDTH), lambda s: (s, 0)),
        out_shape=jax.ShapeDtypeStruct((n_seq * t, ATT_WIDTH), BF16),
        compiler_params=_cparams(("parallel",)),
        name="attn_cached" if has_cache else "attn_meta",
    )(*args)


def _merge_kernel(x_ref, yn_ref, o_ref, g1_ref, g2_ref, wso_ref, wao_ref, wo_ref, bg1_ref, bg2_ref,
                  sub_ref, n2_ref, wr_ref, br_ref, base_ref,
                  h1_ref, u2_ref, eidx_ref, gate_ref, rank_ref, cnt_ref, cnt_scr, *, sub_scale):
    i = pl.program_id(0)
    tm = x_ref.shape[0]

    @pl.when(i == 0)
    def _():
        cnt_scr[...] = base_ref[...]

    y_ssd = jnp.dot(yn_ref[...], wso_ref[...], preferred_element_type=F32)
    o = o_ref[...].astype(F32)
    parts = []
    for h in range(ATT_HEADS):
        oh = o[:, h * ATT_V_DIM:(h + 1) * ATT_V_DIM]
        ms = jnp.mean(oh * oh, axis=-1, keepdims=True)
        parts.append(oh * lax.rsqrt(ms + EPS) * sub_ref[...] * sub_scale)
    on = jnp.concatenate(parts, axis=1).astype(BF16)
    y_att = jnp.dot(on, wao_ref[...], preferred_element_type=F32)
    gs = _sigmoid(g1_ref[...].astype(F32) + bg1_ref[...])
    ga = _sigmoid(g2_ref[...].astype(F32) + bg2_ref[...])
    mix_in = (gs * y_ssd + ga * y_att).astype(BF16)
    h1 = x_ref[...] + jnp.dot(mix_in, wo_ref[...], preferred_element_type=F32)
    h1_ref[...] = h1
    ms = jnp.mean(h1 * h1, axis=-1, keepdims=True)
    u2 = h1 * lax.rsqrt(ms + EPS) * n2_ref[...]
    _store_rows8(u2_ref, u2)

    logits = jnp.dot(u2, wr_ref[...], preferred_element_type=F32, precision=lax.Precision.HIGHEST)
    logits = logits + br_ref[...]
    lane = lax.broadcasted_iota(I32, (tm, LANES), 1).astype(F32)
    work = jnp.where(lane < N_EXPERTS, logits, NEG_BIG)
    vals, idxs, hots = [], [], []
    for _ in range(TOP_K):
        mx = jnp.max(work, axis=-1, keepdims=True)
        ix = jnp.min(jnp.where(work == mx, lane, float(LANES)), axis=-1, keepdims=True)
        hot = lane == ix
        vals.append(mx)
        idxs.append(ix)
        hots.append(hot)
        work = jnp.where(hot, NEG_BIG, work)
    es = [jnp.exp(v - vals[0]) for v in vals]
    den = es[0] + es[1] + es[2] + es[3]
    hot_all = (hots[0] | hots[1] | hots[2] | hots[3])
    ti = lax.broadcasted_iota(I32, (tm, tm), 0)
    si = lax.broadcasted_iota(I32, (tm, tm), 1)
    strict = (si < ti).astype(BF16)
    prefix = jnp.dot(strict, hot_all.astype(BF16), preferred_element_type=F32) + cnt_scr[...]
    eidx = jnp.zeros((tm, LANES), F32)
    gate = jnp.zeros((tm, LANES), F32)
    rank = jnp.zeros((tm, LANES), F32)
    for k in range(TOP_K):
        rk = jnp.sum(jnp.where(hots[k], prefix, 0.0), axis=-1, keepdims=True)
        eidx = jnp.where(lane == k, idxs[k], eidx)
        gate = jnp.where(lane == k, es[k] / den, gate)
        rank = jnp.where(lane == k, rk, rank)
    eidx_ref[...] = eidx.astype(I32)
    gate_ref[...] = gate
    rank_ref[...] = rank.astype(I32)
    cnt_scr[...] = cnt_scr[...] + jnp.sum(hot_all.astype(F32), axis=0, keepdims=True)
    cnt_ref[...] = cnt_scr[...]


def _merge(x, yn, o, proj, wso, wao, wo, bg1, bg2, sub_g, n2_g, wr, br, base_cnt, *, tm, sub_scale):
    rows = x.shape[0]
    full = lambda shape: pl.BlockSpec(shape, lambda i: (0,) * len(shape))
    tok = lambda w: pl.BlockSpec((tm, w), lambda i: (i, 0))
    rows8 = pl.BlockSpec((tm * SUBLANES, LANES), lambda i: (i, 0))
    return pl.pallas_call(
        functools.partial(_merge_kernel, sub_scale=sub_scale),
        grid=(rows // tm,),
        in_specs=[
            tok(D_MODEL), tok(D_INNER), tok(ATT_WIDTH),
            pl.BlockSpec((tm, D_MODEL), lambda i: (i, COL_G // D_MODEL)),
            pl.BlockSpec((tm, D_MODEL), lambda i: (i, COL_G // D_MODEL + 1)),
            full((D_INNER, D_MODEL)), full((ATT_WIDTH, D_MODEL)), full((D_MODEL, D_MODEL)),
            full((1, D_MODEL)), full((1, D_MODEL)), full((1, ATT_V_DIM)), full((1, D_MODEL)),
            full((D_MODEL, LANES)), full((1, LANES)), full((1, LANES)),
        ],
        out_specs=[tok(D_MODEL), rows8, tok(LANES), tok(LANES), tok(LANES), full((1, LANES))],
        out_shape=[
            jax.ShapeDtypeStruct((rows, D_MODEL), F32),
            jax.ShapeDtypeStruct((rows * SUBLANES, LANES), F32),
            jax.ShapeDtypeStruct((rows, LANES), I32),
            jax.ShapeDtypeStruct((rows, LANES), F32),
            jax.ShapeDtypeStruct((rows, LANES), I32),
            jax.ShapeDtypeStruct((1, LANES), F32),
        ],
        scratch_shapes=[pltpu.VMEM((1, LANES), F32)],
        compiler_params=_cparams(("arbitrary",)),
        name="merge_router",
    )(x, yn, o, proj, proj, wso, wao, wo, bg1, bg2, sub_g, n2_g, wr, br, base_cnt)


def _store_rows8(ref, val):
    n = val.shape[0]
    for c in range(D_MODEL // LANES):
        ref[pl.ds(c, n, stride=SUBLANES), :] = val[:, c * LANES:(c + 1) * LANES]


def _load_rows8(ref, n):
    return jnp.concatenate([ref[pl.ds(c, n, stride=SUBLANES), :] for c in range(D_MODEL // LANES)], axis=1)


def _row_copy(src_ref, src_row, dst_ref, dst_row, sem):
    src = src_ref.at[pl.ds(pl.multiple_of(src_row * SUBLANES, SUBLANES), SUBLANES), :]
    dst = dst_ref.at[pl.ds(pl.multiple_of(dst_row * SUBLANES, SUBLANES), SUBLANES), :]
    return pltpu.make_async_copy(src, dst, sem)


def _dispatch_kernel(dest_ref, u_ref, xs_in_ref, xs_ref, sem):
    del xs_in_ref
    n = dest_ref.shape[1]

    def issue(p, _):
        _row_copy(u_ref, p // TOP_K, xs_ref, dest_ref[0, p], sem).start()
        return 0

    lax.fori_loop(0, n, issue, 0, unroll=DMA_UNROLL)
    for _ in range(TOP_K):
        pltpu.make_async_copy(u_ref, u_ref, sem).wait()


def _dispatch(dest, u2r, xs, *, tm):
    rows = u2r.shape[0] // SUBLANES
    return pl.pallas_call(
        _dispatch_kernel,
        grid=(rows // tm,),
        in_specs=[
            pl.BlockSpec((None, 1, tm * TOP_K), lambda i: (i, 0, 0), memory_space=pltpu.SMEM),
            pl.BlockSpec((tm * SUBLANES, LANES), lambda i: (i, 0)),
            pl.BlockSpec(memory_space=pl.ANY),
        ],
        out_specs=pl.BlockSpec(memory_space=pl.ANY),
        out_shape=jax.ShapeDtypeStruct(xs.shape, xs.dtype),
        scratch_shapes=[pltpu.SemaphoreType.DMA(())],
        input_output_aliases={2: 0},
        compiler_params=_cparams(("arbitrary",)),
        name="moe_dispatch",
    )(dest, u2r, xs)


def _ffn_kernel(be_ref, x_ref, wgu_ref, bgu_ref, wd_ref, bd_ref, y_ref):
    del be_ref
    x = _load_rows8(x_ref, MOE_BLK).astype(BF16)
    gu = jnp.dot(x, wgu_ref[...], preferred_element_type=F32) + bgu_ref[...]
    gate = jnp.minimum(gu[:, :D_FF], SWIGLU_LIMIT)
    up = jnp.clip(gu[:, D_FF:], -SWIGLU_LIMIT, SWIGLU_LIMIT)
    hdn = (up + 1.0) * gate * _sigmoid(SWIGLU_ALPHA * gate)
    y = jnp.dot(hdn.astype(BF16), wd_ref[...], preferred_element_type=F32) + bd_ref[...]
    _store_rows8(y_ref, y)


def _ffn(block_exp, xs, wgu, bgu, wd, bd):
    n_blocks = xs.shape[0] // (MOE_BLK * SUBLANES)
    return pl.pallas_call(
        _ffn_kernel,
        grid_spec=pltpu.PrefetchScalarGridSpec(
            num_scalar_prefetch=1,
            grid=(n_blocks,),
            in_specs=[
                pl.BlockSpec((MOE_BLK * SUBLANES, LANES), lambda i, be: (i, 0)),
                pl.BlockSpec((None, D_MODEL, 2 * D_FF), lambda i, be: (be[i], 0, 0)),
                pl.BlockSpec((None, 1, 2 * D_FF), lambda i, be: (be[i], 0, 0)),
                pl.BlockSpec((None, D_FF, D_MODEL), lambda i, be: (be[i], 0, 0)),
                pl.BlockSpec((None, 1, D_MODEL), lambda i, be: (be[i], 0, 0)),
            ],
            out_specs=pl.BlockSpec((MOE_BLK * SUBLANES, LANES), lambda i, be: (i, 0)),
        ),
        out_shape=jax.ShapeDtypeStruct(xs.shape, F32),
        compiler_params=_cparams(("arbitrary",)),
        name="moe_ffn",
    )(block_exp, xs, wgu, bgu, wd, bd)


def _combine_kernel(dest_ref, h1_ref, gate_ref, fg_ref, ys_ref, y_ref, buf, sem):
    tm = h1_ref.shape[0]
    n = dest_ref.shape[1]

    def issue(p, _):
        _row_copy(ys_ref, dest_ref[0, p], buf.at[p % TOP_K], p // TOP_K, sem).start()
        return 0

    lax.fori_loop(0, n, issue, 0, unroll=DMA_UNROLL)
    pltpu.make_async_copy(buf, buf, sem).wait()
    h = h1_ref[...]
    gate = gate_ref[...]
    for k in range(TOP_K):
        h = h + gate[:, k:k + 1] * _load_rows8(buf.at[k], tm)
    ms = jnp.mean(h * h, axis=-1, keepdims=True)
    y_ref[...] = h * lax.rsqrt(ms + EPS) * fg_ref[...]


def _combine(dest, h1, gate, fg, ys, *, tm):
    rows = h1.shape[0]
    return pl.pallas_call(
        _combine_kernel,
        grid=(rows // tm,),
        in_specs=[
            pl.BlockSpec((None, 1, tm * TOP_K), lambda i: (i, 0, 0), memory_space=pltpu.SMEM),
            pl.BlockSpec((tm, D_MODEL), lambda i: (i, 0)),
            pl.BlockSpec((tm, LANES), lambda i: (i, 0)),
            pl.BlockSpec((1, D_MODEL), lambda i: (0, 0)),
            pl.BlockSpec(memory_space=pl.ANY),
        ],
        out_specs=pl.BlockSpec((tm, D_MODEL), lambda i: (i, 0)),
        out_shape=jax.ShapeDtypeStruct((rows, D_MODEL), F32),
        scratch_shapes=[pltpu.VMEM((TOP_K, tm * SUBLANES, LANES), F32), pltpu.SemaphoreType.DMA(())],
        compiler_params=_cparams(("arbitrary",)),
        name="moe_combine",
    )(dest, h1, gate, fg, ys)


def _rope_tables(pos):
    d = ATT_HEAD_DIM
    inv = ROPE_THETA ** (-jnp.arange(0, d, 2, dtype=F32) / d)
    ang = pos.astype(F32)[:, None] * inv[None, :]
    cos = jnp.cos(ang)
    sin = jnp.sin(ang)
    cos_h = jnp.concatenate([cos, cos], axis=1)
    sin_h = jnp.concatenate([-sin, sin], axis=1)
    return jnp.tile(cos_h, (1, LANES // d)), jnp.tile(sin_h, (1, LANES // d))


def _per_group_lanes(v):
    out = jnp.zeros((SSD_GROUPS, 1, LANES), F32)
    return out.at[:, 0, :SSD_HPG].set(v.astype(F32).reshape(SSD_GROUPS, SSD_HPG))


def _conv_by_group(a):
    lead = a.shape[:-1]
    x = a[..., :D_INNER].reshape(lead + (SSD_GROUPS, GROUP_W))
    b = a[..., D_INNER:D_INNER + SSD_GN].reshape(lead + (SSD_GROUPS, SSD_STATE))
    c = a[..., D_INNER + SSD_GN:].reshape(lead + (SSD_GROUPS, SSD_STATE))
    return jnp.concatenate([x, b, c], axis=-1)


def _conv_from_group(a):
    lead = a.shape[:-2]
    x = a[..., :GROUP_W].reshape(lead + (D_INNER,))
    b = a[..., GROUP_W:GROUP_W + SSD_STATE].reshape(lead + (SSD_GN,))
    c = a[..., GROUP_W + SSD_STATE:].reshape(lead + (SSD_GN,))
    return jnp.concatenate([x, b, c], axis=-1)


def _conv_prev_blocks(prev):
    g = jnp.moveaxis(_conv_by_group(prev.astype(F32)), 1, 2)
    return jnp.pad(g, ((0, 0), (0, 0), (SUBLANES - (CONV_W - 1), 0), (0, 0)))


def _conv_tail_rows(ct):
    return _conv_from_group(jnp.moveaxis(ct[:, :, SUBLANES - (CONV_W - 1):, :], 1, 2))


def kernel(x_prompt, x_sample, cache_k, cache_v, state_ssm, state_conv, meta_tokens, norm1_g, w_in, conv_w, conv_b, dt_bias, a_log, d_skip, ssd_norm_g, lambda_q1, lambda_k1, lambda_q2, lambda_k2, subln_g, w_ssd_out, w_att_out, b_gate, w_o, norm2_g, w_router, b_router, w_gu, b_gu, w_down, b_down, final_norm_g):
    batch, seq, _ = x_prompt.shape
    dbatch, dseq, _ = x_sample.shape
    past = cache_k.shape[2]
    depth = norm1_g.shape[0]
    assert depth == 1 and dseq == N_META
    assert seq % SSD_L == 0 and seq % ATT_TQ == 0 and seq % TOKEN_TM == 0 and ATT_TQ % CHUNK == 0
    n_p = batch * seq
    n_dec = dbatch * dseq
    n_s = n_dec + N_META
    lam_init = 0.8 - 0.6 * math.exp(-0.3 * 0)
    l = 0

    wi = w_in[l]
    o_z, o_xbc, o_dt = 0, D_INNER, D_INNER + CONV_DIM
    o_q = o_dt + SSD_HEADS
    w_main = jnp.concatenate([wi[:, o_z:o_xbc], wi[:, o_xbc:o_dt], wi[:, o_q:]], axis=1).astype(BF16)
    w_dt = jnp.zeros((D_MODEL, SSD_GROUPS, LANES), F32).at[:, :, :SSD_HPG].set(
        wi[:, o_dt:o_q].reshape(D_MODEL, SSD_GROUPS, SSD_HPG)).reshape(D_MODEL, SSD_GROUPS * LANES).astype(BF16)
    g1 = norm1_g[l].reshape(1, D_MODEL)
    cw_g = jnp.moveaxis(_conv_by_group(conv_w[l]), 0, 1)
    cb_g = _conv_by_group(conv_b[l])[:, None, :]
    dtb_g = _per_group_lanes(dt_bias[l])
    aneg_g = _per_group_lanes(-jnp.exp(a_log[l].astype(F32)))
    dsk_g = _per_group_lanes(d_skip[l])
    norm_g = ssd_norm_g[l].reshape(1, D_INNER)
    lam = (jnp.exp(jnp.sum(lambda_q1[l].astype(F32) * lambda_k1[l].astype(F32)))
           - jnp.exp(jnp.sum(lambda_q2[l].astype(F32) * lambda_k2[l].astype(F32))) + lam_init).reshape(1)
    wso = w_ssd_out[l].astype(BF16)
    wao = w_att_out[l].astype(BF16)
    wo = w_o[l].astype(BF16)
    bg1 = b_gate[l][:D_MODEL].reshape(1, D_MODEL)
    bg2 = b_gate[l][D_MODEL:].reshape(1, D_MODEL)
    sub_g = subln_g[l].reshape(1, ATT_V_DIM)
    n2_g = norm2_g[l].reshape(1, D_MODEL)
    wr = jnp.zeros((D_MODEL, LANES), F32).at[:, :N_EXPERTS].set(w_router[l])
    br = jnp.zeros((1, LANES), F32).at[0, :N_EXPERTS].set(b_router[l])
    wgu = w_gu[l].astype(BF16)
    bgu = b_gu[l][:, None, :]
    wd = w_down[l].astype(BF16)
    bd = b_down[l][:, None, :]
    fg = final_norm_g.reshape(1, D_MODEL)

    xp = x_prompt.reshape(n_p, D_MODEL)
    xs_rows = jnp.concatenate([x_sample.reshape(n_dec, D_MODEL), meta_tokens.astype(x_prompt.dtype)], axis=0)
    cos_p, sin_p = _rope_tables(N_META + jnp.arange(seq, dtype=I32))
    pos_s = jnp.concatenate([jnp.tile(past + jnp.arange(dseq, dtype=I32), dbatch), jnp.arange(N_META, dtype=I32)])
    cos_s, sin_s = _rope_tables(pos_s)

    tm_p = PROJ_TM if seq % PROJ_TM == 0 else TOKEN_TM
    proj_p, dt_p = _in_proj(xp, g1, w_main, w_dt, cos_p, sin_p, tm_p, seq // tm_p)
    proj_s, dt_s = _in_proj(xs_rows, g1, w_main, w_dt, cos_s, sin_s, n_s, 1)

    ssd_args = (cw_g, cb_g, dtb_g, aneg_g, dsk_g, norm_g)
    zero_h = jnp.zeros((1, SSD_HEADS, SSD_HEAD_DIM, SSD_STATE), F32)
    zero_c = jnp.zeros((1, SSD_GROUPS, SUBLANES, CONV_GW), F32)
    yn_m, h_m, ct_m = _ssd(proj_s, dt_s, zero_h, zero_c, *ssd_args, n_seq=1, t=N_META, l=N_META,
                           row0=n_dec, shared_state=True)
    yn_p, h_p, ct_p = _ssd(proj_p, dt_p, h_m, ct_m, *ssd_args, n_seq=batch, t=seq, l=SSD_L, row0=0,
                           shared_state=True)
    yn_d, h_d, ct_d = _ssd(proj_s, dt_s, state_ssm[l].astype(F32), _conv_prev_blocks(state_conv[l]), *ssd_args,
                           n_seq=dbatch, t=dseq, l=dseq, row0=0, shared_state=False)
    yn_s = jnp.concatenate([yn_d, yn_m], axis=0)

    nq = seq // ATT_TQ
    vt_p = proj_p[:, COL_V:COL_V + ATT_WIDTH].reshape(batch, nq, ATT_TQ, ATT_HEADS, ATT_V_DIM).transpose(0, 3, 1, 4, 2)
    mvt = proj_s[n_dec:n_s, COL_V:COL_V + ATT_WIDTH].reshape(N_META, ATT_HEADS, ATT_V_DIM).transpose(1, 2, 0)
    o_p = _attn_prompt(lam, proj_p, vt_p, proj_s, mvt, batch=batch, seq=seq, meta_row0=n_dec)
    o_d = _attn_short(lam, proj_s, cache_k[l].reshape(dbatch, past, ATT_WIDTH),
                      cache_v[l].reshape(dbatch, past, ATT_WIDTH), n_seq=dbatch, t=dseq, row0=0)
    o_m = _attn_short(lam, proj_s, None, None, n_seq=1, t=N_META, row0=n_dec)
    o_s = jnp.concatenate([o_d, o_m], axis=0)

    merge_w = (wso, wao, wo, bg1, bg2, sub_g, n2_g, wr, br)
    zero_cnt = jnp.zeros((1, LANES), F32)
    h1_p, u2_p, e_p, gt_p, rk_p, cnt_p = _merge(xp, yn_p, o_p, proj_p, *merge_w, zero_cnt, tm=TOKEN_TM,
                                                sub_scale=1.0 - lam_init)
    h1_s, u2_s, e_s, gt_s, rk_s, cnt = _merge(xs_rows, yn_s, o_s, proj_s, *merge_w, cnt_p, tm=n_s,
                                              sub_scale=1.0 - lam_init)

    counts = cnt[0, :N_EXPERTS].astype(I32)
    padded = (counts + MOE_BLK - 1) // MOE_BLK * MOE_BLK
    pend = jnp.cumsum(padded)
    pstart = pend - padded
    n_blocks = -(-((n_p + n_s) * TOP_K) // MOE_BLK) + N_EXPERTS
    block_start = jnp.arange(n_blocks, dtype=I32) * MOE_BLK
    block_exp = jnp.minimum(jnp.sum((pend[None, :] <= block_start[:, None]).astype(I32), axis=1), N_EXPERTS - 1)

    def dest_of(e, rk):
        return (pstart[e[:, :TOP_K]] + rk[:, :TOP_K]).astype(I32)

    dest_p = dest_of(e_p, rk_p).reshape(n_p // TOKEN_TM, 1, TOKEN_TM * TOP_K)
    dest_s = dest_of(e_s, rk_s).reshape(1, 1, n_s * TOP_K)
    xs = jnp.zeros((n_blocks * MOE_BLK * SUBLANES, LANES), F32)
    xs = _dispatch(dest_p, u2_p, xs, tm=TOKEN_TM)
    xs = _dispatch(dest_s, u2_s, xs, tm=n_s)
    ys = _ffn(block_exp, xs, wgu, bgu, wd, bd)
    y_p = _combine(dest_p, h1_p, gt_p, fg, ys, tm=TOKEN_TM)
    y_s = _combine(dest_s, h1_s, gt_s, fg, ys, tm=n_s)

    def kv_rows(proj, col, lo, hi):
        return proj[lo:hi, col:col + ATT_WIDTH].astype(F32)

    k_meta = kv_rows(proj_s, COL_K, n_dec, n_s).reshape(1, N_META, 2 * ATT_HEADS, ATT_HEAD_DIM)
    v_meta = kv_rows(proj_s, COL_V, n_dec, n_s).reshape(1, N_META, ATT_HEADS, ATT_V_DIM)
    k_fr = kv_rows(proj_p, COL_K, 0, n_p).reshape(batch, seq, 2 * ATT_HEADS, ATT_HEAD_DIM)
    v_fr = kv_rows(proj_p, COL_V, 0, n_p).reshape(batch, seq, ATT_HEADS, ATT_V_DIM)
    new_k_p = jnp.concatenate([jnp.broadcast_to(k_meta, (batch,) + k_meta.shape[1:]), k_fr], axis=1)[None]
    new_v_p = jnp.concatenate([jnp.broadcast_to(v_meta, (batch,) + v_meta.shape[1:]), v_fr], axis=1)[None]
    new_k_s = kv_rows(proj_s, COL_K, 0, n_dec).reshape(1, dbatch, dseq, 2 * ATT_HEADS, ATT_HEAD_DIM)
    new_v_s = kv_rows(proj_s, COL_V, 0, n_dec).reshape(1, dbatch, dseq, ATT_HEADS, ATT_V_DIM)
    return (y_p.reshape(batch, seq, D_MODEL),
            y_s[:n_dec].reshape(dbatch, dseq, D_MODEL),
            new_k_p, new_v_p,
            h_p.astype(state_ssm.dtype)[None],
            _conv_tail_rows(ct_p).astype(x_prompt.dtype)[None],
            new_k_s, new_v_s,
            h_d.astype(state_ssm.dtype)[None],
            _conv_tail_rows(ct_d).astype(x_sample.dtype)[None])
```

```python
import functools
import math

import jax
import jax.numpy as jnp
from jax import lax
from jax.experimental import pallas as pl
from jax.experimental.pallas import tpu as pltpu

F32 = jnp.float32
BF16 = jnp.bfloat16
I32 = jnp.int32

D_MODEL = 1024
D_INNER = 2048
SSD_HEADS = 32
SSD_HEAD_DIM = 64
SSD_GROUPS = 8
SSD_HPG = SSD_HEADS // SSD_GROUPS
SSD_STATE = 128
SSD_GN = SSD_GROUPS * SSD_STATE
CONV_W = 4
CONV_DIM = D_INNER + 2 * SSD_GN
ATT_HEADS = 8
ATT_HEAD_DIM = 64
ATT_V_DIM = 128
ATT_WIDTH = 1024
CHUNK = 64
N_META = 16
EPS = 1e-6
ROPE_THETA = 10000.0
N_EXPERTS = 32
TOP_K = 4
D_FF = 1024
SWIGLU_LIMIT = 7.0
SWIGLU_ALPHA = 1.702

COL_Z = 0
COL_X = COL_Z + D_INNER
COL_B = COL_X + D_INNER
COL_C = COL_B + SSD_GN
COL_Q = COL_C + SSD_GN
COL_K = COL_Q + ATT_WIDTH
COL_V = COL_K + ATT_WIDTH
COL_G = COL_V + ATT_WIDTH
N_MAIN = COL_G + 2 * D_MODEL

LANES = 128
SUBLANES = 8
GROUP_W = D_INNER // SSD_GROUPS
CONV_GW = GROUP_W + 2 * SSD_STATE
PROJ_TN = 1024
PROJ_TM = 1024
TOKEN_TM = 512
SSD_L = 256
ATT_TQ = 512
MOE_BLK = 256
DMA_UNROLL = 8
ONES_ROWS = 16
VMEM_LIMIT = 56 * 1024 * 1024
NEG_BIG = -1e30


def _cparams(sem):
    return pltpu.CompilerParams(dimension_semantics=sem, vmem_limit_bytes=VMEM_LIMIT)


def _sigmoid(x):
    return 1.0 / (1.0 + jnp.exp(-x))


def _inproj_kernel(x_ref, g_ref, w_ref, wdt_ref, cos_ref, sin_ref, o_ref, dt_ref, *rest, emit_vt):
    if emit_vt:
        vt_ref, u_scr = rest
    else:
        (u_scr,) = rest
    j = pl.program_id(1)

    @pl.when(j == 0)
    def _():
        x = x_ref[...]
        ms = jnp.mean(x * x, axis=-1, keepdims=True)
        u = (x * lax.rsqrt(ms + EPS) * g_ref[...]).astype(BF16)
        u_scr[...] = u
        dt_ref[...] = jnp.dot(u, wdt_ref[...], preferred_element_type=F32)

    acc = jnp.dot(u_scr[...], w_ref[...], preferred_element_type=F32)
    is_q = j == COL_Q // PROJ_TN
    is_k = j == COL_K // PROJ_TN

    @pl.when(is_q | is_k)
    def _():
        cos = cos_ref[...]
        sin = sin_ref[...]
        lane = lax.broadcasted_iota(I32, cos.shape, 1)
        half = ATT_HEAD_DIM // 2
        first_half = (lane % ATT_HEAD_DIM) < half
        scale = jnp.where(is_q, ATT_HEAD_DIM ** -0.5, 1.0).astype(F32)
        for c in range(PROJ_TN // LANES):
            a = acc[:, c * LANES:(c + 1) * LANES]
            swapped = jnp.where(first_half, pltpu.roll(a, LANES - half, 1), pltpu.roll(a, half, 1))
            o_ref[:, c * LANES:(c + 1) * LANES] = ((a * cos + swapped * sin) * scale).astype(o_ref.dtype)

    @pl.when(jnp.logical_not(is_q | is_k))
    def _():
        o_ref[...] = acc.astype(o_ref.dtype)

    if emit_vt:
        @pl.when(j == COL_V // PROJ_TN)
        def _():
            tk = vt_ref.shape[-1]
            for h in range(ATT_HEADS):
                for s in range(vt_ref.shape[1]):
                    blk = acc[s * tk:(s + 1) * tk, h * ATT_V_DIM:(h + 1) * ATT_V_DIM]
                    vt_ref[h, s] = blk.T.astype(vt_ref.dtype)


def _in_proj(x, g1, w_main, w_dt, cos_t, sin_t, tm, rope_blocks, vt_block=None):
    rows = x.shape[0]
    grid = (rows // tm, N_MAIN // PROJ_TN)
    out_specs = [
        pl.BlockSpec((tm, PROJ_TN), lambda i, j: (i, j)),
        pl.BlockSpec((tm, SSD_GROUPS * LANES), lambda i, j: (i, 0)),
    ]
    out_shape = [
        jax.ShapeDtypeStruct((rows, N_MAIN), BF16),
        jax.ShapeDtypeStruct((rows, SSD_GROUPS * LANES), F32),
    ]
    if vt_block is not None:
        per = tm // vt_block
        out_specs.append(pl.BlockSpec((None, ATT_HEADS, per, ATT_V_DIM, vt_block),
                                      lambda i, j: (i // rope_blocks, 0, i % rope_blocks, 0, 0)))
        out_shape.append(jax.ShapeDtypeStruct(
            (rows // (rope_blocks * tm), ATT_HEADS, rope_blocks * per, ATT_V_DIM, vt_block), BF16))
    return pl.pallas_call(
        functools.partial(_inproj_kernel, emit_vt=vt_block is not None),
        grid=grid,
        in_specs=[
            pl.BlockSpec((tm, D_MODEL), lambda i, j: (i, 0)),
            pl.BlockSpec((1, D_MODEL), lambda i, j: (0, 0)),
            pl.BlockSpec((D_MODEL, PROJ_TN), lambda i, j: (0, j)),
            pl.BlockSpec((D_MODEL, SSD_GROUPS * LANES), lambda i, j: (0, 0)),
            pl.BlockSpec((tm, LANES), lambda i, j: (i % rope_blocks, 0)),
            pl.BlockSpec((tm, LANES), lambda i, j: (i % rope_blocks, 0)),
        ],
        out_specs=out_specs,
        out_shape=out_shape,
        scratch_shapes=[pltpu.VMEM((tm, D_MODEL), BF16)],
        compiler_params=_cparams(("parallel", "arbitrary")),
        name="in_proj",
    )(x, g1, w_main, w_dt, cos_t, sin_t)


def _ssd_kernel(x_ref, b_ref, c_ref, z_ref, dt_ref, h0_ref, cp_ref, cw_ref, cb_ref, dtb_ref,
                aneg_ref, dsk_ref, ng_ref, y_ref, hf_ref, ct_ref, h_scr, f_scr):
    c = pl.program_id(2)
    L = x_ref.shape[0]

    @pl.when(c == 0)
    def _():
        h_scr[...] = h0_ref[...]
        f_scr[0:SUBLANES, :] = cp_ref[...]

    f_scr[SUBLANES:SUBLANES + L, 0:GROUP_W] = x_ref[...].astype(F32)
    f_scr[SUBLANES:SUBLANES + L, GROUP_W:GROUP_W + SSD_STATE] = b_ref[...].astype(F32)
    f_scr[SUBLANES:SUBLANES + L, GROUP_W + SSD_STATE:CONV_GW] = c_ref[...].astype(F32)
    w = cw_ref[...]
    acc = cb_ref[...]
    for i in range(CONV_W):
        lo = SUBLANES - (CONV_W - 1) + i
        acc = acc + w[i:i + 1, :] * f_scr[lo:lo + L, :]
    xc = acc * _sigmoid(acc)
    tail = f_scr[L:L + SUBLANES, :]
    f_scr[0:SUBLANES, :] = tail
    ct_ref[...] = tail

    xg = xc[:, 0:GROUP_W]
    bm = xc[:, GROUP_W:GROUP_W + SSD_STATE].astype(BF16)
    cm = xc[:, GROUP_W + SSD_STATE:CONV_GW].astype(BF16)

    dtr = dt_ref[...] + dtb_ref[...]
    dt = jnp.maximum(dtr, 0.0) + jnp.log(1.0 + jnp.exp(-jnp.abs(dtr)))
    da = dt * aneg_ref[...]
    ti = lax.broadcasted_iota(I32, (L, L), 0)
    si = lax.broadcasted_iota(I32, (L, L), 1)
    causal = si <= ti
    tril = causal.astype(F32)
    cum = jnp.dot(tril, da, preferred_element_type=F32, precision=lax.Precision.HIGHEST)
    sel = (lax.broadcasted_iota(I32, (SUBLANES, LANES), 0)
           == lax.broadcasted_iota(I32, (SUBLANES, LANES), 1)).astype(F32)
    cum_t = lax.dot_general(sel, cum, (((1,), (1,)), ((), ())), preferred_element_type=F32,
                            precision=lax.Precision.HIGHEST)

    cb = lax.dot_general(cm, bm, (((1,), (1,)), ((), ())), preferred_element_type=F32)
    dsk = dsk_ref[...]
    ys = []
    for r in range(SSD_HPG):
        col = cum[:, r:r + 1]
        row = cum_t[r:r + 1, :]
        dec = jnp.exp(jnp.where(causal, col - row, NEG_BIG))
        m = (cb * dec).astype(BF16)
        xh = xg[:, r * SSD_HEAD_DIM:(r + 1) * SSD_HEAD_DIM]
        xdt = xh * dt[:, r:r + 1]
        h_prev = h_scr[r]
        y = jnp.dot(m, xdt.astype(BF16), preferred_element_type=F32)
        y = y + jnp.exp(col) * lax.dot_general(cm, h_prev.astype(BF16), (((1,), (1,)), ((), ())),
                                               preferred_element_type=F32)
        y = y + dsk[:, r:r + 1] * xh
        ys.append(y)
        tot = cum[L - 1:L, r:r + 1]
        xw = (xdt * jnp.exp(tot - col)).astype(BF16)
        upd = lax.dot_general(xw, bm, (((0,), (0,)), ((), ())), preferred_element_type=F32)
        h_scr[r] = h_prev * jnp.exp(tot) + upd
    yg = jnp.concatenate(ys, axis=1)
    z = z_ref[...].astype(F32)
    yz = yg * (z * _sigmoid(z))
    ms = jnp.mean(yz * yz, axis=-1, keepdims=True)
    y_ref[...] = (yz * lax.rsqrt(ms + EPS) * ng_ref[...]).astype(y_ref.dtype)
    hf_ref[...] = h_scr[...]


def _ssd_long_kernel(x_ref, b_ref, c_ref, z_ref, dt_ref, h0_ref, cp_ref, cw_ref, cb_ref, dtb_ref,
                     aneg_ref, dsk_ref, ng_ref, tri_ref, shift_ref, y_ref, hf_ref, ct_ref, h_scr, f_scr):
    c = pl.program_id(2)
    L = x_ref.shape[0]
    reps = L // LANES

    @pl.when(c == 0)
    def _():
        h_scr[...] = h0_ref[...]
        f_scr[0:SUBLANES, :] = cp_ref[...]
        f_scr[SUBLANES:2 * SUBLANES, :] = jnp.zeros((SUBLANES, CONV_GW), F32)

    xb = jnp.concatenate([x_ref[...], b_ref[...], c_ref[...]], axis=1)
    xf = xb.astype(F32)
    w = cw_ref[...]
    acc = cb_ref[...] + w[CONV_W - 1:CONV_W, :] * xf
    for d in range(1, CONV_W):
        sh = jnp.dot(shift_ref[d - 1], xb, preferred_element_type=F32)
        acc = acc + w[CONV_W - 1 - d:CONV_W - d, :] * sh
    corr = jnp.zeros((SUBLANES, CONV_GW), F32)
    for i in range(CONV_W - 1):
        lo = SUBLANES - (CONV_W - 1) + i
        corr = corr + w[i:i + 1, :] * f_scr[lo:lo + SUBLANES, :]
    acc = jnp.concatenate([acc[0:SUBLANES] + corr, acc[SUBLANES:]], axis=0)
    xc = acc * _sigmoid(acc)
    tail = xf[L - SUBLANES:L, :]
    f_scr[0:SUBLANES, :] = tail
    ct_ref[...] = tail

    x_t = xc[:, 0:GROUP_W].T
    bm = xc[:, GROUP_W:GROUP_W + SSD_STATE].astype(BF16)
    cm = xc[:, GROUP_W + SSD_STATE:CONV_GW].astype(BF16)

    dtr = dt_ref[...].T[0:SUBLANES, :] + jnp.tile(dtb_ref[...], (1, reps))
    dt = jnp.maximum(dtr, 0.0) + jnp.log(1.0 + jnp.exp(-jnp.abs(dtr)))
    da = dt * jnp.tile(aneg_ref[...], (1, reps))
    tri = tri_ref[...]
    tri_b = tri.astype(BF16)
    visible = tri > 0.5
    d1 = da.astype(BF16)
    r1 = da - d1.astype(F32)
    d2 = r1.astype(BF16)
    d3 = (r1 - d2.astype(F32)).astype(BF16)
    cum = (jnp.dot(d1, tri_b, preferred_element_type=F32) + jnp.dot(d2, tri_b, preferred_element_type=F32)
           + jnp.dot(d3, tri_b, preferred_element_type=F32))
    cum_col = cum.T

    cb_t = lax.dot_general(bm, cm, (((1,), (1,)), ((), ())), preferred_element_type=F32)
    dsk = dsk_ref[...]
    ys = []
    for r in range(SSD_HPG):
        row = cum[r:r + 1, :]
        dec = jnp.exp(jnp.where(visible, row - cum_col[:, r:r + 1], NEG_BIG))
        m = (cb_t * dec).astype(BF16)
        xh = x_t[r * SSD_HEAD_DIM:(r + 1) * SSD_HEAD_DIM, :]
        xdt = xh * dt[r:r + 1, :]
        h_prev = h_scr[r]
        y = jnp.dot(xdt.astype(BF16), m, preferred_element_type=F32)
        y = y + jnp.exp(row) * lax.dot_general(h_prev.astype(BF16), cm, (((1,), (1,)), ((), ())),
                                               preferred_element_type=F32)
        y = y + dsk[r:r + 1, 0:1] * xh
        ys.append(y)
        tot = row[:, L - 1:L]
        xw = (xdt * jnp.exp(tot - row)).astype(BF16)
        h_scr[r] = h_prev * jnp.exp(tot) + jnp.dot(xw, bm, preferred_element_type=F32)
    yg = jnp.concatenate(ys, axis=0).T
    z = z_ref[...].astype(F32)
    yz = yg * (z * _sigmoid(z))
    ms = jnp.mean(yz * yz, axis=-1, keepdims=True)
    y_ref[...] = (yz * lax.rsqrt(ms + EPS) * ng_ref[...]).astype(y_ref.dtype)
    hf_ref[...] = h_scr[...]


def _per_group_lanes(v):
    out = jnp.zeros((SSD_GROUPS, 1, LANES), F32)
    return out.at[:, 0, :SSD_HPG].set(v.astype(F32).reshape(SSD_GROUPS, SSD_HPG))


def _per_group_rows(v):
    out = jnp.zeros((SSD_GROUPS, SUBLANES, LANES), F32)
    return out.at[:, :SSD_HPG, :].set(
        jnp.broadcast_to(v.astype(F32).reshape(SSD_GROUPS, SSD_HPG, 1), (SSD_GROUPS, SSD_HPG, LANES)))


def _ssd(proj, dt_raw, h0, conv_prev, cw_g, cb_g, dt_bias, a_neg, d_skip, norm_g, *, n_seq, t, l, row0,
         shared_state):
    nc = t // l
    rb0 = row0 // l
    long_chunks = l % LANES == 0

    def rows(s, g, c):
        return rb0 + s * nc + c

    def sidx(s):
        return 0 if shared_state else s

    if long_chunks:
        head_rows = SUBLANES
        head_par = [_per_group_rows(v) for v in (dt_bias, a_neg, d_skip)]
        step = jnp.arange(l, dtype=I32)
        tri = (step[:, None] <= step[None, :]).astype(F32)
        shift = jnp.stack([(step[None, :] == step[:, None] - d) for d in range(1, CONV_W)]).astype(BF16)
        extra_args = [tri, shift]
        extra_specs = [pl.BlockSpec((l, l), lambda s, g, c: (0, 0)),
                       pl.BlockSpec((CONV_W - 1, l, l), lambda s, g, c: (0, 0, 0))]
        body, conv_rows = _ssd_long_kernel, 2 * SUBLANES
    else:
        head_rows = 1
        head_par = [_per_group_lanes(v) for v in (dt_bias, a_neg, d_skip)]
        extra_args, extra_specs = [], []
        body, conv_rows = _ssd_kernel, l + SUBLANES
    head_spec = pl.BlockSpec((None, head_rows, LANES), lambda s, g, c: (g, 0, 0))

    return pl.pallas_call(
        body,
        grid=(n_seq, SSD_GROUPS, nc),
        in_specs=[
            pl.BlockSpec((l, GROUP_W), lambda s, g, c: (rows(s, g, c), COL_X // GROUP_W + g)),
            pl.BlockSpec((l, SSD_STATE), lambda s, g, c: (rows(s, g, c), COL_B // SSD_STATE + g)),
            pl.BlockSpec((l, SSD_STATE), lambda s, g, c: (rows(s, g, c), COL_C // SSD_STATE + g)),
            pl.BlockSpec((l, GROUP_W), lambda s, g, c: (rows(s, g, c), COL_Z // GROUP_W + g)),
            pl.BlockSpec((l, LANES), lambda s, g, c: (rows(s, g, c), g)),
            pl.BlockSpec((None, SSD_HPG, SSD_HEAD_DIM, SSD_STATE), lambda s, g, c: (sidx(s), g, 0, 0)),
            pl.BlockSpec((None, None, SUBLANES, CONV_GW), lambda s, g, c: (sidx(s), g, 0, 0)),
            pl.BlockSpec((None, CONV_W, CONV_GW), lambda s, g, c: (g, 0, 0)),
            pl.BlockSpec((None, 1, CONV_GW), lambda s, g, c: (g, 0, 0)),
            head_spec, head_spec, head_spec,
            pl.BlockSpec((1, GROUP_W), lambda s, g, c: (0, g)),
        ] + extra_specs,
        out_specs=[
            pl.BlockSpec((l, GROUP_W), lambda s, g, c: (s * nc + c, g)),
            pl.BlockSpec((None, SSD_HPG, SSD_HEAD_DIM, SSD_STATE), lambda s, g, c: (s, g, 0, 0)),
            pl.BlockSpec((None, None, SUBLANES, CONV_GW), lambda s, g, c: (s, g, 0, 0)),
        ],
        out_shape=[
            jax.ShapeDtypeStruct((n_seq * t, D_INNER), BF16),
            jax.ShapeDtypeStruct((n_seq, SSD_HEADS, SSD_HEAD_DIM, SSD_STATE), F32),
            jax.ShapeDtypeStruct((n_seq, SSD_GROUPS, SUBLANES, CONV_GW), F32),
        ],
        scratch_shapes=[
            pltpu.VMEM((SSD_HPG, SSD_HEAD_DIM, SSD_STATE), F32),
            pltpu.VMEM((conv_rows, CONV_GW), F32),
        ],
        compiler_params=_cparams(("parallel", "parallel", "arbitrary")),
        name="ssd_long" if long_chunks else "ssd",
    )(proj, proj, proj, proj, dt_raw, h0, conv_prev, cw_g, cb_g, *head_par, norm_g, *extra_args)


def _qk(q, k):
    return lax.dot_general(q, k, (((1,), (1,)), ((), ())), preferred_element_type=F32)


def _attn_prompt_kernel(lam_ref, q_ref, k_ref, vt_ref, mk_ref, mvt_ref, o_ref, sa_scr, sb_scr, acc_scr, m_scr):
    i = pl.program_id(2)
    tq = q_ref.shape[0]
    lam = lam_ref[0]
    q = q_ref[...]
    lane = lax.broadcasted_iota(I32, q.shape, 1)
    zero = jnp.zeros_like(q)
    qm = [jnp.where(lane < ATT_HEAD_DIM, q, zero), jnp.where(lane >= ATT_HEAD_DIM, q, zero)]

    def put_scores(s_ref, j):
        kblk = k_ref[pl.ds(pl.multiple_of(j * tq, tq), tq), :]
        for r in range(2):
            s_ref[r] = _qk(kblk, qm[r])

    def with_ones(vt):
        return jnp.concatenate([vt, jnp.ones((ONES_ROWS, vt.shape[1]), BF16)], axis=0)

    def update(s_ref, j, masked=False):
        vt1 = with_ones(vt_ref[j])
        for r in range(2):
            s = s_ref[r]
            if masked:
                kpos = lax.broadcasted_iota(I32, (tq, tq), 0) // CHUNK
                qpos = lax.broadcasted_iota(I32, (tq, tq), 1) // CHUNK
                s = jnp.where(kpos <= qpos, s, NEG_BIG)
            m_p = m_scr[r]
            m_n = jnp.maximum(m_p, jnp.max(s, axis=0, keepdims=True))
            alpha = jnp.exp(m_p - m_n)
            p = jnp.exp((s - m_n).astype(BF16))
            acc_scr[r] = alpha * acc_scr[r] + jnp.dot(vt1, p, preferred_element_type=F32)
            m_scr[r] = m_n

    mk = mk_ref[...]
    mvt1 = with_ones(mvt_ref[...])
    for r in range(2):
        s0 = _qk(mk, qm[r])
        m0 = jnp.max(s0, axis=0, keepdims=True)
        acc_scr[r] = jnp.dot(mvt1, jnp.exp((s0 - m0).astype(BF16)), preferred_element_type=F32)
        m_scr[r] = m0

    put_scores(sa_scr, 0)

    def body(jj, _):
        j = 2 * jj
        put_scores(sb_scr, j + 1)
        update(sa_scr, j)
        put_scores(sa_scr, j + 2)
        update(sb_scr, j + 1)
        return 0

    lax.fori_loop(0, i // 2, body, 0)

    @pl.when(i % 2 == 0)
    def _():
        update(sa_scr, i, masked=True)

    @pl.when(i % 2 == 1)
    def _():
        put_scores(sb_scr, i)
        update(sa_scr, i - 1)
        update(sb_scr, i, masked=True)

    outs = [acc_scr[r, :ATT_V_DIM, :] / acc_scr[r, ATT_V_DIM:ATT_V_DIM + 1, :] for r in range(2)]
    o_ref[...] = (outs[0] - lam * outs[1]).T.astype(o_ref.dtype)


def _attn_prompt(lam, proj_p, vt_p, proj_s, mvt, *, batch, seq, meta_row0):
    nq = seq // ATT_TQ
    return pl.pallas_call(
        _attn_prompt_kernel,
        grid=(batch, ATT_HEADS, nq),
        in_specs=[
            pl.BlockSpec(memory_space=pltpu.SMEM),
            pl.BlockSpec((ATT_TQ, LANES), lambda b, h, i: (b * nq + i, COL_Q // LANES + h)),
            pl.BlockSpec((seq, LANES), lambda b, h, i: (b, COL_K // LANES + h)),
            pl.BlockSpec((None, None, nq, ATT_V_DIM, ATT_TQ), lambda b, h, i: (b, h, 0, 0, 0)),
            pl.BlockSpec((N_META, LANES), lambda b, h, i: (meta_row0 // N_META, COL_K // LANES + h)),
            pl.BlockSpec((None, ATT_V_DIM, N_META), lambda b, h, i: (h, 0, 0)),
        ],
        out_specs=pl.BlockSpec((ATT_TQ, LANES), lambda b, h, i: (b * nq + i, h)),
        out_shape=jax.ShapeDtypeStruct((batch * seq, ATT_WIDTH), BF16),
        scratch_shapes=[pltpu.VMEM((2, ATT_TQ, ATT_TQ), F32),
                        pltpu.VMEM((2, ATT_TQ, ATT_TQ), F32),
                        pltpu.VMEM((2, ATT_V_DIM + ONES_ROWS, ATT_TQ), F32),
                        pltpu.VMEM((2, 1, ATT_TQ), F32)],
        compiler_params=_cparams(("parallel", "parallel", "arbitrary")),
        name="attn_prompt",
    )(lam, proj_p, proj_p, vt_p, proj_s, mvt)


def _attn_short_kernel(lam_ref, q_ref, kn_ref, vn_ref, *rest, has_cache):
    if has_cache:
        kc_ref, vc_ref, o_ref = rest
    else:
        (o_ref,) = rest
    t = q_ref.shape[0]
    nh = 2 * ATT_HEADS
    lam = lam_ref[0]
    q = q_ref[...].astype(F32)
    qb = jnp.broadcast_to(q[None], (nh, t, ATT_WIDTH)).reshape(nh * t, ATT_WIDTH)
    row_head = lax.broadcasted_iota(I32, (nh * t, ATT_WIDTH), 0) // t
    col_head = lax.broadcasted_iota(I32, (nh * t, ATT_WIDTH), 1) // ATT_HEAD_DIM
    qbd = jnp.where(row_head == col_head, qb, 0.0).astype(BF16)
    s_new = _qk(qbd, kn_ref[...])
    m = jnp.max(s_new, axis=-1, keepdims=True)
    if has_cache:
        s_old = _qk(qbd, kc_ref[...].astype(BF16))
        m = jnp.maximum(m, jnp.max(s_old, axis=-1, keepdims=True))
    p_new = jnp.exp(s_new - m)
    den = jnp.sum(p_new, axis=-1, keepdims=True)
    acc = jnp.dot(p_new.astype(BF16), vn_ref[...], preferred_element_type=F32)
    if has_cache:
        p_old = jnp.exp(s_old - m)
        den = den + jnp.sum(p_old, axis=-1, keepdims=True)
        acc = acc + jnp.dot(p_old.astype(BF16), vc_ref[...].astype(BF16), preferred_element_type=F32)
    acc = acc / den
    for h in range(ATT_HEADS):
        cols = slice(h * ATT_V_DIM, (h + 1) * ATT_V_DIM)
        a1 = acc[(2 * h) * t:(2 * h + 1) * t, cols]
        a2 = acc[(2 * h + 1) * t:(2 * h + 2) * t, cols]
        o_ref[:, cols] = (a1 - lam * a2).astype(o_ref.dtype)


def _attn_short(lam, proj_s, cache_k, cache_v, *, n_seq, t, row0):
    has_cache = cache_k is not None
    rb0 = row0 // t
    in_specs = [
        pl.BlockSpec(memory_space=pltpu.SMEM),
        pl.BlockSpec((t, ATT_WIDTH), lambda s: (rb0 + s, COL_Q // ATT_WIDTH)),
        pl.BlockSpec((t, ATT_WIDTH), lambda s: (rb0 + s, COL_K // ATT_WIDTH)),
        pl.BlockSpec((t, ATT_WIDTH), lambda s: (rb0 + s, COL_V // ATT_WIDTH)),
    ]
    args = [lam, proj_s, proj_s, proj_s]
    if has_cache:
        past = cache_k.shape[1]
        in_specs += [pl.BlockSpec((None, past, ATT_WIDTH), lambda s: (s, 0, 0)),
                     pl.BlockSpec((None, past, ATT_WIDTH), lambda s: (s, 0, 0))]
        args += [cache_k, cache_v]
    return pl.pallas_call(
        functools.partial(_attn_short_kernel, has_cache=has_cache),
        grid=(n_seq,),
        in_specs=in_specs,
        out_specs=pl.BlockSpec((t, ATT_WIDTH), lambda s: (s, 0)),
        out_shape=jax.ShapeDtypeStruct((n_seq * t, ATT_WIDTH), BF16),
        compiler_params=_cparams(("parallel",)),
        name="attn_cached" if has_cache else "attn_meta",
    )(*args)


def _merge_kernel(x_ref, yn_ref, o_ref, g1_ref, g2_ref, wso_ref, wao_ref, wo_ref, bg1_ref, bg2_ref,
                  sub_ref, n2_ref, wr_ref, br_ref, base_ref,
                  h1_ref, u2_ref, eidx_ref, gate_ref, rank_ref, cnt_ref, cnt_scr, *, sub_scale):
    i = pl.program_id(0)
    tm = x_ref.shape[0]

    @pl.when(i == 0)
    def _():
        cnt_scr[...] = base_ref[...]

    y_ssd = jnp.dot(yn_ref[...], wso_ref[...], preferred_element_type=F32)
    o = o_ref[...].astype(F32)
    parts = []
    for h in range(ATT_HEADS):
        oh = o[:, h * ATT_V_DIM:(h + 1) * ATT_V_DIM]
        ms = jnp.mean(oh * oh, axis=-1, keepdims=True)
        parts.append(oh * lax.rsqrt(ms + EPS) * sub_ref[...] * sub_scale)
    on = jnp.concatenate(parts, axis=1).astype(BF16)
    y_att = jnp.dot(on, wao_ref[...], preferred_element_type=F32)
    gs = _sigmoid(g1_ref[...].astype(F32) + bg1_ref[...])
    ga = _sigmoid(g2_ref[...].astype(F32) + bg2_ref[...])
    mix_in = (gs * y_ssd + ga * y_att).astype(BF16)
    h1 = x_ref[...] + jnp.dot(mix_in, wo_ref[...], preferred_element_type=F32)
    h1_ref[...] = h1
    ms = jnp.mean(h1 * h1, axis=-1, keepdims=True)
    u2 = h1 * lax.rsqrt(ms + EPS) * n2_ref[...]
    _store_rows8(u2_ref, u2)

    logits = jnp.dot(u2, wr_ref[...], preferred_element_type=F32, precision=lax.Precision.HIGHEST)
    logits = logits + br_ref[...]
    lane = lax.broadcasted_iota(I32, (tm, LANES), 1).astype(F32)
    work = jnp.where(lane < N_EXPERTS, logits, NEG_BIG)
    vals, idxs, hots = [], [], []
    for _ in range(TOP_K):
        mx = jnp.max(work, axis=-1, keepdims=True)
        ix = jnp.min(jnp.where(work == mx, lane, float(LANES)), axis=-1, keepdims=True)
        hot = lane == ix
        vals.append(mx)
        idxs.append(ix)
        hots.append(hot)
        work = jnp.where(hot, NEG_BIG, work)
    es = [jnp.exp(v - vals[0]) for v in vals]
    den = es[0] + es[1] + es[2] + es[3]
    hot_all = (hots[0] | hots[1] | hots[2] | hots[3])
    ti = lax.broadcasted_iota(I32, (tm, tm), 0)
    si = lax.broadcasted_iota(I32, (tm, tm), 1)
    strict = (si < ti).astype(BF16)
    prefix = jnp.dot(strict, hot_all.astype(BF16), preferred_element_type=F32) + cnt_scr[...]
    eidx = jnp.zeros((tm, LANES), F32)
    gate = jnp.zeros((tm, LANES), F32)
    rank = jnp.zeros((tm, LANES), F32)
    for k in range(TOP_K):
        rk = jnp.sum(jnp.where(hots[k], prefix, 0.0), axis=-1, keepdims=True)
        eidx = jnp.where(lane == k, idxs[k], eidx)
        gate = jnp.where(lane == k, es[k] / den, gate)
        rank = jnp.where(lane == k, rk, rank)
    eidx_ref[...] = eidx.astype(I32)
    gate_ref[...] = gate
    rank_ref[...] = rank.astype(I32)
    cnt_scr[...] = cnt_scr[...] + jnp.sum(hot_all.astype(F32), axis=0, keepdims=True)
    cnt_ref[...] = cnt_scr[...]


def _merge(x, yn, o, proj, wso, wao, wo, bg1, bg2, sub_g, n2_g, wr, br, base_cnt, *, tm, sub_scale):
    rows = x.shape[0]
    full = lambda shape: pl.BlockSpec(shape, lambda i: (0,) * len(shape))
    tok = lambda w: pl.BlockSpec((tm, w), lambda i: (i, 0))
    rows8 = pl.BlockSpec((tm * SUBLANES, LANES), lambda i: (i, 0))
    return pl.pallas_call(
        functools.partial(_merge_kernel, sub_scale=sub_scale),
        grid=(rows // tm,),
        in_specs=[
            tok(D_MODEL), tok(D_INNER), tok(ATT_WIDTH),
            pl.BlockSpec((tm, D_MODEL), lambda i: (i, COL_G // D_MODEL)),
            pl.BlockSpec((tm, D_MODEL), lambda i: (i, COL_G // D_MODEL + 1)),
            full((D_INNER, D_MODEL)), full((ATT_WIDTH, D_MODEL)), full((D_MODEL, D_MODEL)),
            full((1, D_MODEL)), full((1, D_MODEL)), full((1, ATT_V_DIM)), full((1, D_MODEL)),
            full((D_MODEL, LANES)), full((1, LANES)), full((1, LANES)),
        ],
        out_specs=[tok(D_MODEL), rows8, tok(LANES), tok(LANES), tok(LANES), full((1, LANES))],
        out_shape=[
            jax.ShapeDtypeStruct((rows, D_MODEL), F32),
            jax.ShapeDtypeStruct((rows * SUBLANES, LANES), F32),
            jax.ShapeDtypeStruct((rows, LANES), I32),
            jax.ShapeDtypeStruct((rows, LANES), F32),
            jax.ShapeDtypeStruct((rows, LANES), I32),
            jax.ShapeDtypeStruct((1, LANES), F32),
        ],
        scratch_shapes=[pltpu.VMEM((1, LANES), F32)],
        compiler_params=_cparams(("arbitrary",)),
        name="merge_router",
    )(x, yn, o, proj, proj, wso, wao, wo, bg1, bg2, sub_g, n2_g, wr, br, base_cnt)


def _store_rows8(ref, val):
    n = val.shape[0]
    for c in range(D_MODEL // LANES):
        ref[pl.ds(c, n, stride=SUBLANES), :] = val[:, c * LANES:(c + 1) * LANES]


def _load_rows8(ref, n):
    return jnp.concatenate([ref[pl.ds(c, n, stride=SUBLANES), :] for c in range(D_MODEL // LANES)], axis=1)


def _row_copy(src_ref, src_row, dst_ref, dst_row, sem):
    src = src_ref.at[pl.ds(pl.multiple_of(src_row * SUBLANES, SUBLANES), SUBLANES), :]
    dst = dst_ref.at[pl.ds(pl.multiple_of(dst_row * SUBLANES, SUBLANES), SUBLANES), :]
    return pltpu.make_async_copy(src, dst, sem)


def _dispatch_kernel(dest_ref, u_ref, xs_in_ref, xs_ref, sem):
    del xs_in_ref
    n = dest_ref.shape[1]

    def issue(p, _):
        _row_copy(u_ref, p // TOP_K, xs_ref, dest_ref[0, p], sem).start()
        return 0

    lax.fori_loop(0, n, issue, 0, unroll=DMA_UNROLL)
    for _ in range(TOP_K):
        pltpu.make_async_copy(u_ref, u_ref, sem).wait()


def _dispatch(dest, u2r, xs, *, tm):
    rows = u2r.shape[0] // SUBLANES
    return pl.pallas_call(
        _dispatch_kernel,
        grid=(rows // tm,),
        in_specs=[
            pl.BlockSpec((None, 1, tm * TOP_K), lambda i: (i, 0, 0), memory_space=pltpu.SMEM),
            pl.BlockSpec((tm * SUBLANES, LANES), lambda i: (i, 0)),
            pl.BlockSpec(memory_space=pl.ANY),
        ],
        out_specs=pl.BlockSpec(memory_space=pl.ANY),
        out_shape=jax.ShapeDtypeStruct(xs.shape, xs.dtype),
        scratch_shapes=[pltpu.SemaphoreType.DMA(())],
        input_output_aliases={2: 0},
        compiler_params=_cparams(("arbitrary",)),
        name="moe_dispatch",
    )(dest, u2r, xs)


def _ffn_kernel(be_ref, x_ref, wgu_ref, bgu_ref, wd_ref, bd_ref, y_ref):
    del be_ref
    x = _load_rows8(x_ref, MOE_BLK).astype(BF16)
    gu = jnp.dot(x, wgu_ref[...], preferred_element_type=F32) + bgu_ref[...]
    gate = jnp.minimum(gu[:, :D_FF], SWIGLU_LIMIT)
    up = jnp.clip(gu[:, D_FF:], -SWIGLU_LIMIT, SWIGLU_LIMIT)
    hdn = (up + 1.0) * gate * _sigmoid(SWIGLU_ALPHA * gate)
    y = jnp.dot(hdn.astype(BF16), wd_ref[...], preferred_element_type=F32) + bd_ref[...]
    _store_rows8(y_ref, y)


def _ffn(block_exp, xs, wgu, bgu, wd, bd):
    n_blocks = xs.shape[0] // (MOE_BLK * SUBLANES)
    return pl.pallas_call(
        _ffn_kernel,
        grid_spec=pltpu.PrefetchScalarGridSpec(
            num_scalar_prefetch=1,
            grid=(n_blocks,),
            in_specs=[
                pl.BlockSpec((MOE_BLK * SUBLANES, LANES), lambda i, be: (i, 0)),
                pl.BlockSpec((None, D_MODEL, 2 * D_FF), lambda i, be: (be[i], 0, 0)),
                pl.BlockSpec((None, 1, 2 * D_FF), lambda i, be: (be[i], 0, 0)),
                pl.BlockSpec((None, D_FF, D_MODEL), lambda i, be: (be[i], 0, 0)),
                pl.BlockSpec((None, 1, D_MODEL), lambda i, be: (be[i], 0, 0)),
            ],
            out_specs=pl.BlockSpec((MOE_BLK * SUBLANES, LANES), lambda i, be: (i, 0)),
        ),
        out_shape=jax.ShapeDtypeStruct(xs.shape, F32),
        compiler_params=_cparams(("arbitrary",)),
        name="moe_ffn",
    )(block_exp, xs, wgu, bgu, wd, bd)


def _combine_kernel(dest_ref, h1_ref, gate_ref, fg_ref, ys_ref, y_ref, buf, sem):
    tm = h1_ref.shape[0]
    n = dest_ref.shape[1]

    def issue(p, _):
        _row_copy(ys_ref, dest_ref[0, p], buf.at[p % TOP_K], p // TOP_K, sem).start()
        return 0

    lax.fori_loop(0, n, issue, 0, unroll=DMA_UNROLL)
    pltpu.make_async_copy(buf, buf, sem).wait()
    h = h1_ref[...]
    gate = gate_ref[...]
    for k in range(TOP_K):
        h = h + gate[:, k:k + 1] * _load_rows8(buf.at[k], tm)
    ms = jnp.mean(h * h, axis=-1, keepdims=True)
    y_ref[...] = h * lax.rsqrt(ms + EPS) * fg_ref[...]


def _combine(dest, h1, gate, fg, ys, *, tm):
    rows = h1.shape[0]
    return pl.pallas_call(
        _combine_kernel,
        grid=(rows // tm,),
        in_specs=[
            pl.BlockSpec((None, 1, tm * TOP_K), lambda i: (i, 0, 0), memory_space=pltpu.SMEM),
            pl.BlockSpec((tm, D_MODEL), lambda i: (i, 0)),
            pl.BlockSpec((tm, LANES), lambda i: (i, 0)),
            pl.BlockSpec((1, D_MODEL), lambda i: (0, 0)),
            pl.BlockSpec(memory_space=pl.ANY),
        ],
        out_specs=pl.BlockSpec((tm, D_MODEL), lambda i: (i, 0)),
        out_shape=jax.ShapeDtypeStruct((rows, D_MODEL), F32),
        scratch_shapes=[pltpu.VMEM((TOP_K, tm * SUBLANES, LANES), F32), pltpu.SemaphoreType.DMA(())],
        compiler_params=_cparams(("arbitrary",)),
        name="moe_combine",
    )(dest, h1, gate, fg, ys)


def _rope_tables(pos):
    d = ATT_HEAD_DIM
    inv = ROPE_THETA ** (-jnp.arange(0, d, 2, dtype=F32) / d)
    ang = pos.astype(F32)[:, None] * inv[None, :]
    cos = jnp.cos(ang)
    sin = jnp.sin(ang)
    cos_h = jnp.concatenate([cos, cos], axis=1)
    sin_h = jnp.concatenate([-sin, sin], axis=1)
    return jnp.tile(cos_h, (1, LANES // d)), jnp.tile(sin_h, (1, LANES // d))


def _conv_by_group(a):
    lead = a.shape[:-1]
    x = a[..., :D_INNER].reshape(lead + (SSD_GROUPS, GROUP_W))
    b = a[..., D_INNER:D_INNER + SSD_GN].reshape(lead + (SSD_GROUPS, SSD_STATE))
    c = a[..., D_INNER + SSD_GN:].reshape(lead + (SSD_GROUPS, SSD_STATE))
    return jnp.concatenate([x, b, c], axis=-1)


def _conv_from_group(a):
    lead = a.shape[:-2]
    x = a[..., :GROUP_W].reshape(lead + (D_INNER,))
    b = a[..., GROUP_W:GROUP_W + SSD_STATE].reshape(lead + (SSD_GN,))
    c = a[..., GROUP_W + SSD_STATE:].reshape(lead + (SSD_GN,))
    return jnp.concatenate([x, b, c], axis=-1)


def _conv_prev_blocks(prev):
    g = jnp.moveaxis(_conv_by_group(prev.astype(F32)), 1, 2)
    return jnp.pad(g, ((0, 0), (0, 0), (SUBLANES - (CONV_W - 1), 0), (0, 0)))


def _conv_tail_rows(ct):
    return _conv_from_group(jnp.moveaxis(ct[:, :, SUBLANES - (CONV_W - 1):, :], 1, 2))


def kernel(x_prompt, x_sample, cache_k, cache_v, state_ssm, state_conv, meta_tokens, norm1_g, w_in, conv_w, conv_b, dt_bias, a_log, d_skip, ssd_norm_g, lambda_q1, lambda_k1, lambda_q2, lambda_k2, subln_g, w_ssd_out, w_att_out, b_gate, w_o, norm2_g, w_router, b_router, w_gu, b_gu, w_down, b_down, final_norm_g):
    batch, seq, _ = x_prompt.shape
    dbatch, dseq, _ = x_sample.shape
    past = cache_k.shape[2]
    depth = norm1_g.shape[0]
    assert depth == 1 and dseq == N_META
    assert seq % SSD_L == 0 and seq % ATT_TQ == 0 and seq % TOKEN_TM == 0 and ATT_TQ % CHUNK == 0
    assert TOKEN_TM % ATT_TQ == 0 and PROJ_TM % ATT_TQ == 0
    n_p = batch * seq
    n_dec = dbatch * dseq
    n_s = n_dec + N_META
    lam_init = 0.8 - 0.6 * math.exp(-0.3 * 0)
    l = 0

    wi = w_in[l]
    o_z, o_xbc, o_dt = 0, D_INNER, D_INNER + CONV_DIM
    o_q = o_dt + SSD_HEADS
    w_main = jnp.concatenate([wi[:, o_z:o_xbc], wi[:, o_xbc:o_dt], wi[:, o_q:]], axis=1).astype(BF16)
    w_dt = jnp.zeros((D_MODEL, SSD_GROUPS, LANES), F32).at[:, :, :SSD_HPG].set(
        wi[:, o_dt:o_q].reshape(D_MODEL, SSD_GROUPS, SSD_HPG)).reshape(D_MODEL, SSD_GROUPS * LANES).astype(BF16)
    g1 = norm1_g[l].reshape(1, D_MODEL)
    cw_g = jnp.moveaxis(_conv_by_group(conv_w[l]), 0, 1)
    cb_g = _conv_by_group(conv_b[l])[:, None, :]
    a_neg = -jnp.exp(a_log[l].astype(F32))
    norm_g = ssd_norm_g[l].reshape(1, D_INNER)
    lam = (jnp.exp(jnp.sum(lambda_q1[l].astype(F32) * lambda_k1[l].astype(F32)))
           - jnp.exp(jnp.sum(lambda_q2[l].astype(F32) * lambda_k2[l].astype(F32))) + lam_init).reshape(1)
    wso = w_ssd_out[l].astype(BF16)
    wao = w_att_out[l].astype(BF16)
    wo = w_o[l].astype(BF16)
    bg1 = b_gate[l][:D_MODEL].reshape(1, D_MODEL)
    bg2 = b_gate[l][D_MODEL:].reshape(1, D_MODEL)
    sub_g = subln_g[l].reshape(1, ATT_V_DIM)
    n2_g = norm2_g[l].reshape(1, D_MODEL)
    wr = jnp.zeros((D_MODEL, LANES), F32).at[:, :N_EXPERTS].set(w_router[l])
    br = jnp.zeros((1, LANES), F32).at[0, :N_EXPERTS].set(b_router[l])
    wgu = w_gu[l].astype(BF16)
    bgu = b_gu[l][:, None, :]
    wd = w_down[l].astype(BF16)
    bd = b_down[l][:, None, :]
    fg = final_norm_g.reshape(1, D_MODEL)

    xp = x_prompt.reshape(n_p, D_MODEL)
    xs_rows = jnp.concatenate([x_sample.reshape(n_dec, D_MODEL), meta_tokens.astype(x_prompt.dtype)], axis=0)
    cos_p, sin_p = _rope_tables(N_META + jnp.arange(seq, dtype=I32))
    pos_s = jnp.concatenate([jnp.tile(past + jnp.arange(dseq, dtype=I32), dbatch), jnp.arange(N_META, dtype=I32)])
    cos_s, sin_s = _rope_tables(pos_s)

    tm_p = PROJ_TM if seq % PROJ_TM == 0 else TOKEN_TM
    proj_p, dt_p, vt_p = _in_proj(xp, g1, w_main, w_dt, cos_p, sin_p, tm_p, seq // tm_p, vt_block=ATT_TQ)
    proj_s, dt_s = _in_proj(xs_rows, g1, w_main, w_dt, cos_s, sin_s, n_s, 1)

    ssd_args = (cw_g, cb_g, dt_bias[l], a_neg, d_skip[l], norm_g)
    zero_h = jnp.zeros((1, SSD_HEADS, SSD_HEAD_DIM, SSD_STATE), F32)
    zero_c = jnp.zeros((1, SSD_GROUPS, SUBLANES, CONV_GW), F32)
    yn_m, h_m, ct_m = _ssd(proj_s, dt_s, zero_h, zero_c, *ssd_args, n_seq=1, t=N_META, l=N_META,
                           row0=n_dec, shared_state=True)
    yn_p, h_p, ct_p = _ssd(proj_p, dt_p, h_m, ct_m, *ssd_args, n_seq=batch, t=seq, l=SSD_L, row0=0,
                           shared_state=True)
    yn_d, h_d, ct_d = _ssd(proj_s, dt_s, state_ssm[l].astype(F32), _conv_prev_blocks(state_conv[l]), *ssd_args,
                           n_seq=dbatch, t=dseq, l=dseq, row0=0, shared_state=False)
    yn_s = jnp.concatenate([yn_d, yn_m], axis=0)

    mvt =proj_s[n_dec:n_s, COL_V:COL_V + ATT_WIDTH].reshape(N_META, ATT_HEADS, ATT_V_DIM).transpose(1, 2, 0)
    o_p = _attn_prompt(lam, proj_p, vt_p, proj_s, mvt, batch=batch, seq=seq, meta_row0=n_dec)
    o_d = _attn_short(lam, proj_s, cache_k[l].reshape(dbatch, past, ATT_WIDTH),
                      cache_v[l].reshape(dbatch, past, ATT_WIDTH), n_seq=dbatch, t=dseq, row0=0)
    o_m = _attn_short(lam, proj_s, None, None, n_seq=1, t=N_META, row0=n_dec)
    o_s = jnp.concatenate([o_d, o_m], axis=0)

    merge_w = (wso, wao, wo, bg1, bg2, sub_g, n2_g, wr, br)
    zero_cnt = jnp.zeros((1, LANES), F32)
    h1_p, u2_p, e_p, gt_p, rk_p, cnt_p = _merge(xp, yn_p, o_p, proj_p, *merge_w, zero_cnt, tm=TOKEN_TM,
                                                sub_scale=1.0 - lam_init)
    h1_s, u2_s, e_s, gt_s, rk_s, cnt = _merge(xs_rows, yn_s, o_s, proj_s, *merge_w, cnt_p, tm=n_s,
                                              sub_scale=1.0 - lam_init)

    counts = cnt[0, :N_EXPERTS].astype(I32)
    padded = (counts + MOE_BLK - 1) // MOE_BLK * MOE_BLK
    pend = jnp.cumsum(padded)
    pstart = pend - padded
    n_blocks = -(-((n_p + n_s) * TOP_K) // MOE_BLK) + N_EXPERTS
    block_start = jnp.arange(n_blocks, dtype=I32) * MOE_BLK
    block_exp = jnp.minimum(jnp.sum((pend[None, :] <= block_start[:, None]).astype(I32), axis=1), N_EXPERTS - 1)

    def dest_of(e, rk):
        hot = e[:, :TOP_K, None] == jnp.arange(N_EXPERTS, dtype=I32)
        return (jnp.sum(jnp.where(hot, pstart, 0), axis=-1) + rk[:, :TOP_K]).astype(I32)

    dest_p = dest_of(e_p, rk_p).reshape(n_p // TOKEN_TM, 1, TOKEN_TM * TOP_K)
    dest_s = dest_of(e_s, rk_s).reshape(1, 1, n_s * TOP_K)
    xs = jnp.zeros((n_blocks * MOE_BLK * SUBLANES, LANES), F32)
    xs = _dispatch(dest_p, u2_p, xs, tm=TOKEN_TM)
    xs = _dispatch(dest_s, u2_s, xs, tm=n_s)
    ys = _ffn(block_exp, xs, wgu, bgu, wd, bd)
    y_p = _combine(dest_p, h1_p, gt_p, fg, ys, tm=TOKEN_TM)
    y_s = _combine(dest_s, h1_s, gt_s, fg, ys, tm=n_s)

    def kv_rows(proj, col, lo, hi):
        return proj[lo:hi, col:col + ATT_WIDTH].astype(F32)

    k_meta = kv_rows(proj_s, COL_K, n_dec, n_s).reshape(1, N_META, 2 * ATT_HEADS, ATT_HEAD_DIM)
    v_meta = kv_rows(proj_s, COL_V, n_dec, n_s).reshape(1, N_META, ATT_HEADS, ATT_V_DIM)
    k_fr = kv_rows(proj_p, COL_K, 0, n_p).reshape(batch, seq, 2 * ATT_HEADS, ATT_HEAD_DIM)
    v_fr = kv_rows(proj_p, COL_V, 0, n_p).reshape(batch, seq, ATT_HEADS, ATT_V_DIM)
    new_k_p = jnp.concatenate([jnp.broadcast_to(k_meta, (batch,) + k_meta.shape[1:]), k_fr], axis=1)[None]
    new_v_p = jnp.concatenate([jnp.broadcast_to(v_meta, (batch,) + v_meta.shape[1:]), v_fr], axis=1)[None]
    new_k_s = kv_rows(proj_s, COL_K, 0, n_dec).reshape(1, dbatch, dseq, 2 * ATT_HEADS, ATT_HEAD_DIM)
    new_v_s = kv_rows(proj_s, COL_V, 0, n_dec).reshape(1, dbatch, dseq, ATT_HEADS, ATT_V_DIM)
    return (y_p.reshape(batch, seq, D_MODEL),
            y_s[:n_dec].reshape(dbatch, dseq, D_MODEL),
            new_k_p, new_v_p,
            h_p.astype(state_ssm.dtype)[None],
            _conv_tail_rows(ct_p).astype(x_prompt.dtype)[None],
            new_k_s, new_v_s,
            h_d.astype(state_ssm.dtype)[None],
            _conv_tail_rows(ct_d).astype(x_sample.dtype)[None])
```

```python
import functools
import math

import jax
import jax.numpy as jnp
from jax import lax
from jax.experimental import pallas as pl
from jax.experimental.pallas import tpu as pltpu

F32 = jnp.float32
BF16 = jnp.bfloat16
I32 = jnp.int32

D_MODEL = 1024
D_INNER = 2048
SSD_HEADS = 32
SSD_HEAD_DIM = 64
SSD_GROUPS = 8
SSD_HPG = SSD_HEADS // SSD_GROUPS
SSD_STATE = 128
SSD_GN = SSD_GROUPS * SSD_STATE
CONV_W = 4
CONV_DIM = D_INNER + 2 * SSD_GN
ATT_HEADS = 8
ATT_HEAD_DIM = 64
ATT_V_DIM = 128
ATT_WIDTH = 1024
CHUNK = 64
N_META = 16
EPS = 1e-6
ROPE_THETA = 10000.0
N_EXPERTS = 32
TOP_K = 4
D_FF = 1024
SWIGLU_LIMIT = 7.0
SWIGLU_ALPHA = 1.702

COL_Z = 0
COL_X = COL_Z + D_INNER
COL_B = COL_X + D_INNER
COL_C = COL_B + SSD_GN
COL_Q = COL_C + SSD_GN
COL_K = COL_Q + ATT_WIDTH
COL_V = COL_K + ATT_WIDTH
COL_G = COL_V + ATT_WIDTH
N_MAIN = COL_G + 2 * D_MODEL

LANES = 128
SUBLANES = 8
GROUP_W = D_INNER // SSD_GROUPS
CONV_GW = GROUP_W + 2 * SSD_STATE
PROJ_TN = 1024
PROJ_TM = 1024
TOKEN_TM = 512
SSD_L = 256
ATT_TQ = 512
MOE_BLK = 256
SEG_PAD = 16
ONES_ROWS = 16
VMEM_LIMIT = 56 * 1024 * 1024
NEG_BIG = -1e30


def _cparams(sem):
    return pltpu.CompilerParams(dimension_semantics=sem, vmem_limit_bytes=VMEM_LIMIT)


def _sigmoid(x):
    return 1.0 / (1.0 + jnp.exp(-x))


def _inproj_kernel(x_ref, g_ref, w_ref, wdt_ref, cos_ref, sin_ref, o_ref, dt_ref, *rest, emit_vt):
    if emit_vt:
        vt_ref, u_scr = rest
    else:
        (u_scr,) = rest
    j = pl.program_id(1)

    @pl.when(j == 0)
    def _():
        x = x_ref[...]
        ms = jnp.mean(x * x, axis=-1, keepdims=True)
        u = (x * lax.rsqrt(ms + EPS) * g_ref[...]).astype(BF16)
        u_scr[...] = u
        dt_ref[...] = jnp.dot(u, wdt_ref[...], preferred_element_type=F32)

    acc = jnp.dot(u_scr[...], w_ref[...], preferred_element_type=F32)
    is_q = j == COL_Q // PROJ_TN
    is_k = j == COL_K // PROJ_TN

    @pl.when(is_q | is_k)
    def _():
        cos = cos_ref[...]
        sin = sin_ref[...]
        lane = lax.broadcasted_iota(I32, cos.shape, 1)
        half = ATT_HEAD_DIM // 2
        first_half = (lane % ATT_HEAD_DIM) < half
        scale = jnp.where(is_q, ATT_HEAD_DIM ** -0.5, 1.0).astype(F32)
        for c in range(PROJ_TN // LANES):
            a = acc[:, c * LANES:(c + 1) * LANES]
            swapped = jnp.where(first_half, pltpu.roll(a, LANES - half, 1), pltpu.roll(a, half, 1))
            o_ref[:, c * LANES:(c + 1) * LANES] = ((a * cos + swapped * sin) * scale).astype(o_ref.dtype)

    @pl.when(jnp.logical_not(is_q | is_k))
    def _():
        o_ref[...] = acc.astype(o_ref.dtype)

    if emit_vt:
        @pl.when(j == COL_V // PROJ_TN)
        def _():
            tk = vt_ref.shape[-1]
            for h in range(ATT_HEADS):
                for s in range(vt_ref.shape[1]):
                    blk = acc[s * tk:(s + 1) * tk, h * ATT_V_DIM:(h + 1) * ATT_V_DIM]
                    vt_ref[h, s] = blk.T.astype(vt_ref.dtype)


def _in_proj(x, g1, w_main, w_dt, cos_t, sin_t, tm, rope_blocks, vt_block=None):
    rows = x.shape[0]
    grid = (rows // tm, N_MAIN // PROJ_TN)
    out_specs = [
        pl.BlockSpec((tm, PROJ_TN), lambda i, j: (i, j)),
        pl.BlockSpec((tm, SSD_GROUPS * LANES), lambda i, j: (i, 0)),
    ]
    out_shape = [
        jax.ShapeDtypeStruct((rows, N_MAIN), BF16),
        jax.ShapeDtypeStruct((rows, SSD_GROUPS * LANES), F32),
    ]
    if vt_block is not None:
        per = tm // vt_block
        out_specs.append(pl.BlockSpec((None, ATT_HEADS, per, ATT_V_DIM, vt_block),
                                      lambda i, j: (i // rope_blocks, 0, i % rope_blocks, 0, 0)))
        out_shape.append(jax.ShapeDtypeStruct(
            (rows // (rope_blocks * tm), ATT_HEADS, rope_blocks * per, ATT_V_DIM, vt_block), BF16))
    return pl.pallas_call(
        functools.partial(_inproj_kernel, emit_vt=vt_block is not None),
        grid=grid,
        in_specs=[
            pl.BlockSpec((tm, D_MODEL), lambda i, j: (i, 0)),
            pl.BlockSpec((1, D_MODEL), lambda i, j: (0, 0)),
            pl.BlockSpec((D_MODEL, PROJ_TN), lambda i, j: (0, j)),
            pl.BlockSpec((D_MODEL, SSD_GROUPS * LANES), lambda i, j: (0, 0)),
            pl.BlockSpec((tm, LANES), lambda i, j: (i % rope_blocks, 0)),
            pl.BlockSpec((tm, LANES), lambda i, j: (i % rope_blocks, 0)),
        ],
        out_specs=out_specs,
        out_shape=out_shape,
        scratch_shapes=[pltpu.VMEM((tm, D_MODEL), BF16)],
        compiler_params=_cparams(("parallel", "arbitrary")),
        name="in_proj",
    )(x, g1, w_main, w_dt, cos_t, sin_t)


def _ssd_kernel(x_ref, b_ref, c_ref, z_ref, dt_ref, h0_ref, cp_ref, cw_ref, cb_ref, dtb_ref,
                aneg_ref, dsk_ref, ng_ref, y_ref, hf_ref, ct_ref, h_scr, f_scr):
    c = pl.program_id(2)
    L = x_ref.shape[0]

    @pl.when(c == 0)
    def _():
        h_scr[...] = h0_ref[...]
        f_scr[0:SUBLANES, :] = cp_ref[...]

    f_scr[SUBLANES:SUBLANES + L, 0:GROUP_W] = x_ref[...].astype(F32)
    f_scr[SUBLANES:SUBLANES + L, GROUP_W:GROUP_W + SSD_STATE] = b_ref[...].astype(F32)
    f_scr[SUBLANES:SUBLANES + L, GROUP_W + SSD_STATE:CONV_GW] = c_ref[...].astype(F32)
    w = cw_ref[...]
    acc = cb_ref[...]
    for i in range(CONV_W):
        lo = SUBLANES - (CONV_W - 1) + i
        acc = acc + w[i:i + 1, :] * f_scr[lo:lo + L, :]
    xc = acc * _sigmoid(acc)
    tail = f_scr[L:L + SUBLANES, :]
    f_scr[0:SUBLANES, :] = tail
    ct_ref[...] = tail

    xg = xc[:, 0:GROUP_W]
    bm = xc[:, GROUP_W:GROUP_W + SSD_STATE].astype(BF16)
    cm = xc[:, GROUP_W + SSD_STATE:CONV_GW].astype(BF16)

    dtr = dt_ref[...] + dtb_ref[...]
    dt = jnp.maximum(dtr, 0.0) + jnp.log(1.0 + jnp.exp(-jnp.abs(dtr)))
    da = dt * aneg_ref[...]
    ti = lax.broadcasted_iota(I32, (L, L), 0)
    si = lax.broadcasted_iota(I32, (L, L), 1)
    causal = si <= ti
    tril = causal.astype(F32)
    cum = jnp.dot(tril, da, preferred_element_type=F32, precision=lax.Precision.HIGHEST)
    sel = (lax.broadcasted_iota(I32, (SUBLANES, LANES), 0)
           == lax.broadcasted_iota(I32, (SUBLANES, LANES), 1)).astype(F32)
    cum_t = lax.dot_general(sel, cum, (((1,), (1,)), ((), ())), preferred_element_type=F32,
                            precision=lax.Precision.HIGHEST)

    cb = lax.dot_general(cm, bm, (((1,), (1,)), ((), ())), preferred_element_type=F32)
    dsk = dsk_ref[...]
    ys = []
    for r in range(SSD_HPG):
        col = cum[:, r:r + 1]
        row = cum_t[r:r + 1, :]
        dec = jnp.exp(jnp.where(causal, col - row, NEG_BIG))
        m = (cb * dec).astype(BF16)
        xh = xg[:, r * SSD_HEAD_DIM:(r + 1) * SSD_HEAD_DIM]
        xdt = xh * dt[:, r:r + 1]
        h_prev = h_scr[r]
        y = jnp.dot(m, xdt.astype(BF16), preferred_element_type=F32)
        y = y + jnp.exp(col) * lax.dot_general(cm, h_prev.astype(BF16), (((1,), (1,)), ((), ())),
                                               preferred_element_type=F32)
        y = y + dsk[:, r:r + 1] * xh
        ys.append(y)
        tot = cum[L - 1:L, r:r + 1]
        xw = (xdt * jnp.exp(tot - col)).astype(BF16)
        upd = lax.dot_general(xw, bm, (((0,), (0,)), ((), ())), preferred_element_type=F32)
        h_scr[r] = h_prev * jnp.exp(tot) + upd
    yg = jnp.concatenate(ys, axis=1)
    z = z_ref[...].astype(F32)
    yz = yg * (z * _sigmoid(z))
    ms = jnp.mean(yz * yz, axis=-1, keepdims=True)
    y_ref[...] = (yz * lax.rsqrt(ms + EPS) * ng_ref[...]).astype(y_ref.dtype)
    hf_ref[...] = h_scr[...]


def _ssd_long_kernel(x_ref, b_ref, c_ref, z_ref, dt_ref, h0_ref, cp_ref, cw_ref, cb_ref, dtb_ref,
                     aneg_ref, dsk_ref, ng_ref, tri_ref, shift_ref, y_ref, hf_ref, ct_ref, h_scr, f_scr):
    c = pl.program_id(2)
    L = x_ref.shape[0]
    reps = L // LANES

    @pl.when(c == 0)
    def _():
        h_scr[...] = h0_ref[...]
        f_scr[0:SUBLANES, :] = cp_ref[...]
        f_scr[SUBLANES:2 * SUBLANES, :] = jnp.zeros((SUBLANES, CONV_GW), F32)

    xb = jnp.concatenate([x_ref[...], b_ref[...], c_ref[...]], axis=1)
    xf = xb.astype(F32)
    w = cw_ref[...]
    acc = cb_ref[...] + w[CONV_W - 1:CONV_W, :] * xf
    for d in range(1, CONV_W):
        sh = jnp.dot(shift_ref[d - 1], xb, preferred_element_type=F32)
        acc = acc + w[CONV_W - 1 - d:CONV_W - d, :] * sh
    corr = jnp.zeros((SUBLANES, CONV_GW), F32)
    for i in range(CONV_W - 1):
        lo = SUBLANES - (CONV_W - 1) + i
        corr = corr + w[i:i + 1, :] * f_scr[lo:lo + SUBLANES, :]
    acc = jnp.concatenate([acc[0:SUBLANES] + corr, acc[SUBLANES:]], axis=0)
    xc = acc * _sigmoid(acc)
    tail = xf[L - SUBLANES:L, :]
    f_scr[0:SUBLANES, :] = tail
    ct_ref[...] = tail

    x_t = xc[:, 0:GROUP_W].T
    bm = xc[:, GROUP_W:GROUP_W + SSD_STATE].astype(BF16)
    cm = xc[:, GROUP_W + SSD_STATE:CONV_GW].astype(BF16)

    dtr = dt_ref[...].T[0:SUBLANES, :] + jnp.tile(dtb_ref[...], (1, reps))
    dt = jnp.maximum(dtr, 0.0) + jnp.log(1.0 + jnp.exp(-jnp.abs(dtr)))
    da = dt * jnp.tile(aneg_ref[...], (1, reps))
    tri = tri_ref[...]
    tri_b = tri.astype(BF16)
    visible = tri > 0.5
    d1 = da.astype(BF16)
    r1 = da - d1.astype(F32)
    d2 = r1.astype(BF16)
    d3 = (r1 - d2.astype(F32)).astype(BF16)
    cum = (jnp.dot(d1, tri_b, preferred_element_type=F32) + jnp.dot(d2, tri_b, preferred_element_type=F32)
           + jnp.dot(d3, tri_b, preferred_element_type=F32))
    cum_col = cum.T

    cb_t = lax.dot_general(bm, cm, (((1,), (1,)), ((), ())), preferred_element_type=F32)
    dsk = dsk_ref[...]
    ys = []
    for r in range(SSD_HPG):
        row = cum[r:r + 1, :]
        dec = jnp.exp(jnp.where(visible, row - cum_col[:, r:r + 1], NEG_BIG))
        m = (cb_t * dec).astype(BF16)
        xh = x_t[r * SSD_HEAD_DIM:(r + 1) * SSD_HEAD_DIM, :]
        xdt = xh * dt[r:r + 1, :]
        h_prev = h_scr[r]
        y = jnp.dot(xdt.astype(BF16), m, preferred_element_type=F32)
        y = y + jnp.exp(row) * lax.dot_general(h_prev.astype(BF16), cm, (((1,), (1,)), ((), ())),
                                               preferred_element_type=F32)
        y = y + dsk[r:r + 1, 0:1] * xh
        ys.append(y)
        tot = row[:, L - 1:L]
        xw = (xdt * jnp.exp(tot - row)).astype(BF16)
        h_scr[r] = h_prev * jnp.exp(tot) + jnp.dot(xw, bm, preferred_element_type=F32)
    yg = jnp.concatenate(ys, axis=0).T
    z = z_ref[...].astype(F32)
    yz = yg * (z * _sigmoid(z))
    ms = jnp.mean(yz * yz, axis=-1, keepdims=True)
    y_ref[...] = (yz * lax.rsqrt(ms + EPS) * ng_ref[...]).astype(y_ref.dtype)
    hf_ref[...] = h_scr[...]


def _per_group_lanes(v):
    out = jnp.zeros((SSD_GROUPS, 1, LANES), F32)
    return out.at[:, 0, :SSD_HPG].set(v.astype(F32).reshape(SSD_GROUPS, SSD_HPG))


def _per_group_rows(v):
    out = jnp.zeros((SSD_GROUPS, SUBLANES, LANES), F32)
    return out.at[:, :SSD_HPG, :].set(
        jnp.broadcast_to(v.astype(F32).reshape(SSD_GROUPS, SSD_HPG, 1), (SSD_GROUPS, SSD_HPG, LANES)))


def _ssd(proj, dt_raw, h0, conv_prev, cw_g, cb_g, dt_bias, a_neg, d_skip, norm_g, *, n_seq, t, l, row0,
         shared_state):
    nc = t // l
    rb0 = row0 // l
    long_chunks = l % LANES == 0

    def rows(s, g, c):
        return rb0 + s * nc + c

    def sidx(s):
        return 0 if shared_state else s

    if long_chunks:
        head_rows = SUBLANES
        head_par = [_per_group_rows(v) for v in (dt_bias, a_neg, d_skip)]
        step = jnp.arange(l, dtype=I32)
        tri = (step[:, None] <= step[None, :]).astype(F32)
        shift = jnp.stack([(step[None, :] == step[:, None] - d) for d in range(1, CONV_W)]).astype(BF16)
        extra_args = [tri, shift]
        extra_specs = [pl.BlockSpec((l, l), lambda s, g, c: (0, 0)),
                       pl.BlockSpec((CONV_W - 1, l, l), lambda s, g, c: (0, 0, 0))]
        body, conv_rows = _ssd_long_kernel, 2 * SUBLANES
    else:
        head_rows = 1
        head_par = [_per_group_lanes(v) for v in (dt_bias, a_neg, d_skip)]
        extra_args, extra_specs = [], []
        body, conv_rows = _ssd_kernel, l + SUBLANES
    head_spec = pl.BlockSpec((None, head_rows, LANES), lambda s, g, c: (g, 0, 0))

    return pl.pallas_call(
        body,
        grid=(n_seq, SSD_GROUPS, nc),
        in_specs=[
            pl.BlockSpec((l, GROUP_W), lambda s, g, c: (rows(s, g, c), COL_X // GROUP_W + g)),
            pl.BlockSpec((l, SSD_STATE), lambda s, g, c: (rows(s, g, c), COL_B // SSD_STATE + g)),
            pl.BlockSpec((l, SSD_STATE), lambda s, g, c: (rows(s, g, c), COL_C // SSD_STATE + g)),
            pl.BlockSpec((l, GROUP_W), lambda s, g, c: (rows(s, g, c), COL_Z // GROUP_W + g)),
            pl.BlockSpec((l, LANES), lambda s, g, c: (rows(s, g, c), g)),
            pl.BlockSpec((None, SSD_HPG, SSD_HEAD_DIM, SSD_STATE), lambda s, g, c: (sidx(s), g, 0, 0)),
            pl.BlockSpec((None, None, SUBLANES, CONV_GW), lambda s, g, c: (sidx(s), g, 0, 0)),
            pl.BlockSpec((None, CONV_W, CONV_GW), lambda s, g, c: (g, 0, 0)),
            pl.BlockSpec((None, 1, CONV_GW), lambda s, g, c: (g, 0, 0)),
            head_spec, head_spec, head_spec,
            pl.BlockSpec((1, GROUP_W), lambda s, g, c: (0, g)),
        ] + extra_specs,
        out_specs=[
            pl.BlockSpec((l, GROUP_W), lambda s, g, c: (s * nc + c, g)),
            pl.BlockSpec((None, SSD_HPG, SSD_HEAD_DIM, SSD_STATE), lambda s, g, c: (s, g, 0, 0)),
            pl.BlockSpec((None, None, SUBLANES, CONV_GW), lambda s, g, c: (s, g, 0, 0)),
        ],
        out_shape=[
            jax.ShapeDtypeStruct((n_seq * t, D_INNER), BF16),
            jax.ShapeDtypeStruct((n_seq, SSD_HEADS, SSD_HEAD_DIM, SSD_STATE), F32),
            jax.ShapeDtypeStruct((n_seq, SSD_GROUPS, SUBLANES, CONV_GW), F32),
        ],
        scratch_shapes=[
            pltpu.VMEM((SSD_HPG, SSD_HEAD_DIM, SSD_STATE), F32),
            pltpu.VMEM((conv_rows, CONV_GW), F32),
        ],
        compiler_params=_cparams(("parallel", "parallel", "arbitrary")),
        name="ssd_long" if long_chunks else "ssd",
    )(proj, proj, proj, proj, dt_raw, h0, conv_prev, cw_g, cb_g, *head_par, norm_g, *extra_args)


def _qk(q, k):
    return lax.dot_general(q, k, (((1,), (1,)), ((), ())), preferred_element_type=F32)


def _attn_prompt_kernel(lam_ref, q_ref, k_ref, vt_ref, mk_ref, mvt_ref, o_ref, sa_scr, sb_scr, acc_scr, m_scr):
    i = pl.program_id(2)
    tq = q_ref.shape[0]
    lam = lam_ref[0]
    q = q_ref[...]
    lane = lax.broadcasted_iota(I32, q.shape, 1)
    zero = jnp.zeros_like(q)
    qm = [jnp.where(lane < ATT_HEAD_DIM, q, zero), jnp.where(lane >= ATT_HEAD_DIM, q, zero)]

    def put_scores(s_ref, j):
        kblk = k_ref[pl.ds(pl.multiple_of(j * tq, tq), tq), :]
        for r in range(2):
            s_ref[r] = _qk(kblk, qm[r])

    def with_ones(vt):
        return jnp.concatenate([vt, jnp.ones((ONES_ROWS, vt.shape[1]), BF16)], axis=0)

    def update(s_ref, j, masked=False):
        vt1 = with_ones(vt_ref[j])
        for r in range(2):
            s = s_ref[r]
            if masked:
                kpos = lax.broadcasted_iota(I32, (tq, tq), 0) // CHUNK
                qpos = lax.broadcasted_iota(I32, (tq, tq), 1) // CHUNK
                s = jnp.where(kpos <= qpos, s, NEG_BIG)
            m_p = m_scr[r]
            m_n = jnp.maximum(m_p, jnp.max(s, axis=0, keepdims=True))
            alpha = jnp.exp(m_p - m_n)
            p = jnp.exp((s - m_n).astype(BF16))
            acc_scr[r] = alpha * acc_scr[r] + jnp.dot(vt1, p, preferred_element_type=F32)
            m_scr[r] = m_n

    mk = mk_ref[...]
    mvt1 = with_ones(mvt_ref[...])
    for r in range(2):
        s0 = _qk(mk, qm[r])
        m0 = jnp.max(s0, axis=0, keepdims=True)
        acc_scr[r] = jnp.dot(mvt1, jnp.exp((s0 - m0).astype(BF16)), preferred_element_type=F32)
        m_scr[r] = m0

    put_scores(sa_scr, 0)

    def body(jj, _):
        j = 2 * jj
        put_scores(sb_scr, j + 1)
        update(sa_scr, j)
        put_scores(sa_scr, j + 2)
        update(sb_scr, j + 1)
        return 0

    lax.fori_loop(0, i // 2, body, 0)

    @pl.when(i % 2 == 0)
    def _():
        update(sa_scr, i, masked=True)

    @pl.when(i % 2 == 1)
    def _():
        put_scores(sb_scr, i)
        update(sa_scr, i - 1)
        update(sb_scr, i, masked=True)

    outs = [acc_scr[r, :ATT_V_DIM, :] / acc_scr[r, ATT_V_DIM:ATT_V_DIM + 1, :] for r in range(2)]
    o_ref[...] = (outs[0] - lam * outs[1]).T.astype(o_ref.dtype)


def _attn_prompt(lam, proj_p, vt_p, proj_s, mvt, *, batch, seq, meta_row0):
    nq = seq // ATT_TQ
    return pl.pallas_call(
        _attn_prompt_kernel,
        grid=(batch, ATT_HEADS, nq),
        in_specs=[
            pl.BlockSpec(memory_space=pltpu.SMEM),
            pl.BlockSpec((ATT_TQ, LANES), lambda b, h, i: (b * nq + i, COL_Q // LANES + h)),
            pl.BlockSpec((seq, LANES), lambda b, h, i: (b, COL_K // LANES + h)),
            pl.BlockSpec((None, None, nq, ATT_V_DIM, ATT_TQ), lambda b, h, i: (b, h, 0, 0, 0)),
            pl.BlockSpec((N_META, LANES), lambda b, h, i: (meta_row0 // N_META, COL_K // LANES + h)),
            pl.BlockSpec((None, ATT_V_DIM, N_META), lambda b, h, i: (h, 0, 0)),
        ],
        out_specs=pl.BlockSpec((ATT_TQ, LANES), lambda b, h, i: (b * nq + i, h)),
        out_shape=jax.ShapeDtypeStruct((batch * seq, ATT_WIDTH), BF16),
        scratch_shapes=[pltpu.VMEM((2, ATT_TQ, ATT_TQ), F32),
                        pltpu.VMEM((2, ATT_TQ, ATT_TQ), F32),
                        pltpu.VMEM((2, ATT_V_DIM + ONES_ROWS, ATT_TQ), F32),
                        pltpu.VMEM((2, 1, ATT_TQ), F32)],
        compiler_params=_cparams(("parallel", "parallel", "arbitrary")),
        name="attn_prompt",
    )(lam, proj_p, proj_p, vt_p, proj_s, mvt)


def _attn_short_kernel(lam_ref, q_ref, kn_ref, vn_ref, *rest, has_cache):
    if has_cache:
        kc_ref, vc_ref, o_ref = rest
    else:
        (o_ref,) = rest
    t = q_ref.shape[0]
    nh = 2 * ATT_HEADS
    lam = lam_ref[0]
    q = q_ref[...].astype(F32)
    qb = jnp.broadcast_to(q[None], (nh, t, ATT_WIDTH)).reshape(nh * t, ATT_WIDTH)
    row_head = lax.broadcasted_iota(I32, (nh * t, ATT_WIDTH), 0) // t
    col_head = lax.broadcasted_iota(I32, (nh * t, ATT_WIDTH), 1) // ATT_HEAD_DIM
    qbd = jnp.where(row_head == col_head, qb, 0.0).astype(BF16)
    s_new = _qk(qbd, kn_ref[...])
    m = jnp.max(s_new, axis=-1, keepdims=True)
    if has_cache:
        s_old = _qk(qbd, kc_ref[...].astype(BF16))
        m = jnp.maximum(m, jnp.max(s_old, axis=-1, keepdims=True))
    p_new = jnp.exp(s_new - m)
    den = jnp.sum(p_new, axis=-1, keepdims=True)
    acc = jnp.dot(p_new.astype(BF16), vn_ref[...], preferred_element_type=F32)
    if has_cache:
        p_old = jnp.exp(s_old - m)
        den = den + jnp.sum(p_old, axis=-1, keepdims=True)
        acc = acc + jnp.dot(p_old.astype(BF16), vc_ref[...].astype(BF16), preferred_element_type=F32)
    acc = acc / den
    for h in range(ATT_HEADS):
        cols = slice(h * ATT_V_DIM, (h + 1) * ATT_V_DIM)
        a1 = acc[(2 * h) * t:(2 * h + 1) * t, cols]
        a2 = acc[(2 * h + 1) * t:(2 * h + 2) * t, cols]
        o_ref[:, cols] = (a1 - lam * a2).astype(o_ref.dtype)


def _attn_short(lam, proj_s, cache_k, cache_v, *, n_seq, t, row0):
    has_cache = cache_k is not None
    rb0 = row0 // t
    in_specs = [
        pl.BlockSpec(memory_space=pltpu.SMEM),
        pl.BlockSpec((t, ATT_WIDTH), lambda s: (rb0 + s, COL_Q // ATT_WIDTH)),
        pl.BlockSpec((t, ATT_WIDTH), lambda s: (rb0 + s, COL_K // ATT_WIDTH)),
        pl.BlockSpec((t, ATT_WIDTH), lambda s: (rb0 + s, COL_V // ATT_WIDTH)),
    ]
    args = [lam, proj_s, proj_s, proj_s]
    if has_cache:
        past = cache_k.shape[1]
        in_specs += [pl.BlockSpec((None, past, ATT_WIDTH), lambda s: (s, 0, 0)),
                     pl.BlockSpec((None, past, ATT_WIDTH), lambda s: (s, 0, 0))]
        args += [cache_k, cache_v]
    return pl.pallas_call(
        functools.partial(_attn_short_kernel, has_cache=has_cache),
        grid=(n_seq,),
        in_specs=in_specs,
        out_specs=pl.BlockSpec((t, ATT_WIDTH), lambda s: (s, 0)),
        out_shape=jax.ShapeDtypeStruct((n_seq * t, ATT_WIDTH), BF16),
        compiler_params=_cparams(("parallel",)),
        name="attn_cached" if has_cache else "attn_meta",
    )(*args)


def _merge_kernel(x_ref, yn_ref, o_ref, g1_ref, g2_ref, wso_ref, wao_ref, wo_ref, bg1_ref, bg2_ref,
                  sub_ref, n2_ref, wr_ref, br_ref,
                  h1_ref, u2_ref, eidx_ref, gate_ref, rank_ref, cnt_ref, *, sub_scale):
    tm = x_ref.shape[0]
    y_ssd = jnp.dot(yn_ref[...], wso_ref[...], preferred_element_type=F32)
    o = o_ref[...].astype(F32)
    parts = []
    for h in range(ATT_HEADS):
        oh = o[:, h * ATT_V_DIM:(h + 1) * ATT_V_DIM]
        ms = jnp.mean(oh * oh, axis=-1, keepdims=True)
        parts.append(oh * lax.rsqrt(ms + EPS) * sub_ref[...] * sub_scale)
    on = jnp.concatenate(parts, axis=1).astype(BF16)
    y_att = jnp.dot(on, wao_ref[...], preferred_element_type=F32)
    gs = _sigmoid(g1_ref[...].astype(F32) + bg1_ref[...])
    ga = _sigmoid(g2_ref[...].astype(F32) + bg2_ref[...])
    mix_in = (gs * y_ssd + ga * y_att).astype(BF16)
    h1 = x_ref[...] + jnp.dot(mix_in, wo_ref[...], preferred_element_type=F32)
    h1_ref[...] = h1
    ms = jnp.mean(h1 * h1, axis=-1, keepdims=True)
    u2 = h1 * lax.rsqrt(ms + EPS) * n2_ref[...]
    u2_ref[...] = u2.astype(u2_ref.dtype)

    logits = jnp.dot(u2, wr_ref[...], preferred_element_type=F32, precision=lax.Precision.HIGHEST)
    logits = logits + br_ref[...]
    lane = lax.broadcasted_iota(I32, (tm, LANES), 1).astype(F32)
    work = jnp.where(lane < N_EXPERTS, logits, NEG_BIG)
    vals, idxs, hots = [], [], []
    for _ in range(TOP_K):
        mx = jnp.max(work, axis=-1, keepdims=True)
        ix = jnp.min(jnp.where(work == mx, lane, float(LANES)), axis=-1, keepdims=True)
        hot = lane == ix
        vals.append(mx)
        idxs.append(ix)
        hots.append(hot)
        work = jnp.where(hot, NEG_BIG, work)
    es = [jnp.exp(v - vals[0]) for v in vals]
    den = es[0] + es[1] + es[2] + es[3]
    hot_all = (hots[0] | hots[1] | hots[2] | hots[3])
    ti = lax.broadcasted_iota(I32, (tm, tm), 0)
    si = lax.broadcasted_iota(I32, (tm, tm), 1)
    strict = (si < ti).astype(BF16)
    prefix = jnp.dot(strict, hot_all.astype(BF16), preferred_element_type=F32)
    eidx = jnp.zeros((tm, LANES), F32)
    gate = jnp.zeros((tm, LANES), F32)
    rank = jnp.zeros((tm, LANES), F32)
    for k in range(TOP_K):
        rk = jnp.sum(jnp.where(hots[k], prefix, 0.0), axis=-1, keepdims=True)
        eidx = jnp.where(lane == k, idxs[k], eidx)
        gate = jnp.where(lane == k, es[k] / den, gate)
        rank = jnp.where(lane == k, rk, rank)
    eidx_ref[...] = eidx.astype(I32)
    gate_ref[...] = gate
    rank_ref[...] = rank.astype(I32)
    cnt_ref[...] = jnp.sum(hot_all.astype(F32), axis=0, keepdims=True).astype(I32)


def _merge(x, yn, o, proj, wso, wao, wo, bg1, bg2, sub_g, n2_g, wr, br, *, tm, sub_scale):
    rows = x.shape[0]
    full = lambda shape: pl.BlockSpec(shape, lambda i: (0,) * len(shape))
    tok = lambda w: pl.BlockSpec((tm, w), lambda i: (i, 0))
    return pl.pallas_call(
        functools.partial(_merge_kernel, sub_scale=sub_scale),
        grid=(rows // tm,),
        in_specs=[
            tok(D_MODEL), tok(D_INNER), tok(ATT_WIDTH),
            pl.BlockSpec((tm, D_MODEL), lambda i: (i, COL_G // D_MODEL)),
            pl.BlockSpec((tm, D_MODEL), lambda i: (i, COL_G // D_MODEL + 1)),
            full((D_INNER, D_MODEL)), full((ATT_WIDTH, D_MODEL)), full((D_MODEL, D_MODEL)),
            full((1, D_MODEL)), full((1, D_MODEL)), full((1, ATT_V_DIM)), full((1, D_MODEL)),
            full((D_MODEL, LANES)), full((1, LANES)),
        ],
        out_specs=[tok(D_MODEL), tok(D_MODEL), tok(LANES), tok(LANES), tok(LANES),
                   pl.BlockSpec((None, 1, LANES), lambda i: (i, 0, 0))],
        out_shape=[
            jax.ShapeDtypeStruct((rows, D_MODEL), F32),
            jax.ShapeDtypeStruct((rows, D_MODEL), BF16),
            jax.ShapeDtypeStruct((rows, LANES), I32),
            jax.ShapeDtypeStruct((rows, LANES), F32),
            jax.ShapeDtypeStruct((rows, LANES), I32),
            jax.ShapeDtypeStruct((rows // tm, 1, LANES), I32),
        ],
        compiler_params=_cparams(("parallel",)),
        name="merge_router",
    )(x, yn, o, proj, proj, wso, wao, wo, bg1, bg2, sub_g, n2_g, wr, br)


def _seg_copy(local_ref, lo, hbm_ref, hi, sem, to_hbm):
    loc = local_ref.at[pl.ds(pl.multiple_of(lo, SEG_PAD), SEG_PAD), :]
    hbm = hbm_ref.at[pl.ds(pl.multiple_of(hi, SEG_PAD), SEG_PAD), :]
    return pltpu.make_async_copy(loc, hbm, sem) if to_hbm else pltpu.make_async_copy(hbm, loc, sem)


def _seg_copies_start(tab_ref, local_ref, hbm_ref, sem, to_hbm):
    def per_expert(e, total):
        n, lo, hi = tab_ref[0, e], tab_ref[1, e], tab_ref[2, e]

        def per_copy(c, _):
            _seg_copy(local_ref, lo + c * SEG_PAD, hbm_ref, hi + c * SEG_PAD, sem, to_hbm).start()
            return 0

        lax.fori_loop(0, n, per_copy, 0)
        return total + n

    return lax.fori_loop(0, N_EXPERTS, per_expert, 0)


def _seg_copies_wait(n, local_ref, hbm_ref, sem, to_hbm):
    def one(c, _):
        _seg_copy(local_ref, 0, hbm_ref, 0, sem, to_hbm).wait()
        return 0

    lax.fori_loop(0, n, one, 0)


def _dispatch_kernel(tab_ref, lpos_ref, u_ref, xs_in_ref, xs_ref, loc_scr, sem):
    del xs_in_ref
    lr, tm = loc_scr.shape[0], u_ref.shape[0]
    lpos = lpos_ref[...]
    p = lax.broadcasted_iota(I32, (lr, tm), 0)
    perm = jnp.zeros((lr, tm), F32)
    for k in range(TOP_K):
        perm = jnp.where(p == lpos[k:k + 1, :], 1.0, perm)
    loc_scr[...] = jnp.dot(perm.astype(BF16), u_ref[...], preferred_element_type=F32).astype(loc_scr.dtype)
    n = _seg_copies_start(tab_ref, loc_scr, xs_ref, sem, to_hbm=True)
    _seg_copies_wait(n, loc_scr, xs_ref, sem, to_hbm=True)


def _local_rows(tm):
    return tm * TOP_K + N_EXPERTS * SEG_PAD


def _dispatch(tab, lpos_t, u2, xs, *, tm):
    rows = u2.shape[0]
    return pl.pallas_call(
        _dispatch_kernel,
        grid=(rows // tm,),
        in_specs=[
            pl.BlockSpec((None, 3, N_EXPERTS), lambda i: (i, 0, 0), memory_space=pltpu.SMEM),
            pl.BlockSpec((None, TOP_K, tm), lambda i: (i, 0, 0)),
            pl.BlockSpec((tm, D_MODEL), lambda i: (i, 0)),
            pl.BlockSpec(memory_space=pl.ANY),
        ],
        out_specs=pl.BlockSpec(memory_space=pl.ANY),
        out_shape=jax.ShapeDtypeStruct(xs.shape, xs.dtype),
        scratch_shapes=[pltpu.VMEM((_local_rows(tm), D_MODEL), xs.dtype), pltpu.SemaphoreType.DMA(())],
        input_output_aliases={3: 0},
        compiler_params=_cparams(("arbitrary",)),
        name="moe_dispatch",
    )(tab, lpos_t, u2, xs)


def _ffn_kernel(be_ref, na_ref, x_ref, wgu_ref, bgu_ref, wd_ref, bd_ref, y_ref):
    del be_ref

    @pl.when(pl.program_id(0) < na_ref[0])
    def _():
        gu = jnp.dot(x_ref[...], wgu_ref[...], preferred_element_type=F32) + bgu_ref[...]
        gate = jnp.minimum(gu[:, :D_FF], SWIGLU_LIMIT)
        up = jnp.clip(gu[:, D_FF:], -SWIGLU_LIMIT, SWIGLU_LIMIT)
        hdn = (up + 1.0) * gate * _sigmoid(SWIGLU_ALPHA * gate)
        y = jnp.dot(hdn.astype(BF16), wd_ref[...], preferred_element_type=F32) + bd_ref[...]
        y_ref[...] = y.astype(y_ref.dtype)


def _ffn(block_exp, n_active, xs, wgu, bgu, wd, bd):
    n_blocks = xs.shape[0] // MOE_BLK

    def blk(i, be, na):
        return jnp.minimum(i, na[0] - 1)

    return pl.pallas_call(
        _ffn_kernel,
        grid_spec=pltpu.PrefetchScalarGridSpec(
            num_scalar_prefetch=2,
            grid=(n_blocks,),
            in_specs=[
                pl.BlockSpec((MOE_BLK, D_MODEL), lambda i, be, na: (blk(i, be, na), 0)),
                pl.BlockSpec((None, D_MODEL, 2 * D_FF), lambda i, be, na: (be[blk(i, be, na)], 0, 0)),
                pl.BlockSpec((None, 1, 2 * D_FF), lambda i, be, na: (be[blk(i, be, na)], 0, 0)),
                pl.BlockSpec((None, D_FF, D_MODEL), lambda i, be, na: (be[blk(i, be, na)], 0, 0)),
                pl.BlockSpec((None, 1, D_MODEL), lambda i, be, na: (be[blk(i, be, na)], 0, 0)),
            ],
            out_specs=pl.BlockSpec((MOE_BLK, D_MODEL), lambda i, be, na: (blk(i, be, na), 0)),
        ),
        out_shape=jax.ShapeDtypeStruct(xs.shape, xs.dtype),
        compiler_params=_cparams(("arbitrary",)),
        name="moe_ffn",
    )(block_exp, n_active, xs, wgu, bgu, wd, bd)


def _combine_kernel(tab_ref, lpos_ref, gate_ref, h1_ref, fg_ref, ys_ref, y_ref, loc_scr, sem):
    lr, tm = loc_scr.shape[0], h1_ref.shape[0]
    loc_scr[...] = jnp.zeros(loc_scr.shape, loc_scr.dtype)
    n = _seg_copies_start(tab_ref, loc_scr, ys_ref, sem, to_hbm=False)
    lpos = lpos_ref[...]
    gate = gate_ref[...]
    p = lax.broadcasted_iota(I32, (tm, lr), 1)
    pick = jnp.zeros((tm, lr), F32)
    for k in range(TOP_K):
        pick = jnp.where(p == lpos[:, k:k + 1], gate[:, k:k + 1], pick)
    _seg_copies_wait(n, loc_scr, ys_ref, sem, to_hbm=False)
    h = h1_ref[...] + jnp.dot(pick.astype(BF16), loc_scr[...], preferred_element_type=F32)
    ms = jnp.mean(h * h, axis=-1, keepdims=True)
    y_ref[...] = h * lax.rsqrt(ms + EPS) * fg_ref[...]


def _combine(tab, lpos, gate, h1, fg, ys, *, tm):
    rows = h1.shape[0]
    return pl.pallas_call(
        _combine_kernel,
        grid=(rows // tm,),
        in_specs=[
            pl.BlockSpec((None, 3, N_EXPERTS), lambda i: (i, 0, 0), memory_space=pltpu.SMEM),
            pl.BlockSpec((tm, TOP_K), lambda i: (i, 0)),
            pl.BlockSpec((tm, LANES), lambda i: (i, 0)),
            pl.BlockSpec((tm, D_MODEL), lambda i: (i, 0)),
            pl.BlockSpec((1, D_MODEL), lambda i: (0, 0)),
            pl.BlockSpec(memory_space=pl.ANY),
        ],
        out_specs=pl.BlockSpec((tm, D_MODEL), lambda i: (i, 0)),
        out_shape=jax.ShapeDtypeStruct((rows, D_MODEL), F32),
        scratch_shapes=[pltpu.VMEM((_local_rows(tm), D_MODEL), ys.dtype), pltpu.SemaphoreType.DMA(())],
        compiler_params=_cparams(("arbitrary",)),
        name="moe_combine",
    )(tab, lpos, gate, h1, fg, ys)


def _rope_tables(pos):
    d = ATT_HEAD_DIM
    inv = ROPE_THETA ** (-jnp.arange(0, d, 2, dtype=F32) / d)
    ang = pos.astype(F32)[:, None] * inv[None, :]
    cos = jnp.cos(ang)
    sin = jnp.sin(ang)
    cos_h = jnp.concatenate([cos, cos], axis=1)
    sin_h = jnp.concatenate([-sin, sin], axis=1)
    return jnp.tile(cos_h, (1, LANES // d)), jnp.tile(sin_h, (1, LANES // d))


def _conv_by_group(a):
    lead = a.shape[:-1]
    x = a[..., :D_INNER].reshape(lead + (SSD_GROUPS, GROUP_W))
    b = a[..., D_INNER:D_INNER + SSD_GN].reshape(lead + (SSD_GROUPS, SSD_STATE))
    c = a[..., D_INNER + SSD_GN:].reshape(lead + (SSD_GROUPS, SSD_STATE))
    return jnp.concatenate([x, b, c], axis=-1)


def _conv_from_group(a):
    lead = a.shape[:-2]
    x = a[..., :GROUP_W].reshape(lead + (D_INNER,))
    b = a[..., GROUP_W:GROUP_W + SSD_STATE].reshape(lead + (SSD_GN,))
    c = a[..., GROUP_W + SSD_STATE:].reshape(lead + (SSD_GN,))
    return jnp.concatenate([x, b, c], axis=-1)


def _conv_prev_blocks(prev):
    g = jnp.moveaxis(_conv_by_group(prev.astype(F32)), 1, 2)
    return jnp.pad(g, ((0, 0), (0, 0), (SUBLANES - (CONV_W - 1), 0), (0, 0)))


def _conv_tail_rows(ct):
    return _conv_from_group(jnp.moveaxis(ct[:, :, SUBLANES - (CONV_W - 1):, :], 1, 2))


def kernel(x_prompt, x_sample, cache_k, cache_v, state_ssm, state_conv, meta_tokens, norm1_g, w_in, conv_w, conv_b, dt_bias, a_log, d_skip, ssd_norm_g, lambda_q1, lambda_k1, lambda_q2, lambda_k2, subln_g, w_ssd_out, w_att_out, b_gate, w_o, norm2_g, w_router, b_router, w_gu, b_gu, w_down, b_down, final_norm_g):
    batch, seq, _ = x_prompt.shape
    dbatch, dseq, _ = x_sample.shape
    past = cache_k.shape[2]
    depth = norm1_g.shape[0]
    assert depth == 1 and dseq == N_META
    assert seq % SSD_L == 0 and seq % ATT_TQ == 0 and seq % TOKEN_TM == 0 and ATT_TQ % CHUNK == 0
    assert TOKEN_TM % ATT_TQ == 0 and PROJ_TM % ATT_TQ == 0
    n_p = batch * seq
    n_dec = dbatch * dseq
    n_s = n_dec + N_META
    lam_init = 0.8 - 0.6 * math.exp(-0.3 * 0)
    l = 0

    wi = w_in[l]
    o_z, o_xbc, o_dt = 0, D_INNER, D_INNER + CONV_DIM
    o_q = o_dt + SSD_HEADS
    w_main = jnp.concatenate([wi[:, o_z:o_xbc], wi[:, o_xbc:o_dt], wi[:, o_q:]], axis=1).astype(BF16)
    w_dt = jnp.zeros((D_MODEL, SSD_GROUPS, LANES), F32).at[:, :, :SSD_HPG].set(
        wi[:, o_dt:o_q].reshape(D_MODEL, SSD_GROUPS, SSD_HPG)).reshape(D_MODEL, SSD_GROUPS * LANES).astype(BF16)
    g1 = norm1_g[l].reshape(1, D_MODEL)
    cw_g = jnp.moveaxis(_conv_by_group(conv_w[l]), 0, 1)
    cb_g = _conv_by_group(conv_b[l])[:, None, :]
    a_neg = -jnp.exp(a_log[l].astype(F32))
    norm_g = ssd_norm_g[l].reshape(1, D_INNER)
    lam = (jnp.exp(jnp.sum(lambda_q1[l].astype(F32) * lambda_k1[l].astype(F32)))
           - jnp.exp(jnp.sum(lambda_q2[l].astype(F32) * lambda_k2[l].astype(F32))) + lam_init).reshape(1)
    wso = w_ssd_out[l].astype(BF16)
    wao = w_att_out[l].astype(BF16)
    wo = w_o[l].astype(BF16)
    bg1 = b_gate[l][:D_MODEL].reshape(1, D_MODEL)
    bg2 = b_gate[l][D_MODEL:].reshape(1, D_MODEL)
    sub_g = subln_g[l].reshape(1, ATT_V_DIM)
    n2_g = norm2_g[l].reshape(1, D_MODEL)
    wr = jnp.zeros((D_MODEL, LANES), F32).at[:, :N_EXPERTS].set(w_router[l])
    br = jnp.zeros((1, LANES), F32).at[0, :N_EXPERTS].set(b_router[l])
    wgu = w_gu[l].astype(BF16)
    bgu = b_gu[l][:, None, :]
    wd = w_down[l].astype(BF16)
    bd = b_down[l][:, None, :]
    fg = final_norm_g.reshape(1, D_MODEL)

    xp = x_prompt.reshape(n_p, D_MODEL)
    xs_rows = jnp.concatenate([x_sample.reshape(n_dec, D_MODEL), meta_tokens.astype(x_prompt.dtype)], axis=0)
    cos_p, sin_p = _rope_tables(N_META + jnp.arange(seq, dtype=I32))
    pos_s = jnp.concatenate([jnp.tile(past + jnp.arange(dseq, dtype=I32), dbatch), jnp.arange(N_META, dtype=I32)])
    cos_s, sin_s = _rope_tables(pos_s)

    tm_p = PROJ_TM if seq % PROJ_TM == 0 else TOKEN_TM
    proj_p, dt_p, vt_p = _in_proj(xp, g1, w_main, w_dt, cos_p, sin_p, tm_p, seq // tm_p, vt_block=ATT_TQ)
    proj_s, dt_s = _in_proj(xs_rows, g1, w_main, w_dt, cos_s, sin_s, n_s, 1)

    ssd_args = (cw_g, cb_g, dt_bias[l], a_neg, d_skip[l], norm_g)
    zero_h = jnp.zeros((1, SSD_HEADS, SSD_HEAD_DIM, SSD_STATE), F32)
    zero_c = jnp.zeros((1, SSD_GROUPS, SUBLANES, CONV_GW), F32)
    yn_m, h_m, ct_m = _ssd(proj_s, dt_s, zero_h, zero_c, *ssd_args, n_seq=1, t=N_META, l=N_META,
                           row0=n_dec, shared_state=True)
    yn_p, h_p, ct_p = _ssd(proj_p, dt_p, h_m, ct_m, *ssd_args, n_seq=batch, t=seq, l=SSD_L, row0=0,
                           shared_state=True)
    yn_d, h_d, ct_d = _ssd(proj_s, dt_s, state_ssm[l].astype(F32), _conv_prev_blocks(state_conv[l]), *ssd_args,
                           n_seq=dbatch, t=dseq, l=dseq, row0=0, shared_state=False)
    yn_s = jnp.concatenate([yn_d, yn_m], axis=0)

    mvt =proj_s[n_dec:n_s, COL_V:COL_V + ATT_WIDTH].reshape(N_META, ATT_HEADS, ATT_V_DIM).transpose(1, 2, 0)
    o_p = _attn_prompt(lam, proj_p, vt_p, proj_s, mvt, batch=batch, seq=seq, meta_row0=n_dec)
    o_d = _attn_short(lam, proj_s, cache_k[l].reshape(dbatch, past, ATT_WIDTH),
                      cache_v[l].reshape(dbatch, past, ATT_WIDTH), n_seq=dbatch, t=dseq, row0=0)
    o_m = _attn_short(lam, proj_s, None, None, n_seq=1, t=N_META, row0=n_dec)
    o_s = jnp.concatenate([o_d, o_m], axis=0)

    merge_w = (wso, wao, wo, bg1, bg2, sub_g, n2_g, wr, br)
    h1_p, u2_p, e_p, gt_p, rk_p, cnt_p = _merge(xp, yn_p, o_p, proj_p, *merge_w, tm=TOKEN_TM,
                                                sub_scale=1.0 - lam_init)
    h1_s, u2_s, e_s, gt_s, rk_s, cnt_s = _merge(xs_rows, yn_s, o_s, proj_s, *merge_w, tm=n_s,
                                                sub_scale=1.0 - lam_init)

    nb_p = n_p // TOKEN_TM
    cnt = jnp.concatenate([cnt_p[:, 0, :N_EXPERTS], cnt_s[:, 0, :N_EXPERTS]], axis=0)
    seg = (cnt + SEG_PAD - 1) // SEG_PAD * SEG_PAD
    local_start = jnp.cumsum(seg, axis=1) - seg
    per_expert = jnp.sum(seg, axis=0)
    padded = (per_expert + MOE_BLK - 1) // MOE_BLK * MOE_BLK
    pend = jnp.cumsum(padded)
    pstart = pend - padded
    hbm_start = pstart[None, :] + jnp.cumsum(seg, axis=0) - seg
    tab = jnp.stack([seg // SEG_PAD, local_start, hbm_start], axis=1).astype(I32)
    n_rows_max = (n_p + n_s) * TOP_K + (nb_p + 1) * N_EXPERTS * (SEG_PAD - 1)
    n_blocks = -(-n_rows_max // MOE_BLK) + N_EXPERTS
    block_start = jnp.arange(n_blocks, dtype=I32) * MOE_BLK
    block_exp = jnp.minimum(jnp.sum((pend[None, :] <= block_start[:, None]).astype(I32), axis=1), N_EXPERTS - 1)
    n_active = (pend[-1:] // MOE_BLK).astype(I32)

    def local_rows_of(e, rk, starts):
        hot = e[:, :TOP_K, None] == jnp.arange(N_EXPERTS, dtype=I32)
        return (jnp.sum(jnp.where(hot, starts[:, None, :], 0), axis=-1) + rk[:, :TOP_K]).astype(I32)

    lpos_p = local_rows_of(e_p, rk_p, jnp.repeat(local_start[:nb_p], TOKEN_TM, axis=0))
    lpos_s = local_rows_of(e_s, rk_s, jnp.repeat(local_start[nb_p:], n_s, axis=0))
    lpos_pt = lpos_p.reshape(nb_p, TOKEN_TM, TOP_K).transpose(0, 2, 1)
    lpos_st = lpos_s.reshape(1, n_s, TOP_K).transpose(0, 2, 1)
    xs = jnp.zeros((n_blocks * MOE_BLK, D_MODEL), BF16)
    xs = _dispatch(tab[:nb_p], lpos_pt, u2_p, xs, tm=TOKEN_TM)
    xs = _dispatch(tab[nb_p:], lpos_st, u2_s, xs, tm=n_s)
    ys = _ffn(block_exp, n_active, xs, wgu, bgu, wd, bd)
    y_p = _combine(tab[:nb_p], lpos_p, gt_p, h1_p, fg, ys, tm=TOKEN_TM)
    y_s = _combine(tab[nb_p:], lpos_s, gt_s, h1_s, fg, ys, tm=n_s)

    def kv_rows(proj, col, lo, hi):
        return proj[lo:hi, col:col + ATT_WIDTH].astype(F32)

    k_meta = kv_rows(proj_s, COL_K, n_dec, n_s).reshape(1, N_META, 2 * ATT_HEADS, ATT_HEAD_DIM)
    v_meta = kv_rows(proj_s, COL_V, n_dec, n_s).reshape(1, N_META, ATT_HEADS, ATT_V_DIM)
    k_fr = kv_rows(proj_p, COL_K, 0, n_p).reshape(batch, seq, 2 * ATT_HEADS, ATT_HEAD_DIM)
    v_fr = kv_rows(proj_p, COL_V, 0, n_p).reshape(batch, seq, ATT_HEADS, ATT_V_DIM)
    new_k_p = jnp.concatenate([jnp.broadcast_to(k_meta, (batch,) + k_meta.shape[1:]), k_fr], axis=1)[None]
    new_v_p = jnp.concatenate([jnp.broadcast_to(v_meta, (batch,) + v_meta.shape[1:]), v_fr], axis=1)[None]
    new_k_s = kv_rows(proj_s, COL_K, 0, n_dec).reshape(1, dbatch, dseq, 2 * ATT_HEADS, ATT_HEAD_DIM)
    new_v_s = kv_rows(proj_s, COL_V, 0, n_dec).reshape(1, dbatch, dseq, ATT_HEADS, ATT_V_DIM)
    return (y_p.reshape(batch, seq, D_MODEL),
            y_s[:n_dec].reshape(dbatch, dseq, D_MODEL),
            new_k_p, new_v_p,
            h_p.astype(state_ssm.dtype)[None],
            _conv_tail_rows(ct_p).astype(x_prompt.dtype)[None],
            new_k_s, new_v_s,
            h_d.astype(state_ssm.dtype)[None],
            _conv_tail_rows(ct_d).astype(x_sample.dtype)[None])
```

```python
import functools
import math

import jax
import jax.numpy as jnp
from jax import lax
from jax.experimental import pallas as pl
from jax.experimental.pallas import tpu as pltpu

F32 = jnp.float32
BF16 = jnp.bfloat16
I32 = jnp.int32

D_MODEL = 1024
D_INNER = 2048
SSD_HEADS = 32
SSD_HEAD_DIM = 64
SSD_GROUPS = 8
SSD_HPG = SSD_HEADS // SSD_GROUPS
SSD_STATE = 128
SSD_GN = SSD_GROUPS * SSD_STATE
CONV_W = 4
CONV_DIM = D_INNER + 2 * SSD_GN
ATT_HEADS = 8
ATT_HEAD_DIM = 64
ATT_V_DIM = 128
ATT_WIDTH = 1024
CHUNK = 64
N_META = 16
EPS = 1e-6
ROPE_THETA = 10000.0
LOG2_E = math.log2(math.e)
N_EXPERTS = 32
TOP_K = 4
D_FF = 1024
SWIGLU_LIMIT = 7.0
SWIGLU_ALPHA = 1.702

COL_Z = 0
COL_X = COL_Z + D_INNER
COL_B = COL_X + D_INNER
COL_C = COL_B + SSD_GN
COL_Q = COL_C + SSD_GN
COL_K = COL_Q + ATT_WIDTH
COL_V = COL_K + ATT_WIDTH
COL_G = COL_V + ATT_WIDTH
N_MAIN = COL_G + 2 * D_MODEL

LANES = 128
SUBLANES = 8
GROUP_W = D_INNER // SSD_GROUPS
CONV_GW = GROUP_W + 2 * SSD_STATE
PROJ_TN = 1024
PROJ_TM = 1024
TOKEN_TM = 512
SSD_L = 256
ATT_TQ = 512
MOE_BLK = 256
SEG_PAD = 16
ONES_ROWS = 16
VMEM_LIMIT = 56 * 1024 * 1024
NEG_BIG = -1e30


def _cparams(sem):
    return pltpu.CompilerParams(dimension_semantics=sem, vmem_limit_bytes=VMEM_LIMIT)


def _sigmoid(x):
    return 1.0 / (1.0 + jnp.exp(-x))


def _inproj_kernel(x_ref, g_ref, w_ref, wdt_ref, cos_ref, sin_ref, o_ref, dt_ref, *rest, emit_vt,
                   seq_blocks):
    if emit_vt:
        vt_ref, kout_ref, vout_ref, u_scr, kbuf, vbuf, sems = rest
    else:
        (u_scr,) = rest
    j = pl.program_id(1)

    @pl.when(j == 0)
    def _():
        x = x_ref[...]
        ms = jnp.mean(x * x, axis=-1, keepdims=True)
        u = (x * lax.rsqrt(ms + EPS) * g_ref[...]).astype(BF16)
        u_scr[...] = u
        dt_ref[...] = jnp.dot(u, wdt_ref[...], preferred_element_type=F32)

    acc = jnp.dot(u_scr[...], w_ref[...], preferred_element_type=F32)
    is_q = j == COL_Q // PROJ_TN
    is_k = j == COL_K // PROJ_TN

    @pl.when(is_q | is_k)
    def _():
        cos = cos_ref[...]
        sin = sin_ref[...]
        lane = lax.broadcasted_iota(I32, cos.shape, 1)
        half = ATT_HEAD_DIM // 2
        first_half = (lane % ATT_HEAD_DIM) < half
        scale = jnp.where(is_q, ATT_HEAD_DIM ** -0.5 * LOG2_E, 1.0).astype(F32)
        for c in range(PROJ_TN // LANES):
            a = acc[:, c * LANES:(c + 1) * LANES]
            swapped = jnp.where(first_half, pltpu.roll(a, LANES - half, 1), pltpu.roll(a, half, 1))
            roped = (a * cos + swapped * sin) * scale
            o_ref[:, c * LANES:(c + 1) * LANES] = roped.astype(o_ref.dtype)
            if emit_vt:
                kbuf[:, c * LANES:(c + 1) * LANES] = roped

    @pl.when(jnp.logical_not(is_q | is_k))
    def _():
        o_ref[...] = acc.astype(o_ref.dtype)

    if emit_vt:
        i = pl.program_id(0)
        tm = x_ref.shape[0]
        row0 = (i // seq_blocks) * (N_META + seq_blocks * tm) + N_META + (i % seq_blocks) * tm

        def cache_copy(buf, out_ref, sem):
            return pltpu.make_async_copy(buf, out_ref.at[pl.ds(pl.multiple_of(row0, SUBLANES), tm), :], sem)

        @pl.when(is_k)
        def _():
            cache_copy(kbuf, kout_ref, sems.at[0]).start()

        @pl.when(j == COL_V // PROJ_TN)
        def _():
            vbuf[...] = acc
            cache_copy(vbuf, vout_ref, sems.at[1]).start()
            tk = vt_ref.shape[-1]
            for h in range(ATT_HEADS):
                for s in range(vt_ref.shape[1]):
                    blk = acc[s * tk:(s + 1) * tk, h * ATT_V_DIM:(h + 1) * ATT_V_DIM]
                    vt_ref[h, s] = blk.T.astype(vt_ref.dtype)

        @pl.when(j == pl.num_programs(1) - 1)
        def _():
            cache_copy(kbuf, kout_ref, sems.at[0]).wait()
            cache_copy(vbuf, vout_ref, sems.at[1]).wait()


def _in_proj(x, g1, w_main, w_dt, cos_t, sin_t, tm, rope_blocks, vt_block=None):
    rows = x.shape[0]
    grid = (rows // tm, N_MAIN // PROJ_TN)
    scratch = [pltpu.VMEM((tm, D_MODEL), BF16)]
    out_specs = [
        pl.BlockSpec((tm, PROJ_TN), lambda i, j: (i, j)),
        pl.BlockSpec((tm, SSD_GROUPS * LANES), lambda i, j: (i, 0)),
    ]
    out_shape = [
        jax.ShapeDtypeStruct((rows, N_MAIN), BF16),
        jax.ShapeDtypeStruct((rows, SSD_GROUPS * LANES), F32),
    ]
    if vt_block is not None:
        per = tm // vt_block
        out_specs.append(pl.BlockSpec((None, ATT_HEADS, per, ATT_V_DIM, vt_block),
                                      lambda i, j: (i // rope_blocks, 0, i % rope_blocks, 0, 0)))
        streams = rows // (rope_blocks * tm)
        out_shape.append(jax.ShapeDtypeStruct(
            (streams, ATT_HEADS, rope_blocks * per, ATT_V_DIM, vt_block), BF16))
        for _ in range(2):
            out_specs.append(pl.BlockSpec(memory_space=pl.ANY))
            out_shape.append(jax.ShapeDtypeStruct((streams * (N_META + rope_blocks * tm), D_MODEL), F32))
        scratch += [pltpu.VMEM((tm, ATT_WIDTH), F32), pltpu.VMEM((tm, ATT_WIDTH), F32),
                    pltpu.SemaphoreType.DMA((2,))]
    return pl.pallas_call(
        functools.partial(_inproj_kernel, emit_vt=vt_block is not None, seq_blocks=rope_blocks),
        grid=grid,
        in_specs=[
            pl.BlockSpec((tm, D_MODEL), lambda i, j: (i, 0)),
            pl.BlockSpec((1, D_MODEL), lambda i, j: (0, 0)),
            pl.BlockSpec((D_MODEL, PROJ_TN), lambda i, j: (0, j)),
            pl.BlockSpec((D_MODEL, SSD_GROUPS * LANES), lambda i, j: (0, 0)),
            pl.BlockSpec((tm, LANES), lambda i, j: (i % rope_blocks, 0)),
            pl.BlockSpec((tm, LANES), lambda i, j: (i % rope_blocks, 0)),
        ],
        out_specs=out_specs,
        out_shape=out_shape,
        scratch_shapes=scratch,
        compiler_params=_cparams(("parallel", "arbitrary")),
        name="in_proj",
    )(x, g1, w_main, w_dt, cos_t, sin_t)


def _ssd_kernel(x_ref, b_ref, c_ref, z_ref, dt_ref, h0_ref, cp_ref, cw_ref, cb_ref, dtb_ref,
                aneg_ref, dsk_ref, ng_ref, y_ref, hf_ref, ct_ref, h_scr, f_scr):
    c = pl.program_id(2)
    L = x_ref.shape[0]

    @pl.when(c == 0)
    def _():
        h_scr[...] = h0_ref[...]
        f_scr[0:SUBLANES, :] = cp_ref[...]

    f_scr[SUBLANES:SUBLANES + L, 0:GROUP_W] = x_ref[...].astype(F32)
    f_scr[SUBLANES:SUBLANES + L, GROUP_W:GROUP_W + SSD_STATE] = b_ref[...].astype(F32)
    f_scr[SUBLANES:SUBLANES + L, GROUP_W + SSD_STATE:CONV_GW] = c_ref[...].astype(F32)
    w = cw_ref[...]
    acc = cb_ref[...]
    for i in range(CONV_W):
        lo = SUBLANES - (CONV_W - 1) + i
        acc = acc + w[i:i + 1, :] * f_scr[lo:lo + L, :]
    xc = acc * _sigmoid(acc)
    tail = f_scr[L:L + SUBLANES, :]
    f_scr[0:SUBLANES, :] = tail
    ct_ref[...] = tail

    xg = xc[:, 0:GROUP_W]
    bm = xc[:, GROUP_W:GROUP_W + SSD_STATE].astype(BF16)
    cm = xc[:, GROUP_W + SSD_STATE:CONV_GW].astype(BF16)

    dtr = dt_ref[...] + dtb_ref[...]
    dt = jnp.maximum(dtr, 0.0) + jnp.log(1.0 + jnp.exp(-jnp.abs(dtr)))
    da = dt * aneg_ref[...]
    ti = lax.broadcasted_iota(I32, (L, L), 0)
    si = lax.broadcasted_iota(I32, (L, L), 1)
    causal = si <= ti
    tril = causal.astype(F32)
    cum = jnp.dot(tril, da, preferred_element_type=F32, precision=lax.Precision.HIGHEST)
    sel = (lax.broadcasted_iota(I32, (SUBLANES, LANES), 0)
           == lax.broadcasted_iota(I32, (SUBLANES, LANES), 1)).astype(F32)
    cum_t = lax.dot_general(sel, cum, (((1,), (1,)), ((), ())), preferred_element_type=F32,
                            precision=lax.Precision.HIGHEST)

    cb = lax.dot_general(cm, bm, (((1,), (1,)), ((), ())), preferred_element_type=F32)
    dsk = dsk_ref[...]
    ys = []
    for r in range(SSD_HPG):
        col = cum[:, r:r + 1]
        row = cum_t[r:r + 1, :]
        dec = jnp.exp(jnp.where(causal, col - row, NEG_BIG))
        m = (cb * dec).astype(BF16)
        xh = xg[:, r * SSD_HEAD_DIM:(r + 1) * SSD_HEAD_DIM]
        xdt = xh * dt[:, r:r + 1]
        h_prev = h_scr[r]
        y = jnp.dot(m, xdt.astype(BF16), preferred_element_type=F32)
        y = y + jnp.exp(col) * lax.dot_general(cm, h_prev.astype(BF16), (((1,), (1,)), ((), ())),
                                               preferred_element_type=F32)
        y = y + dsk[:, r:r + 1] * xh
        ys.append(y)
        tot = cum[L - 1:L, r:r + 1]
        xw = (xdt * jnp.exp(tot - col)).astype(BF16)
        upd = lax.dot_general(xw, bm, (((0,), (0,)), ((), ())), preferred_element_type=F32)
        h_scr[r] = h_prev * jnp.exp(tot) + upd
    yg = jnp.concatenate(ys, axis=1)
    z = z_ref[...].astype(F32)
    yz = yg * (z * _sigmoid(z))
    ms = jnp.mean(yz * yz, axis=-1, keepdims=True)
    y_ref[...] = (yz * lax.rsqrt(ms + EPS) * ng_ref[...]).astype(y_ref.dtype)
    hf_ref[...] = h_scr[...]


def _ssd_long_kernel(x_ref, b_ref, c_ref, z_ref, dt_ref, h0_ref, cp_ref, cw_ref, cb_ref, dtb_ref,
                     aneg_ref, dsk_ref, ng_ref, tri_ref, shift_ref, y_ref, hf_ref, ct_ref, h_scr, f_scr):
    c = pl.program_id(2)
    L = x_ref.shape[0]
    reps = L // LANES

    @pl.when(c == 0)
    def _():
        h_scr[...] = h0_ref[...]
        f_scr[0:SUBLANES, :] = cp_ref[...]
        f_scr[SUBLANES:2 * SUBLANES, :] = jnp.zeros((SUBLANES, CONV_GW), F32)

    xb = jnp.concatenate([x_ref[...], b_ref[...], c_ref[...]], axis=1)
    xf = xb.astype(F32)
    w = cw_ref[...]
    acc = cb_ref[...] + w[CONV_W - 1:CONV_W, :] * xf
    for d in range(1, CONV_W):
        sh = jnp.dot(shift_ref[d - 1], xb, preferred_element_type=F32)
        acc = acc + w[CONV_W - 1 - d:CONV_W - d, :] * sh
    corr = jnp.zeros((SUBLANES, CONV_GW), F32)
    for i in range(CONV_W - 1):
        lo = SUBLANES - (CONV_W - 1) + i
        corr = corr + w[i:i + 1, :] * f_scr[lo:lo + SUBLANES, :]
    acc = jnp.concatenate([acc[0:SUBLANES] + corr, acc[SUBLANES:]], axis=0)
    xc = acc * _sigmoid(acc)
    tail = xf[L - SUBLANES:L, :]
    f_scr[0:SUBLANES, :] = tail
    ct_ref[...] = tail

    x_t = xc[:, 0:GROUP_W].T
    bm = xc[:, GROUP_W:GROUP_W + SSD_STATE].astype(BF16)
    cm = xc[:, GROUP_W + SSD_STATE:CONV_GW].astype(BF16)

    dtr = dt_ref[...].T[0:SUBLANES, :] + jnp.tile(dtb_ref[...], (1, reps))
    dt = jnp.maximum(dtr, 0.0) + jnp.log(1.0 + jnp.exp(-jnp.abs(dtr)))
    da = dt * jnp.tile(aneg_ref[...], (1, reps))
    tri = tri_ref[...]
    tri_b = tri.astype(BF16)
    visible = tri > 0.5
    d1 = da.astype(BF16)
    r1 = da - d1.astype(F32)
    d2 = r1.astype(BF16)
    d3 = (r1 - d2.astype(F32)).astype(BF16)
    cum = (jnp.dot(d1, tri_b, preferred_element_type=F32) + jnp.dot(d2, tri_b, preferred_element_type=F32)
           + jnp.dot(d3, tri_b, preferred_element_type=F32))
    cum_col = cum.T

    cb_t = lax.dot_general(bm, cm, (((1,), (1,)), ((), ())), preferred_element_type=F32)
    dsk = dsk_ref[...]
    ys = []
    for r in range(SSD_HPG):
        row = cum[r:r + 1, :]
        dec = jnp.exp(jnp.where(visible, row - cum_col[:, r:r + 1], NEG_BIG))
        m = (cb_t * dec).astype(BF16)
        xh = x_t[r * SSD_HEAD_DIM:(r + 1) * SSD_HEAD_DIM, :]
        xdt = xh * dt[r:r + 1, :]
        h_prev = h_scr[r]
        y = jnp.dot(xdt.astype(BF16), m, preferred_element_type=F32)
        y = y + jnp.exp(row) * lax.dot_general(h_prev.astype(BF16), cm, (((1,), (1,)), ((), ())),
                                               preferred_element_type=F32)
        y = y + dsk[r:r + 1, 0:1] * xh
        ys.append(y)
        tot = row[:, L - 1:L]
        xw = (xdt * jnp.exp(tot - row)).astype(BF16)
        h_scr[r] = h_prev * jnp.exp(tot) + jnp.dot(xw, bm, preferred_element_type=F32)
    yg = jnp.concatenate(ys, axis=0).T
    z = z_ref[...].astype(F32)
    yz = yg * (z * _sigmoid(z))
    ms = jnp.mean(yz * yz, axis=-1, keepdims=True)
    y_ref[...] = (yz * lax.rsqrt(ms + EPS) * ng_ref[...]).astype(y_ref.dtype)
    hf_ref[...] = h_scr[...]


def _per_group_lanes(v):
    out = jnp.zeros((SSD_GROUPS, 1, LANES), F32)
    return out.at[:, 0, :SSD_HPG].set(v.astype(F32).reshape(SSD_GROUPS, SSD_HPG))


def _per_group_rows(v):
    out = jnp.zeros((SSD_GROUPS, SUBLANES, LANES), F32)
    return out.at[:, :SSD_HPG, :].set(
        jnp.broadcast_to(v.astype(F32).reshape(SSD_GROUPS, SSD_HPG, 1), (SSD_GROUPS, SSD_HPG, LANES)))


def _ssd(proj, dt_raw, h0, conv_prev, cw_g, cb_g, dt_bias, a_neg, d_skip, norm_g, *, n_seq, t, l, row0,
         shared_state):
    nc = t // l
    rb0 = row0 // l
    long_chunks = l % LANES == 0

    def rows(s, g, c):
        return rb0 + s * nc + c

    def sidx(s):
        return 0 if shared_state else s

    if long_chunks:
        head_rows = SUBLANES
        head_par = [_per_group_rows(v) for v in (dt_bias, a_neg, d_skip)]
        step = jnp.arange(l, dtype=I32)
        tri = (step[:, None] <= step[None, :]).astype(F32)
        shift = jnp.stack([(step[None, :] == step[:, None] - d) for d in range(1, CONV_W)]).astype(BF16)
        extra_args = [tri, shift]
        extra_specs = [pl.BlockSpec((l, l), lambda s, g, c: (0, 0)),
                       pl.BlockSpec((CONV_W - 1, l, l), lambda s, g, c: (0, 0, 0))]
        body, conv_rows = _ssd_long_kernel, 2 * SUBLANES
    else:
        head_rows = 1
        head_par = [_per_group_lanes(v) for v in (dt_bias, a_neg, d_skip)]
        extra_args, extra_specs = [], []
        body, conv_rows = _ssd_kernel, l + SUBLANES
    head_spec = pl.BlockSpec((None, head_rows, LANES), lambda s, g, c: (g, 0, 0))

    return pl.pallas_call(
        body,
        grid=(n_seq, SSD_GROUPS, nc),
        in_specs=[
            pl.BlockSpec((l, GROUP_W), lambda s, g, c: (rows(s, g, c), COL_X // GROUP_W + g)),
            pl.BlockSpec((l, SSD_STATE), lambda s, g, c: (rows(s, g, c), COL_B // SSD_STATE + g)),
            pl.BlockSpec((l, SSD_STATE), lambda s, g, c: (rows(s, g, c), COL_C // SSD_STATE + g)),
            pl.BlockSpec((l, GROUP_W), lambda s, g, c: (rows(s, g, c), COL_Z // GROUP_W + g)),
            pl.BlockSpec((l, LANES), lambda s, g, c: (rows(s, g, c), g)),
            pl.BlockSpec((None, SSD_HPG, SSD_HEAD_DIM, SSD_STATE), lambda s, g, c: (sidx(s), g, 0, 0)),
            pl.BlockSpec((None, None, SUBLANES, CONV_GW), lambda s, g, c: (sidx(s), g, 0, 0)),
            pl.BlockSpec((None, CONV_W, CONV_GW), lambda s, g, c: (g, 0, 0)),
            pl.BlockSpec((None, 1, CONV_GW), lambda s, g, c: (g, 0, 0)),
            head_spec, head_spec, head_spec,
            pl.BlockSpec((1, GROUP_W), lambda s, g, c: (0, g)),
        ] + extra_specs,
        out_specs=[
            pl.BlockSpec((l, GROUP_W), lambda s, g, c: (s * nc + c, g)),
            pl.BlockSpec((None, SSD_HPG, SSD_HEAD_DIM, SSD_STATE), lambda s, g, c: (s, g, 0, 0)),
            pl.BlockSpec((None, None, SUBLANES, CONV_GW), lambda s, g, c: (s, g, 0, 0)),
        ],
        out_shape=[
            jax.ShapeDtypeStruct((n_seq * t, D_INNER), BF16),
            jax.ShapeDtypeStruct((n_seq, SSD_HEADS, SSD_HEAD_DIM, SSD_STATE), F32),
            jax.ShapeDtypeStruct((n_seq, SSD_GROUPS, SUBLANES, CONV_GW), F32),
        ],
        scratch_shapes=[
            pltpu.VMEM((SSD_HPG, SSD_HEAD_DIM, SSD_STATE), F32),
            pltpu.VMEM((conv_rows, CONV_GW), F32),
        ],
        compiler_params=_cparams(("parallel", "parallel", "arbitrary")),
        name="ssd_long" if long_chunks else "ssd",
    )(proj, proj, proj, proj, dt_raw, h0, conv_prev, cw_g, cb_g, *head_par, norm_g, *extra_args)


def _qk(q, k):
    return lax.dot_general(q, k, (((1,), (1,)), ((), ())), preferred_element_type=F32)


def _attn_prompt_kernel(lam_ref, q_ref, k_ref, vt_ref, mk_ref, mvt_ref, o_ref, sa_scr, sb_scr, acc_scr, m_scr):
    i = pl.program_id(2)
    tq = q_ref.shape[0]
    lam = lam_ref[0]
    q = q_ref[...]
    lane = lax.broadcasted_iota(I32, q.shape, 1)
    zero = jnp.zeros_like(q)
    qm = [jnp.where(lane < ATT_HEAD_DIM, q, zero), jnp.where(lane >= ATT_HEAD_DIM, q, zero)]

    def put_scores(s_ref, j):
        kblk = k_ref[pl.ds(pl.multiple_of(j * tq, tq), tq), :]
        for r in range(2):
            s_ref[r] = _qk(kblk, qm[r])

    def with_ones(vt):
        return jnp.concatenate([vt, jnp.ones((ONES_ROWS, vt.shape[1]), BF16)], axis=0)

    def update(s_ref, j, masked=False):
        vt1 = with_ones(vt_ref[j])
        for r in range(2):
            s = s_ref[r]
            if masked:
                kpos = lax.broadcasted_iota(I32, (tq, tq), 0) // CHUNK
                qpos = lax.broadcasted_iota(I32, (tq, tq), 1) // CHUNK
                s = jnp.where(kpos <= qpos, s, NEG_BIG)
            m_p = m_scr[r]
            m_n = jnp.maximum(m_p, jnp.max(s, axis=0, keepdims=True))
            alpha = jnp.exp2(m_p - m_n)
            p = jnp.exp2((s - m_n).astype(BF16))
            acc_scr[r] = alpha * acc_scr[r] + jnp.dot(vt1, p, preferred_element_type=F32)
            m_scr[r] = m_n

    mk = mk_ref[...]
    mvt1 = with_ones(mvt_ref[...])
    for r in range(2):
        s0 = _qk(mk, qm[r])
        m0 = jnp.max(s0, axis=0, keepdims=True)
        acc_scr[r] = jnp.dot(mvt1, jnp.exp2((s0 - m0).astype(BF16)), preferred_element_type=F32)
        m_scr[r] = m0

    put_scores(sa_scr, 0)

    def body(jj, _):
        j = 2 * jj
        put_scores(sb_scr, j + 1)
        update(sa_scr, j)
        put_scores(sa_scr, j + 2)
        update(sb_scr, j + 1)
        return 0

    lax.fori_loop(0, i // 2, body, 0)

    @pl.when(i % 2 == 0)
    def _():
        update(sa_scr, i, masked=True)

    @pl.when(i % 2 == 1)
    def _():
        put_scores(sb_scr, i)
        update(sa_scr, i - 1)
        update(sb_scr, i, masked=True)

    outs = [acc_scr[r, :ATT_V_DIM, :] / acc_scr[r, ATT_V_DIM:ATT_V_DIM + 1, :] for r in range(2)]
    o_ref[...] = (outs[0] - lam * outs[1]).T.astype(o_ref.dtype)


def _attn_prompt(lam, proj_p, vt_p, proj_s, mvt, *, batch, seq, meta_row0):
    nq = seq // ATT_TQ
    return pl.pallas_call(
        _attn_prompt_kernel,
        grid=(batch, ATT_HEADS, nq),
        in_specs=[
            pl.BlockSpec(memory_space=pltpu.SMEM),
            pl.BlockSpec((ATT_TQ, LANES), lambda b, h, i: (b * nq + i, COL_Q // LANES + h)),
            pl.BlockSpec((seq, LANES), lambda b, h, i: (b, COL_K // LANES + h)),
            pl.BlockSpec((None, None, nq, ATT_V_DIM, ATT_TQ), lambda b, h, i: (b, h, 0, 0, 0)),
            pl.BlockSpec((N_META, LANES), lambda b, h, i: (meta_row0 // N_META, COL_K // LANES + h)),
            pl.BlockSpec((None, ATT_V_DIM, N_META), lambda b, h, i: (h, 0, 0)),
        ],
        out_specs=pl.BlockSpec((ATT_TQ, LANES), lambda b, h, i: (b * nq + i, h)),
        out_shape=jax.ShapeDtypeStruct((batch * seq, ATT_WIDTH), BF16),
        scratch_shapes=[pltpu.VMEM((2, ATT_TQ, ATT_TQ), F32),
                        pltpu.VMEM((2, ATT_TQ, ATT_TQ), F32),
                        pltpu.VMEM((2, ATT_V_DIM + ONES_ROWS, ATT_TQ), F32),
                        pltpu.VMEM((2, 1, ATT_TQ), F32)],
        compiler_params=_cparams(("parallel", "parallel", "arbitrary")),
        name="attn_prompt",
    )(lam, proj_p, proj_p, vt_p, proj_s, mvt)


def _attn_short_kernel(lam_ref, q_ref, kn_ref, vn_ref, *rest, has_cache):
    if has_cache:
        kc_ref, vc_ref, o_ref = rest
    else:
        (o_ref,) = rest
    t = q_ref.shape[0]
    nh = 2 * ATT_HEADS
    lam = lam_ref[0]
    q = q_ref[...].astype(F32)
    qb = jnp.broadcast_to(q[None], (nh, t, ATT_WIDTH)).reshape(nh * t, ATT_WIDTH)
    row_head = lax.broadcasted_iota(I32, (nh * t, ATT_WIDTH), 0) // t
    col_head = lax.broadcasted_iota(I32, (nh * t, ATT_WIDTH), 1) // ATT_HEAD_DIM
    qbd = jnp.where(row_head == col_head, qb, 0.0).astype(BF16)
    s_new = _qk(qbd, kn_ref[...])
    m = jnp.max(s_new, axis=-1, keepdims=True)
    if has_cache:
        s_old = _qk(qbd, kc_ref[...].astype(BF16))
        m = jnp.maximum(m, jnp.max(s_old, axis=-1, keepdims=True))
    p_new = jnp.exp2(s_new - m)
    den = jnp.sum(p_new, axis=-1, keepdims=True)
    acc = jnp.dot(p_new.astype(BF16), vn_ref[...], preferred_element_type=F32)
    if has_cache:
        p_old = jnp.exp2(s_old - m)
        den = den + jnp.sum(p_old, axis=-1, keepdims=True)
        acc = acc + jnp.dot(p_old.astype(BF16), vc_ref[...].astype(BF16), preferred_element_type=F32)
    acc = acc / den
    for h in range(ATT_HEADS):
        cols = slice(h * ATT_V_DIM, (h + 1) * ATT_V_DIM)
        a1 = acc[(2 * h) * t:(2 * h + 1) * t, cols]
        a2 = acc[(2 * h + 1) * t:(2 * h + 2) * t, cols]
        o_ref[:, cols] = (a1 - lam * a2).astype(o_ref.dtype)


def _attn_short(lam, proj_s, cache_k, cache_v, *, n_seq, t, row0):
    has_cache = cache_k is not None
    rb0 = row0 // t
    in_specs = [
        pl.BlockSpec(memory_space=pltpu.SMEM),
        pl.BlockSpec((t, ATT_WIDTH), lambda s: (rb0 + s, COL_Q // ATT_WIDTH)),
        pl.BlockSpec((t, ATT_WIDTH), lambda s: (rb0 + s, COL_K // ATT_WIDTH)),
        pl.BlockSpec((t, ATT_WIDTH), lambda s: (rb0 + s, COL_V // ATT_WIDTH)),
    ]
    args = [lam, proj_s, proj_s, proj_s]
    if has_cache:
        past = cache_k.shape[1]
        in_specs += [pl.BlockSpec((None, past, ATT_WIDTH), lambda s: (s, 0, 0)),
                     pl.BlockSpec((None, past, ATT_WIDTH), lambda s: (s, 0, 0))]
        args += [cache_k, cache_v]
    return pl.pallas_call(
        functools.partial(_attn_short_kernel, has_cache=has_cache),
        grid=(n_seq,),
        in_specs=in_specs,
        out_specs=pl.BlockSpec((t, ATT_WIDTH), lambda s: (s, 0)),
        out_shape=jax.ShapeDtypeStruct((n_seq * t, ATT_WIDTH), BF16),
        compiler_params=_cparams(("parallel",)),
        name="attn_cached" if has_cache else "attn_meta",
    )(*args)


def _merge_kernel(x_ref, yn_ref, o_ref, g1_ref, g2_ref, wso_ref, wao_ref, wo_ref, bg1_ref, bg2_ref,
                  sub_ref, n2_ref, wr_ref, br_ref,
                  h1_ref, u2_ref, eidx_ref, gate_ref, rank_ref, cnt_ref, *, sub_scale):
    tm = x_ref.shape[0]
    y_ssd = jnp.dot(yn_ref[...], wso_ref[...], preferred_element_type=F32)
    o = o_ref[...].astype(F32)
    parts = []
    for h in range(ATT_HEADS):
        oh = o[:, h * ATT_V_DIM:(h + 1) * ATT_V_DIM]
        ms = jnp.mean(oh * oh, axis=-1, keepdims=True)
        parts.append(oh * lax.rsqrt(ms + EPS) * sub_ref[...] * sub_scale)
    on = jnp.concatenate(parts, axis=1).astype(BF16)
    y_att = jnp.dot(on, wao_ref[...], preferred_element_type=F32)
    gs = _sigmoid(g1_ref[...].astype(F32) + bg1_ref[...])
    ga = _sigmoid(g2_ref[...].astype(F32) + bg2_ref[...])
    mix_in = (gs * y_ssd + ga * y_att).astype(BF16)
    h1 = x_ref[...] + jnp.dot(mix_in, wo_ref[...], preferred_element_type=F32)
    h1_ref[...] = h1
    ms = jnp.mean(h1 * h1, axis=-1, keepdims=True)
    u2 = h1 * lax.rsqrt(ms + EPS) * n2_ref[...]
    u2_ref[...] = u2.astype(u2_ref.dtype)

    logits = jnp.dot(u2, wr_ref[...], preferred_element_type=F32, precision=lax.Precision.HIGHEST)
    logits = logits + br_ref[...]
    lane = lax.broadcasted_iota(I32, (tm, LANES), 1).astype(F32)
    work = jnp.where(lane < N_EXPERTS, logits, NEG_BIG)
    vals, idxs, hots = [], [], []
    for _ in range(TOP_K):
        mx = jnp.max(work, axis=-1, keepdims=True)
        ix = jnp.min(jnp.where(work == mx, lane, float(LANES)), axis=-1, keepdims=True)
        hot = lane == ix
        vals.append(mx)
        idxs.append(ix)
        hots.append(hot)
        work = jnp.where(hot, NEG_BIG, work)
    es = [jnp.exp(v - vals[0]) for v in vals]
    den = es[0] + es[1] + es[2] + es[3]
    hot_all = (hots[0] | hots[1] | hots[2] | hots[3])
    ti = lax.broadcasted_iota(I32, (tm, tm), 0)
    si = lax.broadcasted_iota(I32, (tm, tm), 1)
    strict = (si < ti).astype(BF16)
    prefix = jnp.dot(strict, hot_all.astype(BF16), preferred_element_type=F32)
    eidx = jnp.zeros((tm, LANES), F32)
    gate = jnp.zeros((tm, LANES), F32)
    rank = jnp.zeros((tm, LANES), F32)
    for k in range(TOP_K):
        rk = jnp.sum(jnp.where(hots[k], prefix, 0.0), axis=-1, keepdims=True)
        eidx = jnp.where(lane == k, idxs[k], eidx)
        gate = jnp.where(lane == k, es[k] / den, gate)
        rank = jnp.where(lane == k, rk, rank)
    eidx_ref[...] = eidx.astype(I32)
    gate_ref[...] = gate
    rank_ref[...] = rank.astype(I32)
    cnt_ref[...] = jnp.sum(hot_all.astype(F32), axis=0, keepdims=True).astype(I32)


def _merge(x, yn, o, proj, wso, wao, wo, bg1, bg2, sub_g, n2_g, wr, br, *, tm, sub_scale):
    rows = x.shape[0]
    full = lambda shape: pl.BlockSpec(shape, lambda i: (0,) * len(shape))
    tok = lambda w: pl.BlockSpec((tm, w), lambda i: (i, 0))
    return pl.pallas_call(
        functools.partial(_merge_kernel, sub_scale=sub_scale),
        grid=(rows // tm,),
        in_specs=[
            tok(D_MODEL), tok(D_INNER), tok(ATT_WIDTH),
            pl.BlockSpec((tm, D_MODEL), lambda i: (i, COL_G // D_MODEL)),
            pl.BlockSpec((tm, D_MODEL), lambda i: (i, COL_G // D_MODEL + 1)),
            full((D_INNER, D_MODEL)), full((ATT_WIDTH, D_MODEL)), full((D_MODEL, D_MODEL)),
            full((1, D_MODEL)), full((1, D_MODEL)), full((1, ATT_V_DIM)), full((1, D_MODEL)),
            full((D_MODEL, LANES)), full((1, LANES)),
        ],
        out_specs=[tok(D_MODEL), tok(D_MODEL), tok(LANES), tok(LANES), tok(LANES),
                   pl.BlockSpec((None, 1, LANES), lambda i: (i, 0, 0))],
        out_shape=[
            jax.ShapeDtypeStruct((rows, D_MODEL), F32),
            jax.ShapeDtypeStruct((rows, D_MODEL), BF16),
            jax.ShapeDtypeStruct((rows, LANES), I32),
            jax.ShapeDtypeStruct((rows, LANES), F32),
            jax.ShapeDtypeStruct((rows, LANES), I32),
            jax.ShapeDtypeStruct((rows // tm, 1, LANES), I32),
        ],
        compiler_params=_cparams(("parallel",)),
        name="merge_router",
    )(x, yn, o, proj, proj, wso, wao, wo, bg1, bg2, sub_g, n2_g, wr, br)


def _seg_copy(local_ref, lo, hbm_ref, hi, sem, to_hbm):
    loc = local_ref.at[pl.ds(pl.multiple_of(lo, SEG_PAD), SEG_PAD), :]
    hbm = hbm_ref.at[pl.ds(pl.multiple_of(hi, SEG_PAD), SEG_PAD), :]
    return pltpu.make_async_copy(loc, hbm, sem) if to_hbm else pltpu.make_async_copy(hbm, loc, sem)


def _seg_copies_start(tab_ref, local_ref, hbm_ref, sem, to_hbm):
    def per_expert(e, total):
        n, lo, hi = tab_ref[0, e], tab_ref[1, e], tab_ref[2, e]

        def per_copy(c, _):
            _seg_copy(local_ref, lo + c * SEG_PAD, hbm_ref, hi + c * SEG_PAD, sem, to_hbm).start()
            return 0

        lax.fori_loop(0, n, per_copy, 0)
        return total + n

    return lax.fori_loop(0, N_EXPERTS, per_expert, 0)


def _seg_copies_wait(n, local_ref, hbm_ref, sem, to_hbm):
    def one(c, _):
        _seg_copy(local_ref, 0, hbm_ref, 0, sem, to_hbm).wait()
        return 0

    lax.fori_loop(0, n, one, 0)


def _dispatch_kernel(tab_ref, lpos_ref, u_ref, *rest, fill_tails):
    if fill_tails:
        tail_ref, _, xs_ref, loc_scr, zero_scr, sem = rest
    else:
        xs_ref, loc_scr, sem = rest
    lr, tm = loc_scr.shape[0], u_ref.shape[0]
    lpos = lpos_ref[...]
    p = lax.broadcasted_iota(I32, (lr, tm), 0)
    perm = jnp.zeros((lr, tm), F32)
    for k in range(TOP_K):
        perm = jnp.where(p == lpos[k:k + 1, :], 1.0, perm)
    loc_scr[...] = jnp.dot(perm.astype(BF16), u_ref[...], preferred_element_type=F32).astype(loc_scr.dtype)
    n = _seg_copies_start(tab_ref, loc_scr, xs_ref, sem, to_hbm=True)
    if fill_tails:
        zero_scr[...] = jnp.zeros(zero_scr.shape, zero_scr.dtype)

        def per_expert(e, total):
            cnt, hi = tail_ref[0, e], tail_ref[1, e]

            def per_copy(c, _):
                _seg_copy(zero_scr, 0, xs_ref, hi + c * SEG_PAD, sem, True).start()
                return 0

            lax.fori_loop(0, cnt, per_copy, 0)
            return total + cnt

        n = n + lax.fori_loop(0, N_EXPERTS, per_expert, 0)
    _seg_copies_wait(n, loc_scr, xs_ref, sem, to_hbm=True)


def _local_rows(tm):
    return tm * TOP_K + N_EXPERTS * SEG_PAD


def _dispatch(tab, lpos_t, u2, *, tm, xs_rows=None, xs=None, tail=None):
    rows = u2.shape[0]
    in_specs = [
        pl.BlockSpec((None, 3, N_EXPERTS), lambda i: (i, 0, 0), memory_space=pltpu.SMEM),
        pl.BlockSpec((None, TOP_K, tm), lambda i: (i, 0, 0)),
        pl.BlockSpec((tm, D_MODEL), lambda i: (i, 0)),
    ]
    args = [tab, lpos_t, u2]
    scratch = [pltpu.VMEM((_local_rows(tm), D_MODEL), BF16)]
    aliases = {}
    if tail is not None:
        assert xs is not None and rows == tm
        in_specs += [pl.BlockSpec(memory_space=pltpu.SMEM), pl.BlockSpec(memory_space=pl.ANY)]
        args += [tail, xs]
        scratch.append(pltpu.VMEM((SEG_PAD, D_MODEL), BF16))
        aliases = {4: 0}
        xs_rows = xs.shape[0]
    return pl.pallas_call(
        functools.partial(_dispatch_kernel, fill_tails=tail is not None),
        grid=(rows // tm,),
        in_specs=in_specs,
        out_specs=pl.BlockSpec(memory_space=pl.ANY),
        out_shape=jax.ShapeDtypeStruct((xs_rows, D_MODEL), BF16),
        scratch_shapes=scratch + [pltpu.SemaphoreType.DMA(())],
        input_output_aliases=aliases,
        compiler_params=_cparams(("arbitrary",)),
        name="moe_dispatch",
    )(*args)


def _ffn_kernel(be_ref, na_ref, x_ref, wgu_ref, bgu_ref, wd_ref, bd_ref, y_ref, wgu_scr, wd_scr):
    i = pl.program_id(0)
    active = i < na_ref[0]

    @pl.when(active & ((i == 0) | (be_ref[i] != be_ref[jnp.maximum(i - 1, 0)])))
    def _():
        wgu_scr[...] = wgu_ref[...].astype(BF16)
        wd_scr[...] = wd_ref[...].astype(BF16)

    @pl.when(active)
    def _():
        gu = jnp.dot(x_ref[...], wgu_scr[...], preferred_element_type=F32) + bgu_ref[...]
        gate = jnp.minimum(gu[:, :D_FF], SWIGLU_LIMIT)
        up = jnp.clip(gu[:, D_FF:], -SWIGLU_LIMIT, SWIGLU_LIMIT)
        hdn = (up + 1.0) * gate * _sigmoid(SWIGLU_ALPHA * gate)
        y = jnp.dot(hdn.astype(BF16), wd_scr[...], preferred_element_type=F32) + bd_ref[...]
        y_ref[...] = y.astype(y_ref.dtype)


def _ffn(block_exp, n_active, xs, wgu, bgu, wd, bd):
    n_blocks = xs.shape[0] // MOE_BLK

    def blk(i, be, na):
        return jnp.minimum(i, na[0] - 1)

    return pl.pallas_call(
        _ffn_kernel,
        grid_spec=pltpu.PrefetchScalarGridSpec(
            num_scalar_prefetch=2,
            grid=(n_blocks,),
            in_specs=[
                pl.BlockSpec((MOE_BLK, D_MODEL), lambda i, be, na: (blk(i, be, na), 0)),
                pl.BlockSpec((None, D_MODEL, 2 * D_FF), lambda i, be, na: (be[blk(i, be, na)], 0, 0)),
                pl.BlockSpec((None, 1, 2 * D_FF), lambda i, be, na: (be[blk(i, be, na)], 0, 0)),
                pl.BlockSpec((None, D_FF, D_MODEL), lambda i, be, na: (be[blk(i, be, na)], 0, 0)),
                pl.BlockSpec((None, 1, D_MODEL), lambda i, be, na: (be[blk(i, be, na)], 0, 0)),
            ],
            out_specs=pl.BlockSpec((MOE_BLK, D_MODEL), lambda i, be, na: (blk(i, be, na), 0)),
            scratch_shapes=[pltpu.VMEM((D_MODEL, 2 * D_FF), BF16), pltpu.VMEM((D_FF, D_MODEL), BF16)],
        ),
        out_shape=jax.ShapeDtypeStruct(xs.shape, xs.dtype),
        compiler_params=_cparams(("arbitrary",)),
        name="moe_ffn",
    )(block_exp, n_active, xs, wgu, bgu, wd, bd)


def _combine_kernel(tab_ref, lpos_ref, gate_ref, h1_ref, fg_ref, ys_ref, y_ref, loc_scr, sem):
    lr, tm = loc_scr.shape[0], h1_ref.shape[0]
    loc_scr[...] = jnp.zeros(loc_scr.shape, loc_scr.dtype)
    n = _seg_copies_start(tab_ref, loc_scr, ys_ref, sem, to_hbm=False)
    lpos = lpos_ref[...]
    gate = gate_ref[...]
    p = lax.broadcasted_iota(I32, (tm, lr), 1)
    pick = jnp.zeros((tm, lr), F32)
    for k in range(TOP_K):
        pick = jnp.where(p == lpos[:, k:k + 1], gate[:, k:k + 1], pick)
    _seg_copies_wait(n, loc_scr, ys_ref, sem, to_hbm=False)
    h = h1_ref[...] + jnp.dot(pick.astype(BF16), loc_scr[...], preferred_element_type=F32)
    ms = jnp.mean(h * h, axis=-1, keepdims=True)
    y_ref[...] = h * lax.rsqrt(ms + EPS) * fg_ref[...]


def _combine(tab, lpos, gate, h1, fg, ys, *, tm):
    rows = h1.shape[0]
    return pl.pallas_call(
        _combine_kernel,
        grid=(rows // tm,),
        in_specs=[
            pl.BlockSpec((None, 3, N_EXPERTS), lambda i: (i, 0, 0), memory_space=pltpu.SMEM),
            pl.BlockSpec((tm, TOP_K), lambda i: (i, 0)),
            pl.BlockSpec((tm, LANES), lambda i: (i, 0)),
            pl.BlockSpec((tm, D_MODEL), lambda i: (i, 0)),
            pl.BlockSpec((1, D_MODEL), lambda i: (0, 0)),
            pl.BlockSpec(memory_space=pl.ANY),
        ],
        out_specs=pl.BlockSpec((tm, D_MODEL), lambda i: (i, 0)),
        out_shape=jax.ShapeDtypeStruct((rows, D_MODEL), F32),
        scratch_shapes=[pltpu.VMEM((_local_rows(tm), D_MODEL), ys.dtype), pltpu.SemaphoreType.DMA(())],
        compiler_params=_cparams(("arbitrary",)),
        name="moe_combine",
    )(tab, lpos, gate, h1, fg, ys)


def _rope_tables(pos):
    d = ATT_HEAD_DIM
    inv = ROPE_THETA ** (-jnp.arange(0, d, 2, dtype=F32) / d)
    ang = pos.astype(F32)[:, None] * inv[None, :]
    cos = jnp.cos(ang)
    sin = jnp.sin(ang)
    cos_h = jnp.concatenate([cos, cos], axis=1)
    sin_h = jnp.concatenate([-sin, sin], axis=1)
    return jnp.tile(cos_h, (1, LANES // d)), jnp.tile(sin_h, (1, LANES // d))


def _conv_by_group(a):
    lead = a.shape[:-1]
    x = a[..., :D_INNER].reshape(lead + (SSD_GROUPS, GROUP_W))
    b = a[..., D_INNER:D_INNER + SSD_GN].reshape(lead + (SSD_GROUPS, SSD_STATE))
    c = a[..., D_INNER + SSD_GN:].reshape(lead + (SSD_GROUPS, SSD_STATE))
    return jnp.concatenate([x, b, c], axis=-1)


def _conv_from_group(a):
    lead = a.shape[:-2]
    x = a[..., :GROUP_W].reshape(lead + (D_INNER,))
    b = a[..., GROUP_W:GROUP_W + SSD_STATE].reshape(lead + (SSD_GN,))
    c = a[..., GROUP_W + SSD_STATE:].reshape(lead + (SSD_GN,))
    return jnp.concatenate([x, b, c], axis=-1)


def _conv_prev_blocks(prev):
    g = jnp.moveaxis(_conv_by_group(prev.astype(F32)), 1, 2)
    return jnp.pad(g, ((0, 0), (0, 0), (SUBLANES - (CONV_W - 1), 0), (0, 0)))


def _conv_tail_rows(ct):
    return _conv_from_group(jnp.moveaxis(ct[:, :, SUBLANES - (CONV_W - 1):, :], 1, 2))


def kernel(x_prompt, x_sample, cache_k, cache_v, state_ssm, state_conv, meta_tokens, norm1_g, w_in, conv_w, conv_b, dt_bias, a_log, d_skip, ssd_norm_g, lambda_q1, lambda_k1, lambda_q2, lambda_k2, subln_g, w_ssd_out, w_att_out, b_gate, w_o, norm2_g, w_router, b_router, w_gu, b_gu, w_down, b_down, final_norm_g):
    batch, seq, _ = x_prompt.shape
    dbatch, dseq, _ = x_sample.shape
    past = cache_k.shape[2]
    depth = norm1_g.shape[0]
    assert depth == 1 and dseq == N_META
    assert seq % SSD_L == 0 and seq % ATT_TQ == 0 and seq % TOKEN_TM == 0 and ATT_TQ % CHUNK == 0
    assert TOKEN_TM % ATT_TQ == 0 and PROJ_TM % ATT_TQ == 0
    n_p = batch * seq
    n_dec = dbatch * dseq
    n_s = n_dec + N_META
    lam_init = 0.8 - 0.6 * math.exp(-0.3 * 0)
    l = 0

    wi = w_in[l]
    o_z, o_xbc, o_dt = 0, D_INNER, D_INNER + CONV_DIM
    o_q = o_dt + SSD_HEADS
    w_main = jnp.concatenate([wi[:, o_z:o_xbc], wi[:, o_xbc:o_dt], wi[:, o_q:]], axis=1).astype(BF16)
    w_dt = jnp.zeros((D_MODEL, SSD_GROUPS, LANES), F32).at[:, :, :SSD_HPG].set(
        wi[:, o_dt:o_q].reshape(D_MODEL, SSD_GROUPS, SSD_HPG)).reshape(D_MODEL, SSD_GROUPS * LANES).astype(BF16)
    g1 = norm1_g[l].reshape(1, D_MODEL)
    cw_g = jnp.moveaxis(_conv_by_group(conv_w[l]), 0, 1)
    cb_g = _conv_by_group(conv_b[l])[:, None, :]
    a_neg = -jnp.exp(a_log[l].astype(F32))
    norm_g = ssd_norm_g[l].reshape(1, D_INNER)
    lam = (jnp.exp(jnp.sum(lambda_q1[l].astype(F32) * lambda_k1[l].astype(F32)))
           - jnp.exp(jnp.sum(lambda_q2[l].astype(F32) * lambda_k2[l].astype(F32))) + lam_init).reshape(1)
    wso = w_ssd_out[l].astype(BF16)
    wao = w_att_out[l].astype(BF16)
    wo = w_o[l].astype(BF16)
    bg1 = b_gate[l][:D_MODEL].reshape(1, D_MODEL)
    bg2 = b_gate[l][D_MODEL:].reshape(1, D_MODEL)
    sub_g = subln_g[l].reshape(1, ATT_V_DIM)
    n2_g = norm2_g[l].reshape(1, D_MODEL)
    wr = jnp.zeros((D_MODEL, LANES), F32).at[:, :N_EXPERTS].set(w_router[l])
    br = jnp.zeros((1, LANES), F32).at[0, :N_EXPERTS].set(b_router[l])
    wgu = w_gu[l]
    bgu = b_gu[l][:, None, :]
    wd = w_down[l]
    bd = b_down[l][:, None, :]
    fg = final_norm_g.reshape(1, D_MODEL)

    xp = x_prompt.reshape(n_p, D_MODEL)
    xs_rows = jnp.concatenate([x_sample.reshape(n_dec, D_MODEL), meta_tokens.astype(x_prompt.dtype)], axis=0)
    cos_p, sin_p = _rope_tables(N_META + jnp.arange(seq, dtype=I32))
    pos_s = jnp.concatenate([jnp.tile(past + jnp.arange(dseq, dtype=I32), dbatch), jnp.arange(N_META, dtype=I32)])
    cos_s, sin_s = _rope_tables(pos_s)

    tm_p = PROJ_TM if seq % PROJ_TM == 0 else TOKEN_TM
    proj_p, dt_p, vt_p, kc_p, vc_p = _in_proj(xp, g1, w_main, w_dt, cos_p, sin_p, tm_p, seq // tm_p, vt_block=ATT_TQ)
    proj_s, dt_s = _in_proj(xs_rows, g1, w_main, w_dt, cos_s, sin_s, n_s, 1)

    ssd_args = (cw_g, cb_g, dt_bias[l], a_neg, d_skip[l], norm_g)
    zero_h = jnp.zeros((1, SSD_HEADS, SSD_HEAD_DIM, SSD_STATE), F32)
    zero_c = jnp.zeros((1, SSD_GROUPS, SUBLANES, CONV_GW), F32)
    yn_m, h_m, ct_m = _ssd(proj_s, dt_s, zero_h, zero_c, *ssd_args, n_seq=1, t=N_META, l=N_META,
                           row0=n_dec, shared_state=True)
    yn_p, h_p, ct_p = _ssd(proj_p, dt_p, h_m, ct_m, *ssd_args, n_seq=batch, t=seq, l=SSD_L, row0=0,
                           shared_state=True)
    yn_d, h_d, ct_d = _ssd(proj_s, dt_s, state_ssm[l].astype(F32), _conv_prev_blocks(state_conv[l]), *ssd_args,
                           n_seq=dbatch, t=dseq, l=dseq, row0=0, shared_state=False)
    yn_s = jnp.concatenate([yn_d, yn_m], axis=0)

    mvt =proj_s[n_dec:n_s, COL_V:COL_V + ATT_WIDTH].reshape(N_META, ATT_HEADS, ATT_V_DIM).transpose(1, 2, 0)
    o_p = _attn_prompt(lam, proj_p, vt_p, proj_s, mvt, batch=batch, seq=seq, meta_row0=n_dec)
    o_d = _attn_short(lam, proj_s, cache_k[l].reshape(dbatch, past, ATT_WIDTH),
                      cache_v[l].reshape(dbatch, past, ATT_WIDTH), n_seq=dbatch, t=dseq, row0=0)
    o_m = _attn_short(lam, proj_s, None, None, n_seq=1, t=N_META, row0=n_dec)
    o_s = jnp.concatenate([o_d, o_m], axis=0)

    merge_w = (wso, wao, wo, bg1, bg2, sub_g, n2_g, wr, br)
    h1_p, u2_p, e_p, gt_p, rk_p, cnt_p = _merge(xp, yn_p, o_p, proj_p, *merge_w, tm=TOKEN_TM,
                                                sub_scale=1.0 - lam_init)
    h1_s, u2_s, e_s, gt_s, rk_s, cnt_s = _merge(xs_rows, yn_s, o_s, proj_s, *merge_w, tm=n_s,
                                                sub_scale=1.0 - lam_init)

    nb_p = n_p // TOKEN_TM
    cnt = jnp.concatenate([cnt_p[:, 0, :N_EXPERTS], cnt_s[:, 0, :N_EXPERTS]], axis=0)
    seg = (cnt + SEG_PAD - 1) // SEG_PAD * SEG_PAD
    local_start = jnp.cumsum(seg, axis=1) - seg
    per_expert = jnp.sum(seg, axis=0)
    padded = (per_expert + MOE_BLK - 1) // MOE_BLK * MOE_BLK
    pend = jnp.cumsum(padded)
    pstart = pend - padded
    hbm_start = pstart[None, :] + jnp.cumsum(seg, axis=0) - seg
    tab = jnp.stack([seg // SEG_PAD, local_start, hbm_start], axis=1).astype(I32)
    n_rows_max = (n_p + n_s) * TOP_K + (nb_p + 1) * N_EXPERTS * (SEG_PAD - 1)
    n_blocks = -(-n_rows_max // MOE_BLK) + N_EXPERTS
    block_start = jnp.arange(n_blocks, dtype=I32) * MOE_BLK
    block_exp = jnp.minimum(jnp.sum((pend[None, :] <= block_start[:, None]).astype(I32), axis=1), N_EXPERTS - 1)
    n_active = (pend[-1:] // MOE_BLK).astype(I32)

    def local_rows_of(e, rk, starts):
        hot = e[:, :TOP_K, None] == jnp.arange(N_EXPERTS, dtype=I32)
        return (jnp.sum(jnp.where(hot, starts[:, None, :], 0), axis=-1) + rk[:, :TOP_K]).astype(I32)

    lpos_p = local_rows_of(e_p, rk_p, jnp.repeat(local_start[:nb_p], TOKEN_TM, axis=0))
    lpos_s = local_rows_of(e_s, rk_s, jnp.repeat(local_start[nb_p:], n_s, axis=0))
    lpos_pt = lpos_p.reshape(nb_p, TOKEN_TM, TOP_K).transpose(0, 2, 1)
    lpos_st = lpos_s.reshape(1, n_s, TOP_K).transpose(0, 2, 1)
    tail = jnp.stack([(padded - per_expert) // SEG_PAD, pstart + per_expert]).astype(I32)
    xs = _dispatch(tab[:nb_p], lpos_pt, u2_p, tm=TOKEN_TM, xs_rows=n_blocks * MOE_BLK)
    xs = _dispatch(tab[nb_p:], lpos_st, u2_s, tm=n_s, xs=xs, tail=tail)
    ys = _ffn(block_exp, n_active, xs, wgu, bgu, wd, bd)
    y_p = _combine(tab[:nb_p], lpos_p, gt_p, h1_p, fg, ys, tm=TOKEN_TM)
    y_s = _combine(tab[nb_p:], lpos_s, gt_s, h1_s, fg, ys, tm=n_s)

    def kv_rows(proj, col, lo, hi):
        return proj[lo:hi, col:col + ATT_WIDTH].astype(F32)

    def with_meta(cache, col):
        meta = jnp.broadcast_to(kv_rows(proj_s, col, n_dec, n_s)[None], (batch, N_META, ATT_WIDTH))
        return cache.reshape(batch, N_META + seq, ATT_WIDTH).at[:, :N_META].set(meta)

    new_k_p = with_meta(kc_p, COL_K).reshape(1, batch, N_META + seq, 2 * ATT_HEADS, ATT_HEAD_DIM)
    new_v_p = with_meta(vc_p, COL_V).reshape(1, batch, N_META + seq, ATT_HEADS, ATT_V_DIM)
    new_k_s = kv_rows(proj_s, COL_K, 0, n_dec).reshape(1, dbatch, dseq, 2 * ATT_HEADS, ATT_HEAD_DIM)
    new_v_s = kv_rows(proj_s, COL_V, 0, n_dec).reshape(1, dbatch, dseq, ATT_HEADS, ATT_V_DIM)
    return (y_p.reshape(batch, seq, D_MODEL),
            y_s[:n_dec].reshape(dbatch, dseq, D_MODEL),
            new_k_p, new_v_p,
            h_p.astype(state_ssm.dtype)[None],
            _conv_tail_rows(ct_p).astype(x_prompt.dtype)[None],
            new_k_s, new_v_s,
            h_d.astype(state_ssm.dtype)[None],
            _conv_tail_rows(ct_d).astype(x_sample.dtype)[None])
```

```python
import functools
import math

import jax
import jax.numpy as jnp
from jax import lax
from jax.experimental import pallas as pl
from jax.experimental.pallas import tpu as pltpu

F32 = jnp.float32
BF16 = jnp.bfloat16
I32 = jnp.int32

D_MODEL = 1024
D_INNER = 2048
SSD_HEADS = 32
SSD_HEAD_DIM = 64
SSD_GROUPS = 8
SSD_HPG = SSD_HEADS // SSD_GROUPS
SSD_STATE = 128
SSD_GN = SSD_GROUPS * SSD_STATE
CONV_W = 4
CONV_DIM = D_INNER + 2 * SSD_GN
ATT_HEADS = 8
ATT_HEAD_DIM = 64
ATT_V_DIM = 128
ATT_WIDTH = 1024
CHUNK = 64
N_META = 16
EPS = 1e-6
ROPE_THETA = 10000.0
LOG2_E = math.log2(math.e)
N_EXPERTS = 32
TOP_K = 4
D_FF = 1024
SWIGLU_LIMIT = 7.0
SWIGLU_ALPHA = 1.702

COL_Z = 0
COL_X = COL_Z + D_INNER
COL_B = COL_X + D_INNER
COL_C = COL_B + SSD_GN
COL_Q = COL_C + SSD_GN
COL_K = COL_Q + ATT_WIDTH
COL_V = COL_K + ATT_WIDTH
COL_G = COL_V + ATT_WIDTH
N_MAIN = COL_G + 2 * D_MODEL

LANES = 128
SUBLANES = 8
GROUP_W = D_INNER // SSD_GROUPS
CONV_GW = GROUP_W + 2 * SSD_STATE
PROJ_TN = 1024
PROJ_TM = 1024
TOKEN_TM = 512
SSD_L = 256
ATT_TQ = 512
MOE_BLK = 512
SEG_PAD = 16
ONES_ROWS = 16
VMEM_LIMIT = 56 * 1024 * 1024
NEG_BIG = -1e30


def _cparams(sem):
    return pltpu.CompilerParams(dimension_semantics=sem, vmem_limit_bytes=VMEM_LIMIT)


def _sigmoid(x):
    return 1.0 / (1.0 + jnp.exp(-x))


def _inproj_kernel(x_ref, g_ref, w_ref, wdt_ref, cos_ref, sin_ref, o_ref, dt_ref, *rest, emit_vt,
                   seq_blocks):
    if emit_vt:
        vt_ref, kout_ref, vout_ref, u_scr, kbuf, vbuf, sems = rest
    else:
        (u_scr,) = rest
    j = pl.program_id(1)

    @pl.when(j == 0)
    def _():
        x = x_ref[...]
        ms = jnp.mean(x * x, axis=-1, keepdims=True)
        u = (x * lax.rsqrt(ms + EPS) * g_ref[...]).astype(BF16)
        u_scr[...] = u
        dt_ref[...] = jnp.dot(u, wdt_ref[...], preferred_element_type=F32)

    acc = jnp.dot(u_scr[...], w_ref[...], preferred_element_type=F32)
    is_q = j == COL_Q // PROJ_TN
    is_k = j == COL_K // PROJ_TN

    @pl.when(is_q | is_k)
    def _():
        cos = cos_ref[...]
        sin = sin_ref[...]
        lane = lax.broadcasted_iota(I32, cos.shape, 1)
        half = ATT_HEAD_DIM // 2
        first_half = (lane % ATT_HEAD_DIM) < half
        scale = jnp.where(is_q, ATT_HEAD_DIM ** -0.5 * LOG2_E, 1.0).astype(F32)
        for c in range(PROJ_TN // LANES):
            a = acc[:, c * LANES:(c + 1) * LANES]
            swapped = jnp.where(first_half, pltpu.roll(a, LANES - half, 1), pltpu.roll(a, half, 1))
            roped = (a * cos + swapped * sin) * scale
            o_ref[:, c * LANES:(c + 1) * LANES] = roped.astype(o_ref.dtype)
            if emit_vt:
                kbuf[:, c * LANES:(c + 1) * LANES] = roped

    @pl.when(jnp.logical_not(is_q | is_k))
    def _():
        o_ref[...] = acc.astype(o_ref.dtype)

    if emit_vt:
        i = pl.program_id(0)
        tm = x_ref.shape[0]
        row0 = (i // seq_blocks) * (N_META + seq_blocks * tm) + N_META + (i % seq_blocks) * tm

        def cache_copy(buf, out_ref, sem):
            per_row = buf.shape[0] // tm
            rows = out_ref.at[pl.ds(pl.multiple_of(row0 * per_row, SUBLANES), tm * per_row), :]
            return pltpu.make_async_copy(buf, rows, sem)

        @pl.when(is_k)
        def _():
            cache_copy(kbuf, kout_ref, sems.at[0]).start()

        @pl.when(j == COL_V // PROJ_TN)
        def _():
            for h in range(ATT_HEADS):
                vbuf[pl.ds(h, tm, stride=ATT_HEADS), :] = acc[:, h * ATT_V_DIM:(h + 1) * ATT_V_DIM]
            cache_copy(vbuf, vout_ref, sems.at[1]).start()
            tk = vt_ref.shape[-1]
            for h in range(ATT_HEADS):
                for s in range(vt_ref.shape[1]):
                    blk = acc[s * tk:(s + 1) * tk, h * ATT_V_DIM:(h + 1) * ATT_V_DIM]
                    vt_ref[h, s] = blk.T.astype(vt_ref.dtype)

        @pl.when(j == pl.num_programs(1) - 1)
        def _():
            cache_copy(kbuf, kout_ref, sems.at[0]).wait()
            cache_copy(vbuf, vout_ref, sems.at[1]).wait()


def _in_proj(x, g1, w_main, w_dt, cos_t, sin_t, tm, rope_blocks, vt_block=None):
    rows = x.shape[0]
    grid = (rows // tm, N_MAIN // PROJ_TN)
    scratch = [pltpu.VMEM((tm, D_MODEL), BF16)]
    out_specs = [
        pl.BlockSpec((tm, PROJ_TN), lambda i, j: (i, j)),
        pl.BlockSpec((tm, SSD_GROUPS * LANES), lambda i, j: (i, 0)),
    ]
    out_shape = [
        jax.ShapeDtypeStruct((rows, N_MAIN), BF16),
        jax.ShapeDtypeStruct((rows, SSD_GROUPS * LANES), F32),
    ]
    if vt_block is not None:
        per = tm // vt_block
        out_specs.append(pl.BlockSpec((None, ATT_HEADS, per, ATT_V_DIM, vt_block),
                                      lambda i, j: (i // rope_blocks, 0, i % rope_blocks, 0, 0)))
        streams = rows // (rope_blocks * tm)
        out_shape.append(jax.ShapeDtypeStruct(
            (streams, ATT_HEADS, rope_blocks * per, ATT_V_DIM, vt_block), BF16))
        cache_rows = streams * (N_META + rope_blocks * tm)
        out_specs += [pl.BlockSpec(memory_space=pl.ANY), pl.BlockSpec(memory_space=pl.ANY)]
        out_shape += [jax.ShapeDtypeStruct((cache_rows, ATT_WIDTH), F32),
                      jax.ShapeDtypeStruct((cache_rows * ATT_HEADS, ATT_V_DIM), F32)]
        scratch += [pltpu.VMEM((tm, ATT_WIDTH), F32), pltpu.VMEM((tm * ATT_HEADS, ATT_V_DIM), F32),
                    pltpu.SemaphoreType.DMA((2,))]
    return pl.pallas_call(
        functools.partial(_inproj_kernel, emit_vt=vt_block is not None, seq_blocks=rope_blocks),
        grid=grid,
        in_specs=[
            pl.BlockSpec((tm, D_MODEL), lambda i, j: (i, 0)),
            pl.BlockSpec((1, D_MODEL), lambda i, j: (0, 0)),
            pl.BlockSpec((D_MODEL, PROJ_TN), lambda i, j: (0, j)),
            pl.BlockSpec((D_MODEL, SSD_GROUPS * LANES), lambda i, j: (0, 0)),
            pl.BlockSpec((tm, LANES), lambda i, j: (i % rope_blocks, 0)),
            pl.BlockSpec((tm, LANES), lambda i, j: (i % rope_blocks, 0)),
        ],
        out_specs=out_specs,
        out_shape=out_shape,
        scratch_shapes=scratch,
        compiler_params=_cparams(("parallel", "arbitrary")),
        name="in_proj",
    )(x, g1, w_main, w_dt, cos_t, sin_t)


def _ssd_kernel(x_ref, b_ref, c_ref, z_ref, dt_ref, h0_ref, cp_ref, cw_ref, cb_ref, dtb_ref,
                aneg_ref, dsk_ref, ng_ref, y_ref, hf_ref, ct_ref, h_scr, f_scr):
    c = pl.program_id(2)
    L = x_ref.shape[0]

    @pl.when(c == 0)
    def _():
        h_scr[...] = h0_ref[...]
        f_scr[0:SUBLANES, :] = cp_ref[...]

    f_scr[SUBLANES:SUBLANES + L, 0:GROUP_W] = x_ref[...].astype(F32)
    f_scr[SUBLANES:SUBLANES + L, GROUP_W:GROUP_W + SSD_STATE] = b_ref[...].astype(F32)
    f_scr[SUBLANES:SUBLANES + L, GROUP_W + SSD_STATE:CONV_GW] = c_ref[...].astype(F32)
    w = cw_ref[...]
    acc = cb_ref[...]
    for i in range(CONV_W):
        lo = SUBLANES - (CONV_W - 1) + i
        acc = acc + w[i:i + 1, :] * f_scr[lo:lo + L, :]
    xc = acc * _sigmoid(acc)
    tail = f_scr[L:L + SUBLANES, :]
    f_scr[0:SUBLANES, :] = tail
    ct_ref[...] = tail

    xg = xc[:, 0:GROUP_W]
    bm = xc[:, GROUP_W:GROUP_W + SSD_STATE].astype(BF16)
    cm = xc[:, GROUP_W + SSD_STATE:CONV_GW].astype(BF16)

    dtr = dt_ref[...] + dtb_ref[...]
    dt = jnp.maximum(dtr, 0.0) + jnp.log(1.0 + jnp.exp(-jnp.abs(dtr)))
    da = dt * aneg_ref[...]
    ti = lax.broadcasted_iota(I32, (L, L), 0)
    si = lax.broadcasted_iota(I32, (L, L), 1)
    causal = si <= ti
    tril = causal.astype(F32)
    cum = jnp.dot(tril, da, preferred_element_type=F32, precision=lax.Precision.HIGHEST)
    sel = (lax.broadcasted_iota(I32, (SUBLANES, LANES), 0)
           == lax.broadcasted_iota(I32, (SUBLANES, LANES), 1)).astype(F32)
    cum_t = lax.dot_general(sel, cum, (((1,), (1,)), ((), ())), preferred_element_type=F32,
                            precision=lax.Precision.HIGHEST)

    cb = lax.dot_general(cm, bm, (((1,), (1,)), ((), ())), preferred_element_type=F32)
    dsk = dsk_ref[...]
    ys = []
    for r in range(SSD_HPG):
        col = cum[:, r:r + 1]
        row = cum_t[r:r + 1, :]
        dec = jnp.exp(jnp.where(causal, col - row, NEG_BIG))
        m = (cb * dec).astype(BF16)
        xh = xg[:, r * SSD_HEAD_DIM:(r + 1) * SSD_HEAD_DIM]
        xdt = xh * dt[:, r:r + 1]
        h_prev = h_scr[r]
        y = jnp.dot(m, xdt.astype(BF16), preferred_element_type=F32)
        y = y + jnp.exp(col) * lax.dot_general(cm, h_prev.astype(BF16), (((1,), (1,)), ((), ())),
                                               preferred_element_type=F32)
        y = y + dsk[:, r:r + 1] * xh
        ys.append(y)
        tot = cum[L - 1:L, r:r + 1]
        xw = (xdt * jnp.exp(tot - col)).astype(BF16)
        upd = lax.dot_general(xw, bm, (((0,), (0,)), ((), ())), preferred_element_type=F32)
        h_scr[r] = h_prev * jnp.exp(tot) + upd
    yg = jnp.concatenate(ys, axis=1)
    z = z_ref[...].astype(F32)
    yz = yg * (z * _sigmoid(z))
    ms = jnp.mean(yz * yz, axis=-1, keepdims=True)
    y_ref[...] = (yz * lax.rsqrt(ms + EPS) * ng_ref[...]).astype(y_ref.dtype)
    hf_ref[...] = h_scr[...]


def _ssd_long_kernel(x_ref, b_ref, c_ref, z_ref, dt_ref, h0_ref, cp_ref, cw_ref, cb_ref, dtb_ref,
                     aneg_ref, dsk_ref, ng_ref, tri_ref, shift_ref, y_ref, hf_ref, ct_ref, h_scr, f_scr):
    c = pl.program_id(2)
    L = x_ref.shape[0]
    reps = L // LANES

    @pl.when(c == 0)
    def _():
        h_scr[...] = h0_ref[...]
        f_scr[0:SUBLANES, :] = cp_ref[...]
        f_scr[SUBLANES:2 * SUBLANES, :] = jnp.zeros((SUBLANES, CONV_GW), F32)

    xb = jnp.concatenate([x_ref[...], b_ref[...], c_ref[...]], axis=1)
    xf = xb.astype(F32)
    w = cw_ref[...]
    acc = cb_ref[...] + w[CONV_W - 1:CONV_W, :] * xf
    for d in range(1, CONV_W):
        sh = jnp.dot(shift_ref[d - 1], xb, preferred_element_type=F32)
        acc = acc + w[CONV_W - 1 - d:CONV_W - d, :] * sh
    corr = jnp.zeros((SUBLANES, CONV_GW), F32)
    for i in range(CONV_W - 1):
        lo = SUBLANES - (CONV_W - 1) + i
        corr = corr + w[i:i + 1, :] * f_scr[lo:lo + SUBLANES, :]
    acc = jnp.concatenate([acc[0:SUBLANES] + corr, acc[SUBLANES:]], axis=0)
    xc = acc * _sigmoid(acc)
    tail = xf[L - SUBLANES:L, :]
    f_scr[0:SUBLANES, :] = tail
    ct_ref[...] = tail

    x_t = xc[:, 0:GROUP_W].T
    bm = xc[:, GROUP_W:GROUP_W + SSD_STATE].astype(BF16)
    cm = xc[:, GROUP_W + SSD_STATE:CONV_GW].astype(BF16)

    dtr = dt_ref[...].T[0:SUBLANES, :] + jnp.tile(dtb_ref[...], (1, reps))
    dt = jnp.maximum(dtr, 0.0) + jnp.log(1.0 + jnp.exp(-jnp.abs(dtr)))
    da = dt * jnp.tile(aneg_ref[...], (1, reps))
    tri = tri_ref[...]
    tri_b = tri.astype(BF16)
    visible = tri > 0.5
    d1 = da.astype(BF16)
    r1 = da - d1.astype(F32)
    d2 = r1.astype(BF16)
    d3 = (r1 - d2.astype(F32)).astype(BF16)
    cum = (jnp.dot(d1, tri_b, preferred_element_type=F32) + jnp.dot(d2, tri_b, preferred_element_type=F32)
           + jnp.dot(d3, tri_b, preferred_element_type=F32))
    cum_col = cum.T

    cb_t = lax.dot_general(bm, cm, (((1,), (1,)), ((), ())), preferred_element_type=F32)
    dsk = dsk_ref[...]
    ys = []
    for r in range(SSD_HPG):
        row = cum[r:r + 1, :]
        dec = jnp.exp(jnp.where(visible, row - cum_col[:, r:r + 1], NEG_BIG))
        m = (cb_t * dec).astype(BF16)
        xh = x_t[r * SSD_HEAD_DIM:(r + 1) * SSD_HEAD_DIM, :]
        xdt = xh * dt[r:r + 1, :]
        h_prev = h_scr[r]
        y = jnp.dot(xdt.astype(BF16), m, preferred_element_type=F32)
        y = y + jnp.exp(row) * lax.dot_general(h_prev.astype(BF16), cm, (((1,), (1,)), ((), ())),
                                               preferred_element_type=F32)
        y = y + dsk[r:r + 1, 0:1] * xh
        ys.append(y)
        tot = row[:, L - 1:L]
        xw = (xdt * jnp.exp(tot - row)).astype(BF16)
        h_scr[r] = h_prev * jnp.exp(tot) + jnp.dot(xw, bm, preferred_element_type=F32)
    yg = jnp.concatenate(ys, axis=0).T
    z = z_ref[...].astype(F32)
    yz = yg * (z * _sigmoid(z))
    ms = jnp.mean(yz * yz, axis=-1, keepdims=True)
    y_ref[...] = (yz * lax.rsqrt(ms + EPS) * ng_ref[...]).astype(y_ref.dtype)
    hf_ref[...] = h_scr[...]


def _per_group_lanes(v):
    out = jnp.zeros((SSD_GROUPS, 1, LANES), F32)
    return out.at[:, 0, :SSD_HPG].set(v.astype(F32).reshape(SSD_GROUPS, SSD_HPG))


def _per_group_rows(v):
    out = jnp.zeros((SSD_GROUPS, SUBLANES, LANES), F32)
    return out.at[:, :SSD_HPG, :].set(
        jnp.broadcast_to(v.astype(F32).reshape(SSD_GROUPS, SSD_HPG, 1), (SSD_GROUPS, SSD_HPG, LANES)))


def _ssd(proj, dt_raw, h0, conv_prev, cw_g, cb_g, dt_bias, a_neg, d_skip, norm_g, *, n_seq, t, l, row0,
         shared_state):
    nc = t // l
    rb0 = row0 // l
    long_chunks = l % LANES == 0

    def rows(s, g, c):
        return rb0 + s * nc + c

    def sidx(s):
        return 0 if shared_state else s

    if long_chunks:
        head_rows = SUBLANES
        head_par = [_per_group_rows(v) for v in (dt_bias, a_neg, d_skip)]
        step = jnp.arange(l, dtype=I32)
        tri = (step[:, None] <= step[None, :]).astype(F32)
        shift = jnp.stack([(step[None, :] == step[:, None] - d) for d in range(1, CONV_W)]).astype(BF16)
        extra_args = [tri, shift]
        extra_specs = [pl.BlockSpec((l, l), lambda s, g, c: (0, 0)),
                       pl.BlockSpec((CONV_W - 1, l, l), lambda s, g, c: (0, 0, 0))]
        body, conv_rows = _ssd_long_kernel, 2 * SUBLANES
    else:
        head_rows = 1
        head_par = [_per_group_lanes(v) for v in (dt_bias, a_neg, d_skip)]
        extra_args, extra_specs = [], []
        body, conv_rows = _ssd_kernel, l + SUBLANES
    head_spec = pl.BlockSpec((None, head_rows, LANES), lambda s, g, c: (g, 0, 0))

    return pl.pallas_call(
        body,
        grid=(n_seq, SSD_GROUPS, nc),
        in_specs=[
            pl.BlockSpec((l, GROUP_W), lambda s, g, c: (rows(s, g, c), COL_X // GROUP_W + g)),
            pl.BlockSpec((l, SSD_STATE), lambda s, g, c: (rows(s, g, c), COL_B // SSD_STATE + g)),
            pl.BlockSpec((l, SSD_STATE), lambda s, g, c: (rows(s, g, c), COL_C // SSD_STATE + g)),
            pl.BlockSpec((l, GROUP_W), lambda s, g, c: (rows(s, g, c), COL_Z // GROUP_W + g)),
            pl.BlockSpec((l, LANES), lambda s, g, c: (rows(s, g, c), g)),
            pl.BlockSpec((None, SSD_HPG, SSD_HEAD_DIM, SSD_STATE), lambda s, g, c: (sidx(s), g, 0, 0)),
            pl.BlockSpec((None, None, SUBLANES, CONV_GW), lambda s, g, c: (sidx(s), g, 0, 0)),
            pl.BlockSpec((None, CONV_W, CONV_GW), lambda s, g, c: (g, 0, 0)),
            pl.BlockSpec((None, 1, CONV_GW), lambda s, g, c: (g, 0, 0)),
            head_spec, head_spec, head_spec,
            pl.BlockSpec((1, GROUP_W), lambda s, g, c: (0, g)),
        ] + extra_specs,
        out_specs=[
            pl.BlockSpec((l, GROUP_W), lambda s, g, c: (s * nc + c, g)),
            pl.BlockSpec((None, SSD_HPG, SSD_HEAD_DIM, SSD_STATE), lambda s, g, c: (s, g, 0, 0)),
            pl.BlockSpec((None, None, SUBLANES, CONV_GW), lambda s, g, c: (s, g, 0, 0)),
        ],
        out_shape=[
            jax.ShapeDtypeStruct((n_seq * t, D_INNER), BF16),
            jax.ShapeDtypeStruct((n_seq, SSD_HEADS, SSD_HEAD_DIM, SSD_STATE), F32),
            jax.ShapeDtypeStruct((n_seq, SSD_GROUPS, SUBLANES, CONV_GW), F32),
        ],
        scratch_shapes=[
            pltpu.VMEM((SSD_HPG, SSD_HEAD_DIM, SSD_STATE), F32),
            pltpu.VMEM((conv_rows, CONV_GW), F32),
        ],
        compiler_params=_cparams(("parallel", "parallel", "arbitrary")),
        name="ssd_long" if long_chunks else "ssd",
    )(proj, proj, proj, proj, dt_raw, h0, conv_prev, cw_g, cb_g, *head_par, norm_g, *extra_args)


def _qk(q, k):
    return lax.dot_general(q, k, (((1,), (1,)), ((), ())), preferred_element_type=F32)


def _attn_prompt_kernel(lam_ref, q_ref, k_ref, vt_ref, mk_ref, mvt_ref, o_ref, sa_scr, sb_scr, acc_scr, m_scr):
    i = pl.program_id(2)
    tq = q_ref.shape[0]
    lam = lam_ref[0]
    q = q_ref[...]
    lane = lax.broadcasted_iota(I32, q.shape, 1)
    zero = jnp.zeros_like(q)
    qm = [jnp.where(lane < ATT_HEAD_DIM, q, zero), jnp.where(lane >= ATT_HEAD_DIM, q, zero)]

    def put_scores(s_ref, j):
        kblk = k_ref[pl.ds(pl.multiple_of(j * tq, tq), tq), :]
        for r in range(2):
            s_ref[r] = _qk(kblk, qm[r])

    def with_ones(vt):
        return jnp.concatenate([vt, jnp.ones((ONES_ROWS, vt.shape[1]), BF16)], axis=0)

    def update(s_ref, j, last=False):
        vt1 = with_ones(vt_ref[j])
        for r in range(2):
            s = s_ref[r]
            m_p = m_scr[r]
            if last:
                kpos = lax.broadcasted_iota(I32, (tq, tq), 0) // CHUNK
                qpos = lax.broadcasted_iota(I32, (tq, tq), 1) // CHUNK
                s = jnp.where(kpos <= qpos, s, NEG_BIG)
                s_meta = _qk(mk_ref[...], qm[r])
                m_p = jnp.maximum(m_p, jnp.max(s_meta, axis=0, keepdims=True))
            m_n = jnp.maximum(m_p, jnp.max(s, axis=0, keepdims=True))
            alpha = jnp.exp2(m_scr[r] - m_n)
            p = jnp.exp2((s - m_n).astype(BF16))
            acc = alpha * acc_scr[r] + jnp.dot(vt1, p, preferred_element_type=F32)
            if last:
                p_meta = jnp.exp2((s_meta - m_n).astype(BF16))
                acc = acc + jnp.dot(with_ones(mvt_ref[...]), p_meta, preferred_element_type=F32)
            acc_scr[r] = acc
            m_scr[r] = m_n

    m_scr[...] = jnp.full(m_scr.shape, NEG_BIG, F32)
    acc_scr[...] = jnp.zeros(acc_scr.shape, F32)

    put_scores(sa_scr, 0)

    def body(jj, _):
        j = 2 * jj
        put_scores(sb_scr, j + 1)
        update(sa_scr, j)
        put_scores(sa_scr, j + 2)
        update(sb_scr, j + 1)
        return 0

    lax.fori_loop(0, i // 2, body, 0)

    @pl.when(i % 2 == 0)
    def _():
        update(sa_scr, i, last=True)

    @pl.when(i % 2 == 1)
    def _():
        put_scores(sb_scr, i)
        update(sa_scr, i - 1)
        update(sb_scr, i, last=True)

    outs = [acc_scr[r, :ATT_V_DIM, :] / acc_scr[r, ATT_V_DIM:ATT_V_DIM + 1, :] for r in range(2)]
    o_ref[...] = (outs[0] - lam * outs[1]).T.astype(o_ref.dtype)


def _attn_prompt(lam, proj_p, vt_p, proj_s, mvt, *, batch, seq, meta_row0):
    nq = seq // ATT_TQ
    return pl.pallas_call(
        _attn_prompt_kernel,
        grid=(batch, ATT_HEADS, nq),
        in_specs=[
            pl.BlockSpec(memory_space=pltpu.SMEM),
            pl.BlockSpec((ATT_TQ, LANES), lambda b, h, i: (b * nq + i, COL_Q // LANES + h)),
            pl.BlockSpec((seq, LANES), lambda b, h, i: (b, COL_K // LANES + h)),
            pl.BlockSpec((None, None, nq, ATT_V_DIM, ATT_TQ), lambda b, h, i: (b, h, 0, 0, 0)),
            pl.BlockSpec((N_META, LANES), lambda b, h, i: (meta_row0 // N_META, COL_K // LANES + h)),
            pl.BlockSpec((None, ATT_V_DIM, N_META), lambda b, h, i: (h, 0, 0)),
        ],
        out_specs=pl.BlockSpec((ATT_TQ, LANES), lambda b, h, i: (b * nq + i, h)),
        out_shape=jax.ShapeDtypeStruct((batch * seq, ATT_WIDTH), BF16),
        scratch_shapes=[pltpu.VMEM((2, ATT_TQ, ATT_TQ), F32),
                        pltpu.VMEM((2, ATT_TQ, ATT_TQ), F32),
                        pltpu.VMEM((2, ATT_V_DIM + ONES_ROWS, ATT_TQ), F32),
                        pltpu.VMEM((2, 1, ATT_TQ), F32)],
        compiler_params=_cparams(("parallel", "parallel", "arbitrary")),
        name="attn_prompt",
    )(lam, proj_p, proj_p, vt_p, proj_s, mvt)


def _attn_short_kernel(lam_ref, q_ref, kn_ref, vn_ref, *rest, has_cache):
    if has_cache:
        kc_ref, vc_ref, o_ref = rest
    else:
        (o_ref,) = rest
    t = q_ref.shape[0]
    nh = 2 * ATT_HEADS
    lam = lam_ref[0]
    q = q_ref[...].astype(F32)
    qb = jnp.broadcast_to(q[None], (nh, t, ATT_WIDTH)).reshape(nh * t, ATT_WIDTH)
    row_head = lax.broadcasted_iota(I32, (nh * t, ATT_WIDTH), 0) // t
    col_head = lax.broadcasted_iota(I32, (nh * t, ATT_WIDTH), 1) // ATT_HEAD_DIM
    qbd = jnp.where(row_head == col_head, qb, 0.0).astype(BF16)
    s_new = _qk(qbd, kn_ref[...])
    m = jnp.max(s_new, axis=-1, keepdims=True)
    if has_cache:
        s_old = _qk(qbd, kc_ref[...].astype(BF16))
        m = jnp.maximum(m, jnp.max(s_old, axis=-1, keepdims=True))
    p_new = jnp.exp2(s_new - m)
    den = jnp.sum(p_new, axis=-1, keepdims=True)
    acc = jnp.dot(p_new.astype(BF16), vn_ref[...], preferred_element_type=F32)
    if has_cache:
        p_old = jnp.exp2(s_old - m)
        den = den + jnp.sum(p_old, axis=-1, keepdims=True)
        acc = acc + jnp.dot(p_old.astype(BF16), vc_ref[...].astype(BF16), preferred_element_type=F32)
    acc = acc / den
    for h in range(ATT_HEADS):
        cols = slice(h * ATT_V_DIM, (h + 1) * ATT_V_DIM)
        a1 = acc[(2 * h) * t:(2 * h + 1) * t, cols]
        a2 = acc[(2 * h + 1) * t:(2 * h + 2) * t, cols]
        o_ref[:, cols] = (a1 - lam * a2).astype(o_ref.dtype)


def _attn_short(lam, proj_s, cache_k, cache_v, *, n_seq, t, row0):
    has_cache = cache_k is not None
    rb0 = row0 // t
    in_specs = [
        pl.BlockSpec(memory_space=pltpu.SMEM),
        pl.BlockSpec((t, ATT_WIDTH), lambda s: (rb0 + s, COL_Q // ATT_WIDTH)),
        pl.BlockSpec((t, ATT_WIDTH), lambda s: (rb0 + s, COL_K // ATT_WIDTH)),
        pl.BlockSpec((t, ATT_WIDTH), lambda s: (rb0 + s, COL_V // ATT_WIDTH)),
    ]
    args = [lam, proj_s, proj_s, proj_s]
    if has_cache:
        past = cache_k.shape[1]
        in_specs += [pl.BlockSpec((None, past, ATT_WIDTH), lambda s: (s, 0, 0)),
                     pl.BlockSpec((None, past, ATT_WIDTH), lambda s: (s, 0, 0))]
        args += [cache_k, cache_v]
    return pl.pallas_call(
        functools.partial(_attn_short_kernel, has_cache=has_cache),
        grid=(n_seq,),
        in_specs=in_specs,
        out_specs=pl.BlockSpec((t, ATT_WIDTH), lambda s: (s, 0)),
        out_shape=jax.ShapeDtypeStruct((n_seq * t, ATT_WIDTH), BF16),
        compiler_params=_cparams(("parallel",)),
        name="attn_cached" if has_cache else "attn_meta",
    )(*args)


def _merge_kernel(x_ref, yn_ref, o_ref, g1_ref, g2_ref, wso_ref, wao_ref, wo_ref, bg1_ref, bg2_ref,
                  sub_ref, n2_ref, wr_ref, br_ref,
                  h1_ref, u2_ref, eidx_ref, gate_ref, rank_ref, cnt_ref, *, sub_scale):
    tm = x_ref.shape[0]
    y_ssd = jnp.dot(yn_ref[...], wso_ref[...], preferred_element_type=F32)
    o = o_ref[...].astype(F32)
    parts = []
    for h in range(ATT_HEADS):
        oh = o[:, h * ATT_V_DIM:(h + 1) * ATT_V_DIM]
        ms = jnp.mean(oh * oh, axis=-1, keepdims=True)
        parts.append(oh * lax.rsqrt(ms + EPS) * sub_ref[...] * sub_scale)
    on = jnp.concatenate(parts, axis=1).astype(BF16)
    y_att = jnp.dot(on, wao_ref[...], preferred_element_type=F32)
    gs = _sigmoid(g1_ref[...].astype(F32) + bg1_ref[...])
    ga = _sigmoid(g2_ref[...].astype(F32) + bg2_ref[...])
    mix_in = (gs * y_ssd + ga * y_att).astype(BF16)
    h1 = x_ref[...] + jnp.dot(mix_in, wo_ref[...], preferred_element_type=F32)
    h1_ref[...] = h1
    ms = jnp.mean(h1 * h1, axis=-1, keepdims=True)
    u2 = h1 * lax.rsqrt(ms + EPS) * n2_ref[...]
    u2_ref[...] = u2.astype(u2_ref.dtype)

    logits = lax.dot_general(wr_ref[...], u2, (((1,), (1,)), ((), ())), preferred_element_type=F32,
                             precision=lax.Precision.HIGHEST)
    work = (logits + br_ref[...])[0:N_EXPERTS, :]
    expert = lax.broadcasted_iota(I32, (N_EXPERTS, tm), 0).astype(F32)
    vals, idxs, hots = [], [], []
    for _ in range(TOP_K):
        mx = jnp.max(work, axis=0, keepdims=True)
        ix = jnp.min(jnp.where(work == mx, expert, float(N_EXPERTS)), axis=0, keepdims=True)
        hot = expert == ix
        vals.append(mx)
        idxs.append(ix)
        hots.append(hot)
        work = jnp.where(hot, NEG_BIG, work)
    es = [jnp.exp(v - vals[0]) for v in vals]
    den = es[0] + es[1] + es[2] + es[3]
    hot_all = jnp.where(hots[0] | hots[1] | hots[2] | hots[3], 1.0, 0.0)
    si = lax.broadcasted_iota(I32, (tm, tm), 0)
    ti = lax.broadcasted_iota(I32, (tm, tm), 1)
    earlier = (si < ti).astype(BF16)
    prefix = jnp.dot(hot_all.astype(BF16), earlier, preferred_element_type=F32)
    row = lax.broadcasted_iota(I32, (SUBLANES, tm), 0)
    eidx = jnp.zeros((SUBLANES, tm), F32)
    gate = jnp.zeros((SUBLANES, tm), F32)
    rank = jnp.zeros((SUBLANES, tm), F32)
    for k in range(TOP_K):
        rk = jnp.sum(jnp.where(hots[k], prefix, 0.0), axis=0, keepdims=True)
        eidx = jnp.where(row == k, idxs[k], eidx)
        gate = jnp.where(row == k, es[k] / den, gate)
        rank = jnp.where(row == k, rk, rank)
    eidx_ref[...] = eidx.astype(I32)
    gate_ref[...] = gate
    rank_ref[...] = rank.astype(I32)
    cnt_ref[...] = jnp.sum(hot_all, axis=1, keepdims=True).astype(I32)


def _merge(x, yn, o, proj, wso, wao, wo, bg1, bg2, sub_g, n2_g, wr_t, br_c, *, tm, sub_scale):
    rows = x.shape[0]
    full = lambda shape: pl.BlockSpec(shape, lambda i: (0,) * len(shape))
    tok = lambda w: pl.BlockSpec((tm, w), lambda i: (i, 0))
    per_block = lambda a, b: pl.BlockSpec((None, a, b), lambda i: (i, 0, 0))
    return pl.pallas_call(
        functools.partial(_merge_kernel, sub_scale=sub_scale),
        grid=(rows // tm,),
        in_specs=[
            tok(D_MODEL), tok(D_INNER), tok(ATT_WIDTH),
            pl.BlockSpec((tm, D_MODEL), lambda i: (i, COL_G // D_MODEL)),
            pl.BlockSpec((tm, D_MODEL), lambda i: (i, COL_G // D_MODEL + 1)),
            full((D_INNER, D_MODEL)), full((ATT_WIDTH, D_MODEL)), full((D_MODEL, D_MODEL)),
            full((1, D_MODEL)), full((1, D_MODEL)), full((1, ATT_V_DIM)), full((1, D_MODEL)),
            full((LANES, D_MODEL)), full((LANES, 1)),
        ],
        out_specs=[tok(D_MODEL), tok(D_MODEL), per_block(SUBLANES, tm), per_block(SUBLANES, tm),
                   per_block(SUBLANES, tm), per_block(N_EXPERTS, 1)],
        out_shape=[
            jax.ShapeDtypeStruct((rows, D_MODEL), F32),
            jax.ShapeDtypeStruct((rows, D_MODEL), BF16),
            jax.ShapeDtypeStruct((rows // tm, SUBLANES, tm), I32),
            jax.ShapeDtypeStruct((rows // tm, SUBLANES, tm), F32),
            jax.ShapeDtypeStruct((rows // tm, SUBLANES, tm), I32),
            jax.ShapeDtypeStruct((rows // tm, N_EXPERTS, 1), I32),
        ],
        compiler_params=_cparams(("parallel",)),
        name="merge_router",
    )(x, yn, o, proj, proj, wso, wao, wo, bg1, bg2, sub_g, n2_g, wr_t, br_c)


def _seg_copy(local_ref, lo, hbm_ref, hi, sem, to_hbm):
    loc = local_ref.at[pl.ds(pl.multiple_of(lo, SEG_PAD), SEG_PAD), :]
    hbm = hbm_ref.at[pl.ds(pl.multiple_of(hi, SEG_PAD), SEG_PAD), :]
    return pltpu.make_async_copy(loc, hbm, sem) if to_hbm else pltpu.make_async_copy(hbm, loc, sem)


def _seg_copies_start(tab_ref, local_ref, hbm_ref, sem, to_hbm):
    def per_expert(e, total):
        n, lo, hi = tab_ref[0, e], tab_ref[1, e], tab_ref[2, e]

        def per_copy(c, _):
            _seg_copy(local_ref, lo + c * SEG_PAD, hbm_ref, hi + c * SEG_PAD, sem, to_hbm).start()
            return 0

        lax.fori_loop(0, n, per_copy, 0)
        return total + n

    return lax.fori_loop(0, N_EXPERTS, per_expert, 0)


def _seg_copies_wait(n, local_ref, hbm_ref, sem, to_hbm):
    def one(c, _):
        _seg_copy(local_ref, 0, hbm_ref, 0, sem, to_hbm).wait()
        return 0

    lax.fori_loop(0, n, one, 0)


def _dispatch_kernel(tab_ref, lpos_ref, u_ref, *rest, fill_tails):
    if fill_tails:
        tail_ref, _, xs_ref, loc_scr, zero_scr, sem = rest
    else:
        xs_ref, loc_scr, sem = rest
    lr, tm = loc_scr.shape[0], u_ref.shape[0]
    lpos = lpos_ref[...]
    p = lax.broadcasted_iota(I32, (lr, tm), 0)
    perm = jnp.zeros((lr, tm), F32)
    for k in range(TOP_K):
        perm = jnp.where(p == lpos[k:k + 1, :], 1.0, perm)
    loc_scr[...] = jnp.dot(perm.astype(BF16), u_ref[...], preferred_element_type=F32).astype(loc_scr.dtype)
    n = _seg_copies_start(tab_ref, loc_scr, xs_ref, sem, to_hbm=True)
    if fill_tails:
        zero_scr[...] = jnp.zeros(zero_scr.shape, zero_scr.dtype)

        def per_expert(e, total):
            cnt, hi = tail_ref[0, e], tail_ref[1, e]

            def per_copy(c, _):
                _seg_copy(zero_scr, 0, xs_ref, hi + c * SEG_PAD, sem, True).start()
                return 0

            lax.fori_loop(0, cnt, per_copy, 0)
            return total + cnt

        n = n + lax.fori_loop(0, N_EXPERTS, per_expert, 0)
    _seg_copies_wait(n, loc_scr, xs_ref, sem, to_hbm=True)


def _local_rows(tm):
    return tm * TOP_K + N_EXPERTS * SEG_PAD


def _dispatch(tab, lpos_t, u2, *, tm, xs_rows=None, xs=None, tail=None):
    rows = u2.shape[0]
    in_specs = [
        pl.BlockSpec((None, 3, N_EXPERTS), lambda i: (i, 0, 0), memory_space=pltpu.SMEM),
        pl.BlockSpec((None, TOP_K, tm), lambda i: (i, 0, 0)),
        pl.BlockSpec((tm, D_MODEL), lambda i: (i, 0)),
    ]
    args = [tab, lpos_t, u2]
    scratch = [pltpu.VMEM((_local_rows(tm), D_MODEL), BF16)]
    aliases = {}
    if tail is not None:
        assert xs is not None and rows == tm
        in_specs += [pl.BlockSpec(memory_space=pltpu.SMEM), pl.BlockSpec(memory_space=pl.ANY)]
        args += [tail, xs]
        scratch.append(pltpu.VMEM((SEG_PAD, D_MODEL), BF16))
        aliases = {4: 0}
        xs_rows = xs.shape[0]
    return pl.pallas_call(
        functools.partial(_dispatch_kernel, fill_tails=tail is not None),
        grid=(rows // tm,),
        in_specs=in_specs,
        out_specs=pl.BlockSpec(memory_space=pl.ANY),
        out_shape=jax.ShapeDtypeStruct((xs_rows, D_MODEL), BF16),
        scratch_shapes=scratch + [pltpu.SemaphoreType.DMA(())],
        input_output_aliases=aliases,
        compiler_params=_cparams(("arbitrary",)),
        name="moe_dispatch",
    )(*args)


def _ffn_kernel(be_ref, na_ref, x_ref, wgu_ref, bgu_ref, wd_ref, bd_ref, y_ref, wgu_scr, wd_scr):
    i = pl.program_id(0)
    active = i < na_ref[0]

    @pl.when(active & ((i == 0) | (be_ref[i] != be_ref[jnp.maximum(i - 1, 0)])))
    def _():
        wgu_scr[...] = wgu_ref[...].astype(BF16)
        wd_scr[...] = wd_ref[...].astype(BF16)

    @pl.when(active)
    def _():
        gu = jnp.dot(x_ref[...], wgu_scr[...], preferred_element_type=F32) + bgu_ref[...]
        gate = jnp.minimum(gu[:, :D_FF], SWIGLU_LIMIT)
        up = jnp.clip(gu[:, D_FF:], -SWIGLU_LIMIT, SWIGLU_LIMIT)
        hdn = (up + 1.0) * gate * _sigmoid(SWIGLU_ALPHA * gate)
        y = jnp.dot(hdn.astype(BF16), wd_scr[...], preferred_element_type=F32) + bd_ref[...]
        y_ref[...] = y.astype(y_ref.dtype)


def _ffn(block_exp, n_active, xs, wgu, bgu, wd, bd):
    n_blocks = xs.shape[0] // MOE_BLK

    def blk(i, be, na):
        return jnp.minimum(i, na[0] - 1)

    return pl.pallas_call(
        _ffn_kernel,
        grid_spec=pltpu.PrefetchScalarGridSpec(
            num_scalar_prefetch=2,
            grid=(n_blocks,),
            in_specs=[
                pl.BlockSpec((MOE_BLK, D_MODEL), lambda i, be, na: (blk(i, be, na), 0)),
                pl.BlockSpec((None, D_MODEL, 2 * D_FF), lambda i, be, na: (be[blk(i, be, na)], 0, 0)),
                pl.BlockSpec((None, 1, 2 * D_FF), lambda i, be, na: (be[blk(i, be, na)], 0, 0)),
                pl.BlockSpec((None, D_FF, D_MODEL), lambda i, be, na: (be[blk(i, be, na)], 0, 0)),
                pl.BlockSpec((None, 1, D_MODEL), lambda i, be, na: (be[blk(i, be, na)], 0, 0)),
            ],
            out_specs=pl.BlockSpec((MOE_BLK, D_MODEL), lambda i, be, na: (blk(i, be, na), 0)),
            scratch_shapes=[pltpu.VMEM((D_MODEL, 2 * D_FF), BF16), pltpu.VMEM((D_FF, D_MODEL), BF16)],
        ),
        out_shape=jax.ShapeDtypeStruct(xs.shape, xs.dtype),
        compiler_params=_cparams(("arbitrary",)),
        name="moe_ffn",
    )(block_exp, n_active, xs, wgu, bgu, wd, bd)


def _combine_kernel(tab_ref, lpos_ref, gate_ref, h1_ref, fg_ref, ys_ref, y_ref, loc_scr, sem):
    lr, tm = loc_scr.shape[0], h1_ref.shape[0]
    loc_scr[...] = jnp.zeros(loc_scr.shape, loc_scr.dtype)
    n = _seg_copies_start(tab_ref, loc_scr, ys_ref, sem, to_hbm=False)
    lpos = lpos_ref[...]
    gate = gate_ref[...]
    p = lax.broadcasted_iota(I32, (tm, lr), 1)
    pick = jnp.zeros((tm, lr), F32)
    for k in range(TOP_K):
        pick = jnp.where(p == lpos[:, k:k + 1], gate[:, k:k + 1], pick)
    _seg_copies_wait(n, loc_scr, ys_ref, sem, to_hbm=False)
    h = h1_ref[...] + jnp.dot(pick.astype(BF16), loc_scr[...], preferred_element_type=F32)
    ms = jnp.mean(h * h, axis=-1, keepdims=True)
    y_ref[...] = h * lax.rsqrt(ms + EPS) * fg_ref[...]


def _combine(tab, lpos, gate, h1, fg, ys, *, tm):
    rows = h1.shape[0]
    return pl.pallas_call(
        _combine_kernel,
        grid=(rows // tm,),
        in_specs=[
            pl.BlockSpec((None, 3, N_EXPERTS), lambda i: (i, 0, 0), memory_space=pltpu.SMEM),
            pl.BlockSpec((tm, TOP_K), lambda i: (i, 0)),
            pl.BlockSpec((tm, TOP_K), lambda i: (i, 0)),
            pl.BlockSpec((tm, D_MODEL), lambda i: (i, 0)),
            pl.BlockSpec((1, D_MODEL), lambda i: (0, 0)),
            pl.BlockSpec(memory_space=pl.ANY),
        ],
        out_specs=pl.BlockSpec((tm, D_MODEL), lambda i: (i, 0)),
        out_shape=jax.ShapeDtypeStruct((rows, D_MODEL), F32),
        scratch_shapes=[pltpu.VMEM((_local_rows(tm), D_MODEL), ys.dtype), pltpu.SemaphoreType.DMA(())],
        compiler_params=_cparams(("arbitrary",)),
        name="moe_combine",
    )(tab, lpos, gate, h1, fg, ys)


def _rope_tables(pos):
    d = ATT_HEAD_DIM
    inv = ROPE_THETA ** (-jnp.arange(0, d, 2, dtype=F32) / d)
    ang = pos.astype(F32)[:, None] * inv[None, :]
    cos = jnp.cos(ang)
    sin = jnp.sin(ang)
    cos_h = jnp.concatenate([cos, cos], axis=1)
    sin_h = jnp.concatenate([-sin, sin], axis=1)
    return jnp.tile(cos_h, (1, LANES // d)), jnp.tile(sin_h, (1, LANES // d))


def _conv_by_group(a):
    lead = a.shape[:-1]
    x = a[..., :D_INNER].reshape(lead + (SSD_GROUPS, GROUP_W))
    b = a[..., D_INNER:D_INNER + SSD_GN].reshape(lead + (SSD_GROUPS, SSD_STATE))
    c = a[..., D_INNER + SSD_GN:].reshape(lead + (SSD_GROUPS, SSD_STATE))
    return jnp.concatenate([x, b, c], axis=-1)


def _conv_from_group(a):
    lead = a.shape[:-2]
    x = a[..., :GROUP_W].reshape(lead + (D_INNER,))
    b = a[..., GROUP_W:GROUP_W + SSD_STATE].reshape(lead + (SSD_GN,))
    c = a[..., GROUP_W + SSD_STATE:].reshape(lead + (SSD_GN,))
    return jnp.concatenate([x, b, c], axis=-1)


def _conv_prev_blocks(prev):
    g = jnp.moveaxis(_conv_by_group(prev.astype(F32)), 1, 2)
    return jnp.pad(g, ((0, 0), (0, 0), (SUBLANES - (CONV_W - 1), 0), (0, 0)))


def _conv_tail_rows(ct):
    return _conv_from_group(jnp.moveaxis(ct[:, :, SUBLANES - (CONV_W - 1):, :], 1, 2))


def kernel(x_prompt, x_sample, cache_k, cache_v, state_ssm, state_conv, meta_tokens, norm1_g, w_in, conv_w, conv_b, dt_bias, a_log, d_skip, ssd_norm_g, lambda_q1, lambda_k1, lambda_q2, lambda_k2, subln_g, w_ssd_out, w_att_out, b_gate, w_o, norm2_g, w_router, b_router, w_gu, b_gu, w_down, b_down, final_norm_g):
    batch, seq, _ = x_prompt.shape
    dbatch, dseq, _ = x_sample.shape
    past = cache_k.shape[2]
    depth = norm1_g.shape[0]
    assert depth == 1 and dseq == N_META
    assert seq % SSD_L == 0 and seq % ATT_TQ == 0 and seq % TOKEN_TM == 0 and ATT_TQ % CHUNK == 0
    assert TOKEN_TM % ATT_TQ == 0 and PROJ_TM % ATT_TQ == 0
    n_p = batch * seq
    n_dec = dbatch * dseq
    n_s = n_dec + N_META
    lam_init = 0.8 - 0.6 * math.exp(-0.3 * 0)
    l = 0

    wi = w_in[l]
    o_z, o_xbc, o_dt = 0, D_INNER, D_INNER + CONV_DIM
    o_q = o_dt + SSD_HEADS
    w_main = jnp.concatenate([wi[:, o_z:o_xbc], wi[:, o_xbc:o_dt], wi[:, o_q:]], axis=1).astype(BF16)
    w_dt = jnp.zeros((D_MODEL, SSD_GROUPS, LANES), F32).at[:, :, :SSD_HPG].set(
        wi[:, o_dt:o_q].reshape(D_MODEL, SSD_GROUPS, SSD_HPG)).reshape(D_MODEL, SSD_GROUPS * LANES).astype(BF16)
    g1 = norm1_g[l].reshape(1, D_MODEL)
    cw_g = jnp.moveaxis(_conv_by_group(conv_w[l]), 0, 1)
    cb_g = _conv_by_group(conv_b[l])[:, None, :]
    a_neg = -jnp.exp(a_log[l].astype(F32))
    norm_g = ssd_norm_g[l].reshape(1, D_INNER)
    lam = (jnp.exp(jnp.sum(lambda_q1[l].astype(F32) * lambda_k1[l].astype(F32)))
           - jnp.exp(jnp.sum(lambda_q2[l].astype(F32) * lambda_k2[l].astype(F32))) + lam_init).reshape(1)
    wso = w_ssd_out[l].astype(BF16)
    wao = w_att_out[l].astype(BF16)
    wo = w_o[l].astype(BF16)
    bg1 = b_gate[l][:D_MODEL].reshape(1, D_MODEL)
    bg2 = b_gate[l][D_MODEL:].reshape(1, D_MODEL)
    sub_g = subln_g[l].reshape(1, ATT_V_DIM)
    n2_g = norm2_g[l].reshape(1, D_MODEL)
    wr = jnp.zeros((LANES, D_MODEL), F32).at[:N_EXPERTS, :].set(w_router[l].T)
    br = jnp.zeros((LANES, 1), F32).at[:N_EXPERTS, 0].set(b_router[l])
    wgu = w_gu[l]
    bgu = b_gu[l][:, None, :]
    wd = w_down[l]
    bd = b_down[l][:, None, :]
    fg = final_norm_g.reshape(1, D_MODEL)

    xp = x_prompt.reshape(n_p, D_MODEL)
    xs_rows = jnp.concatenate([x_sample.reshape(n_dec, D_MODEL), meta_tokens.astype(x_prompt.dtype)], axis=0)
    cos_p, sin_p = _rope_tables(N_META + jnp.arange(seq, dtype=I32))
    pos_s = jnp.concatenate([jnp.tile(past + jnp.arange(dseq, dtype=I32), dbatch), jnp.arange(N_META, dtype=I32)])
    cos_s, sin_s = _rope_tables(pos_s)

    tm_p = PROJ_TM if seq % PROJ_TM == 0 else TOKEN_TM
    proj_p, dt_p, vt_p, kc_p, vc_p = _in_proj(xp, g1, w_main, w_dt, cos_p, sin_p, tm_p, seq // tm_p, vt_block=ATT_TQ)
    proj_s, dt_s = _in_proj(xs_rows, g1, w_main, w_dt, cos_s, sin_s, n_s, 1)

    ssd_args = (cw_g, cb_g, dt_bias[l], a_neg, d_skip[l], norm_g)
    zero_h = jnp.zeros((1, SSD_HEADS, SSD_HEAD_DIM, SSD_STATE), F32)
    zero_c = jnp.zeros((1, SSD_GROUPS, SUBLANES, CONV_GW), F32)
    yn_m, h_m, ct_m = _ssd(proj_s, dt_s, zero_h, zero_c, *ssd_args, n_seq=1, t=N_META, l=N_META,
                           row0=n_dec, shared_state=True)
    yn_p, h_p, ct_p = _ssd(proj_p, dt_p, h_m, ct_m, *ssd_args, n_seq=batch, t=seq, l=SSD_L, row0=0,
                           shared_state=True)
    yn_d, h_d, ct_d = _ssd(proj_s, dt_s, state_ssm[l].astype(F32), _conv_prev_blocks(state_conv[l]), *ssd_args,
                           n_seq=dbatch, t=dseq, l=dseq, row0=0, shared_state=False)
    yn_s = jnp.concatenate([yn_d, yn_m], axis=0)

    mvt =proj_s[n_dec:n_s, COL_V:COL_V + ATT_WIDTH].reshape(N_META, ATT_HEADS, ATT_V_DIM).transpose(1, 2, 0)
    o_p = _attn_prompt(lam, proj_p, vt_p, proj_s, mvt, batch=batch, seq=seq, meta_row0=n_dec)
    o_d = _attn_short(lam, proj_s, cache_k[l].reshape(dbatch, past, ATT_WIDTH),
                      cache_v[l].reshape(dbatch, past, ATT_WIDTH), n_seq=dbatch, t=dseq, row0=0)
    o_m = _attn_short(lam, proj_s, None, None, n_seq=1, t=N_META, row0=n_dec)
    o_s = jnp.concatenate([o_d, o_m], axis=0)

    merge_w = (wso, wao, wo, bg1, bg2, sub_g, n2_g, wr, br)
    h1_p, u2_p, e_p, gt_p, rk_p, cnt_p = _merge(xp, yn_p, o_p, proj_p, *merge_w, tm=TOKEN_TM,
                                                sub_scale=1.0 - lam_init)
    h1_s, u2_s, e_s, gt_s, rk_s, cnt_s = _merge(xs_rows, yn_s, o_s, proj_s, *merge_w, tm=n_s,
                                                sub_scale=1.0 - lam_init)

    nb_p = n_p // TOKEN_TM
    cnt = jnp.concatenate([cnt_p[:, :, 0], cnt_s[:, :, 0]], axis=0)
    seg = (cnt + SEG_PAD - 1) // SEG_PAD * SEG_PAD
    local_start = jnp.cumsum(seg, axis=1) - seg
    per_expert = jnp.sum(seg, axis=0)
    padded = (per_expert + MOE_BLK - 1) // MOE_BLK * MOE_BLK
    pend = jnp.cumsum(padded)
    pstart = pend - padded
    hbm_start = pstart[None, :] + jnp.cumsum(seg, axis=0) - seg
    tab = jnp.stack([seg // SEG_PAD, local_start, hbm_start], axis=1).astype(I32)
    n_rows_max = (n_p + n_s) * TOP_K + (nb_p + 1) * N_EXPERTS * (SEG_PAD - 1)
    n_blocks = -(-n_rows_max // MOE_BLK) + N_EXPERTS
    block_start = jnp.arange(n_blocks, dtype=I32) * MOE_BLK
    block_exp = jnp.minimum(jnp.sum((pend[None, :] <= block_start[:, None]).astype(I32), axis=1), N_EXPERTS - 1)
    n_active = (pend[-1:] // MOE_BLK).astype(I32)

    def local_rows_of(e, rk, starts):
        hot = e[:, :TOP_K, :, None] == jnp.arange(N_EXPERTS, dtype=I32)
        return (jnp.sum(jnp.where(hot, starts[:, None, None, :], 0), axis=-1) + rk[:, :TOP_K]).astype(I32)

    def by_token(a):
        return a[:, :TOP_K].transpose(0, 2, 1).reshape(-1, TOP_K)

    lpos_pt = local_rows_of(e_p, rk_p, local_start[:nb_p])
    lpos_st = local_rows_of(e_s, rk_s, local_start[nb_p:])
    lpos_p, lpos_s = by_token(lpos_pt), by_token(lpos_st)
    gt_p, gt_s = by_token(gt_p), by_token(gt_s)
    tail = jnp.stack([(padded - per_expert) // SEG_PAD, pstart + per_expert]).astype(I32)
    xs = _dispatch(tab[:nb_p], lpos_pt, u2_p, tm=TOKEN_TM, xs_rows=n_blocks * MOE_BLK)
    xs = _dispatch(tab[nb_p:], lpos_st, u2_s, tm=n_s, xs=xs, tail=tail)
    ys = _ffn(block_exp, n_active, xs, wgu, bgu, wd, bd)
    y_p = _combine(tab[:nb_p], lpos_p, gt_p, h1_p, fg, ys, tm=TOKEN_TM)
    y_s = _combine(tab[nb_p:], lpos_s, gt_s, h1_s, fg, ys, tm=n_s)

    def kv_rows(proj, col, lo, hi):
        return proj[lo:hi, col:col + ATT_WIDTH].astype(F32)

    def with_meta(cache, col, tail_shape):
        meta = kv_rows(proj_s, col, n_dec, n_s).reshape((1, N_META) + tail_shape)
        slab = cache.reshape((batch, N_META + seq) + tail_shape)
        return slab.at[:, :N_META].set(jnp.broadcast_to(meta, (batch, N_META) + tail_shape))

    new_k_p = with_meta(kc_p, COL_K, (ATT_WIDTH,)).reshape(1, batch, N_META + seq, 2 * ATT_HEADS, ATT_HEAD_DIM)
    new_v_p = with_meta(vc_p, COL_V, (ATT_HEADS, ATT_V_DIM))[None]
    new_k_s = kv_rows(proj_s, COL_K, 0, n_dec).reshape(1, dbatch, dseq, 2 * ATT_HEADS, ATT_HEAD_DIM)
    new_v_s = kv_rows(proj_s, COL_V, 0, n_dec).reshape(1, dbatch, dseq, ATT_HEADS, ATT_V_DIM)
    return (y_p.reshape(batch, seq, D_MODEL),
            y_s[:n_dec].reshape(dbatch, dseq, D_MODEL),
            new_k_p, new_v_p,
            h_p.astype(state_ssm.dtype)[None],
            _conv_tail_rows(ct_p).astype(x_prompt.dtype)[None],
            new_k_s, new_v_s,
            h_d.astype(state_ssm.dtype)[None],
            _conv_tail_rows(ct_d).astype(x_sample.dtype)[None])
```

```python
import functools
import math

import jax
import jax.numpy as jnp
from jax import lax
from jax.experimental import pallas as pl
from jax.experimental.pallas import tpu as pltpu

F32 = jnp.float32
BF16 = jnp.bfloat16
I32 = jnp.int32

D_MODEL = 1024
D_INNER = 2048
SSD_HEADS = 32
SSD_HEAD_DIM = 64
SSD_GROUPS = 8
SSD_HPG = SSD_HEADS // SSD_GROUPS
SSD_STATE = 128
SSD_GN = SSD_GROUPS * SSD_STATE
CONV_W = 4
CONV_DIM = D_INNER + 2 * SSD_GN
ATT_HEADS = 8
ATT_HEAD_DIM = 64
ATT_V_DIM = 128
ATT_WIDTH = 1024
CHUNK = 64
N_META = 16
EPS = 1e-6
ROPE_THETA = 10000.0
LOG2_E = math.log2(math.e)
N_EXPERTS = 32
TOP_K = 4
D_FF = 1024
SWIGLU_LIMIT = 7.0
SWIGLU_ALPHA = 1.702

COL_Z = 0
COL_X = COL_Z + D_INNER
COL_B = COL_X + D_INNER
COL_C = COL_B + SSD_GN
COL_Q = COL_C + SSD_GN
COL_K = COL_Q + ATT_WIDTH
COL_V = COL_K + ATT_WIDTH
COL_G = COL_V + ATT_WIDTH
N_MAIN = COL_G + 2 * D_MODEL

LANES = 128
SUBLANES = 8
GROUP_W = D_INNER // SSD_GROUPS
CONV_GW = GROUP_W + 2 * SSD_STATE
PROJ_TN = 1024
PROJ_TM = 1024
TOKEN_TM = 512
SSD_L = 256
SSD_GP = 4
ATT_TQ = 512
MOE_BLK = 512
SEG_PAD = 16
ONES_ROWS = 16
VMEM_LIMIT = 56 * 1024 * 1024
NEG_BIG = -1e30


def _cparams(sem):
    return pltpu.CompilerParams(dimension_semantics=sem, vmem_limit_bytes=VMEM_LIMIT)


def _sigmoid(x):
    return 1.0 / (1.0 + jnp.exp(-x))


def _inproj_kernel(x_ref, g_ref, w_ref, wdt_ref, cos_ref, sin_ref, o_ref, dt_ref, *rest, emit_vt,
                   seq_blocks):
    if emit_vt:
        vt_ref, kout_ref, vout_ref, u_scr, kbuf, vbuf, sems = rest
    else:
        (u_scr,) = rest
    j = pl.program_id(1)

    @pl.when(j == 0)
    def _():
        x = x_ref[...]
        ms = jnp.mean(x * x, axis=-1, keepdims=True)
        u = (x * lax.rsqrt(ms + EPS) * g_ref[...]).astype(BF16)
        u_scr[...] = u
        dt_ref[...] = jnp.dot(u, wdt_ref[...], preferred_element_type=F32)

    acc = jnp.dot(u_scr[...], w_ref[...], preferred_element_type=F32)
    is_q = j == COL_Q // PROJ_TN
    is_k = j == COL_K // PROJ_TN

    @pl.when(is_q | is_k)
    def _():
        cos = cos_ref[...]
        sin = sin_ref[...]
        lane = lax.broadcasted_iota(I32, cos.shape, 1)
        half = ATT_HEAD_DIM // 2
        first_half = (lane % ATT_HEAD_DIM) < half
        scale = jnp.where(is_q, ATT_HEAD_DIM ** -0.5 * LOG2_E, 1.0).astype(F32)
        for c in range(PROJ_TN // LANES):
            a = acc[:, c * LANES:(c + 1) * LANES]
            swapped = jnp.where(first_half, pltpu.roll(a, LANES - half, 1), pltpu.roll(a, half, 1))
            roped = (a * cos + swapped * sin) * scale
            o_ref[:, c * LANES:(c + 1) * LANES] = roped.astype(o_ref.dtype)
            if emit_vt:
                kbuf[:, c * LANES:(c + 1) * LANES] = roped

    @pl.when(jnp.logical_not(is_q | is_k))
    def _():
        o_ref[...] = acc.astype(o_ref.dtype)

    if emit_vt:
        i = pl.program_id(0)
        tm = x_ref.shape[0]
        row0 = (i // seq_blocks) * (N_META + seq_blocks * tm) + N_META + (i % seq_blocks) * tm

        def cache_copy(buf, out_ref, sem):
            per_row = buf.shape[0] // tm
            rows = out_ref.at[pl.ds(pl.multiple_of(row0 * per_row, SUBLANES), tm * per_row), :]
            return pltpu.make_async_copy(buf, rows, sem)

        @pl.when(is_k)
        def _():
            cache_copy(kbuf, kout_ref, sems.at[0]).start()

        @pl.when(j == COL_V // PROJ_TN)
        def _():
            for h in range(ATT_HEADS):
                vbuf[pl.ds(h, tm, stride=ATT_HEADS), :] = acc[:, h * ATT_V_DIM:(h + 1) * ATT_V_DIM]
            cache_copy(vbuf, vout_ref, sems.at[1]).start()
            tk = vt_ref.shape[-1]
            for h in range(ATT_HEADS):
                for s in range(vt_ref.shape[1]):
                    blk = acc[s * tk:(s + 1) * tk, h * ATT_V_DIM:(h + 1) * ATT_V_DIM]
                    vt_ref[h, s] = blk.T.astype(vt_ref.dtype)

        @pl.when(j == pl.num_programs(1) - 1)
        def _():
            cache_copy(kbuf, kout_ref, sems.at[0]).wait()
            cache_copy(vbuf, vout_ref, sems.at[1]).wait()


def _in_proj(x, g1, w_main, w_dt, cos_t, sin_t, tm, rope_blocks, vt_block=None):
    rows = x.shape[0]
    grid = (rows // tm, N_MAIN // PROJ_TN)
    scratch = [pltpu.VMEM((tm, D_MODEL), BF16)]
    out_specs = [
        pl.BlockSpec((tm, PROJ_TN), lambda i, j: (i, j)),
        pl.BlockSpec((tm, SSD_GROUPS * LANES), lambda i, j: (i, 0)),
    ]
    out_shape = [
        jax.ShapeDtypeStruct((rows, N_MAIN), BF16),
        jax.ShapeDtypeStruct((rows, SSD_GROUPS * LANES), F32),
    ]
    if vt_block is not None:
        per = tm // vt_block
        out_specs.append(pl.BlockSpec((None, ATT_HEADS, per, ATT_V_DIM, vt_block),
                                      lambda i, j: (i // rope_blocks, 0, i % rope_blocks, 0, 0)))
        streams = rows // (rope_blocks * tm)
        out_shape.append(jax.ShapeDtypeStruct(
            (streams, ATT_HEADS, rope_blocks * per, ATT_V_DIM, vt_block), BF16))
        cache_rows = streams * (N_META + rope_blocks * tm)
        out_specs += [pl.BlockSpec(memory_space=pl.ANY), pl.BlockSpec(memory_space=pl.ANY)]
        out_shape += [jax.ShapeDtypeStruct((cache_rows, ATT_WIDTH), F32),
                      jax.ShapeDtypeStruct((cache_rows * ATT_HEADS, ATT_V_DIM), F32)]
        scratch += [pltpu.VMEM((tm, ATT_WIDTH), F32), pltpu.VMEM((tm * ATT_HEADS, ATT_V_DIM), F32),
                    pltpu.SemaphoreType.DMA((2,))]
    return pl.pallas_call(
        functools.partial(_inproj_kernel, emit_vt=vt_block is not None, seq_blocks=rope_blocks),
        grid=grid,
        in_specs=[
            pl.BlockSpec((tm, D_MODEL), lambda i, j: (i, 0)),
            pl.BlockSpec((1, D_MODEL), lambda i, j: (0, 0)),
            pl.BlockSpec((D_MODEL, PROJ_TN), lambda i, j: (0, j)),
            pl.BlockSpec((D_MODEL, SSD_GROUPS * LANES), lambda i, j: (0, 0)),
            pl.BlockSpec((tm, LANES), lambda i, j: (i % rope_blocks, 0)),
            pl.BlockSpec((tm, LANES), lambda i, j: (i % rope_blocks, 0)),
        ],
        out_specs=out_specs,
        out_shape=out_shape,
        scratch_shapes=scratch,
        compiler_params=_cparams(("parallel", "arbitrary")),
        name="in_proj",
    )(x, g1, w_main, w_dt, cos_t, sin_t)


def _ssd_kernel(x_ref, b_ref, c_ref, z_ref, dt_ref, h0_ref, cp_ref, cw_ref, cb_ref, dtb_ref,
                aneg_ref, dsk_ref, ng_ref, y_ref, hf_ref, ct_ref, h_scr, f_scr):
    c = pl.program_id(2)
    L = x_ref.shape[0]

    @pl.when(c == 0)
    def _():
        h_scr[...] = h0_ref[...]
        f_scr[0:SUBLANES, :] = cp_ref[...]

    f_scr[SUBLANES:SUBLANES + L, 0:GROUP_W] = x_ref[...].astype(F32)
    f_scr[SUBLANES:SUBLANES + L, GROUP_W:GROUP_W + SSD_STATE] = b_ref[...].astype(F32)
    f_scr[SUBLANES:SUBLANES + L, GROUP_W + SSD_STATE:CONV_GW] = c_ref[...].astype(F32)
    w = cw_ref[...]
    acc = cb_ref[...]
    for i in range(CONV_W):
        lo = SUBLANES - (CONV_W - 1) + i
        acc = acc + w[i:i + 1, :] * f_scr[lo:lo + L, :]
    xc = acc * _sigmoid(acc)
    tail = f_scr[L:L + SUBLANES, :]
    f_scr[0:SUBLANES, :] = tail
    ct_ref[...] = tail

    xg = xc[:, 0:GROUP_W]
    bm = xc[:, GROUP_W:GROUP_W + SSD_STATE].astype(BF16)
    cm = xc[:, GROUP_W + SSD_STATE:CONV_GW].astype(BF16)

    dtr = dt_ref[...] + dtb_ref[...]
    dt = jnp.maximum(dtr, 0.0) + jnp.log(1.0 + jnp.exp(-jnp.abs(dtr)))
    da = dt * aneg_ref[...]
    ti = lax.broadcasted_iota(I32, (L, L), 0)
    si = lax.broadcasted_iota(I32, (L, L), 1)
    causal = si <= ti
    tril = causal.astype(F32)
    cum = jnp.dot(tril, da, preferred_element_type=F32, precision=lax.Precision.HIGHEST)
    sel = (lax.broadcasted_iota(I32, (SUBLANES, LANES), 0)
           == lax.broadcasted_iota(I32, (SUBLANES, LANES), 1)).astype(F32)
    cum_t = lax.dot_general(sel, cum, (((1,), (1,)), ((), ())), preferred_element_type=F32,
                            precision=lax.Precision.HIGHEST)

    cb = lax.dot_general(cm, bm, (((1,), (1,)), ((), ())), preferred_element_type=F32)
    dsk = dsk_ref[...]
    ys = []
    for r in range(SSD_HPG):
        col = cum[:, r:r + 1]
        row = cum_t[r:r + 1, :]
        dec = jnp.exp(jnp.where(causal, col - row, NEG_BIG))
        m = (cb * dec).astype(BF16)
        xh = xg[:, r * SSD_HEAD_DIM:(r + 1) * SSD_HEAD_DIM]
        xdt = xh * dt[:, r:r + 1]
        h_prev = h_scr[r]
        y = jnp.dot(m, xdt.astype(BF16), preferred_element_type=F32)
        y = y + jnp.exp(col) * lax.dot_general(cm, h_prev.astype(BF16), (((1,), (1,)), ((), ())),
                                               preferred_element_type=F32)
        y = y + dsk[:, r:r + 1] * xh
        ys.append(y)
        tot = cum[L - 1:L, r:r + 1]
        xw = (xdt * jnp.exp(tot - col)).astype(BF16)
        upd = lax.dot_general(xw, bm, (((0,), (0,)), ((), ())), preferred_element_type=F32)
        h_scr[r] = h_prev * jnp.exp(tot) + upd
    yg = jnp.concatenate(ys, axis=1)
    z = z_ref[...].astype(F32)
    yz = yg * (z * _sigmoid(z))
    ms = jnp.mean(yz * yz, axis=-1, keepdims=True)
    y_ref[...] = (yz * lax.rsqrt(ms + EPS) * ng_ref[...]).astype(y_ref.dtype)
    hf_ref[...] = h_scr[...]


def _ssd_long_kernel(x_ref, b_ref, c_ref, z_ref, dt_ref, h0_ref, cp_ref, cw_ref, cb_ref, dtb_ref,
                     aneg_ref, dsk_ref, ng_ref, tri_ref, shift_ref, y_ref, hf_ref, ct_ref, h_scr, f_scr):
    c = pl.program_id(2)
    L = x_ref.shape[0]
    reps = L // LANES

    @pl.when(c == 0)
    def _():
        h_scr[...] = h0_ref[...]
        f_scr[:, 0:SUBLANES, :] = cp_ref[...]
        f_scr[:, SUBLANES:2 * SUBLANES, :] = jnp.zeros((SSD_GP, SUBLANES, CONV_GW), F32)

    tri = tri_ref[...]
    tri_b = tri.astype(BF16)
    visible = tri > 0.5
    for gi in range(SSD_GP):
        xs_ = slice(gi * GROUP_W, (gi + 1) * GROUP_W)
        ns_ = slice(gi * SSD_STATE, (gi + 1) * SSD_STATE)
        xb = jnp.concatenate([x_ref[:, xs_], b_ref[:, ns_], c_ref[:, ns_]], axis=1)
        xf = xb.astype(F32)
        w = cw_ref[gi]
        acc = cb_ref[gi] + w[CONV_W - 1:CONV_W, :] * xf
        for d in range(1, CONV_W):
            sh = jnp.dot(shift_ref[d - 1], xb, preferred_element_type=F32)
            acc = acc + w[CONV_W - 1 - d:CONV_W - d, :] * sh
        corr = jnp.zeros((SUBLANES, CONV_GW), F32)
        for i in range(CONV_W - 1):
            lo = SUBLANES - (CONV_W - 1) + i
            corr = corr + w[i:i + 1, :] * f_scr[gi, lo:lo + SUBLANES, :]
        acc = jnp.concatenate([acc[0:SUBLANES] + corr, acc[SUBLANES:]], axis=0)
        xc = acc * _sigmoid(acc)
        tail = xf[L - SUBLANES:L, :]
        f_scr[gi, 0:SUBLANES, :] = tail
        ct_ref[gi] = tail

        x_t = xc[:, 0:GROUP_W].T
        bm = xc[:, GROUP_W:GROUP_W + SSD_STATE].astype(BF16)
        cm = xc[:, GROUP_W + SSD_STATE:CONV_GW].astype(BF16)

        dtr = dt_ref[:, ns_].T[0:SUBLANES, :] + jnp.tile(dtb_ref[gi], (1, reps))
        dt = jnp.maximum(dtr, 0.0) + jnp.log(1.0 + jnp.exp(-jnp.abs(dtr)))
        da = dt * jnp.tile(aneg_ref[gi], (1, reps))
        d1 = da.astype(BF16)
        r1 = da - d1.astype(F32)
        d2 = r1.astype(BF16)
        d3 = (r1 - d2.astype(F32)).astype(BF16)
        cum = (jnp.dot(d1, tri_b, preferred_element_type=F32) + jnp.dot(d2, tri_b, preferred_element_type=F32)
               + jnp.dot(d3, tri_b, preferred_element_type=F32))
        cum_col = cum.T

        cb_t = lax.dot_general(bm, cm, (((1,), (1,)), ((), ())), preferred_element_type=F32)
        dsk = dsk_ref[gi]
        ys = []
        for r in range(SSD_HPG):
            hd = gi * SSD_HPG + r
            row = cum[r:r + 1, :]
            dec = jnp.exp(jnp.where(visible, row - cum_col[:, r:r + 1], NEG_BIG))
            m = (cb_t * dec).astype(BF16)
            xh = x_t[r * SSD_HEAD_DIM:(r + 1) * SSD_HEAD_DIM, :]
            xdt = xh * dt[r:r + 1, :]
            h_prev = h_scr[hd]
            y = jnp.dot(xdt.astype(BF16), m, preferred_element_type=F32)
            y = y + jnp.exp(row) * lax.dot_general(h_prev.astype(BF16), cm, (((1,), (1,)), ((), ())),
                                                   preferred_element_type=F32)
            y = y + dsk[r:r + 1, 0:1] * xh
            ys.append(y)
            tot = row[:, L - 1:L]
            xw = (xdt * jnp.exp(tot - row)).astype(BF16)
            h_scr[hd] = h_prev * jnp.exp(tot) + jnp.dot(xw, bm, preferred_element_type=F32)
        yg = jnp.concatenate(ys, axis=0).T
        z = z_ref[:, xs_].astype(F32)
        yz = yg * (z * _sigmoid(z))
        ms = jnp.mean(yz * yz, axis=-1, keepdims=True)
        y_ref[:, xs_] = (yz * lax.rsqrt(ms + EPS) * ng_ref[:, xs_]).astype(y_ref.dtype)
    hf_ref[...] = h_scr[...]


def _per_group_lanes(v):
    out = jnp.zeros((SSD_GROUPS, 1, LANES), F32)
    return out.at[:, 0, :SSD_HPG].set(v.astype(F32).reshape(SSD_GROUPS, SSD_HPG))


def _per_group_rows(v):
    out = jnp.zeros((SSD_GROUPS, SUBLANES, LANES), F32)
    return out.at[:, :SSD_HPG, :].set(
        jnp.broadcast_to(v.astype(F32).reshape(SSD_GROUPS, SSD_HPG, 1), (SSD_GROUPS, SSD_HPG, LANES)))


def _ssd(proj, dt_raw, h0, conv_prev, cw_g, cb_g, dt_bias, a_neg, d_skip, norm_g, *, n_seq, t, l, row0,
         shared_state):
    nc = t // l
    rb0 = row0 // l
    long_chunks = l % LANES == 0

    def rows(s, g, c):
        return rb0 + s * nc + c

    def sidx(s):
        return 0 if shared_state else s

    if long_chunks:
        head_rows = SUBLANES
        head_par = [_per_group_rows(v) for v in (dt_bias, a_neg, d_skip)]
        step = jnp.arange(l, dtype=I32)
        tri = (step[:, None] <= step[None, :]).astype(F32)
        shift = jnp.stack([(step[None, :] == step[:, None] - d) for d in range(1, CONV_W)]).astype(BF16)
        extra_args = [tri, shift]
        extra_specs = [pl.BlockSpec((l, l), lambda s, g, c: (0, 0)),
                       pl.BlockSpec((CONV_W - 1, l, l), lambda s, g, c: (0, 0, 0))]
        body, gp = _ssd_long_kernel, SSD_GP
        lead = (gp,)
        conv_scr = pltpu.VMEM((gp, 2 * SUBLANES, CONV_GW), F32)
    else:
        head_rows = 1
        head_par = [_per_group_lanes(v) for v in (dt_bias, a_neg, d_skip)]
        extra_args, extra_specs = [], []
        body, gp = _ssd_kernel, 1
        lead = (None,)
        conv_scr = pltpu.VMEM((l + SUBLANES, CONV_GW), F32)

    def per_group(*shape):
        return pl.BlockSpec(lead + shape, lambda s, g, c: (g,) + (0,) * len(shape))

    def per_seq_group(*shape, shared):
        return pl.BlockSpec((None,) + lead + shape,
                            lambda s, g, c: ((sidx(s) if shared else s), g) + (0,) * len(shape))

    def state_spec(shared):
        return pl.BlockSpec((None, gp * SSD_HPG, SSD_HEAD_DIM, SSD_STATE),
                            lambda s, g, c: ((sidx(s) if shared else s), g, 0, 0))

    return pl.pallas_call(
        body,
        grid=(n_seq, SSD_GROUPS // gp, nc),
        in_specs=[
            pl.BlockSpec((l, gp * GROUP_W), lambda s, g, c: (rows(s, g, c), COL_X // (gp * GROUP_W) + g)),
            pl.BlockSpec((l, gp * SSD_STATE), lambda s, g, c: (rows(s, g, c), COL_B // (gp * SSD_STATE) + g)),
            pl.BlockSpec((l, gp * SSD_STATE), lambda s, g, c: (rows(s, g, c), COL_C // (gp * SSD_STATE) + g)),
            pl.BlockSpec((l, gp * GROUP_W), lambda s, g, c: (rows(s, g, c), COL_Z // (gp * GROUP_W) + g)),
            pl.BlockSpec((l, gp * LANES), lambda s, g, c: (rows(s, g, c), g)),
            state_spec(True),
            per_seq_group(SUBLANES, CONV_GW, shared=True),
            per_group(CONV_W, CONV_GW),
            per_group(1, CONV_GW),
            per_group(head_rows, LANES), per_group(head_rows, LANES), per_group(head_rows, LANES),
            pl.BlockSpec((1, gp * GROUP_W), lambda s, g, c: (0, g)),
        ] + extra_specs,
        out_specs=[
            pl.BlockSpec((l, gp * GROUP_W), lambda s, g, c: (s * nc + c, g)),
            state_spec(False),
            per_seq_group(SUBLANES, CONV_GW, shared=False),
        ],
        out_shape=[
            jax.ShapeDtypeStruct((n_seq * t, D_INNER), BF16),
            jax.ShapeDtypeStruct((n_seq, SSD_HEADS, SSD_HEAD_DIM, SSD_STATE), F32),
            jax.ShapeDtypeStruct((n_seq, SSD_GROUPS, SUBLANES, CONV_GW), F32),
        ],
        scratch_shapes=[
            pltpu.VMEM((gp * SSD_HPG, SSD_HEAD_DIM, SSD_STATE), F32),
            conv_scr,
        ],
        compiler_params=_cparams(("parallel", "parallel", "arbitrary")),
        name="ssd_long" if long_chunks else "ssd",
    )(proj, proj, proj, proj, dt_raw, h0, conv_prev, cw_g, cb_g, *head_par, norm_g, *extra_args)


def _qk(q, k):
    return lax.dot_general(q, k, (((1,), (1,)), ((), ())), preferred_element_type=F32)


def _attn_prompt_kernel(lam_ref, q_ref, k_ref, vt_ref, mk_ref, mvt_ref, o_ref, sa_scr, sb_scr, acc_scr, m_scr):
    i = pl.program_id(2)
    tq = q_ref.shape[0]
    lam = lam_ref[0]
    q = q_ref[...]
    lane = lax.broadcasted_iota(I32, q.shape, 1)
    zero = jnp.zeros_like(q)
    qm = [jnp.where(lane < ATT_HEAD_DIM, q, zero), jnp.where(lane >= ATT_HEAD_DIM, q, zero)]

    def put_scores(s_ref, j):
        kblk = k_ref[pl.ds(pl.multiple_of(j * tq, tq), tq), :]
        for r in range(2):
            s_ref[r] = _qk(kblk, qm[r])

    def with_ones(vt):
        return jnp.concatenate([vt, jnp.ones((ONES_ROWS, vt.shape[1]), BF16)], axis=0)

    def update(s_ref, j, last=False):
        vt1 = with_ones(vt_ref[j])
        for r in range(2):
            s = s_ref[r]
            m_p = m_scr[r]
            if last:
                kpos = lax.broadcasted_iota(I32, (tq, tq), 0) // CHUNK
                qpos = lax.broadcasted_iota(I32, (tq, tq), 1) // CHUNK
                s = jnp.where(kpos <= qpos, s, NEG_BIG)
                s_meta = _qk(mk_ref[...], qm[r])
                m_p = jnp.maximum(m_p, jnp.max(s_meta, axis=0, keepdims=True))
            m_n = jnp.maximum(m_p, jnp.max(s, axis=0, keepdims=True))
            alpha = jnp.exp2(m_scr[r] - m_n)
            p = jnp.exp2((s - m_n).astype(BF16))
            acc = alpha * acc_scr[r] + jnp.dot(vt1, p, preferred_element_type=F32)
            if last:
                p_meta = jnp.exp2((s_meta - m_n).astype(BF16))
                acc = acc + jnp.dot(with_ones(mvt_ref[...]), p_meta, preferred_element_type=F32)
            acc_scr[r] = acc
            m_scr[r] = m_n

    m_scr[...] = jnp.full(m_scr.shape, NEG_BIG, F32)
    acc_scr[...] = jnp.zeros(acc_scr.shape, F32)

    put_scores(sa_scr, 0)

    def body(jj, _):
        j = 2 * jj
        put_scores(sb_scr, j + 1)
        update(sa_scr, j)
        put_scores(sa_scr, j + 2)
        update(sb_scr, j + 1)
        return 0

    lax.fori_loop(0, i // 2, body, 0)

    @pl.when(i % 2 == 0)
    def _():
        update(sa_scr, i, last=True)

    @pl.when(i % 2 == 1)
    def _():
        put_scores(sb_scr, i)
        update(sa_scr, i - 1)
        update(sb_scr, i, last=True)

    outs = [acc_scr[r, :ATT_V_DIM, :] / acc_scr[r, ATT_V_DIM:ATT_V_DIM + 1, :] for r in range(2)]
    o_ref[...] = (outs[0] - lam * outs[1]).T.astype(o_ref.dtype)


def _attn_prompt(lam, proj_p, vt_p, proj_s, mvt, *, batch, seq, meta_row0):
    nq = seq // ATT_TQ
    return pl.pallas_call(
        _attn_prompt_kernel,
        grid=(batch, ATT_HEADS, nq),
        in_specs=[
            pl.BlockSpec(memory_space=pltpu.SMEM),
            pl.BlockSpec((ATT_TQ, LANES), lambda b, h, i: (b * nq + i, COL_Q // LANES + h)),
            pl.BlockSpec((seq, LANES), lambda b, h, i: (b, COL_K // LANES + h)),
            pl.BlockSpec((None, None, nq, ATT_V_DIM, ATT_TQ), lambda b, h, i: (b, h, 0, 0, 0)),
            pl.BlockSpec((N_META, LANES), lambda b, h, i: (meta_row0 // N_META, COL_K // LANES + h)),
            pl.BlockSpec((None, ATT_V_DIM, N_META), lambda b, h, i: (h, 0, 0)),
        ],
        out_specs=pl.BlockSpec((ATT_TQ, LANES), lambda b, h, i: (b * nq + i, h)),
        out_shape=jax.ShapeDtypeStruct((batch * seq, ATT_WIDTH), BF16),
        scratch_shapes=[pltpu.VMEM((2, ATT_TQ, ATT_TQ), F32),
                        pltpu.VMEM((2, ATT_TQ, ATT_TQ), F32),
                        pltpu.VMEM((2, ATT_V_DIM + ONES_ROWS, ATT_TQ), F32),
                        pltpu.VMEM((2, 1, ATT_TQ), F32)],
        compiler_params=_cparams(("parallel", "parallel", "arbitrary")),
        name="attn_prompt",
    )(lam, proj_p, proj_p, vt_p, proj_s, mvt)


def _attn_short_kernel(lam_ref, q_ref, kn_ref, vn_ref, *rest, has_cache):
    if has_cache:
        kc_ref, vc_ref, o_ref = rest
    else:
        (o_ref,) = rest
    t = q_ref.shape[0]
    nh = 2 * ATT_HEADS
    lam = lam_ref[0]
    q = q_ref[...].astype(F32)
    qb = jnp.broadcast_to(q[None], (nh, t, ATT_WIDTH)).reshape(nh * t, ATT_WIDTH)
    row_head = lax.broadcasted_iota(I32, (nh * t, ATT_WIDTH), 0) // t
    col_head = lax.broadcasted_iota(I32, (nh * t, ATT_WIDTH), 1) // ATT_HEAD_DIM
    qbd = jnp.where(row_head == col_head, qb, 0.0).astype(BF16)
    s_new = _qk(qbd, kn_ref[...])
    m = jnp.max(s_new, axis=-1, keepdims=True)
    if has_cache:
        s_old = _qk(qbd, kc_ref[...].astype(BF16))
        m = jnp.maximum(m, jnp.max(s_old, axis=-1, keepdims=True))
    p_new = jnp.exp2(s_new - m)
    den = jnp.sum(p_new, axis=-1, keepdims=True)
    acc = jnp.dot(p_new.astype(BF16), vn_ref[...], preferred_element_type=F32)
    if has_cache:
        p_old = jnp.exp2(s_old - m)
        den = den + jnp.sum(p_old, axis=-1, keepdims=True)
        acc = acc + jnp.dot(p_old.astype(BF16), vc_ref[...].astype(BF16), preferred_element_type=F32)
    acc = acc / den
    for h in range(ATT_HEADS):
        cols = slice(h * ATT_V_DIM, (h + 1) * ATT_V_DIM)
        a1 = acc[(2 * h) * t:(2 * h + 1) * t, cols]
        a2 = acc[(2 * h + 1) * t:(2 * h + 2) * t, cols]
        o_ref[:, cols] = (a1 - lam * a2).astype(o_ref.dtype)


def _attn_short(lam, proj_s, cache_k, cache_v, *, n_seq, t, row0):
    has_cache = cache_k is not None
    rb0 = row0 // t
    in_specs = [
        pl.BlockSpec(memory_space=pltpu.SMEM),
        pl.BlockSpec((t, ATT_WIDTH), lambda s: (rb0 + s, COL_Q // ATT_WIDTH)),
        pl.BlockSpec((t, ATT_WIDTH), lambda s: (rb0 + s, COL_K // ATT_WIDTH)),
        pl.BlockSpec((t, ATT_WIDTH), lambda s: (rb0 + s, COL_V // ATT_WIDTH)),
    ]
    args = [lam, proj_s, proj_s, proj_s]
    if has_cache:
        past = cache_k.shape[1]
        in_specs += [pl.BlockSpec((None, past, ATT_WIDTH), lambda s: (s, 0, 0)),
                     pl.BlockSpec((None, past, ATT_WIDTH), lambda s: (s, 0, 0))]
        args += [cache_k, cache_v]
    return pl.pallas_call(
        functools.partial(_attn_short_kernel, has_cache=has_cache),
        grid=(n_seq,),
        in_specs=in_specs,
        out_specs=pl.BlockSpec((t, ATT_WIDTH), lambda s: (s, 0)),
        out_shape=jax.ShapeDtypeStruct((n_seq * t, ATT_WIDTH), BF16),
        compiler_params=_cparams(("parallel",)),
        name="attn_cached" if has_cache else "attn_meta",
    )(*args)


def _merge_kernel(x_ref, yn_ref, o_ref, g1_ref, g2_ref, wso_ref, wao_ref, wo_ref, bg1_ref, bg2_ref,
                  sub_ref, n2_ref, wr_ref, br_ref,
                  h1_ref, u2_ref, eidx_ref, gate_ref, rank_ref, cnt_ref, *, sub_scale):
    tm = x_ref.shape[0]
    y_ssd = jnp.dot(yn_ref[...], wso_ref[...], preferred_element_type=F32)
    o = o_ref[...].astype(F32)
    parts = []
    for h in range(ATT_HEADS):
        oh = o[:, h * ATT_V_DIM:(h + 1) * ATT_V_DIM]
        ms = jnp.mean(oh * oh, axis=-1, keepdims=True)
        parts.append(oh * lax.rsqrt(ms + EPS) * sub_ref[...] * sub_scale)
    on = jnp.concatenate(parts, axis=1).astype(BF16)
    y_att = jnp.dot(on, wao_ref[...], preferred_element_type=F32)
    gs = _sigmoid(g1_ref[...].astype(F32) + bg1_ref[...])
    ga = _sigmoid(g2_ref[...].astype(F32) + bg2_ref[...])
    mix_in = (gs * y_ssd + ga * y_att).astype(BF16)
    h1 = x_ref[...] + jnp.dot(mix_in, wo_ref[...], preferred_element_type=F32)
    h1_ref[...] = h1
    ms = jnp.mean(h1 * h1, axis=-1, keepdims=True)
    u2 = h1 * lax.rsqrt(ms + EPS) * n2_ref[...]
    u2_ref[...] = u2.astype(u2_ref.dtype)

    logits = lax.dot_general(wr_ref[...], u2, (((1,), (1,)), ((), ())), preferred_element_type=F32,
                             precision=lax.Precision.HIGHEST)
    work = (logits + br_ref[...])[0:N_EXPERTS, :]
    expert = lax.broadcasted_iota(I32, (N_EXPERTS, tm), 0).astype(F32)
    vals, idxs, hots = [], [], []
    for _ in range(TOP_K):
        mx = jnp.max(work, axis=0, keepdims=True)
        ix = jnp.min(jnp.where(work == mx, expert, float(N_EXPERTS)), axis=0, keepdims=True)
        hot = expert == ix
        vals.append(mx)
        idxs.append(ix)
        hots.append(hot)
        work = jnp.where(hot, NEG_BIG, work)
    es = [jnp.exp(v - vals[0]) for v in vals]
    den = es[0] + es[1] + es[2] + es[3]
    hot_all = jnp.where(hots[0] | hots[1] | hots[2] | hots[3], 1.0, 0.0)
    si = lax.broadcasted_iota(I32, (tm, tm), 0)
    ti = lax.broadcasted_iota(I32, (tm, tm), 1)
    earlier = (si < ti).astype(BF16)
    prefix = jnp.dot(hot_all.astype(BF16), earlier, preferred_element_type=F32)
    row = lax.broadcasted_iota(I32, (SUBLANES, tm), 0)
    eidx = jnp.zeros((SUBLANES, tm), F32)
    gate = jnp.zeros((SUBLANES, tm), F32)
    rank = jnp.zeros((SUBLANES, tm), F32)
    for k in range(TOP_K):
        rk = jnp.sum(jnp.where(hots[k], prefix, 0.0), axis=0, keepdims=True)
        eidx = jnp.where(row == k, idxs[k], eidx)
        gate = jnp.where(row == k, es[k] / den, gate)
        rank = jnp.where(row == k, rk, rank)
    eidx_ref[...] = eidx.astype(I32)
    gate_ref[...] = gate
    rank_ref[...] = rank.astype(I32)
    cnt_ref[...] = jnp.sum(hot_all, axis=1, keepdims=True).astype(I32)


def _merge(x, yn, o, proj, wso, wao, wo, bg1, bg2, sub_g, n2_g, wr_t, br_c, *, tm, sub_scale):
    rows = x.shape[0]
    full = lambda shape: pl.BlockSpec(shape, lambda i: (0,) * len(shape))
    tok = lambda w: pl.BlockSpec((tm, w), lambda i: (i, 0))
    per_block = lambda a, b: pl.BlockSpec((None, a, b), lambda i: (i, 0, 0))
    return pl.pallas_call(
        functools.partial(_merge_kernel, sub_scale=sub_scale),
        grid=(rows // tm,),
        in_specs=[
            tok(D_MODEL), tok(D_INNER), tok(ATT_WIDTH),
            pl.BlockSpec((tm, D_MODEL), lambda i: (i, COL_G // D_MODEL)),
            pl.BlockSpec((tm, D_MODEL), lambda i: (i, COL_G // D_MODEL + 1)),
            full((D_INNER, D_MODEL)), full((ATT_WIDTH, D_MODEL)), full((D_MODEL, D_MODEL)),
            full((1, D_MODEL)), full((1, D_MODEL)), full((1, ATT_V_DIM)), full((1, D_MODEL)),
            full((LANES, D_MODEL)), full((LANES, 1)),
        ],
        out_specs=[tok(D_MODEL), tok(D_MODEL), per_block(SUBLANES, tm), per_block(SUBLANES, tm),
                   per_block(SUBLANES, tm), per_block(N_EXPERTS, 1)],
        out_shape=[
            jax.ShapeDtypeStruct((rows, D_MODEL), F32),
            jax.ShapeDtypeStruct((rows, D_MODEL), BF16),
            jax.ShapeDtypeStruct((rows // tm, SUBLANES, tm), I32),
            jax.ShapeDtypeStruct((rows // tm, SUBLANES, tm), F32),
            jax.ShapeDtypeStruct((rows // tm, SUBLANES, tm), I32),
            jax.ShapeDtypeStruct((rows // tm, N_EXPERTS, 1), I32),
        ],
        compiler_params=_cparams(("parallel",)),
        name="merge_router",
    )(x, yn, o, proj, proj, wso, wao, wo, bg1, bg2, sub_g, n2_g, wr_t, br_c)


def _seg_copy(local_ref, lo, hbm_ref, hi, sem, to_hbm):
    loc = local_ref.at[pl.ds(pl.multiple_of(lo, SEG_PAD), SEG_PAD), :]
    hbm = hbm_ref.at[pl.ds(pl.multiple_of(hi, SEG_PAD), SEG_PAD), :]
    return pltpu.make_async_copy(loc, hbm, sem) if to_hbm else pltpu.make_async_copy(hbm, loc, sem)


def _seg_copies_start(tab_ref, local_ref, hbm_ref, sem, to_hbm):
    def per_expert(e, total):
        n, lo, hi = tab_ref[0, e], tab_ref[1, e], tab_ref[2, e]

        def per_copy(c, _):
            _seg_copy(local_ref, lo + c * SEG_PAD, hbm_ref, hi + c * SEG_PAD, sem, to_hbm).start()
            return 0

        lax.fori_loop(0, n, per_copy, 0)
        return total + n

    return lax.fori_loop(0, N_EXPERTS, per_expert, 0)


def _seg_copies_wait(n, local_ref, hbm_ref, sem, to_hbm):
    def one(c, _):
        _seg_copy(local_ref, 0, hbm_ref, 0, sem, to_hbm).wait()
        return 0

    lax.fori_loop(0, n, one, 0)


def _dispatch_kernel(tab_ref, lpos_ref, u_ref, *rest, fill_tails):
    if fill_tails:
        tail_ref, _, xs_ref, loc_scr, zero_scr, sem = rest
    else:
        xs_ref, loc_scr, sem = rest
    lr, tm = loc_scr.shape[0], u_ref.shape[0]
    lpos = lpos_ref[...]
    p = lax.broadcasted_iota(I32, (lr, tm), 0)
    perm = jnp.zeros((lr, tm), F32)
    for k in range(TOP_K):
        perm = jnp.where(p == lpos[k:k + 1, :], 1.0, perm)
    loc_scr[...] = jnp.dot(perm.astype(BF16), u_ref[...], preferred_element_type=F32).astype(loc_scr.dtype)
    n = _seg_copies_start(tab_ref, loc_scr, xs_ref, sem, to_hbm=True)
    if fill_tails:
        zero_scr[...] = jnp.zeros(zero_scr.shape, zero_scr.dtype)

        def per_expert(e, total):
            cnt, hi = tail_ref[0, e], tail_ref[1, e]

            def per_copy(c, _):
                _seg_copy(zero_scr, 0, xs_ref, hi + c * SEG_PAD, sem, True).start()
                return 0

            lax.fori_loop(0, cnt, per_copy, 0)
            return total + cnt

        n = n + lax.fori_loop(0, N_EXPERTS, per_expert, 0)
    _seg_copies_wait(n, loc_scr, xs_ref, sem, to_hbm=True)


def _local_rows(tm):
    return tm * TOP_K + N_EXPERTS * SEG_PAD


def _dispatch(tab, lpos_t, u2, *, tm, xs_rows=None, xs=None, tail=None):
    rows = u2.shape[0]
    in_specs = [
        pl.BlockSpec((None, 3, N_EXPERTS), lambda i: (i, 0, 0), memory_space=pltpu.SMEM),
        pl.BlockSpec((None, TOP_K, tm), lambda i: (i, 0, 0)),
        pl.BlockSpec((tm, D_MODEL), lambda i: (i, 0)),
    ]
    args = [tab, lpos_t, u2]
    scratch = [pltpu.VMEM((_local_rows(tm), D_MODEL), BF16)]
    aliases = {}
    if tail is not None:
        assert xs is not None and rows == tm
        in_specs += [pl.BlockSpec(memory_space=pltpu.SMEM), pl.BlockSpec(memory_space=pl.ANY)]
        args += [tail, xs]
        scratch.append(pltpu.VMEM((SEG_PAD, D_MODEL), BF16))
        aliases = {4: 0}
        xs_rows = xs.shape[0]
    return pl.pallas_call(
        functools.partial(_dispatch_kernel, fill_tails=tail is not None),
        grid=(rows // tm,),
        in_specs=in_specs,
        out_specs=pl.BlockSpec(memory_space=pl.ANY),
        out_shape=jax.ShapeDtypeStruct((xs_rows, D_MODEL), BF16),
        scratch_shapes=scratch + [pltpu.SemaphoreType.DMA(())],
        input_output_aliases=aliases,
        compiler_params=_cparams(("arbitrary",)),
        name="moe_dispatch",
    )(*args)


def _ffn_kernel(be_ref, na_ref, x_ref, wgu_ref, bgu_ref, wd_ref, bd_ref, y_ref, wgu_scr, wd_scr):
    i = pl.program_id(0)
    active = i < na_ref[0]

    @pl.when(active & ((i == 0) | (be_ref[i] != be_ref[jnp.maximum(i - 1, 0)])))
    def _():
        wgu_scr[...] = wgu_ref[...].astype(BF16)
        wd_scr[...] = wd_ref[...].astype(BF16)

    @pl.when(active)
    def _():
        gu = jnp.dot(x_ref[...], wgu_scr[...], preferred_element_type=F32) + bgu_ref[...]
        gate = jnp.minimum(gu[:, :D_FF], SWIGLU_LIMIT)
        up = jnp.clip(gu[:, D_FF:], -SWIGLU_LIMIT, SWIGLU_LIMIT)
        hdn = (up + 1.0) * gate * _sigmoid(SWIGLU_ALPHA * gate)
        y = jnp.dot(hdn.astype(BF16), wd_scr[...], preferred_element_type=F32) + bd_ref[...]
        y_ref[...] = y.astype(y_ref.dtype)


def _ffn(block_exp, n_active, xs, wgu, bgu, wd, bd):
    n_blocks = xs.shape[0] // MOE_BLK

    def blk(i, be, na):
        return jnp.minimum(i, na[0] - 1)

    return pl.pallas_call(
        _ffn_kernel,
        grid_spec=pltpu.PrefetchScalarGridSpec(
            num_scalar_prefetch=2,
            grid=(n_blocks,),
            in_specs=[
                pl.BlockSpec((MOE_BLK, D_MODEL), lambda i, be, na: (blk(i, be, na), 0)),
                pl.BlockSpec((None, D_MODEL, 2 * D_FF), lambda i, be, na: (be[blk(i, be, na)], 0, 0)),
                pl.BlockSpec((None, 1, 2 * D_FF), lambda i, be, na: (be[blk(i, be, na)], 0, 0)),
                pl.BlockSpec((None, D_FF, D_MODEL), lambda i, be, na: (be[blk(i, be, na)], 0, 0)),
                pl.BlockSpec((None, 1, D_MODEL), lambda i, be, na: (be[blk(i, be, na)], 0, 0)),
            ],
            out_specs=pl.BlockSpec((MOE_BLK, D_MODEL), lambda i, be, na: (blk(i, be, na), 0)),
            scratch_shapes=[pltpu.VMEM((D_MODEL, 2 * D_FF), BF16), pltpu.VMEM((D_FF, D_MODEL), BF16)],
        ),
        out_shape=jax.ShapeDtypeStruct(xs.shape, xs.dtype),
        compiler_params=_cparams(("arbitrary",)),
        name="moe_ffn",
    )(block_exp, n_active, xs, wgu, bgu, wd, bd)


def _combine_kernel(tab_ref, lpos_ref, gate_ref, h1_ref, fg_ref, ys_ref, y_ref, loc_scr, sem):
    lr, tm = loc_scr.shape[0], h1_ref.shape[0]
    loc_scr[...] = jnp.zeros(loc_scr.shape, loc_scr.dtype)
    n = _seg_copies_start(tab_ref, loc_scr, ys_ref, sem, to_hbm=False)
    lpos = lpos_ref[...]
    gate = gate_ref[...]
    p = lax.broadcasted_iota(I32, (tm, lr), 1)
    pick = jnp.zeros((tm, lr), F32)
    for k in range(TOP_K):
        pick = jnp.where(p == lpos[:, k:k + 1], gate[:, k:k + 1], pick)
    _seg_copies_wait(n, loc_scr, ys_ref, sem, to_hbm=False)
    h = h1_ref[...] + jnp.dot(pick.astype(BF16), loc_scr[...], preferred_element_type=F32)
    ms = jnp.mean(h * h, axis=-1, keepdims=True)
    y_ref[...] = h * lax.rsqrt(ms + EPS) * fg_ref[...]


def _combine(tab, lpos, gate, h1, fg, ys, *, tm):
    rows = h1.shape[0]
    return pl.pallas_call(
        _combine_kernel,
        grid=(rows // tm,),
        in_specs=[
            pl.BlockSpec((None, 3, N_EXPERTS), lambda i: (i, 0, 0), memory_space=pltpu.SMEM),
            pl.BlockSpec((tm, TOP_K), lambda i: (i, 0)),
            pl.BlockSpec((tm, TOP_K), lambda i: (i, 0)),
            pl.BlockSpec((tm, D_MODEL), lambda i: (i, 0)),
            pl.BlockSpec((1, D_MODEL), lambda i: (0, 0)),
            pl.BlockSpec(memory_space=pl.ANY),
        ],
        out_specs=pl.BlockSpec((tm, D_MODEL), lambda i: (i, 0)),
        out_shape=jax.ShapeDtypeStruct((rows, D_MODEL), F32),
        scratch_shapes=[pltpu.VMEM((_local_rows(tm), D_MODEL), ys.dtype), pltpu.SemaphoreType.DMA(())],
        compiler_params=_cparams(("arbitrary",)),
        name="moe_combine",
    )(tab, lpos, gate, h1, fg, ys)


def _rope_tables(pos):
    d = ATT_HEAD_DIM
    inv = ROPE_THETA ** (-jnp.arange(0, d, 2, dtype=F32) / d)
    ang = pos.astype(F32)[:, None] * inv[None, :]
    cos = jnp.cos(ang)
    sin = jnp.sin(ang)
    cos_h = jnp.concatenate([cos, cos], axis=1)
    sin_h = jnp.concatenate([-sin, sin], axis=1)
    return jnp.tile(cos_h, (1, LANES // d)), jnp.tile(sin_h, (1, LANES // d))


def _conv_by_group(a):
    lead = a.shape[:-1]
    x = a[..., :D_INNER].reshape(lead + (SSD_GROUPS, GROUP_W))
    b = a[..., D_INNER:D_INNER + SSD_GN].reshape(lead + (SSD_GROUPS, SSD_STATE))
    c = a[..., D_INNER + SSD_GN:].reshape(lead + (SSD_GROUPS, SSD_STATE))
    return jnp.concatenate([x, b, c], axis=-1)


def _conv_from_group(a):
    lead = a.shape[:-2]
    x = a[..., :GROUP_W].reshape(lead + (D_INNER,))
    b = a[..., GROUP_W:GROUP_W + SSD_STATE].reshape(lead + (SSD_GN,))
    c = a[..., GROUP_W + SSD_STATE:].reshape(lead + (SSD_GN,))
    return jnp.concatenate([x, b, c], axis=-1)


def _conv_prev_blocks(prev):
    g = jnp.moveaxis(_conv_by_group(prev.astype(F32)), 1, 2)
    return jnp.pad(g, ((0, 0), (0, 0), (SUBLANES - (CONV_W - 1), 0), (0, 0)))


def _conv_tail_rows(ct):
    return _conv_from_group(jnp.moveaxis(ct[:, :, SUBLANES - (CONV_W - 1):, :], 1, 2))


def kernel(x_prompt, x_sample, cache_k, cache_v, state_ssm, state_conv, meta_tokens, norm1_g, w_in, conv_w, conv_b, dt_bias, a_log, d_skip, ssd_norm_g, lambda_q1, lambda_k1, lambda_q2, lambda_k2, subln_g, w_ssd_out, w_att_out, b_gate, w_o, norm2_g, w_router, b_router, w_gu, b_gu, w_down, b_down, final_norm_g):
    batch, seq, _ = x_prompt.shape
    dbatch, dseq, _ = x_sample.shape
    past = cache_k.shape[2]
    depth = norm1_g.shape[0]
    assert depth == 1 and dseq == N_META
    assert seq % SSD_L == 0 and seq % ATT_TQ == 0 and seq % TOKEN_TM == 0 and ATT_TQ % CHUNK == 0
    assert TOKEN_TM % ATT_TQ == 0 and PROJ_TM % ATT_TQ == 0
    n_p = batch * seq
    n_dec = dbatch * dseq
    n_s = n_dec + N_META
    lam_init = 0.8 - 0.6 * math.exp(-0.3 * 0)
    l = 0

    wi = w_in[l]
    o_z, o_xbc, o_dt = 0, D_INNER, D_INNER + CONV_DIM
    o_q = o_dt + SSD_HEADS
    w_main = jnp.concatenate([wi[:, o_z:o_xbc], wi[:, o_xbc:o_dt], wi[:, o_q:]], axis=1).astype(BF16)
    w_dt = jnp.zeros((D_MODEL, SSD_GROUPS, LANES), F32).at[:, :, :SSD_HPG].set(
        wi[:, o_dt:o_q].reshape(D_MODEL, SSD_GROUPS, SSD_HPG)).reshape(D_MODEL, SSD_GROUPS * LANES).astype(BF16)
    g1 = norm1_g[l].reshape(1, D_MODEL)
    cw_g = jnp.moveaxis(_conv_by_group(conv_w[l]), 0, 1)
    cb_g = _conv_by_group(conv_b[l])[:, None, :]
    a_neg = -jnp.exp(a_log[l].astype(F32))
    norm_g = ssd_norm_g[l].reshape(1, D_INNER)
    lam = (jnp.exp(jnp.sum(lambda_q1[l].astype(F32) * lambda_k1[l].astype(F32)))
           - jnp.exp(jnp.sum(lambda_q2[l].astype(F32) * lambda_k2[l].astype(F32))) + lam_init).reshape(1)
    wso = w_ssd_out[l].astype(BF16)
    wao = w_att_out[l].astype(BF16)
    wo = w_o[l].astype(BF16)
    bg1 = b_gate[l][:D_MODEL].reshape(1, D_MODEL)
    bg2 = b_gate[l][D_MODEL:].reshape(1, D_MODEL)
    sub_g = subln_g[l].reshape(1, ATT_V_DIM)
    n2_g = norm2_g[l].reshape(1, D_MODEL)
    wr = jnp.zeros((LANES, D_MODEL), F32).at[:N_EXPERTS, :].set(w_router[l].T)
    br = jnp.zeros((LANES, 1), F32).at[:N_EXPERTS, 0].set(b_router[l])
    wgu = w_gu[l]
    bgu = b_gu[l][:, None, :]
    wd = w_down[l]
    bd = b_down[l][:, None, :]
    fg = final_norm_g.reshape(1, D_MODEL)

    xp = x_prompt.reshape(n_p, D_MODEL)
    xs_rows = jnp.concatenate([x_sample.reshape(n_dec, D_MODEL), meta_tokens.astype(x_prompt.dtype)], axis=0)
    cos_p, sin_p = _rope_tables(N_META + jnp.arange(seq, dtype=I32))
    pos_s = jnp.concatenate([jnp.tile(past + jnp.arange(dseq, dtype=I32), dbatch), jnp.arange(N_META, dtype=I32)])
    cos_s, sin_s = _rope_tables(pos_s)

    tm_p = PROJ_TM if seq % PROJ_TM == 0 else TOKEN_TM
    proj_p, dt_p, vt_p, kc_p, vc_p = _in_proj(xp, g1, w_main, w_dt, cos_p, sin_p, tm_p, seq // tm_p, vt_block=ATT_TQ)
    proj_s, dt_s = _in_proj(xs_rows, g1, w_main, w_dt, cos_s, sin_s, n_s, 1)

    ssd_args = (cw_g, cb_g, dt_bias[l], a_neg, d_skip[l], norm_g)
    zero_h = jnp.zeros((1, SSD_HEADS, SSD_HEAD_DIM, SSD_STATE), F32)
    zero_c = jnp.zeros((1, SSD_GROUPS, SUBLANES, CONV_GW), F32)
    yn_m, h_m, ct_m = _ssd(proj_s, dt_s, zero_h, zero_c, *ssd_args, n_seq=1, t=N_META, l=N_META,
                           row0=n_dec, shared_state=True)
    yn_p, h_p, ct_p = _ssd(proj_p, dt_p, h_m, ct_m, *ssd_args, n_seq=batch, t=seq, l=SSD_L, row0=0,
                           shared_state=True)
    yn_d, h_d, ct_d = _ssd(proj_s, dt_s, state_ssm[l].astype(F32), _conv_prev_blocks(state_conv[l]), *ssd_args,
                           n_seq=dbatch, t=dseq, l=dseq, row0=0, shared_state=False)
    yn_s = jnp.concatenate([yn_d, yn_m], axis=0)

    mvt =proj_s[n_dec:n_s, COL_V:COL_V + ATT_WIDTH].reshape(N_META, ATT_HEADS, ATT_V_DIM).transpose(1, 2, 0)
    o_p = _attn_prompt(lam, proj_p, vt_p, proj_s, mvt, batch=batch, seq=seq, meta_row0=n_dec)
    o_d = _attn_short(lam, proj_s, cache_k[l].reshape(dbatch, past, ATT_WIDTH),
                      cache_v[l].reshape(dbatch, past, ATT_WIDTH), n_seq=dbatch, t=dseq, row0=0)
    o_m = _attn_short(lam, proj_s, None, None, n_seq=1, t=N_META, row0=n_dec)
    o_s = jnp.concatenate([o_d, o_m], axis=0)

    merge_w = (wso, wao, wo, bg1, bg2, sub_g, n2_g, wr, br)
    h1_p, u2_p, e_p, gt_p, rk_p, cnt_p = _merge(xp, yn_p, o_p, proj_p, *merge_w, tm=TOKEN_TM,
                                                sub_scale=1.0 - lam_init)
    h1_s, u2_s, e_s, gt_s, rk_s, cnt_s = _merge(xs_rows, yn_s, o_s, proj_s, *merge_w, tm=n_s,
                                                sub_scale=1.0 - lam_init)

    nb_p = n_p // TOKEN_TM
    cnt = jnp.concatenate([cnt_p[:, :, 0], cnt_s[:, :, 0]], axis=0)
    seg = (cnt + SEG_PAD - 1) // SEG_PAD * SEG_PAD
    local_start = jnp.cumsum(seg, axis=1) - seg
    per_expert = jnp.sum(seg, axis=0)
    padded = (per_expert + MOE_BLK - 1) // MOE_BLK * MOE_BLK
    pend = jnp.cumsum(padded)
    pstart = pend - padded
    hbm_start = pstart[None, :] + jnp.cumsum(seg, axis=0) - seg
    tab = jnp.stack([seg // SEG_PAD, local_start, hbm_start], axis=1).astype(I32)
    n_rows_max = (n_p + n_s) * TOP_K + (nb_p + 1) * N_EXPERTS * (SEG_PAD - 1)
    n_blocks = -(-n_rows_max // MOE_BLK) + N_EXPERTS
    block_start = jnp.arange(n_blocks, dtype=I32) * MOE_BLK
    block_exp = jnp.minimum(jnp.sum((pend[None, :] <= block_start[:, None]).astype(I32), axis=1), N_EXPERTS - 1)
    n_active = (pend[-1:] // MOE_BLK).astype(I32)

    def local_rows_of(e, rk, starts):
        hot = e[:, :TOP_K, :, None] == jnp.arange(N_EXPERTS, dtype=I32)
        return (jnp.sum(jnp.where(hot, starts[:, None, None, :], 0), axis=-1) + rk[:, :TOP_K]).astype(I32)

    def by_token(a):
        return a[:, :TOP_K].transpose(0, 2, 1).reshape(-1, TOP_K)

    lpos_pt = local_rows_of(e_p, rk_p, local_start[:nb_p])
    lpos_st = local_rows_of(e_s, rk_s, local_start[nb_p:])
    lpos_p, lpos_s = by_token(lpos_pt), by_token(lpos_st)
    gt_p, gt_s = by_token(gt_p), by_token(gt_s)
    tail = jnp.stack([(padded - per_expert) // SEG_PAD, pstart + per_expert]).astype(I32)
    xs = _dispatch(tab[:nb_p], lpos_pt, u2_p, tm=TOKEN_TM, xs_rows=n_blocks * MOE_BLK)
    xs = _dispatch(tab[nb_p:], lpos_st, u2_s, tm=n_s, xs=xs, tail=tail)
    ys = _ffn(block_exp, n_active, xs, wgu, bgu, wd, bd)
    y_p = _combine(tab[:nb_p], lpos_p, gt_p, h1_p, fg, ys, tm=TOKEN_TM)
    y_s = _combine(tab[nb_p:], lpos_s, gt_s, h1_s, fg, ys, tm=n_s)

    def kv_rows(proj, col, lo, hi):
        return proj[lo:hi, col:col + ATT_WIDTH].astype(F32)

    def with_meta(cache, col, tail_shape):
        meta = kv_rows(proj_s, col, n_dec, n_s).reshape((1, N_META) + tail_shape)
        slab = cache.reshape((batch, N_META + seq) + tail_shape)
        return slab.at[:, :N_META].set(jnp.broadcast_to(meta, (batch, N_META) + tail_shape))

    new_k_p = with_meta(kc_p, COL_K, (ATT_WIDTH,)).reshape(1, batch, N_META + seq, 2 * ATT_HEADS, ATT_HEAD_DIM)
    new_v_p = with_meta(vc_p, COL_V, (ATT_HEADS, ATT_V_DIM))[None]
    new_k_s = kv_rows(proj_s, COL_K, 0, n_dec).reshape(1, dbatch, dseq, 2 * ATT_HEADS, ATT_HEAD_DIM)
    new_v_s = kv_rows(proj_s, COL_V, 0, n_dec).reshape(1, dbatch, dseq, ATT_HEADS, ATT_V_DIM)
    return (y_p.reshape(batch, seq, D_MODEL),
            y_s[:n_dec].reshape(dbatch, dseq, D_MODEL),
            new_k_p, new_v_p,
            h_p.astype(state_ssm.dtype)[None],
            _conv_tail_rows(ct_p).astype(x_prompt.dtype)[None],
            new_k_s, new_v_s,
            h_d.astype(state_ssm.dtype)[None],
            _conv_tail_rows(ct_d).astype(x_sample.dtype)[None])
```

```python
import functools
import math

import jax
import jax.numpy as jnp
from jax import lax
from jax.experimental import pallas as pl
from jax.experimental.pallas import tpu as pltpu

F32 = jnp.float32
BF16 = jnp.bfloat16
I32 = jnp.int32

D_MODEL = 1024
D_INNER = 2048
SSD_HEADS = 32
SSD_HEAD_DIM = 64
SSD_GROUPS = 8
SSD_HPG = SSD_HEADS // SSD_GROUPS
SSD_STATE = 128
SSD_GN = SSD_GROUPS * SSD_STATE
CONV_W = 4
CONV_DIM = D_INNER + 2 * SSD_GN
ATT_HEADS = 8
ATT_HEAD_DIM = 64
ATT_V_DIM = 128
ATT_WIDTH = 1024
CHUNK = 64
N_META = 16
EPS = 1e-6
ROPE_THETA = 10000.0
LOG2_E = math.log2(math.e)
N_EXPERTS = 32
TOP_K = 4
D_FF = 1024
SWIGLU_LIMIT = 7.0
SWIGLU_ALPHA = 1.702

COL_Z = 0
COL_X = COL_Z + D_INNER
COL_B = COL_X + D_INNER
COL_C = COL_B + SSD_GN
COL_Q = COL_C + SSD_GN
COL_K = COL_Q + ATT_WIDTH
COL_V = COL_K + ATT_WIDTH
COL_G = COL_V + ATT_WIDTH
N_MAIN = COL_G + 2 * D_MODEL

LANES = 128
SUBLANES = 8
GROUP_W = D_INNER // SSD_GROUPS
CONV_GW = GROUP_W + 2 * SSD_STATE
PROJ_TN = 1024
PROJ_TM = 1024
TOKEN_TM = 512
SSD_L = 256
SSD_GP = 4
ATT_TQ = 512
MOE_BLK = 512
SEG_PAD = 16
ONES_ROWS = 16
VMEM_LIMIT = 56 * 1024 * 1024
NEG_BIG = -1e30


def _cparams(sem):
    return pltpu.CompilerParams(dimension_semantics=sem, vmem_limit_bytes=VMEM_LIMIT)


def _sigmoid(x):
    return 1.0 / (1.0 + jnp.exp(-x))


def _inproj_kernel(x_ref, g_ref, w_ref, wdt_ref, cos_ref, sin_ref, o_ref, dt_ref, *rest, emit_vt,
                   seq_blocks):
    if emit_vt:
        vt_ref, kout_ref, vout_ref, u_scr, kbuf, vbuf, sems = rest
    else:
        (u_scr,) = rest
    j = pl.program_id(1)

    @pl.when(j == 0)
    def _():
        x = x_ref[...]
        ms = jnp.mean(x * x, axis=-1, keepdims=True)
        u = (x * lax.rsqrt(ms + EPS) * g_ref[...]).astype(BF16)
        u_scr[...] = u
        dt_ref[...] = jnp.dot(u, wdt_ref[...], preferred_element_type=F32)

    acc = jnp.dot(u_scr[...], w_ref[...], preferred_element_type=F32)
    is_q = j == COL_Q // PROJ_TN
    is_k = j == COL_K // PROJ_TN

    @pl.when(is_q | is_k)
    def _():
        cos = cos_ref[...]
        sin = sin_ref[...]
        lane = lax.broadcasted_iota(I32, cos.shape, 1)
        half = ATT_HEAD_DIM // 2
        first_half = (lane % ATT_HEAD_DIM) < half
        scale = jnp.where(is_q, ATT_HEAD_DIM ** -0.5 * LOG2_E, 1.0).astype(F32)
        for c in range(PROJ_TN // LANES):
            a = acc[:, c * LANES:(c + 1) * LANES]
            swapped = jnp.where(first_half, pltpu.roll(a, LANES - half, 1), pltpu.roll(a, half, 1))
            roped = (a * cos + swapped * sin) * scale
            o_ref[:, c * LANES:(c + 1) * LANES] = roped.astype(o_ref.dtype)
            if emit_vt:
                kbuf[:, c * LANES:(c + 1) * LANES] = roped

    @pl.when(jnp.logical_not(is_q | is_k))
    def _():
        o_ref[...] = acc.astype(o_ref.dtype)

    if emit_vt:
        i = pl.program_id(0)
        tm = x_ref.shape[0]
        row0 = (i // seq_blocks) * (N_META + seq_blocks * tm) + N_META + (i % seq_blocks) * tm

        def cache_copy(buf, out_ref, sem):
            per_row = buf.shape[0] // tm
            rows = out_ref.at[pl.ds(pl.multiple_of(row0 * per_row, SUBLANES), tm * per_row), :]
            return pltpu.make_async_copy(buf, rows, sem)

        @pl.when(is_k)
        def _():
            cache_copy(kbuf, kout_ref, sems.at[0]).start()

        @pl.when(j == COL_V // PROJ_TN)
        def _():
            for h in range(ATT_HEADS):
                vbuf[pl.ds(h, tm, stride=ATT_HEADS), :] = acc[:, h * ATT_V_DIM:(h + 1) * ATT_V_DIM]
            cache_copy(vbuf, vout_ref, sems.at[1]).start()
            tk = vt_ref.shape[-1]
            for h in range(ATT_HEADS):
                for s in range(vt_ref.shape[1]):
                    blk = acc[s * tk:(s + 1) * tk, h * ATT_V_DIM:(h + 1) * ATT_V_DIM]
                    vt_ref[h, s] = blk.T.astype(vt_ref.dtype)

        @pl.when(j == pl.num_programs(1) - 1)
        def _():
            cache_copy(kbuf, kout_ref, sems.at[0]).wait()
            cache_copy(vbuf, vout_ref, sems.at[1]).wait()


def _in_proj(x, g1, w_main, w_dt, cos_t, sin_t, tm, rope_blocks, vt_block=None):
    rows = x.shape[0]
    grid = (rows // tm, N_MAIN // PROJ_TN)
    scratch = [pltpu.VMEM((tm, D_MODEL), BF16)]
    dt_w = w_dt.shape[1]
    out_specs = [
        pl.BlockSpec((tm, PROJ_TN), lambda i, j: (i, j)),
        pl.BlockSpec((tm, dt_w), lambda i, j: (i, 0)),
    ]
    out_shape = [
        jax.ShapeDtypeStruct((rows, N_MAIN), BF16),
        jax.ShapeDtypeStruct((rows, dt_w), F32),
    ]
    if vt_block is not None:
        per = tm // vt_block
        out_specs.append(pl.BlockSpec((None, ATT_HEADS, per, ATT_V_DIM, vt_block),
                                      lambda i, j: (i // rope_blocks, 0, i % rope_blocks, 0, 0)))
        streams = rows // (rope_blocks * tm)
        out_shape.append(jax.ShapeDtypeStruct(
            (streams, ATT_HEADS, rope_blocks * per, ATT_V_DIM, vt_block), BF16))
        cache_rows = streams * (N_META + rope_blocks * tm)
        out_specs += [pl.BlockSpec(memory_space=pl.ANY), pl.BlockSpec(memory_space=pl.ANY)]
        out_shape += [jax.ShapeDtypeStruct((cache_rows, ATT_WIDTH), F32),
                      jax.ShapeDtypeStruct((cache_rows * ATT_HEADS, ATT_V_DIM), F32)]
        scratch += [pltpu.VMEM((tm, ATT_WIDTH), F32), pltpu.VMEM((tm * ATT_HEADS, ATT_V_DIM), F32),
                    pltpu.SemaphoreType.DMA((2,))]
    return pl.pallas_call(
        functools.partial(_inproj_kernel, emit_vt=vt_block is not None, seq_blocks=rope_blocks),
        grid=grid,
        in_specs=[
            pl.BlockSpec((tm, D_MODEL), lambda i, j: (i, 0)),
            pl.BlockSpec((1, D_MODEL), lambda i, j: (0, 0)),
            pl.BlockSpec((D_MODEL, PROJ_TN), lambda i, j: (0, j)),
            pl.BlockSpec((D_MODEL, dt_w), lambda i, j: (0, 0)),
            pl.BlockSpec((tm, LANES), lambda i, j: (i % rope_blocks, 0)),
            pl.BlockSpec((tm, LANES), lambda i, j: (i % rope_blocks, 0)),
        ],
        out_specs=out_specs,
        out_shape=out_shape,
        scratch_shapes=scratch,
        compiler_params=_cparams(("parallel", "arbitrary")),
        name="in_proj",
    )(x, g1, w_main, w_dt, cos_t, sin_t)


def _ssd_kernel(x_ref, b_ref, c_ref, z_ref, dt_ref, h0_ref, cp_ref, cw_ref, cb_ref, dtb_ref,
                aneg_ref, dsk_ref, ng_ref, y_ref, hf_ref, ct_ref, h_scr, f_scr):
    c = pl.program_id(2)
    L = x_ref.shape[0]

    @pl.when(c == 0)
    def _():
        h_scr[...] = h0_ref[...]
        f_scr[0:SUBLANES, :] = cp_ref[...]

    f_scr[SUBLANES:SUBLANES + L, 0:GROUP_W] = x_ref[...].astype(F32)
    f_scr[SUBLANES:SUBLANES + L, GROUP_W:GROUP_W + SSD_STATE] = b_ref[...].astype(F32)
    f_scr[SUBLANES:SUBLANES + L, GROUP_W + SSD_STATE:CONV_GW] = c_ref[...].astype(F32)
    w = cw_ref[...]
    acc = cb_ref[...]
    for i in range(CONV_W):
        lo = SUBLANES - (CONV_W - 1) + i
        acc = acc + w[i:i + 1, :] * f_scr[lo:lo + L, :]
    xc = acc * _sigmoid(acc)
    tail = f_scr[L:L + SUBLANES, :]
    f_scr[0:SUBLANES, :] = tail
    ct_ref[...] = tail

    xg = xc[:, 0:GROUP_W]
    bm = xc[:, GROUP_W:GROUP_W + SSD_STATE].astype(BF16)
    cm = xc[:, GROUP_W + SSD_STATE:CONV_GW].astype(BF16)

    dtr = dt_ref[...] + dtb_ref[...]
    dt = jnp.maximum(dtr, 0.0) + jnp.log(1.0 + jnp.exp(-jnp.abs(dtr)))
    da = dt * aneg_ref[...]
    ti = lax.broadcasted_iota(I32, (L, L), 0)
    si = lax.broadcasted_iota(I32, (L, L), 1)
    causal = si <= ti
    tril = causal.astype(F32)
    cum = jnp.dot(tril, da, preferred_element_type=F32, precision=lax.Precision.HIGHEST)
    sel = (lax.broadcasted_iota(I32, (SUBLANES, LANES), 0)
           == lax.broadcasted_iota(I32, (SUBLANES, LANES), 1)).astype(F32)
    cum_t = lax.dot_general(sel, cum, (((1,), (1,)), ((), ())), preferred_element_type=F32,
                            precision=lax.Precision.HIGHEST)

    cb = lax.dot_general(cm, bm, (((1,), (1,)), ((), ())), preferred_element_type=F32)
    dsk = dsk_ref[...]
    ys = []
    for r in range(SSD_HPG):
        col = cum[:, r:r + 1]
        row = cum_t[r:r + 1, :]
        dec = jnp.exp(jnp.where(causal, col - row, NEG_BIG))
        m = (cb * dec).astype(BF16)
        xh = xg[:, r * SSD_HEAD_DIM:(r + 1) * SSD_HEAD_DIM]
        xdt = xh * dt[:, r:r + 1]
        h_prev = h_scr[r]
        y = jnp.dot(m, xdt.astype(BF16), preferred_element_type=F32)
        y = y + jnp.exp(col) * lax.dot_general(cm, h_prev.astype(BF16), (((1,), (1,)), ((), ())),
                                               preferred_element_type=F32)
        y = y + dsk[:, r:r + 1] * xh
        ys.append(y)
        tot = cum[L - 1:L, r:r + 1]
        xw = (xdt * jnp.exp(tot - col)).astype(BF16)
        upd = lax.dot_general(xw, bm, (((0,), (0,)), ((), ())), preferred_element_type=F32)
        h_scr[r] = h_prev * jnp.exp(tot) + upd
    yg = jnp.concatenate(ys, axis=1)
    z = z_ref[...].astype(F32)
    yz = yg * (z * _sigmoid(z))
    ms = jnp.mean(yz * yz, axis=-1, keepdims=True)
    y_ref[...] = (yz * lax.rsqrt(ms + EPS) * ng_ref[...]).astype(y_ref.dtype)
    hf_ref[...] = h_scr[...]


def _ssd_long_kernel(x_ref, b_ref, c_ref, z_ref, dt_ref, h0_ref, cp_ref, cw_ref, cb_ref, dtb_ref,
                     aneg_ref, dsk_ref, ng_ref, tri_ref, shift_ref, y_ref, hf_ref, ct_ref, h_scr, f_scr,
                     dtt_scr):
    c = pl.program_id(2)
    L = x_ref.shape[0]
    reps = L // LANES

    @pl.when(c == 0)
    def _():
        h_scr[...] = h0_ref[...]
        f_scr[:, 0:SUBLANES, :] = cp_ref[...]
        f_scr[:, SUBLANES:2 * SUBLANES, :] = jnp.zeros((SSD_GP, SUBLANES, CONV_GW), F32)

    tri = tri_ref[...]
    tri_b = tri.astype(BF16)
    visible = tri > 0.5
    dtt_scr[...] = dt_ref[...].T
    for gi in range(SSD_GP):
        xs_ = slice(gi * GROUP_W, (gi + 1) * GROUP_W)
        ns_ = slice(gi * SSD_STATE, (gi + 1) * SSD_STATE)
        xb = jnp.concatenate([x_ref[:, xs_], b_ref[:, ns_], c_ref[:, ns_]], axis=1)
        xf = xb.astype(F32)
        w = cw_ref[gi]
        acc = cb_ref[gi] + w[CONV_W - 1:CONV_W, :] * xf
        for d in range(1, CONV_W):
            sh = jnp.dot(shift_ref[d - 1], xb, preferred_element_type=F32)
            acc = acc + w[CONV_W - 1 - d:CONV_W - d, :] * sh
        corr = jnp.zeros((SUBLANES, CONV_GW), F32)
        for i in range(CONV_W - 1):
            lo = SUBLANES - (CONV_W - 1) + i
            corr = corr + w[i:i + 1, :] * f_scr[gi, lo:lo + SUBLANES, :]
        acc = jnp.concatenate([acc[0:SUBLANES] + corr, acc[SUBLANES:]], axis=0)
        xc = acc * _sigmoid(acc)
        tail = xf[L - SUBLANES:L, :]
        f_scr[gi, 0:SUBLANES, :] = tail
        ct_ref[gi] = tail

        x_t = xc[:, 0:GROUP_W].T
        bm = xc[:, GROUP_W:GROUP_W + SSD_STATE].astype(BF16)
        cm = xc[:, GROUP_W + SSD_STATE:CONV_GW].astype(BF16)

        grp = pl.program_id(1) * SSD_GP + gi
        dt_rows = [dtt_scr[pl.ds(SUBLANES * r + grp, 1), :] for r in range(SSD_HPG)]
        dtr = jnp.concatenate(dt_rows + [jnp.zeros((SUBLANES - SSD_HPG, L), F32)], axis=0)
        dtr = dtr + jnp.tile(dtb_ref[gi], (1, reps))
        dt = jnp.maximum(dtr, 0.0) + jnp.log(1.0 + jnp.exp(-jnp.abs(dtr)))
        da = dt * jnp.tile(aneg_ref[gi], (1, reps))
        d1 = da.astype(BF16)
        r1 = da - d1.astype(F32)
        d2 = r1.astype(BF16)
        d3 = (r1 - d2.astype(F32)).astype(BF16)
        cum = (jnp.dot(d1, tri_b, preferred_element_type=F32) + jnp.dot(d2, tri_b, preferred_element_type=F32)
               + jnp.dot(d3, tri_b, preferred_element_type=F32))
        cum_col = cum.T

        cb_t = lax.dot_general(bm, cm, (((1,), (1,)), ((), ())), preferred_element_type=F32)
        dsk = dsk_ref[gi]
        ys = []
        for r in range(SSD_HPG):
            hd = gi * SSD_HPG + r
            row = cum[r:r + 1, :]
            dec = jnp.exp(jnp.where(visible, row - cum_col[:, r:r + 1], NEG_BIG))
            m = (cb_t * dec).astype(BF16)
            xh = x_t[r * SSD_HEAD_DIM:(r + 1) * SSD_HEAD_DIM, :]
            xdt = xh * dt[r:r + 1, :]
            h_prev = h_scr[hd]
            y = jnp.dot(xdt.astype(BF16), m, preferred_element_type=F32)
            y = y + jnp.exp(row) * lax.dot_general(h_prev.astype(BF16), cm, (((1,), (1,)), ((), ())),
                                                   preferred_element_type=F32)
            y = y + dsk[r:r + 1, 0:1] * xh
            ys.append(y)
            tot = row[:, L - 1:L]
            xw = (xdt * jnp.exp(tot - row)).astype(BF16)
            h_scr[hd] = h_prev * jnp.exp(tot) + jnp.dot(xw, bm, preferred_element_type=F32)
        yg = jnp.concatenate(ys, axis=0).T
        z = z_ref[:, xs_].astype(F32)
        yz = yg * (z * _sigmoid(z))
        ms = jnp.mean(yz * yz, axis=-1, keepdims=True)
        y_ref[:, xs_] = (yz * lax.rsqrt(ms + EPS) * ng_ref[:, xs_]).astype(y_ref.dtype)
    hf_ref[...] = h_scr[...]


def _per_group_lanes(v):
    out = jnp.zeros((SSD_GROUPS, 1, LANES), F32)
    return out.at[:, 0, :SSD_HPG].set(v.astype(F32).reshape(SSD_GROUPS, SSD_HPG))


def _per_group_rows(v):
    out = jnp.zeros((SSD_GROUPS, SUBLANES, LANES), F32)
    return out.at[:, :SSD_HPG, :].set(
        jnp.broadcast_to(v.astype(F32).reshape(SSD_GROUPS, SSD_HPG, 1), (SSD_GROUPS, SSD_HPG, LANES)))


def _ssd(proj, dt_raw, h0, conv_prev, cw_g, cb_g, dt_bias, a_neg, d_skip, norm_g, *, n_seq, t, l, row0,
         shared_state):
    nc = t // l
    rb0 = row0 // l
    long_chunks = l % LANES == 0

    def rows(s, g, c):
        return rb0 + s * nc + c

    def sidx(s):
        return 0 if shared_state else s

    if long_chunks:
        head_rows = SUBLANES
        head_par = [_per_group_rows(v) for v in (dt_bias, a_neg, d_skip)]
        step = jnp.arange(l, dtype=I32)
        tri = (step[:, None] <= step[None, :]).astype(F32)
        shift = jnp.stack([(step[None, :] == step[:, None] - d) for d in range(1, CONV_W)]).astype(BF16)
        extra_args = [tri, shift]
        extra_specs = [pl.BlockSpec((l, l), lambda s, g, c: (0, 0)),
                       pl.BlockSpec((CONV_W - 1, l, l), lambda s, g, c: (0, 0, 0))]
        body, gp = _ssd_long_kernel, SSD_GP
        lead = (gp,)
        scratch = [pltpu.VMEM((gp, 2 * SUBLANES, CONV_GW), F32), pltpu.VMEM((LANES, l), F32)]
        dt_spec = pl.BlockSpec((l, LANES), lambda s, g, c: (rows(s, g, c), 0))
    else:
        head_rows = 1
        head_par = [_per_group_lanes(v) for v in (dt_bias, a_neg, d_skip)]
        extra_args, extra_specs = [], []
        body, gp = _ssd_kernel, 1
        lead = (None,)
        scratch = [pltpu.VMEM((l + SUBLANES, CONV_GW), F32)]
        dt_spec = pl.BlockSpec((l, LANES), lambda s, g, c: (rows(s, g, c), g))

    def per_group(*shape):
        return pl.BlockSpec(lead + shape, lambda s, g, c: (g,) + (0,) * len(shape))

    def per_seq_group(*shape, shared):
        return pl.BlockSpec((None,) + lead + shape,
                            lambda s, g, c: ((sidx(s) if shared else s), g) + (0,) * len(shape))

    def state_spec(shared):
        return pl.BlockSpec((None, gp * SSD_HPG, SSD_HEAD_DIM, SSD_STATE),
                            lambda s, g, c: ((sidx(s) if shared else s), g, 0, 0))

    return pl.pallas_call(
        body,
        grid=(n_seq, SSD_GROUPS // gp, nc),
        in_specs=[
            pl.BlockSpec((l, gp * GROUP_W), lambda s, g, c: (rows(s, g, c), COL_X // (gp * GROUP_W) + g)),
            pl.BlockSpec((l, gp * SSD_STATE), lambda s, g, c: (rows(s, g, c), COL_B // (gp * SSD_STATE) + g)),
            pl.BlockSpec((l, gp * SSD_STATE), lambda s, g, c: (rows(s, g, c), COL_C // (gp * SSD_STATE) + g)),
            pl.BlockSpec((l, gp * GROUP_W), lambda s, g, c: (rows(s, g, c), COL_Z // (gp * GROUP_W) + g)),
            dt_spec,
            state_spec(True),
            per_seq_group(SUBLANES, CONV_GW, shared=True),
            per_group(CONV_W, CONV_GW),
            per_group(1, CONV_GW),
            per_group(head_rows, LANES), per_group(head_rows, LANES), per_group(head_rows, LANES),
            pl.BlockSpec((1, gp * GROUP_W), lambda s, g, c: (0, g)),
        ] + extra_specs,
        out_specs=[
            pl.BlockSpec((l, gp * GROUP_W), lambda s, g, c: (s * nc + c, g)),
            state_spec(False),
            per_seq_group(SUBLANES, CONV_GW, shared=False),
        ],
        out_shape=[
            jax.ShapeDtypeStruct((n_seq * t, D_INNER), BF16),
            jax.ShapeDtypeStruct((n_seq, SSD_HEADS, SSD_HEAD_DIM, SSD_STATE), F32),
            jax.ShapeDtypeStruct((n_seq, SSD_GROUPS, SUBLANES, CONV_GW), F32),
        ],
        scratch_shapes=[pltpu.VMEM((gp * SSD_HPG, SSD_HEAD_DIM, SSD_STATE), F32)] + scratch,
        compiler_params=_cparams(("parallel", "parallel", "arbitrary")),
        name="ssd_long" if long_chunks else "ssd",
    )(proj, proj, proj, proj, dt_raw, h0, conv_prev, cw_g, cb_g, *head_par, norm_g, *extra_args)


def _qk(q, k):
    return lax.dot_general(q, k, (((1,), (1,)), ((), ())), preferred_element_type=F32)


def _attn_prompt_kernel(lam_ref, q_ref, k_ref, vt_ref, mk_ref, mvt_ref, o_ref, sa_scr, sb_scr, acc_scr, m_scr):
    i = pl.program_id(2)
    tq = q_ref.shape[0]
    lam = lam_ref[0]
    q = q_ref[...]
    lane = lax.broadcasted_iota(I32, q.shape, 1)
    zero = jnp.zeros_like(q)
    qm = [jnp.where(lane < ATT_HEAD_DIM, q, zero), jnp.where(lane >= ATT_HEAD_DIM, q, zero)]

    def put_scores(s_ref, j):
        kblk = k_ref[pl.ds(pl.multiple_of(j * tq, tq), tq), :]
        for r in range(2):
            s_ref[r] = _qk(kblk, qm[r])

    def with_ones(vt):
        return jnp.concatenate([vt, jnp.ones((ONES_ROWS, vt.shape[1]), BF16)], axis=0)

    def update(s_ref, j, last=False):
        vt1 = with_ones(vt_ref[j])
        for r in range(2):
            s = s_ref[r]
            m_p = m_scr[r]
            if last:
                kpos = lax.broadcasted_iota(I32, (tq, tq), 0) // CHUNK
                qpos = lax.broadcasted_iota(I32, (tq, tq), 1) // CHUNK
                s = jnp.where(kpos <= qpos, s, NEG_BIG)
                s_meta = _qk(mk_ref[...], qm[r])
                m_p = jnp.maximum(m_p, jnp.max(s_meta, axis=0, keepdims=True))
            m_n = jnp.maximum(m_p, jnp.max(s, axis=0, keepdims=True))
            alpha = jnp.exp2(m_scr[r] - m_n)
            p = jnp.exp2((s - m_n).astype(BF16))
            acc = alpha * acc_scr[r] + jnp.dot(vt1, p, preferred_element_type=F32)
            if last:
                p_meta = jnp.exp2((s_meta - m_n).astype(BF16))
                acc = acc + jnp.dot(with_ones(mvt_ref[...]), p_meta, preferred_element_type=F32)
            acc_scr[r] = acc
            m_scr[r] = m_n

    m_scr[...] = jnp.full(m_scr.shape, NEG_BIG, F32)
    acc_scr[...] = jnp.zeros(acc_scr.shape, F32)

    put_scores(sa_scr, 0)

    def body(jj, _):
        j = 2 * jj
        put_scores(sb_scr, j + 1)
        update(sa_scr, j)
        put_scores(sa_scr, j + 2)
        update(sb_scr, j + 1)
        return 0

    lax.fori_loop(0, i // 2, body, 0)

    @pl.when(i % 2 == 0)
    def _():
        update(sa_scr, i, last=True)

    @pl.when(i % 2 == 1)
    def _():
        put_scores(sb_scr, i)
        update(sa_scr, i - 1)
        update(sb_scr, i, last=True)

    outs = [acc_scr[r, :ATT_V_DIM, :] / acc_scr[r, ATT_V_DIM:ATT_V_DIM + 1, :] for r in range(2)]
    o_ref[...] = (outs[0] - lam * outs[1]).T.astype(o_ref.dtype)


def _attn_prompt(lam, proj_p, vt_p, proj_s, mvt, *, batch, seq, meta_row0):
    nq = seq // ATT_TQ
    return pl.pallas_call(
        _attn_prompt_kernel,
        grid=(batch, ATT_HEADS, nq),
        in_specs=[
            pl.BlockSpec(memory_space=pltpu.SMEM),
            pl.BlockSpec((ATT_TQ, LANES), lambda b, h, i: (b * nq + i, COL_Q // LANES + h)),
            pl.BlockSpec((seq, LANES), lambda b, h, i: (b, COL_K // LANES + h)),
            pl.BlockSpec((None, None, nq, ATT_V_DIM, ATT_TQ), lambda b, h, i: (b, h, 0, 0, 0)),
            pl.BlockSpec((N_META, LANES), lambda b, h, i: (meta_row0 // N_META, COL_K // LANES + h)),
            pl.BlockSpec((None, ATT_V_DIM, N_META), lambda b, h, i: (h, 0, 0)),
        ],
        out_specs=pl.BlockSpec((ATT_TQ, LANES), lambda b, h, i: (b * nq + i, h)),
        out_shape=jax.ShapeDtypeStruct((batch * seq, ATT_WIDTH), BF16),
        scratch_shapes=[pltpu.VMEM((2, ATT_TQ, ATT_TQ), F32),
                        pltpu.VMEM((2, ATT_TQ, ATT_TQ), F32),
                        pltpu.VMEM((2, ATT_V_DIM + ONES_ROWS, ATT_TQ), F32),
                        pltpu.VMEM((2, 1, ATT_TQ), F32)],
        compiler_params=_cparams(("parallel", "parallel", "arbitrary")),
        name="attn_prompt",
    )(lam, proj_p, proj_p, vt_p, proj_s, mvt)


def _attn_short_kernel(lam_ref, q_ref, kn_ref, vn_ref, *rest, has_cache):
    if has_cache:
        kc_ref, vc_ref, o_ref = rest
    else:
        (o_ref,) = rest
    t = q_ref.shape[0]
    nh = 2 * ATT_HEADS
    lam = lam_ref[0]
    q = q_ref[...].astype(F32)
    qb = jnp.broadcast_to(q[None], (nh, t, ATT_WIDTH)).reshape(nh * t, ATT_WIDTH)
    row_head = lax.broadcasted_iota(I32, (nh * t, ATT_WIDTH), 0) // t
    col_head = lax.broadcasted_iota(I32, (nh * t, ATT_WIDTH), 1) // ATT_HEAD_DIM
    qbd = jnp.where(row_head == col_head, qb, 0.0).astype(BF16)
    s_new = _qk(qbd, kn_ref[...])
    m = jnp.max(s_new, axis=-1, keepdims=True)
    if has_cache:
        s_old = _qk(qbd, kc_ref[...].astype(BF16))
        m = jnp.maximum(m, jnp.max(s_old, axis=-1, keepdims=True))
    p_new = jnp.exp2(s_new - m)
    den = jnp.sum(p_new, axis=-1, keepdims=True)
    p_new = p_new.astype(BF16)
    if has_cache:
        p_old = jnp.exp2(s_old - m)
        den = den + jnp.sum(p_old, axis=-1, keepdims=True)
        p_old = p_old.astype(BF16)
        past = p_old.shape[1]
    for h in range(ATT_HEADS):
        cols = slice(h * ATT_V_DIM, (h + 1) * ATT_V_DIM)
        pair = slice(2 * h * t, (2 * h + 2) * t)
        acc = jnp.dot(p_new[pair], vn_ref[:, cols], preferred_element_type=F32)
        if has_cache:
            v_h = vc_ref[pl.ds(h, past, stride=ATT_HEADS), :].astype(BF16)
            acc = acc + jnp.dot(p_old[pair], v_h, preferred_element_type=F32)
        acc = acc / den[pair]
        o_ref[:, cols] = (acc[0:t] - lam * acc[t:2 * t]).astype(o_ref.dtype)


def _attn_short(lam, proj_s, cache_k, cache_v, *, n_seq, t, row0):
    has_cache = cache_k is not None
    rb0 = row0 // t
    in_specs = [
        pl.BlockSpec(memory_space=pltpu.SMEM),
        pl.BlockSpec((t, ATT_WIDTH), lambda s: (rb0 + s, COL_Q // ATT_WIDTH)),
        pl.BlockSpec((t, ATT_WIDTH), lambda s: (rb0 + s, COL_K // ATT_WIDTH)),
        pl.BlockSpec((t, ATT_WIDTH), lambda s: (rb0 + s, COL_V // ATT_WIDTH)),
    ]
    args = [lam, proj_s, proj_s, proj_s]
    if has_cache:
        past = cache_k.shape[1]
        in_specs += [pl.BlockSpec((None, past, ATT_WIDTH), lambda s: (s, 0, 0)),
                     pl.BlockSpec((None, past * ATT_HEADS, ATT_V_DIM), lambda s: (s, 0, 0))]
        args += [cache_k, cache_v]
    return pl.pallas_call(
        functools.partial(_attn_short_kernel, has_cache=has_cache),
        grid=(n_seq,),
        in_specs=in_specs,
        out_specs=pl.BlockSpec((t, ATT_WIDTH), lambda s: (s, 0)),
        out_shape=jax.ShapeDtypeStruct((n_seq * t, ATT_WIDTH), BF16),
        compiler_params=_cparams(("parallel",)),
        name="attn_cached" if has_cache else "attn_meta",
    )(*args)


def _merge_kernel(x_ref, yn_ref, o_ref, g1_ref, g2_ref, wso_ref, wao_ref, wo_ref, bg1_ref, bg2_ref,
                  sub_ref, n2_ref, wr_ref, br_ref,
                  h1_ref, u2_ref, eidx_ref, gate_ref, rank_ref, cnt_ref, *, sub_scale):
    tm = x_ref.shape[0]
    y_ssd = jnp.dot(yn_ref[...], wso_ref[...], preferred_element_type=F32)
    o = o_ref[...].astype(F32)
    parts = []
    for h in range(ATT_HEADS):
        oh = o[:, h * ATT_V_DIM:(h + 1) * ATT_V_DIM]
        ms = jnp.mean(oh * oh, axis=-1, keepdims=True)
        parts.append(oh * lax.rsqrt(ms + EPS) * sub_ref[...] * sub_scale)
    on = jnp.concatenate(parts, axis=1).astype(BF16)
    y_att = jnp.dot(on, wao_ref[...], preferred_element_type=F32)
    gs = _sigmoid(g1_ref[...].astype(F32) + bg1_ref[...])
    ga = _sigmoid(g2_ref[...].astype(F32) + bg2_ref[...])
    mix_in = (gs * y_ssd + ga * y_att).astype(BF16)
    h1 = x_ref[...] + jnp.dot(mix_in, wo_ref[...], preferred_element_type=F32)
    h1_ref[...] = h1
    ms = jnp.mean(h1 * h1, axis=-1, keepdims=True)
    u2 = h1 * lax.rsqrt(ms + EPS) * n2_ref[...]
    u2_ref[...] = u2.astype(u2_ref.dtype)

    logits = lax.dot_general(wr_ref[...], u2, (((1,), (1,)), ((), ())), preferred_element_type=F32,
                             precision=lax.Precision.HIGHEST)
    work = (logits + br_ref[...])[0:N_EXPERTS, :]
    expert = lax.broadcasted_iota(I32, (N_EXPERTS, tm), 0).astype(F32)
    vals, idxs, hots = [], [], []
    for _ in range(TOP_K):
        mx = jnp.max(work, axis=0, keepdims=True)
        ix = jnp.min(jnp.where(work == mx, expert, float(N_EXPERTS)), axis=0, keepdims=True)
        hot = expert == ix
        vals.append(mx)
        idxs.append(ix)
        hots.append(hot)
        work = jnp.where(hot, NEG_BIG, work)
    es = [jnp.exp(v - vals[0]) for v in vals]
    den = es[0] + es[1] + es[2] + es[3]
    hot_all = jnp.where(hots[0] | hots[1] | hots[2] | hots[3], 1.0, 0.0)
    si = lax.broadcasted_iota(I32, (tm, tm), 0)
    ti = lax.broadcasted_iota(I32, (tm, tm), 1)
    earlier = (si < ti).astype(BF16)
    prefix = jnp.dot(hot_all.astype(BF16), earlier, preferred_element_type=F32)
    row = lax.broadcasted_iota(I32, (SUBLANES, tm), 0)
    eidx = jnp.zeros((SUBLANES, tm), F32)
    gate = jnp.zeros((SUBLANES, tm), F32)
    rank = jnp.zeros((SUBLANES, tm), F32)
    for k in range(TOP_K):
        rk = jnp.sum(jnp.where(hots[k], prefix, 0.0), axis=0, keepdims=True)
        eidx = jnp.where(row == k, idxs[k], eidx)
        gate = jnp.where(row == k, es[k] / den, gate)
        rank = jnp.where(row == k, rk, rank)
    eidx_ref[...] = eidx.astype(I32)
    gate_ref[...] = gate
    rank_ref[...] = rank.astype(I32)
    cnt_ref[...] = jnp.sum(hot_all, axis=1, keepdims=True).astype(I32)


def _merge(x, yn, o, proj, wso, wao, wo, bg1, bg2, sub_g, n2_g, wr_t, br_c, *, tm, sub_scale):
    rows = x.shape[0]
    full = lambda shape: pl.BlockSpec(shape, lambda i: (0,) * len(shape))
    tok = lambda w: pl.BlockSpec((tm, w), lambda i: (i, 0))
    per_block = lambda a, b: pl.BlockSpec((None, a, b), lambda i: (i, 0, 0))
    return pl.pallas_call(
        functools.partial(_merge_kernel, sub_scale=sub_scale),
        grid=(rows // tm,),
        in_specs=[
            tok(D_MODEL), tok(D_INNER), tok(ATT_WIDTH),
            pl.BlockSpec((tm, D_MODEL), lambda i: (i, COL_G // D_MODEL)),
            pl.BlockSpec((tm, D_MODEL), lambda i: (i, COL_G // D_MODEL + 1)),
            full((D_INNER, D_MODEL)), full((ATT_WIDTH, D_MODEL)), full((D_MODEL, D_MODEL)),
            full((1, D_MODEL)), full((1, D_MODEL)), full((1, ATT_V_DIM)), full((1, D_MODEL)),
            full((LANES, D_MODEL)), full((LANES, 1)),
        ],
        out_specs=[tok(D_MODEL), tok(D_MODEL), per_block(SUBLANES, tm), per_block(SUBLANES, tm),
                   per_block(SUBLANES, tm), per_block(N_EXPERTS, 1)],
        out_shape=[
            jax.ShapeDtypeStruct((rows, D_MODEL), F32),
            jax.ShapeDtypeStruct((rows, D_MODEL), BF16),
            jax.ShapeDtypeStruct((rows // tm, SUBLANES, tm), I32),
            jax.ShapeDtypeStruct((rows // tm, SUBLANES, tm), F32),
            jax.ShapeDtypeStruct((rows // tm, SUBLANES, tm), I32),
            jax.ShapeDtypeStruct((rows // tm, N_EXPERTS, 1), I32),
        ],
        compiler_params=_cparams(("parallel",)),
        name="merge_router",
    )(x, yn, o, proj, proj, wso, wao, wo, bg1, bg2, sub_g, n2_g, wr_t, br_c)


def _seg_copy(local_ref, lo, hbm_ref, hi, sem, to_hbm):
    loc = local_ref.at[pl.ds(pl.multiple_of(lo, SEG_PAD), SEG_PAD), :]
    hbm = hbm_ref.at[pl.ds(pl.multiple_of(hi, SEG_PAD), SEG_PAD), :]
    return pltpu.make_async_copy(loc, hbm, sem) if to_hbm else pltpu.make_async_copy(hbm, loc, sem)


def _seg_copies_start(tab_ref, local_ref, hbm_ref, sem, to_hbm):
    def per_expert(e, total):
        n, lo, hi = tab_ref[0, e], tab_ref[1, e], tab_ref[2, e]

        def per_copy(c, _):
            _seg_copy(local_ref, lo + c * SEG_PAD, hbm_ref, hi + c * SEG_PAD, sem, to_hbm).start()
            return 0

        lax.fori_loop(0, n, per_copy, 0)
        return total + n

    return lax.fori_loop(0, N_EXPERTS, per_expert, 0)


def _seg_copies_wait(n, local_ref, hbm_ref, sem, to_hbm):
    def one(c, _):
        _seg_copy(local_ref, 0, hbm_ref, 0, sem, to_hbm).wait()
        return 0

    lax.fori_loop(0, n, one, 0)


def _dispatch_kernel(tab_ref, lpos_ref, u_ref, *rest, fill_tails):
    if fill_tails:
        tail_ref, _, xs_ref, loc_scr, zero_scr, sem = rest
    else:
        xs_ref, loc_scr, sem = rest
    lr, tm = loc_scr.shape[0], u_ref.shape[0]
    lpos = lpos_ref[...]
    p = lax.broadcasted_iota(I32, (lr, tm), 0)
    perm = jnp.zeros((lr, tm), F32)
    for k in range(TOP_K):
        perm = jnp.where(p == lpos[k:k + 1, :], 1.0, perm)
    loc_scr[...] = jnp.dot(perm.astype(BF16), u_ref[...], preferred_element_type=F32).astype(loc_scr.dtype)
    n = _seg_copies_start(tab_ref, loc_scr, xs_ref, sem, to_hbm=True)
    if fill_tails:
        zero_scr[...] = jnp.zeros(zero_scr.shape, zero_scr.dtype)

        def per_expert(e, total):
            cnt, hi = tail_ref[0, e], tail_ref[1, e]

            def per_copy(c, _):
                _seg_copy(zero_scr, 0, xs_ref, hi + c * SEG_PAD, sem, True).start()
                return 0

            lax.fori_loop(0, cnt, per_copy, 0)
            return total + cnt

        n = n + lax.fori_loop(0, N_EXPERTS, per_expert, 0)
    _seg_copies_wait(n, loc_scr, xs_ref, sem, to_hbm=True)


def _local_rows(tm):
    return tm * TOP_K + N_EXPERTS * SEG_PAD


def _dispatch(tab, lpos_t, u2, *, tm, xs_rows=None, xs=None, tail=None):
    rows = u2.shape[0]
    in_specs = [
        pl.BlockSpec((None, 3, N_EXPERTS), lambda i: (i, 0, 0), memory_space=pltpu.SMEM),
        pl.BlockSpec((None, TOP_K, tm), lambda i: (i, 0, 0)),
        pl.BlockSpec((tm, D_MODEL), lambda i: (i, 0)),
    ]
    args = [tab, lpos_t, u2]
    scratch = [pltpu.VMEM((_local_rows(tm), D_MODEL), BF16)]
    aliases = {}
    if tail is not None:
        assert xs is not None and rows == tm
        in_specs += [pl.BlockSpec(memory_space=pltpu.SMEM), pl.BlockSpec(memory_space=pl.ANY)]
        args += [tail, xs]
        scratch.append(pltpu.VMEM((SEG_PAD, D_MODEL), BF16))
        aliases = {4: 0}
        xs_rows = xs.shape[0]
    return pl.pallas_call(
        functools.partial(_dispatch_kernel, fill_tails=tail is not None),
        grid=(rows // tm,),
        in_specs=in_specs,
        out_specs=pl.BlockSpec(memory_space=pl.ANY),
        out_shape=jax.ShapeDtypeStruct((xs_rows, D_MODEL), BF16),
        scratch_shapes=scratch + [pltpu.SemaphoreType.DMA(())],
        input_output_aliases=aliases,
        compiler_params=_cparams(("arbitrary",)),
        name="moe_dispatch",
    )(*args)


def _ffn_kernel(be_ref, na_ref, x_ref, wgu_ref, bgu_ref, wd_ref, bd_ref, y_ref, wgu_scr, wd_scr):
    i = pl.program_id(0)
    active = i < na_ref[0]

    @pl.when(active & ((i == 0) | (be_ref[i] != be_ref[jnp.maximum(i - 1, 0)])))
    def _():
        wgu_scr[...] = wgu_ref[...].astype(BF16)
        wd_scr[...] = wd_ref[...].astype(BF16)

    @pl.when(active)
    def _():
        gu = jnp.dot(x_ref[...], wgu_scr[...], preferred_element_type=F32) + bgu_ref[...]
        gate = jnp.minimum(gu[:, :D_FF], SWIGLU_LIMIT)
        up = jnp.clip(gu[:, D_FF:], -SWIGLU_LIMIT, SWIGLU_LIMIT)
        hdn = (up + 1.0) * gate * _sigmoid(SWIGLU_ALPHA * gate)
        y = jnp.dot(hdn.astype(BF16), wd_scr[...], preferred_element_type=F32) + bd_ref[...]
        y_ref[...] = y.astype(y_ref.dtype)


def _ffn(block_exp, n_active, xs, wgu, bgu, wd, bd):
    n_blocks = xs.shape[0] // MOE_BLK

    def blk(i, be, na):
        return jnp.minimum(i, na[0] - 1)

    return pl.pallas_call(
        _ffn_kernel,
        grid_spec=pltpu.PrefetchScalarGridSpec(
            num_scalar_prefetch=2,
            grid=(n_blocks,),
            in_specs=[
                pl.BlockSpec((MOE_BLK, D_MODEL), lambda i, be, na: (blk(i, be, na), 0)),
                pl.BlockSpec((None, D_MODEL, 2 * D_FF), lambda i, be, na: (be[blk(i, be, na)], 0, 0)),
                pl.BlockSpec((None, 1, 2 * D_FF), lambda i, be, na: (be[blk(i, be, na)], 0, 0)),
                pl.BlockSpec((None, D_FF, D_MODEL), lambda i, be, na: (be[blk(i, be, na)], 0, 0)),
                pl.BlockSpec((None, 1, D_MODEL), lambda i, be, na: (be[blk(i, be, na)], 0, 0)),
            ],
            out_specs=pl.BlockSpec((MOE_BLK, D_MODEL), lambda i, be, na: (blk(i, be, na), 0)),
            scratch_shapes=[pltpu.VMEM((D_MODEL, 2 * D_FF), BF16), pltpu.VMEM((D_FF, D_MODEL), BF16)],
        ),
        out_shape=jax.ShapeDtypeStruct(xs.shape, xs.dtype),
        compiler_params=_cparams(("arbitrary",)),
        name="moe_ffn",
    )(block_exp, n_active, xs, wgu, bgu, wd, bd)


def _combine_kernel(tab_ref, lpos_ref, gate_ref, h1_ref, fg_ref, ys_ref, y_ref, loc_scr, sem):
    lr, tm = loc_scr.shape[0], h1_ref.shape[0]
    loc_scr[...] = jnp.zeros(loc_scr.shape, loc_scr.dtype)
    n = _seg_copies_start(tab_ref, loc_scr, ys_ref, sem, to_hbm=False)
    lpos = lpos_ref[...]
    gate = gate_ref[...]
    p = lax.broadcasted_iota(I32, (tm, lr), 1)
    pick = jnp.zeros((tm, lr), F32)
    for k in range(TOP_K):
        pick = jnp.where(p == lpos[:, k:k + 1], gate[:, k:k + 1], pick)
    _seg_copies_wait(n, loc_scr, ys_ref, sem, to_hbm=False)
    h = h1_ref[...] + jnp.dot(pick.astype(BF16), loc_scr[...], preferred_element_type=F32)
    ms = jnp.mean(h * h, axis=-1, keepdims=True)
    y_ref[...] = h * lax.rsqrt(ms + EPS) * fg_ref[...]


def _combine(tab, lpos, gate, h1, fg, ys, *, tm):
    rows = h1.shape[0]
    return pl.pallas_call(
        _combine_kernel,
        grid=(rows // tm,),
        in_specs=[
            pl.BlockSpec((None, 3, N_EXPERTS), lambda i: (i, 0, 0), memory_space=pltpu.SMEM),
            pl.BlockSpec((tm, TOP_K), lambda i: (i, 0)),
            pl.BlockSpec((tm, TOP_K), lambda i: (i, 0)),
            pl.BlockSpec((tm, D_MODEL), lambda i: (i, 0)),
            pl.BlockSpec((1, D_MODEL), lambda i: (0, 0)),
            pl.BlockSpec(memory_space=pl.ANY),
        ],
        out_specs=pl.BlockSpec((tm, D_MODEL), lambda i: (i, 0)),
        out_shape=jax.ShapeDtypeStruct((rows, D_MODEL), F32),
        scratch_shapes=[pltpu.VMEM((_local_rows(tm), D_MODEL), ys.dtype), pltpu.SemaphoreType.DMA(())],
        compiler_params=_cparams(("arbitrary",)),
        name="moe_combine",
    )(tab, lpos, gate, h1, fg, ys)


def _rope_tables(pos):
    d = ATT_HEAD_DIM
    inv = ROPE_THETA ** (-jnp.arange(0, d, 2, dtype=F32) / d)
    ang = pos.astype(F32)[:, None] * inv[None, :]
    cos = jnp.cos(ang)
    sin = jnp.sin(ang)
    cos_h = jnp.concatenate([cos, cos], axis=1)
    sin_h = jnp.concatenate([-sin, sin], axis=1)
    return jnp.tile(cos_h, (1, LANES // d)), jnp.tile(sin_h, (1, LANES // d))


def _conv_by_group(a):
    lead = a.shape[:-1]
    x = a[..., :D_INNER].reshape(lead + (SSD_GROUPS, GROUP_W))
    b = a[..., D_INNER:D_INNER + SSD_GN].reshape(lead + (SSD_GROUPS, SSD_STATE))
    c = a[..., D_INNER + SSD_GN:].reshape(lead + (SSD_GROUPS, SSD_STATE))
    return jnp.concatenate([x, b, c], axis=-1)


def _conv_from_group(a):
    lead = a.shape[:-2]
    x = a[..., :GROUP_W].reshape(lead + (D_INNER,))
    b = a[..., GROUP_W:GROUP_W + SSD_STATE].reshape(lead + (SSD_GN,))
    c = a[..., GROUP_W + SSD_STATE:].reshape(lead + (SSD_GN,))
    return jnp.concatenate([x, b, c], axis=-1)


def _conv_prev_blocks(prev):
    g = jnp.moveaxis(_conv_by_group(prev.astype(F32)), 1, 2)
    return jnp.pad(g, ((0, 0), (0, 0), (SUBLANES - (CONV_W - 1), 0), (0, 0)))


def _conv_tail_rows(ct):
    return _conv_from_group(jnp.moveaxis(ct[:, :, SUBLANES - (CONV_W - 1):, :], 1, 2))


def kernel(x_prompt, x_sample, cache_k, cache_v, state_ssm, state_conv, meta_tokens, norm1_g, w_in, conv_w, conv_b, dt_bias, a_log, d_skip, ssd_norm_g, lambda_q1, lambda_k1, lambda_q2, lambda_k2, subln_g, w_ssd_out, w_att_out, b_gate, w_o, norm2_g, w_router, b_router, w_gu, b_gu, w_down, b_down, final_norm_g):
    batch, seq, _ = x_prompt.shape
    dbatch, dseq, _ = x_sample.shape
    past = cache_k.shape[2]
    depth = norm1_g.shape[0]
    assert depth == 1 and dseq == N_META
    assert seq % SSD_L == 0 and seq % ATT_TQ == 0 and seq % TOKEN_TM == 0 and ATT_TQ % CHUNK == 0
    assert TOKEN_TM % ATT_TQ == 0 and PROJ_TM % ATT_TQ == 0
    n_p = batch * seq
    n_dec = dbatch * dseq
    n_s = n_dec + N_META
    lam_init = 0.8 - 0.6 * math.exp(-0.3 * 0)
    l = 0

    wi = w_in[l]
    o_z, o_xbc, o_dt = 0, D_INNER, D_INNER + CONV_DIM
    o_q = o_dt + SSD_HEADS
    w_main = jnp.concatenate([wi[:, o_z:o_xbc], wi[:, o_xbc:o_dt], wi[:, o_q:]], axis=1).astype(BF16)
    w_dt_heads = wi[:, o_dt:o_q].reshape(D_MODEL, SSD_GROUPS, SSD_HPG)
    w_dt = jnp.zeros((D_MODEL, SSD_GROUPS, LANES), F32).at[:, :, :SSD_HPG].set(
        w_dt_heads).reshape(D_MODEL, SSD_GROUPS * LANES).astype(BF16)
    w_dt_c = jnp.zeros((D_MODEL, LANES), F32).at[:, :SSD_HEADS].set(
        w_dt_heads.transpose(0, 2, 1).reshape(D_MODEL, SSD_HEADS)).astype(BF16)
    g1 = norm1_g[l].reshape(1, D_MODEL)
    cw_g = jnp.moveaxis(_conv_by_group(conv_w[l]), 0, 1)
    cb_g = _conv_by_group(conv_b[l])[:, None, :]
    a_neg = -jnp.exp(a_log[l].astype(F32))
    norm_g = ssd_norm_g[l].reshape(1, D_INNER)
    lam = (jnp.exp(jnp.sum(lambda_q1[l].astype(F32) * lambda_k1[l].astype(F32)))
           - jnp.exp(jnp.sum(lambda_q2[l].astype(F32) * lambda_k2[l].astype(F32))) + lam_init).reshape(1)
    wso = w_ssd_out[l].astype(BF16)
    wao = w_att_out[l].astype(BF16)
    wo = w_o[l].astype(BF16)
    bg1 = b_gate[l][:D_MODEL].reshape(1, D_MODEL)
    bg2 = b_gate[l][D_MODEL:].reshape(1, D_MODEL)
    sub_g = subln_g[l].reshape(1, ATT_V_DIM)
    n2_g = norm2_g[l].reshape(1, D_MODEL)
    wr = jnp.zeros((LANES, D_MODEL), F32).at[:N_EXPERTS, :].set(w_router[l].T)
    br = jnp.zeros((LANES, 1), F32).at[:N_EXPERTS, 0].set(b_router[l])
    wgu = w_gu[l]
    bgu = b_gu[l][:, None, :]
    wd = w_down[l]
    bd = b_down[l][:, None, :]
    fg = final_norm_g.reshape(1, D_MODEL)

    xp = x_prompt.reshape(n_p, D_MODEL)
    xs_rows = jnp.concatenate([x_sample.reshape(n_dec, D_MODEL), meta_tokens.astype(x_prompt.dtype)], axis=0)
    cos_p, sin_p = _rope_tables(N_META + jnp.arange(seq, dtype=I32))
    pos_s = jnp.concatenate([jnp.tile(past + jnp.arange(dseq, dtype=I32), dbatch), jnp.arange(N_META, dtype=I32)])
    cos_s, sin_s = _rope_tables(pos_s)

    tm_p = PROJ_TM if seq % PROJ_TM == 0 else TOKEN_TM
    proj_p, dt_p, vt_p, kc_p, vc_p = _in_proj(xp, g1, w_main, w_dt_c, cos_p, sin_p, tm_p, seq // tm_p, vt_block=ATT_TQ)
    proj_s, dt_s = _in_proj(xs_rows, g1, w_main, w_dt, cos_s, sin_s, n_s, 1)

    ssd_args = (cw_g, cb_g, dt_bias[l], a_neg, d_skip[l], norm_g)
    zero_h = jnp.zeros((1, SSD_HEADS, SSD_HEAD_DIM, SSD_STATE), F32)
    zero_c = jnp.zeros((1, SSD_GROUPS, SUBLANES, CONV_GW), F32)
    yn_m, h_m, ct_m = _ssd(proj_s, dt_s, zero_h, zero_c, *ssd_args, n_seq=1, t=N_META, l=N_META,
                           row0=n_dec, shared_state=True)
    yn_p, h_p, ct_p = _ssd(proj_p, dt_p, h_m, ct_m, *ssd_args, n_seq=batch, t=seq, l=SSD_L, row0=0,
                           shared_state=True)
    yn_d, h_d, ct_d = _ssd(proj_s, dt_s, state_ssm[l].astype(F32), _conv_prev_blocks(state_conv[l]), *ssd_args,
                           n_seq=dbatch, t=dseq, l=dseq, row0=0, shared_state=False)
    yn_s = jnp.concatenate([yn_d, yn_m], axis=0)

    mvt =proj_s[n_dec:n_s, COL_V:COL_V + ATT_WIDTH].reshape(N_META, ATT_HEADS, ATT_V_DIM).transpose(1, 2, 0)
    o_p = _attn_prompt(lam, proj_p, vt_p, proj_s, mvt, batch=batch, seq=seq, meta_row0=n_dec)
    o_d = _attn_short(lam, proj_s, cache_k[l].reshape(dbatch, past, ATT_WIDTH),
                      cache_v[l].reshape(dbatch, past * ATT_HEADS, ATT_V_DIM), n_seq=dbatch, t=dseq, row0=0)
    o_m = _attn_short(lam, proj_s, None, None, n_seq=1, t=N_META, row0=n_dec)
    o_s = jnp.concatenate([o_d, o_m], axis=0)

    merge_w = (wso, wao, wo, bg1, bg2, sub_g, n2_g, wr, br)
    h1_p, u2_p, e_p, gt_p, rk_p, cnt_p = _merge(xp, yn_p, o_p, proj_p, *merge_w, tm=TOKEN_TM,
                                                sub_scale=1.0 - lam_init)
    h1_s, u2_s, e_s, gt_s, rk_s, cnt_s = _merge(xs_rows, yn_s, o_s, proj_s, *merge_w, tm=n_s,
                                                sub_scale=1.0 - lam_init)

    nb_p = n_p // TOKEN_TM
    cnt = jnp.concatenate([cnt_p[:, :, 0], cnt_s[:, :, 0]], axis=0)
    seg = (cnt + SEG_PAD - 1) // SEG_PAD * SEG_PAD
    local_start = jnp.cumsum(seg, axis=1) - seg
    per_expert = jnp.sum(seg, axis=0)
    padded = (per_expert + MOE_BLK - 1) // MOE_BLK * MOE_BLK
    pend = jnp.cumsum(padded)
    pstart = pend - padded
    hbm_start = pstart[None, :] + jnp.cumsum(seg, axis=0) - seg
    tab = jnp.stack([seg // SEG_PAD, local_start, hbm_start], axis=1).astype(I32)
    n_rows_max = (n_p + n_s) * TOP_K + (nb_p + 1) * N_EXPERTS * (SEG_PAD - 1)
    n_blocks = -(-n_rows_max // MOE_BLK) + N_EXPERTS
    block_start = jnp.arange(n_blocks, dtype=I32) * MOE_BLK
    block_exp = jnp.minimum(jnp.sum((pend[None, :] <= block_start[:, None]).astype(I32), axis=1), N_EXPERTS - 1)
    n_active = (pend[-1:] // MOE_BLK).astype(I32)

    def local_rows_of(e, rk, starts):
        hot = e[:, :TOP_K, :, None] == jnp.arange(N_EXPERTS, dtype=I32)
        return (jnp.sum(jnp.where(hot, starts[:, None, None, :], 0), axis=-1) + rk[:, :TOP_K]).astype(I32)

    def by_token(a):
        return a[:, :TOP_K].transpose(0, 2, 1).reshape(-1, TOP_K)

    lpos_pt = local_rows_of(e_p, rk_p, local_start[:nb_p])
    lpos_st = local_rows_of(e_s, rk_s, local_start[nb_p:])
    lpos_p, lpos_s = by_token(lpos_pt), by_token(lpos_st)
    gt_p, gt_s = by_token(gt_p), by_token(gt_s)
    tail = jnp.stack([(padded - per_expert) // SEG_PAD, pstart + per_expert]).astype(I32)
    xs = _dispatch(tab[:nb_p], lpos_pt, u2_p, tm=TOKEN_TM, xs_rows=n_blocks * MOE_BLK)
    xs = _dispatch(tab[nb_p:], lpos_st, u2_s, tm=n_s, xs=xs, tail=tail)
    ys = _ffn(block_exp, n_active, xs, wgu, bgu, wd, bd)
    y_p = _combine(tab[:nb_p], lpos_p, gt_p, h1_p, fg, ys, tm=TOKEN_TM)
    y_s = _combine(tab[nb_p:], lpos_s, gt_s, h1_s, fg, ys, tm=n_s)

    def kv_rows(proj, col, lo, hi):
        return proj[lo:hi, col:col + ATT_WIDTH].astype(F32)

    def with_meta(cache, col, tail_shape):
        meta = kv_rows(proj_s, col, n_dec, n_s).reshape((1, N_META) + tail_shape)
        slab = cache.reshape((batch, N_META + seq) + tail_shape)
        return slab.at[:, :N_META].set(jnp.broadcast_to(meta, (batch, N_META) + tail_shape))

    new_k_p = with_meta(kc_p, COL_K, (ATT_WIDTH,)).reshape(1, batch, N_META + seq, 2 * ATT_HEADS, ATT_HEAD_DIM)
    new_v_p = with_meta(vc_p, COL_V, (ATT_HEADS, ATT_V_DIM))[None]
    new_k_s = kv_rows(proj_s, COL_K, 0, n_dec).reshape(1, dbatch, dseq, 2 * ATT_HEADS, ATT_HEAD_DIM)
    new_v_s = kv_rows(proj_s, COL_V, 0, n_dec).reshape(1, dbatch, dseq, ATT_HEADS, ATT_V_DIM)
    return (y_p.reshape(batch, seq, D_MODEL),
            y_s[:n_dec].reshape(dbatch, dseq, D_MODEL),
            new_k_p, new_v_p,
            h_p.astype(state_ssm.dtype)[None],
            _conv_tail_rows(ct_p).astype(x_prompt.dtype)[None],
            new_k_s, new_v_s,
            h_d.astype(state_ssm.dtype)[None],
            _conv_tail_rows(ct_d).astype(x_sample.dtype)[None])
```

```python
import functools
import math

import jax
import jax.numpy as jnp
from jax import lax
from jax.experimental import pallas as pl
from jax.experimental.pallas import tpu as pltpu

F32 = jnp.float32
BF16 = jnp.bfloat16
I32 = jnp.int32

D_MODEL = 1024
D_INNER = 2048
SSD_HEADS = 32
SSD_HEAD_DIM = 64
SSD_GROUPS = 8
SSD_HPG = SSD_HEADS // SSD_GROUPS
SSD_STATE = 128
SSD_GN = SSD_GROUPS * SSD_STATE
CONV_W = 4
CONV_DIM = D_INNER + 2 * SSD_GN
ATT_HEADS = 8
ATT_HEAD_DIM = 64
ATT_V_DIM = 128
ATT_WIDTH = 1024
CHUNK = 64
N_META = 16
EPS = 1e-6
ROPE_THETA = 10000.0
LOG2_E = math.log2(math.e)
N_EXPERTS = 32
TOP_K = 4
D_FF = 1024
SWIGLU_LIMIT = 7.0
SWIGLU_ALPHA = 1.702

COL_Z = 0
COL_X = COL_Z + D_INNER
COL_B = COL_X + D_INNER
COL_C = COL_B + SSD_GN
COL_Q = COL_C + SSD_GN
COL_K = COL_Q + ATT_WIDTH
COL_V = COL_K + ATT_WIDTH
COL_G = COL_V + ATT_WIDTH
N_MAIN = COL_G + 2 * D_MODEL

LANES = 128
SUBLANES = 8
GROUP_W = D_INNER // SSD_GROUPS
CONV_GW = GROUP_W + 2 * SSD_STATE
PROJ_TN = 1024
PROJ_TM = 1024
TOKEN_TM = 512
SSD_L = 256
SSD_GP = 4
ATT_TQ = 512
MOE_BLK = 512
SEG_PAD = 16
ONES_ROWS = 16
VMEM_LIMIT = 56 * 1024 * 1024
NEG_BIG = -1e30


def _cparams(sem):
    return pltpu.CompilerParams(dimension_semantics=sem, vmem_limit_bytes=VMEM_LIMIT)


def _sigmoid(x):
    return 1.0 / (1.0 + jnp.exp(-x))


def _inproj_kernel(x_ref, g_ref, w_ref, wdt_ref, cos_ref, sin_ref, o_ref, dt_ref, *rest, emit_vt,
                   seq_blocks):
    if emit_vt:
        vt_ref, kout_ref, vout_ref, u_scr, kbuf, vbuf, sems = rest
    else:
        (u_scr,) = rest
    j = pl.program_id(1)

    @pl.when(j == 0)
    def _():
        x = x_ref[...]
        ms = jnp.mean(x * x, axis=-1, keepdims=True)
        u = (x * lax.rsqrt(ms + EPS) * g_ref[...]).astype(BF16)
        u_scr[...] = u
        dt_ref[...] = jnp.dot(u, wdt_ref[...], preferred_element_type=F32)

    acc = jnp.dot(u_scr[...], w_ref[...], preferred_element_type=F32)
    is_q = j == COL_Q // PROJ_TN
    is_k = j == COL_K // PROJ_TN

    @pl.when(is_q | is_k)
    def _():
        cos = cos_ref[...]
        sin = sin_ref[...]
        lane = lax.broadcasted_iota(I32, cos.shape, 1)
        half = ATT_HEAD_DIM // 2
        first_half = (lane % ATT_HEAD_DIM) < half
        scale = jnp.where(is_q, ATT_HEAD_DIM ** -0.5 * LOG2_E, 1.0).astype(F32)
        for c in range(PROJ_TN // LANES):
            a = acc[:, c * LANES:(c + 1) * LANES]
            swapped = jnp.where(first_half, pltpu.roll(a, LANES - half, 1), pltpu.roll(a, half, 1))
            roped = (a * cos + swapped * sin) * scale
            o_ref[:, c * LANES:(c + 1) * LANES] = roped.astype(o_ref.dtype)
            if emit_vt:
                kbuf[:, c * LANES:(c + 1) * LANES] = roped

    @pl.when(jnp.logical_not(is_q | is_k))
    def _():
        o_ref[...] = acc.astype(o_ref.dtype)

    if emit_vt:
        i = pl.program_id(0)
        tm = x_ref.shape[0]
        row0 = (i // seq_blocks) * (N_META + seq_blocks * tm) + N_META + (i % seq_blocks) * tm

        def cache_copy(buf, out_ref, sem):
            per_row = buf.shape[0] // tm
            rows = out_ref.at[pl.ds(pl.multiple_of(row0 * per_row, SUBLANES), tm * per_row), :]
            return pltpu.make_async_copy(buf, rows, sem)

        @pl.when(is_k)
        def _():
            cache_copy(kbuf, kout_ref, sems.at[0]).start()

        @pl.when(j == COL_V // PROJ_TN)
        def _():
            for h in range(ATT_HEADS):
                vbuf[pl.ds(h, tm, stride=ATT_HEADS), :] = acc[:, h * ATT_V_DIM:(h + 1) * ATT_V_DIM]
            cache_copy(vbuf, vout_ref, sems.at[1]).start()
            tk = vt_ref.shape[-1]
            for h in range(ATT_HEADS):
                for s in range(vt_ref.shape[1]):
                    blk = acc[s * tk:(s + 1) * tk, h * ATT_V_DIM:(h + 1) * ATT_V_DIM]
                    vt_ref[h, s] = blk.T.astype(vt_ref.dtype)

        @pl.when(j == pl.num_programs(1) - 1)
        def _():
            cache_copy(kbuf, kout_ref, sems.at[0]).wait()
            cache_copy(vbuf, vout_ref, sems.at[1]).wait()


def _in_proj(x, g1, w_main, w_dt, cos_t, sin_t, tm, rope_blocks, vt_block=None):
    rows = x.shape[0]
    grid = (rows // tm, N_MAIN // PROJ_TN)
    scratch = [pltpu.VMEM((tm, D_MODEL), BF16)]
    dt_w = w_dt.shape[1]
    out_specs = [
        pl.BlockSpec((tm, PROJ_TN), lambda i, j: (i, j)),
        pl.BlockSpec((tm, dt_w), lambda i, j: (i, 0)),
    ]
    out_shape = [
        jax.ShapeDtypeStruct((rows, N_MAIN), BF16),
        jax.ShapeDtypeStruct((rows, dt_w), F32),
    ]
    if vt_block is not None:
        per = tm // vt_block
        out_specs.append(pl.BlockSpec((None, ATT_HEADS, per, ATT_V_DIM, vt_block),
                                      lambda i, j: (i // rope_blocks, 0, i % rope_blocks, 0, 0)))
        streams = rows // (rope_blocks * tm)
        out_shape.append(jax.ShapeDtypeStruct(
            (streams, ATT_HEADS, rope_blocks * per, ATT_V_DIM, vt_block), BF16))
        cache_rows = streams * (N_META + rope_blocks * tm)
        out_specs += [pl.BlockSpec(memory_space=pl.ANY), pl.BlockSpec(memory_space=pl.ANY)]
        out_shape += [jax.ShapeDtypeStruct((cache_rows, ATT_WIDTH), F32),
                      jax.ShapeDtypeStruct((cache_rows * ATT_HEADS, ATT_V_DIM), F32)]
        scratch += [pltpu.VMEM((tm, ATT_WIDTH), F32), pltpu.VMEM((tm * ATT_HEADS, ATT_V_DIM), F32),
                    pltpu.SemaphoreType.DMA((2,))]
    return pl.pallas_call(
        functools.partial(_inproj_kernel, emit_vt=vt_block is not None, seq_blocks=rope_blocks),
        grid=grid,
        in_specs=[
            pl.BlockSpec((tm, D_MODEL), lambda i, j: (i, 0)),
            pl.BlockSpec((1, D_MODEL), lambda i, j: (0, 0)),
            pl.BlockSpec((D_MODEL, PROJ_TN), lambda i, j: (0, j)),
            pl.BlockSpec((D_MODEL, dt_w), lambda i, j: (0, 0)),
            pl.BlockSpec((tm, LANES), lambda i, j: (i % rope_blocks, 0)),
            pl.BlockSpec((tm, LANES), lambda i, j: (i % rope_blocks, 0)),
        ],
        out_specs=out_specs,
        out_shape=out_shape,
        scratch_shapes=scratch,
        compiler_params=_cparams(("parallel", "arbitrary")),
        name="in_proj",
    )(x, g1, w_main, w_dt, cos_t, sin_t)


def _ssd_kernel(x_ref, b_ref, c_ref, z_ref, dt_ref, h0_ref, cp_ref, cw_ref, cb_ref, dtb_ref,
                aneg_ref, dsk_ref, ng_ref, y_ref, hf_ref, ct_ref, h_scr, f_scr):
    c = pl.program_id(2)
    L = x_ref.shape[0]

    @pl.when(c == 0)
    def _():
        h_scr[...] = h0_ref[...]
        f_scr[0:SUBLANES, :] = cp_ref[...]

    f_scr[SUBLANES:SUBLANES + L, 0:GROUP_W] = x_ref[...].astype(F32)
    f_scr[SUBLANES:SUBLANES + L, GROUP_W:GROUP_W + SSD_STATE] = b_ref[...].astype(F32)
    f_scr[SUBLANES:SUBLANES + L, GROUP_W + SSD_STATE:CONV_GW] = c_ref[...].astype(F32)
    w = cw_ref[...]
    acc = cb_ref[...]
    for i in range(CONV_W):
        lo = SUBLANES - (CONV_W - 1) + i
        acc = acc + w[i:i + 1, :] * f_scr[lo:lo + L, :]
    xc = acc * _sigmoid(acc)
    tail = f_scr[L:L + SUBLANES, :]
    f_scr[0:SUBLANES, :] = tail
    ct_ref[...] = tail

    xg = xc[:, 0:GROUP_W]
    bm = xc[:, GROUP_W:GROUP_W + SSD_STATE].astype(BF16)
    cm = xc[:, GROUP_W + SSD_STATE:CONV_GW].astype(BF16)

    dtr = dt_ref[...] + dtb_ref[...]
    dt = jnp.maximum(dtr, 0.0) + jnp.log(1.0 + jnp.exp(-jnp.abs(dtr)))
    da = dt * aneg_ref[...]
    ti = lax.broadcasted_iota(I32, (L, L), 0)
    si = lax.broadcasted_iota(I32, (L, L), 1)
    causal = si <= ti
    tril = causal.astype(F32)
    cum = jnp.dot(tril, da, preferred_element_type=F32, precision=lax.Precision.HIGHEST)
    sel = (lax.broadcasted_iota(I32, (SUBLANES, LANES), 0)
           == lax.broadcasted_iota(I32, (SUBLANES, LANES), 1)).astype(F32)
    cum_t = lax.dot_general(sel, cum, (((1,), (1,)), ((), ())), preferred_element_type=F32,
                            precision=lax.Precision.HIGHEST)

    cb = lax.dot_general(cm, bm, (((1,), (1,)), ((), ())), preferred_element_type=F32)
    dsk = dsk_ref[...]
    ys = []
    for r in range(SSD_HPG):
        col = cum[:, r:r + 1]
        row = cum_t[r:r + 1, :]
        dec = jnp.exp(jnp.where(causal, col - row, NEG_BIG))
        m = (cb * dec).astype(BF16)
        xh = xg[:, r * SSD_HEAD_DIM:(r + 1) * SSD_HEAD_DIM]
        xdt = xh * dt[:, r:r + 1]
        h_prev = h_scr[r]
        y = jnp.dot(m, xdt.astype(BF16), preferred_element_type=F32)
        y = y + jnp.exp(col) * lax.dot_general(cm, h_prev.astype(BF16), (((1,), (1,)), ((), ())),
                                               preferred_element_type=F32)
        y = y + dsk[:, r:r + 1] * xh
        ys.append(y)
        tot = cum[L - 1:L, r:r + 1]
        xw = (xdt * jnp.exp(tot - col)).astype(BF16)
        upd = lax.dot_general(xw, bm, (((0,), (0,)), ((), ())), preferred_element_type=F32)
        h_scr[r] = h_prev * jnp.exp(tot) + upd
    yg = jnp.concatenate(ys, axis=1)
    z = z_ref[...].astype(F32)
    yz = yg * (z * _sigmoid(z))
    ms = jnp.mean(yz * yz, axis=-1, keepdims=True)
    y_ref[...] = (yz * lax.rsqrt(ms + EPS) * ng_ref[...]).astype(y_ref.dtype)
    hf_ref[...] = h_scr[...]


def _ssd_long_kernel(x_ref, b_ref, c_ref, z_ref, dt_ref, h0_ref, cp_ref, cw_ref, cb_ref, dtb_ref,
                     aneg_ref, dsk_ref, ng_ref, tri_ref, shift_ref, y_ref, hf_ref, ct_ref, h_scr, f_scr,
                     dtt_scr):
    c = pl.program_id(2)
    L = x_ref.shape[0]
    reps = L // LANES

    @pl.when(c == 0)
    def _():
        h_scr[...] = h0_ref[...]
        f_scr[:, 0:SUBLANES, :] = cp_ref[...]
        f_scr[:, SUBLANES:2 * SUBLANES, :] = jnp.zeros((SSD_GP, SUBLANES, CONV_GW), F32)

    tri = tri_ref[...]
    tri_b = tri.astype(BF16)
    visible = tri > 0.5
    dtt_scr[...] = dt_ref[...].T
    for gi in range(SSD_GP):
        xs_ = slice(gi * GROUP_W, (gi + 1) * GROUP_W)
        ns_ = slice(gi * SSD_STATE, (gi + 1) * SSD_STATE)
        xb = jnp.concatenate([x_ref[:, xs_], b_ref[:, ns_], c_ref[:, ns_]], axis=1)
        xf = xb.astype(F32)
        w = cw_ref[gi]
        acc = cb_ref[gi] + w[CONV_W - 1:CONV_W, :] * xf
        for d in range(1, CONV_W):
            sh = jnp.dot(shift_ref[d - 1], xb, preferred_element_type=F32)
            acc = acc + w[CONV_W - 1 - d:CONV_W - d, :] * sh
        corr = jnp.zeros((SUBLANES, CONV_GW), F32)
        for i in range(CONV_W - 1):
            lo = SUBLANES - (CONV_W - 1) + i
            corr = corr + w[i:i + 1, :] * f_scr[gi, lo:lo + SUBLANES, :]
        acc = jnp.concatenate([acc[0:SUBLANES] + corr, acc[SUBLANES:]], axis=0)
        xc = acc * _sigmoid(acc)
        tail = xf[L - SUBLANES:L, :]
        f_scr[gi, 0:SUBLANES, :] = tail
        ct_ref[gi] = tail

        x_t = xc[:, 0:GROUP_W].T
        bm = xc[:, GROUP_W:GROUP_W + SSD_STATE].astype(BF16)
        cm = xc[:, GROUP_W + SSD_STATE:CONV_GW].astype(BF16)

        grp = pl.program_id(1) * SSD_GP + gi
        dt_rows = [dtt_scr[pl.ds(SUBLANES * r + grp, 1), :] for r in range(SSD_HPG)]
        dtr = jnp.concatenate(dt_rows + [jnp.zeros((SUBLANES - SSD_HPG, L), F32)], axis=0)
        dtr = dtr + jnp.tile(dtb_ref[gi], (1, reps))
        dt = jnp.maximum(dtr, 0.0) + jnp.log(1.0 + jnp.exp(-jnp.abs(dtr)))
        da = dt * jnp.tile(aneg_ref[gi], (1, reps))
        d1 = da.astype(BF16)
        r1 = da - d1.astype(F32)
        d2 = r1.astype(BF16)
        d3 = (r1 - d2.astype(F32)).astype(BF16)
        cum = (jnp.dot(d1, tri_b, preferred_element_type=F32) + jnp.dot(d2, tri_b, preferred_element_type=F32)
               + jnp.dot(d3, tri_b, preferred_element_type=F32))
        cum_col = cum.T

        cb_t = lax.dot_general(bm, cm, (((1,), (1,)), ((), ())), preferred_element_type=F32)
        dsk = dsk_ref[gi]
        ys = []
        for r in range(SSD_HPG):
            hd = gi * SSD_HPG + r
            row = cum[r:r + 1, :]
            dec = jnp.exp(jnp.where(visible, row - cum_col[:, r:r + 1], NEG_BIG))
            m = (cb_t * dec).astype(BF16)
            xh = x_t[r * SSD_HEAD_DIM:(r + 1) * SSD_HEAD_DIM, :]
            xdt = xh * dt[r:r + 1, :]
            h_prev = h_scr[hd]
            y = jnp.dot(xdt.astype(BF16), m, preferred_element_type=F32)
            y = y + jnp.exp(row) * lax.dot_general(h_prev.astype(BF16), cm, (((1,), (1,)), ((), ())),
                                                   preferred_element_type=F32)
            y = y + dsk[r:r + 1, 0:1] * xh
            ys.append(y)
            tot = row[:, L - 1:L]
            xw = (xdt * jnp.exp(tot - row)).astype(BF16)
            h_scr[hd] = h_prev * jnp.exp(tot) + jnp.dot(xw, bm, preferred_element_type=F32)
        yg = jnp.concatenate(ys, axis=0).T
        z = z_ref[:, xs_].astype(F32)
        yz = yg * (z * _sigmoid(z))
        ms = jnp.mean(yz * yz, axis=-1, keepdims=True)
        y_ref[:, xs_] = (yz * lax.rsqrt(ms + EPS) * ng_ref[:, xs_]).astype(y_ref.dtype)
    hf_ref[...] = h_scr[...]


def _per_group_lanes(v):
    out = jnp.zeros((SSD_GROUPS, 1, LANES), F32)
    return out.at[:, 0, :SSD_HPG].set(v.astype(F32).reshape(SSD_GROUPS, SSD_HPG))


def _per_group_rows(v):
    out = jnp.zeros((SSD_GROUPS, SUBLANES, LANES), F32)
    return out.at[:, :SSD_HPG, :].set(
        jnp.broadcast_to(v.astype(F32).reshape(SSD_GROUPS, SSD_HPG, 1), (SSD_GROUPS, SSD_HPG, LANES)))


def _ssd(proj, dt_raw, h0, conv_prev, cw_g, cb_g, dt_bias, a_neg, d_skip, norm_g, *, n_seq, t, l, row0,
         shared_state):
    nc = t // l
    rb0 = row0 // l
    long_chunks = l % LANES == 0

    def rows(s, g, c):
        return rb0 + s * nc + c

    def sidx(s):
        return 0 if shared_state else s

    if long_chunks:
        head_rows = SUBLANES
        head_par = [_per_group_rows(v) for v in (dt_bias, a_neg, d_skip)]
        step = jnp.arange(l, dtype=I32)
        tri = (step[:, None] <= step[None, :]).astype(F32)
        shift = jnp.stack([(step[None, :] == step[:, None] - d) for d in range(1, CONV_W)]).astype(BF16)
        extra_args = [tri, shift]
        extra_specs = [pl.BlockSpec((l, l), lambda s, g, c: (0, 0)),
                       pl.BlockSpec((CONV_W - 1, l, l), lambda s, g, c: (0, 0, 0))]
        body, gp = _ssd_long_kernel, SSD_GP
        lead = (gp,)
        scratch = [pltpu.VMEM((gp, 2 * SUBLANES, CONV_GW), F32), pltpu.VMEM((LANES, l), F32)]
        dt_spec = pl.BlockSpec((l, LANES), lambda s, g, c: (rows(s, g, c), 0))
    else:
        head_rows = 1
        head_par = [_per_group_lanes(v) for v in (dt_bias, a_neg, d_skip)]
        extra_args, extra_specs = [], []
        body, gp = _ssd_kernel, 1
        lead = (None,)
        scratch = [pltpu.VMEM((l + SUBLANES, CONV_GW), F32)]
        dt_spec = pl.BlockSpec((l, LANES), lambda s, g, c: (rows(s, g, c), g))

    def per_group(*shape):
        return pl.BlockSpec(lead + shape, lambda s, g, c: (g,) + (0,) * len(shape))

    def per_seq_group(*shape, shared):
        return pl.BlockSpec((None,) + lead + shape,
                            lambda s, g, c: ((sidx(s) if shared else s), g) + (0,) * len(shape))

    def state_spec(shared):
        return pl.BlockSpec((None, gp * SSD_HPG, SSD_HEAD_DIM, SSD_STATE),
                            lambda s, g, c: ((sidx(s) if shared else s), g, 0, 0))

    return pl.pallas_call(
        body,
        grid=(n_seq, SSD_GROUPS // gp, nc),
        in_specs=[
            pl.BlockSpec((l, gp * GROUP_W), lambda s, g, c: (rows(s, g, c), COL_X // (gp * GROUP_W) + g)),
            pl.BlockSpec((l, gp * SSD_STATE), lambda s, g, c: (rows(s, g, c), COL_B // (gp * SSD_STATE) + g)),
            pl.BlockSpec((l, gp * SSD_STATE), lambda s, g, c: (rows(s, g, c), COL_C // (gp * SSD_STATE) + g)),
            pl.BlockSpec((l, gp * GROUP_W), lambda s, g, c: (rows(s, g, c), COL_Z // (gp * GROUP_W) + g)),
            dt_spec,
            state_spec(True),
            per_seq_group(SUBLANES, CONV_GW, shared=True),
            per_group(CONV_W, CONV_GW),
            per_group(1, CONV_GW),
            per_group(head_rows, LANES), per_group(head_rows, LANES), per_group(head_rows, LANES),
            pl.BlockSpec((1, gp * GROUP_W), lambda s, g, c: (0, g)),
        ] + extra_specs,
        out_specs=[
            pl.BlockSpec((l, gp * GROUP_W), lambda s, g, c: (s * nc + c, g)),
            state_spec(False),
            per_seq_group(SUBLANES, CONV_GW, shared=False),
        ],
        out_shape=[
            jax.ShapeDtypeStruct((n_seq * t, D_INNER), BF16),
            jax.ShapeDtypeStruct((n_seq, SSD_HEADS, SSD_HEAD_DIM, SSD_STATE), F32),
            jax.ShapeDtypeStruct((n_seq, SSD_GROUPS, SUBLANES, CONV_GW), F32),
        ],
        scratch_shapes=[pltpu.VMEM((gp * SSD_HPG, SSD_HEAD_DIM, SSD_STATE), F32)] + scratch,
        compiler_params=_cparams(("parallel", "parallel", "arbitrary")),
        name="ssd_long" if long_chunks else "ssd",
    )(proj, proj, proj, proj, dt_raw, h0, conv_prev, cw_g, cb_g, *head_par, norm_g, *extra_args)


def _qk(q, k):
    return lax.dot_general(q, k, (((1,), (1,)), ((), ())), preferred_element_type=F32)


def _attn_prompt_kernel(lam_ref, q_ref, k_ref, vt_ref, mk_ref, mvt_ref, o_ref, sa_scr, sb_scr, acc_scr, m_scr):
    i = pl.program_id(2)
    tq = q_ref.shape[0]
    lam = lam_ref[0]
    q = q_ref[...]
    lane = lax.broadcasted_iota(I32, q.shape, 1)
    zero = jnp.zeros_like(q)
    qm = [jnp.where(lane < ATT_HEAD_DIM, q, zero), jnp.where(lane >= ATT_HEAD_DIM, q, zero)]

    def put_scores(s_ref, j):
        kblk = k_ref[pl.ds(pl.multiple_of(j * tq, tq), tq), :]
        for r in range(2):
            s_ref[r] = _qk(kblk, qm[r])

    def with_ones(vt):
        return jnp.concatenate([vt, jnp.ones((ONES_ROWS, vt.shape[1]), BF16)], axis=0)

    def update(s_ref, j, last=False):
        vt1 = with_ones(vt_ref[j])
        for r in range(2):
            s = s_ref[r]
            m_p = m_scr[r]
            if last:
                kpos = lax.broadcasted_iota(I32, (tq, tq), 0) // CHUNK
                qpos = lax.broadcasted_iota(I32, (tq, tq), 1) // CHUNK
                s = jnp.where(kpos <= qpos, s, NEG_BIG)
                s_meta = _qk(mk_ref[...], qm[r])
                m_p = jnp.maximum(m_p, jnp.max(s_meta, axis=0, keepdims=True))
            m_n = jnp.maximum(m_p, jnp.max(s, axis=0, keepdims=True))
            alpha = jnp.exp2(m_scr[r] - m_n)
            p = jnp.exp2((s - m_n).astype(BF16))
            acc = alpha * acc_scr[r] + jnp.dot(vt1, p, preferred_element_type=F32)
            if last:
                p_meta = jnp.exp2((s_meta - m_n).astype(BF16))
                acc = acc + jnp.dot(with_ones(mvt_ref[...]), p_meta, preferred_element_type=F32)
            acc_scr[r] = acc
            m_scr[r] = m_n

    m_scr[...] = jnp.full(m_scr.shape, NEG_BIG, F32)
    acc_scr[...] = jnp.zeros(acc_scr.shape, F32)

    put_scores(sa_scr, 0)

    def body(jj, _):
        j = 2 * jj
        put_scores(sb_scr, j + 1)
        update(sa_scr, j)
        put_scores(sa_scr, j + 2)
        update(sb_scr, j + 1)
        return 0

    lax.fori_loop(0, i // 2, body, 0)

    @pl.when(i % 2 == 0)
    def _():
        update(sa_scr, i, last=True)

    @pl.when(i % 2 == 1)
    def _():
        put_scores(sb_scr, i)
        update(sa_scr, i - 1)
        update(sb_scr, i, last=True)

    outs = [acc_scr[r, :ATT_V_DIM, :] / acc_scr[r, ATT_V_DIM:ATT_V_DIM + 1, :] for r in range(2)]
    o_ref[...] = (outs[0] - lam * outs[1]).T.astype(o_ref.dtype)


def _attn_prompt(lam, proj_p, vt_p, proj_s, mvt, *, batch, seq, meta_row0):
    nq = seq // ATT_TQ
    return pl.pallas_call(
        _attn_prompt_kernel,
        grid=(batch, ATT_HEADS, nq),
        in_specs=[
            pl.BlockSpec(memory_space=pltpu.SMEM),
            pl.BlockSpec((ATT_TQ, LANES), lambda b, h, i: (b * nq + i, COL_Q // LANES + h)),
            pl.BlockSpec((seq, LANES), lambda b, h, i: (b, COL_K // LANES + h)),
            pl.BlockSpec((None, None, nq, ATT_V_DIM, ATT_TQ), lambda b, h, i: (b, h, 0, 0, 0)),
            pl.BlockSpec((N_META, LANES), lambda b, h, i: (meta_row0 // N_META, COL_K // LANES + h)),
            pl.BlockSpec((None, ATT_V_DIM, N_META), lambda b, h, i: (h, 0, 0)),
        ],
        out_specs=pl.BlockSpec((ATT_TQ, LANES), lambda b, h, i: (b * nq + i, h)),
        out_shape=jax.ShapeDtypeStruct((batch * seq, ATT_WIDTH), BF16),
        scratch_shapes=[pltpu.VMEM((2, ATT_TQ, ATT_TQ), F32),
                        pltpu.VMEM((2, ATT_TQ, ATT_TQ), F32),
                        pltpu.VMEM((2, ATT_V_DIM + ONES_ROWS, ATT_TQ), F32),
                        pltpu.VMEM((2, 1, ATT_TQ), F32)],
        compiler_params=_cparams(("parallel", "parallel", "arbitrary")),
        name="attn_prompt",
    )(lam, proj_p, proj_p, vt_p, proj_s, mvt)


def _attn_short_kernel(lam_ref, q_ref, kn_ref, vn_ref, *rest, has_cache):
    if has_cache:
        kc_ref, vc_ref, o_ref = rest
    else:
        (o_ref,) = rest
    t = q_ref.shape[0]
    nh = 2 * ATT_HEADS
    lam = lam_ref[0]
    q = q_ref[...].astype(F32)
    qb = jnp.broadcast_to(q[None], (nh, t, ATT_WIDTH)).reshape(nh * t, ATT_WIDTH)
    row_head = lax.broadcasted_iota(I32, (nh * t, ATT_WIDTH), 0) // t
    col_head = lax.broadcasted_iota(I32, (nh * t, ATT_WIDTH), 1) // ATT_HEAD_DIM
    qbd = jnp.where(row_head == col_head, qb, 0.0).astype(BF16)
    s_new = _qk(qbd, kn_ref[...])
    m = jnp.max(s_new, axis=-1, keepdims=True)
    if has_cache:
        s_old = jnp.dot(qbd, kc_ref[...].astype(BF16), preferred_element_type=F32)
        m = jnp.maximum(m, jnp.max(s_old, axis=-1, keepdims=True))
    p_new = jnp.exp2(s_new - m)
    den = jnp.sum(p_new, axis=-1, keepdims=True)
    p_new = p_new.astype(BF16)
    if has_cache:
        p_old = jnp.exp2(s_old - m)
        den = den + jnp.sum(p_old, axis=-1, keepdims=True)
        p_old = p_old.astype(BF16)
        past = p_old.shape[1]
    for h in range(ATT_HEADS):
        cols = slice(h * ATT_V_DIM, (h + 1) * ATT_V_DIM)
        pair = slice(2 * h * t, (2 * h + 2) * t)
        acc = jnp.dot(p_new[pair], vn_ref[:, cols], preferred_element_type=F32)
        if has_cache:
            v_h = vc_ref[pl.ds(h, past, stride=ATT_HEADS), :].astype(BF16)
            acc = acc + jnp.dot(p_old[pair], v_h, preferred_element_type=F32)
        acc = acc / den[pair]
        o_ref[:, cols] = (acc[0:t] - lam * acc[t:2 * t]).astype(o_ref.dtype)


def _attn_short(lam, proj_s, cache_k, cache_v, *, n_seq, t, row0):
    has_cache = cache_k is not None
    rb0 = row0 // t
    in_specs = [
        pl.BlockSpec(memory_space=pltpu.SMEM),
        pl.BlockSpec((t, ATT_WIDTH), lambda s: (rb0 + s, COL_Q // ATT_WIDTH)),
        pl.BlockSpec((t, ATT_WIDTH), lambda s: (rb0 + s, COL_K // ATT_WIDTH)),
        pl.BlockSpec((t, ATT_WIDTH), lambda s: (rb0 + s, COL_V // ATT_WIDTH)),
    ]
    args = [lam, proj_s, proj_s, proj_s]
    if has_cache:
        past = cache_k.shape[2]
        in_specs += [pl.BlockSpec((None, ATT_WIDTH, past), lambda s: (s, 0, 0)),
                     pl.BlockSpec((None, past * ATT_HEADS, ATT_V_DIM), lambda s: (s, 0, 0))]
        args += [cache_k, cache_v]
    return pl.pallas_call(
        functools.partial(_attn_short_kernel, has_cache=has_cache),
        grid=(n_seq,),
        in_specs=in_specs,
        out_specs=pl.BlockSpec((t, ATT_WIDTH), lambda s: (s, 0)),
        out_shape=jax.ShapeDtypeStruct((n_seq * t, ATT_WIDTH), BF16),
        compiler_params=_cparams(("parallel",)),
        name="attn_cached" if has_cache else "attn_meta",
    )(*args)


def _merge_kernel(x_ref, yn_ref, o_ref, g1_ref, g2_ref, wso_ref, wao_ref, wo_ref, bg1_ref, bg2_ref,
                  sub_ref, n2_ref, wr_ref, br_ref,
                  h1_ref, u2_ref, eidx_ref, gate_ref, rank_ref, cnt_ref, *, sub_scale):
    tm = x_ref.shape[0]
    y_ssd = jnp.dot(yn_ref[...], wso_ref[...], preferred_element_type=F32)
    o = o_ref[...].astype(F32)
    parts = []
    for h in range(ATT_HEADS):
        oh = o[:, h * ATT_V_DIM:(h + 1) * ATT_V_DIM]
        ms = jnp.mean(oh * oh, axis=-1, keepdims=True)
        parts.append(oh * lax.rsqrt(ms + EPS) * sub_ref[...] * sub_scale)
    on = jnp.concatenate(parts, axis=1).astype(BF16)
    y_att = jnp.dot(on, wao_ref[...], preferred_element_type=F32)
    gs = _sigmoid(g1_ref[...].astype(F32) + bg1_ref[...])
    ga = _sigmoid(g2_ref[...].astype(F32) + bg2_ref[...])
    mix_in = (gs * y_ssd + ga * y_att).astype(BF16)
    h1 = x_ref[...] + jnp.dot(mix_in, wo_ref[...], preferred_element_type=F32)
    h1_ref[...] = h1
    ms = jnp.mean(h1 * h1, axis=-1, keepdims=True)
    u2 = h1 * lax.rsqrt(ms + EPS) * n2_ref[...]
    u2_ref[...] = u2.astype(u2_ref.dtype)

    logits = lax.dot_general(wr_ref[...], u2, (((1,), (1,)), ((), ())), preferred_element_type=F32,
                             precision=lax.Precision.HIGHEST)
    work = (logits + br_ref[...])[0:N_EXPERTS, :]
    expert = lax.broadcasted_iota(I32, (N_EXPERTS, tm), 0).astype(F32)
    vals, idxs, hots = [], [], []
    for _ in range(TOP_K):
        mx = jnp.max(work, axis=0, keepdims=True)
        ix = jnp.min(jnp.where(work == mx, expert, float(N_EXPERTS)), axis=0, keepdims=True)
        hot = expert == ix
        vals.append(mx)
        idxs.append(ix)
        hots.append(hot)
        work = jnp.where(hot, NEG_BIG, work)
    es = [jnp.exp(v - vals[0]) for v in vals]
    den = es[0] + es[1] + es[2] + es[3]
    hot_all = jnp.where(hots[0] | hots[1] | hots[2] | hots[3], 1.0, 0.0)
    si = lax.broadcasted_iota(I32, (tm, tm), 0)
    ti = lax.broadcasted_iota(I32, (tm, tm), 1)
    earlier = (si < ti).astype(BF16)
    prefix = jnp.dot(hot_all.astype(BF16), earlier, preferred_element_type=F32)
    row = lax.broadcasted_iota(I32, (SUBLANES, tm), 0)
    eidx = jnp.zeros((SUBLANES, tm), F32)
    gate = jnp.zeros((SUBLANES, tm), F32)
    rank = jnp.zeros((SUBLANES, tm), F32)
    for k in range(TOP_K):
        rk = jnp.sum(jnp.where(hots[k], prefix, 0.0), axis=0, keepdims=True)
        eidx = jnp.where(row == k, idxs[k], eidx)
        gate = jnp.where(row == k, es[k] / den, gate)
        rank = jnp.where(row == k, rk, rank)
    eidx_ref[...] = eidx.astype(I32)
    gate_ref[...] = gate
    rank_ref[...] = rank.astype(I32)
    cnt_ref[...] = jnp.sum(hot_all, axis=1, keepdims=True).astype(I32)


def _merge(x, yn, o, proj, wso, wao, wo, bg1, bg2, sub_g, n2_g, wr_t, br_c, *, tm, sub_scale):
    rows = x.shape[0]
    full = lambda shape: pl.BlockSpec(shape, lambda i: (0,) * len(shape))
    tok = lambda w: pl.BlockSpec((tm, w), lambda i: (i, 0))
    per_block = lambda a, b: pl.BlockSpec((None, a, b), lambda i: (i, 0, 0))
    return pl.pallas_call(
        functools.partial(_merge_kernel, sub_scale=sub_scale),
        grid=(rows // tm,),
        in_specs=[
            tok(D_MODEL), tok(D_INNER), tok(ATT_WIDTH),
            pl.BlockSpec((tm, D_MODEL), lambda i: (i, COL_G // D_MODEL)),
            pl.BlockSpec((tm, D_MODEL), lambda i: (i, COL_G // D_MODEL + 1)),
            full((D_INNER, D_MODEL)), full((ATT_WIDTH, D_MODEL)), full((D_MODEL, D_MODEL)),
            full((1, D_MODEL)), full((1, D_MODEL)), full((1, ATT_V_DIM)), full((1, D_MODEL)),
            full((LANES, D_MODEL)), full((LANES, 1)),
        ],
        out_specs=[tok(D_MODEL), tok(D_MODEL), per_block(SUBLANES, tm), per_block(SUBLANES, tm),
                   per_block(SUBLANES, tm), per_block(N_EXPERTS, 1)],
        out_shape=[
            jax.ShapeDtypeStruct((rows, D_MODEL), F32),
            jax.ShapeDtypeStruct((rows, D_MODEL), BF16),
            jax.ShapeDtypeStruct((rows // tm, SUBLANES, tm), I32),
            jax.ShapeDtypeStruct((rows // tm, SUBLANES, tm), F32),
            jax.ShapeDtypeStruct((rows // tm, SUBLANES, tm), I32),
            jax.ShapeDtypeStruct((rows // tm, N_EXPERTS, 1), I32),
        ],
        compiler_params=_cparams(("parallel",)),
        name="merge_router",
    )(x, yn, o, proj, proj, wso, wao, wo, bg1, bg2, sub_g, n2_g, wr_t, br_c)


def _seg_copy(local_ref, lo, hbm_ref, hi, sem, to_hbm):
    loc = local_ref.at[pl.ds(pl.multiple_of(lo, SEG_PAD), SEG_PAD), :]
    hbm = hbm_ref.at[pl.ds(pl.multiple_of(hi, SEG_PAD), SEG_PAD), :]
    return pltpu.make_async_copy(loc, hbm, sem) if to_hbm else pltpu.make_async_copy(hbm, loc, sem)


def _seg_copies_start(tab_ref, local_ref, hbm_ref, sem, to_hbm):
    def per_expert(e, total):
        n, lo, hi = tab_ref[0, e], tab_ref[1, e], tab_ref[2, e]

        def per_copy(c, _):
            _seg_copy(local_ref, lo + c * SEG_PAD, hbm_ref, hi + c * SEG_PAD, sem, to_hbm).start()
            return 0

        lax.fori_loop(0, n, per_copy, 0)
        return total + n

    return lax.fori_loop(0, N_EXPERTS, per_expert, 0)


def _seg_copies_wait(n, local_ref, hbm_ref, sem, to_hbm):
    def one(c, _):
        _seg_copy(local_ref, 0, hbm_ref, 0, sem, to_hbm).wait()
        return 0

    lax.fori_loop(0, n, one, 0)


def _dispatch_kernel(tab_ref, lpos_ref, u_ref, *rest, fill_tails):
    if fill_tails:
        tail_ref, _, xs_ref, loc_scr, zero_scr, sem = rest
    else:
        xs_ref, loc_scr, sem = rest
    lr, tm = loc_scr.shape[0], u_ref.shape[0]
    lpos = lpos_ref[...].astype(jnp.int16)
    p = lax.broadcasted_iota(I32, (lr, tm), 0).astype(jnp.int16)
    perm = jnp.zeros((lr, tm), BF16)
    for k in range(TOP_K):
        perm = jnp.where(p == lpos[k:k + 1, :], jnp.ones((lr, tm), BF16), perm)
    loc_scr[...] = jnp.dot(perm, u_ref[...], preferred_element_type=F32).astype(loc_scr.dtype)
    n = _seg_copies_start(tab_ref, loc_scr, xs_ref, sem, to_hbm=True)
    if fill_tails:
        zero_scr[...] = jnp.zeros(zero_scr.shape, zero_scr.dtype)

        def per_expert(e, total):
            cnt, hi = tail_ref[0, e], tail_ref[1, e]

            def per_copy(c, _):
                _seg_copy(zero_scr, 0, xs_ref, hi + c * SEG_PAD, sem, True).start()
                return 0

            lax.fori_loop(0, cnt, per_copy, 0)
            return total + cnt

        n = n + lax.fori_loop(0, N_EXPERTS, per_expert, 0)
    _seg_copies_wait(n, loc_scr, xs_ref, sem, to_hbm=True)


def _local_rows(tm):
    return tm * TOP_K + N_EXPERTS * SEG_PAD


def _dispatch(tab, lpos_t, u2, *, tm, xs_rows=None, xs=None, tail=None):
    rows = u2.shape[0]
    in_specs = [
        pl.BlockSpec((None, 3, N_EXPERTS), lambda i: (i, 0, 0), memory_space=pltpu.SMEM),
        pl.BlockSpec((None, TOP_K, tm), lambda i: (i, 0, 0)),
        pl.BlockSpec((tm, D_MODEL), lambda i: (i, 0)),
    ]
    args = [tab, lpos_t, u2]
    scratch = [pltpu.VMEM((_local_rows(tm), D_MODEL), BF16)]
    aliases = {}
    if tail is not None:
        assert xs is not None and rows == tm
        in_specs += [pl.BlockSpec(memory_space=pltpu.SMEM), pl.BlockSpec(memory_space=pl.ANY)]
        args += [tail, xs]
        scratch.append(pltpu.VMEM((SEG_PAD, D_MODEL), BF16))
        aliases = {4: 0}
        xs_rows = xs.shape[0]
    return pl.pallas_call(
        functools.partial(_dispatch_kernel, fill_tails=tail is not None),
        grid=(rows // tm,),
        in_specs=in_specs,
        out_specs=pl.BlockSpec(memory_space=pl.ANY),
        out_shape=jax.ShapeDtypeStruct((xs_rows, D_MODEL), BF16),
        scratch_shapes=scratch + [pltpu.SemaphoreType.DMA(())],
        input_output_aliases=aliases,
        compiler_params=_cparams(("arbitrary",)),
        name="moe_dispatch",
    )(*args)


def _ffn_kernel(be_ref, na_ref, x_ref, wgu_ref, bgu_ref, wd_ref, bd_ref, y_ref, wgu_scr, wd_scr):
    i = pl.program_id(0)
    active = i < na_ref[0]

    @pl.when(active & ((i == 0) | (be_ref[i] != be_ref[jnp.maximum(i - 1, 0)])))
    def _():
        wgu_scr[...] = wgu_ref[...].astype(BF16)
        wd_scr[...] = wd_ref[...].astype(BF16)

    @pl.when(active)
    def _():
        gu = jnp.dot(x_ref[...], wgu_scr[...], preferred_element_type=F32) + bgu_ref[...]
        gate = jnp.minimum(gu[:, :D_FF], SWIGLU_LIMIT)
        up = jnp.clip(gu[:, D_FF:], -SWIGLU_LIMIT, SWIGLU_LIMIT)
        hdn = (up + 1.0) * gate * _sigmoid(SWIGLU_ALPHA * gate)
        y = jnp.dot(hdn.astype(BF16), wd_scr[...], preferred_element_type=F32) + bd_ref[...]
        y_ref[...] = y.astype(y_ref.dtype)


def _ffn(block_exp, n_active, xs, wgu, bgu, wd, bd):
    n_blocks = xs.shape[0] // MOE_BLK

    def blk(i, be, na):
        return jnp.minimum(i, na[0] - 1)

    return pl.pallas_call(
        _ffn_kernel,
        grid_spec=pltpu.PrefetchScalarGridSpec(
            num_scalar_prefetch=2,
            grid=(n_blocks,),
            in_specs=[
                pl.BlockSpec((MOE_BLK, D_MODEL), lambda i, be, na: (blk(i, be, na), 0)),
                pl.BlockSpec((None, D_MODEL, 2 * D_FF), lambda i, be, na: (be[blk(i, be, na)], 0, 0)),
                pl.BlockSpec((None, 1, 2 * D_FF), lambda i, be, na: (be[blk(i, be, na)], 0, 0)),
                pl.BlockSpec((None, D_FF, D_MODEL), lambda i, be, na: (be[blk(i, be, na)], 0, 0)),
                pl.BlockSpec((None, 1, D_MODEL), lambda i, be, na: (be[blk(i, be, na)], 0, 0)),
            ],
            out_specs=pl.BlockSpec((MOE_BLK, D_MODEL), lambda i, be, na: (blk(i, be, na), 0)),
            scratch_shapes=[pltpu.VMEM((D_MODEL, 2 * D_FF), BF16), pltpu.VMEM((D_FF, D_MODEL), BF16)],
        ),
        out_shape=jax.ShapeDtypeStruct(xs.shape, xs.dtype),
        compiler_params=_cparams(("arbitrary",)),
        name="moe_ffn",
    )(block_exp, n_active, xs, wgu, bgu, wd, bd)


def _combine_kernel(tab_ref, lpos_ref, gate_ref, h1_ref, fg_ref, ys_ref, y_ref, loc_scr, sem):
    lr, tm = loc_scr.shape[0], h1_ref.shape[0]
    loc_scr[...] = jnp.zeros(loc_scr.shape, loc_scr.dtype)
    n = _seg_copies_start(tab_ref, loc_scr, ys_ref, sem, to_hbm=False)
    lpos = lpos_ref[...].astype(jnp.int16)
    gate = gate_ref[...].astype(BF16)
    p = lax.broadcasted_iota(I32, (tm, lr), 1).astype(jnp.int16)
    pick = jnp.zeros((tm, lr), BF16)
    for k in range(TOP_K):
        pick = jnp.where(p == lpos[:, k:k + 1], jnp.broadcast_to(gate[:, k:k + 1], (tm, lr)), pick)
    _seg_copies_wait(n, loc_scr, ys_ref, sem, to_hbm=False)
    h = h1_ref[...] + jnp.dot(pick, loc_scr[...], preferred_element_type=F32)
    ms = jnp.mean(h * h, axis=-1, keepdims=True)
    y_ref[...] = h * lax.rsqrt(ms + EPS) * fg_ref[...]


def _combine(tab, lpos, gate, h1, fg, ys, *, tm):
    rows = h1.shape[0]
    return pl.pallas_call(
        _combine_kernel,
        grid=(rows // tm,),
        in_specs=[
            pl.BlockSpec((None, 3, N_EXPERTS), lambda i: (i, 0, 0), memory_space=pltpu.SMEM),
            pl.BlockSpec((tm, TOP_K), lambda i: (i, 0)),
            pl.BlockSpec((tm, TOP_K), lambda i: (i, 0)),
            pl.BlockSpec((tm, D_MODEL), lambda i: (i, 0)),
            pl.BlockSpec((1, D_MODEL), lambda i: (0, 0)),
            pl.BlockSpec(memory_space=pl.ANY),
        ],
        out_specs=pl.BlockSpec((tm, D_MODEL), lambda i: (i, 0)),
        out_shape=jax.ShapeDtypeStruct((rows, D_MODEL), F32),
        scratch_shapes=[pltpu.VMEM((_local_rows(tm), D_MODEL), ys.dtype), pltpu.SemaphoreType.DMA(())],
        compiler_params=_cparams(("arbitrary",)),
        name="moe_combine",
    )(tab, lpos, gate, h1, fg, ys)


def _rope_tables(pos):
    d = ATT_HEAD_DIM
    inv = ROPE_THETA ** (-jnp.arange(0, d, 2, dtype=F32) / d)
    ang = pos.astype(F32)[:, None] * inv[None, :]
    cos = jnp.cos(ang)
    sin = jnp.sin(ang)
    cos_h = jnp.concatenate([cos, cos], axis=1)
    sin_h = jnp.concatenate([-sin, sin], axis=1)
    return jnp.tile(cos_h, (1, LANES // d)), jnp.tile(sin_h, (1, LANES // d))


def _conv_by_group(a):
    lead = a.shape[:-1]
    x = a[..., :D_INNER].reshape(lead + (SSD_GROUPS, GROUP_W))
    b = a[..., D_INNER:D_INNER + SSD_GN].reshape(lead + (SSD_GROUPS, SSD_STATE))
    c = a[..., D_INNER + SSD_GN:].reshape(lead + (SSD_GROUPS, SSD_STATE))
    return jnp.concatenate([x, b, c], axis=-1)


def _conv_from_group(a):
    lead = a.shape[:-2]
    x = a[..., :GROUP_W].reshape(lead + (D_INNER,))
    b = a[..., GROUP_W:GROUP_W + SSD_STATE].reshape(lead + (SSD_GN,))
    c = a[..., GROUP_W + SSD_STATE:].reshape(lead + (SSD_GN,))
    return jnp.concatenate([x, b, c], axis=-1)


def _conv_prev_blocks(prev):
    g = jnp.moveaxis(_conv_by_group(prev.astype(F32)), 1, 2)
    return jnp.pad(g, ((0, 0), (0, 0), (SUBLANES - (CONV_W - 1), 0), (0, 0)))


def _conv_tail_rows(ct):
    return _conv_from_group(jnp.moveaxis(ct[:, :, SUBLANES - (CONV_W - 1):, :], 1, 2))


def kernel(x_prompt, x_sample, cache_k, cache_v, state_ssm, state_conv, meta_tokens, norm1_g, w_in, conv_w, conv_b, dt_bias, a_log, d_skip, ssd_norm_g, lambda_q1, lambda_k1, lambda_q2, lambda_k2, subln_g, w_ssd_out, w_att_out, b_gate, w_o, norm2_g, w_router, b_router, w_gu, b_gu, w_down, b_down, final_norm_g):
    batch, seq, _ = x_prompt.shape
    dbatch, dseq, _ = x_sample.shape
    past = cache_k.shape[2]
    depth = norm1_g.shape[0]
    assert depth == 1 and dseq == N_META
    assert seq % SSD_L == 0 and seq % ATT_TQ == 0 and seq % TOKEN_TM == 0 and ATT_TQ % CHUNK == 0
    assert TOKEN_TM % ATT_TQ == 0 and PROJ_TM % ATT_TQ == 0
    n_p = batch * seq
    n_dec = dbatch * dseq
    n_s = n_dec + N_META
    lam_init = 0.8 - 0.6 * math.exp(-0.3 * 0)
    l = 0

    wi = w_in[l]
    o_z, o_xbc, o_dt = 0, D_INNER, D_INNER + CONV_DIM
    o_q = o_dt + SSD_HEADS
    w_main = jnp.concatenate([wi[:, o_z:o_xbc], wi[:, o_xbc:o_dt], wi[:, o_q:]], axis=1).astype(BF16)
    w_dt_heads = wi[:, o_dt:o_q].reshape(D_MODEL, SSD_GROUPS, SSD_HPG)
    w_dt = jnp.zeros((D_MODEL, SSD_GROUPS, LANES), F32).at[:, :, :SSD_HPG].set(
        w_dt_heads).reshape(D_MODEL, SSD_GROUPS * LANES).astype(BF16)
    w_dt_c = jnp.zeros((D_MODEL, LANES), F32).at[:, :SSD_HEADS].set(
        w_dt_heads.transpose(0, 2, 1).reshape(D_MODEL, SSD_HEADS)).astype(BF16)
    g1 = norm1_g[l].reshape(1, D_MODEL)
    cw_g = jnp.moveaxis(_conv_by_group(conv_w[l]), 0, 1)
    cb_g = _conv_by_group(conv_b[l])[:, None, :]
    a_neg = -jnp.exp(a_log[l].astype(F32))
    norm_g = ssd_norm_g[l].reshape(1, D_INNER)
    lam = (jnp.exp(jnp.sum(lambda_q1[l].astype(F32) * lambda_k1[l].astype(F32)))
           - jnp.exp(jnp.sum(lambda_q2[l].astype(F32) * lambda_k2[l].astype(F32))) + lam_init).reshape(1)
    wso = w_ssd_out[l].astype(BF16)
    wao = w_att_out[l].astype(BF16)
    wo = w_o[l].astype(BF16)
    bg1 = b_gate[l][:D_MODEL].reshape(1, D_MODEL)
    bg2 = b_gate[l][D_MODEL:].reshape(1, D_MODEL)
    sub_g = subln_g[l].reshape(1, ATT_V_DIM)
    n2_g = norm2_g[l].reshape(1, D_MODEL)
    wr = jnp.zeros((LANES, D_MODEL), F32).at[:N_EXPERTS, :].set(w_router[l].T)
    br = jnp.zeros((LANES, 1), F32).at[:N_EXPERTS, 0].set(b_router[l])
    wgu = w_gu[l]
    bgu = b_gu[l][:, None, :]
    wd = w_down[l]
    bd = b_down[l][:, None, :]
    fg = final_norm_g.reshape(1, D_MODEL)

    xp = x_prompt.reshape(n_p, D_MODEL)
    xs_rows = jnp.concatenate([x_sample.reshape(n_dec, D_MODEL), meta_tokens.astype(x_prompt.dtype)], axis=0)
    cos_p, sin_p = _rope_tables(N_META + jnp.arange(seq, dtype=I32))
    pos_s = jnp.concatenate([jnp.tile(past + jnp.arange(dseq, dtype=I32), dbatch), jnp.arange(N_META, dtype=I32)])
    cos_s, sin_s = _rope_tables(pos_s)

    tm_p = PROJ_TM if seq % PROJ_TM == 0 else TOKEN_TM
    proj_p, dt_p, vt_p, kc_p, vc_p = _in_proj(xp, g1, w_main, w_dt_c, cos_p, sin_p, tm_p, seq // tm_p, vt_block=ATT_TQ)
    proj_s, dt_s = _in_proj(xs_rows, g1, w_main, w_dt, cos_s, sin_s, n_s, 1)

    ssd_args = (cw_g, cb_g, dt_bias[l], a_neg, d_skip[l], norm_g)
    zero_h = jnp.zeros((1, SSD_HEADS, SSD_HEAD_DIM, SSD_STATE), F32)
    zero_c = jnp.zeros((1, SSD_GROUPS, SUBLANES, CONV_GW), F32)
    yn_m, h_m, ct_m = _ssd(proj_s, dt_s, zero_h, zero_c, *ssd_args, n_seq=1, t=N_META, l=N_META,
                           row0=n_dec, shared_state=True)
    yn_p, h_p, ct_p = _ssd(proj_p, dt_p, h_m, ct_m, *ssd_args, n_seq=batch, t=seq, l=SSD_L, row0=0,
                           shared_state=True)
    yn_d, h_d, ct_d = _ssd(proj_s, dt_s, state_ssm[l].astype(F32), _conv_prev_blocks(state_conv[l]), *ssd_args,
                           n_seq=dbatch, t=dseq, l=dseq, row0=0, shared_state=False)
    yn_s = jnp.concatenate([yn_d, yn_m], axis=0)

    mvt =proj_s[n_dec:n_s, COL_V:COL_V + ATT_WIDTH].reshape(N_META, ATT_HEADS, ATT_V_DIM).transpose(1, 2, 0)
    o_p = _attn_prompt(lam, proj_p, vt_p, proj_s, mvt, batch=batch, seq=seq, meta_row0=n_dec)
    o_d = _attn_short(lam, proj_s, cache_k[l].reshape(dbatch, past, ATT_WIDTH).transpose(0, 2, 1),
                      cache_v[l].reshape(dbatch, past * ATT_HEADS, ATT_V_DIM), n_seq=dbatch, t=dseq, row0=0)
    o_m = _attn_short(lam, proj_s, None, None, n_seq=1, t=N_META, row0=n_dec)
    o_s = jnp.concatenate([o_d, o_m], axis=0)

    merge_w = (wso, wao, wo, bg1, bg2, sub_g, n2_g, wr, br)
    h1_p, u2_p, e_p, gt_p, rk_p, cnt_p = _merge(xp, yn_p, o_p, proj_p, *merge_w, tm=TOKEN_TM,
                                                sub_scale=1.0 - lam_init)
    h1_s, u2_s, e_s, gt_s, rk_s, cnt_s = _merge(xs_rows, yn_s, o_s, proj_s, *merge_w, tm=n_s,
                                                sub_scale=1.0 - lam_init)

    nb_p = n_p // TOKEN_TM
    cnt = jnp.concatenate([cnt_p[:, :, 0], cnt_s[:, :, 0]], axis=0)
    seg = (cnt + SEG_PAD - 1) // SEG_PAD * SEG_PAD
    local_start = jnp.cumsum(seg, axis=1) - seg
    per_expert = jnp.sum(seg, axis=0)
    padded = (per_expert + MOE_BLK - 1) // MOE_BLK * MOE_BLK
    pend = jnp.cumsum(padded)
    pstart = pend - padded
    hbm_start = pstart[None, :] + jnp.cumsum(seg, axis=0) - seg
    tab = jnp.stack([seg // SEG_PAD, local_start, hbm_start], axis=1).astype(I32)
    n_rows_max = (n_p + n_s) * TOP_K + (nb_p + 1) * N_EXPERTS * (SEG_PAD - 1)
    n_blocks = -(-n_rows_max // MOE_BLK) + N_EXPERTS
    block_start = jnp.arange(n_blocks, dtype=I32) * MOE_BLK
    block_exp = jnp.minimum(jnp.sum((pend[None, :] <= block_start[:, None]).astype(I32), axis=1), N_EXPERTS - 1)
    n_active = (pend[-1:] // MOE_BLK).astype(I32)

    def local_rows_of(e, rk, starts):
        hot = e[:, :TOP_K, :, None] == jnp.arange(N_EXPERTS, dtype=I32)
        return (jnp.sum(jnp.where(hot, starts[:, None, None, :], 0), axis=-1) + rk[:, :TOP_K]).astype(I32)

    def by_token(a):
        return a[:, :TOP_K].transpose(0, 2, 1).reshape(-1, TOP_K)

    lpos_pt = local_rows_of(e_p, rk_p, local_start[:nb_p])
    lpos_st = local_rows_of(e_s, rk_s, local_start[nb_p:])
    lpos_p, lpos_s = by_token(lpos_pt), by_token(lpos_st)
    gt_p, gt_s = by_token(gt_p), by_token(gt_s)
    tail = jnp.stack([(padded - per_expert) // SEG_PAD, pstart + per_expert]).astype(I32)
    xs = _dispatch(tab[:nb_p], lpos_pt, u2_p, tm=TOKEN_TM, xs_rows=n_blocks * MOE_BLK)
    xs = _dispatch(tab[nb_p:], lpos_st, u2_s, tm=n_s, xs=xs, tail=tail)
    ys = _ffn(block_exp, n_active, xs, wgu, bgu, wd, bd)
    y_p = _combine(tab[:nb_p], lpos_p, gt_p, h1_p, fg, ys, tm=TOKEN_TM)
    y_s = _combine(tab[nb_p:], lpos_s, gt_s, h1_s, fg, ys, tm=n_s)

    def kv_rows(proj, col, lo, hi):
        return proj[lo:hi, col:col + ATT_WIDTH].astype(F32)

    def with_meta(cache, col, tail_shape):
        meta = kv_rows(proj_s, col, n_dec, n_s).reshape((1, N_META) + tail_shape)
        slab = cache.reshape((batch, N_META + seq) + tail_shape)
        return slab.at[:, :N_META].set(jnp.broadcast_to(meta, (batch, N_META) + tail_shape))

    new_k_p = with_meta(kc_p, COL_K, (ATT_WIDTH,)).reshape(1, batch, N_META + seq, 2 * ATT_HEADS, ATT_HEAD_DIM)
    new_v_p = with_meta(vc_p, COL_V, (ATT_HEADS, ATT_V_DIM))[None]
    new_k_s = kv_rows(proj_s, COL_K, 0, n_dec).reshape(1, dbatch, dseq, 2 * ATT_HEADS, ATT_HEAD_DIM)
    new_v_s = kv_rows(proj_s, COL_V, 0, n_dec).reshape(1, dbatch, dseq, ATT_HEADS, ATT_V_DIM)
    return (y_p.reshape(batch, seq, D_MODEL),
            y_s[:n_dec].reshape(dbatch, dseq, D_MODEL),
            new_k_p, new_v_p,
            h_p.astype(state_ssm.dtype)[None],
            _conv_tail_rows(ct_p).astype(x_prompt.dtype)[None],
            new_k_s, new_v_s,
            h_d.astype(state_ssm.dtype)[None],
            _conv_tail_rows(ct_d).astype(x_sample.dtype)[None])
```

```python
import functools
import math

import jax
import jax.numpy as jnp
from jax import lax
from jax.experimental import pallas as pl
from jax.experimental.pallas import tpu as pltpu

F32 = jnp.float32
BF16 = jnp.bfloat16
I32 = jnp.int32

D_MODEL = 1024
D_INNER = 2048
SSD_HEADS = 32
SSD_HEAD_DIM = 64
SSD_GROUPS = 8
SSD_HPG = SSD_HEADS // SSD_GROUPS
SSD_STATE = 128
SSD_GN = SSD_GROUPS * SSD_STATE
CONV_W = 4
CONV_DIM = D_INNER + 2 * SSD_GN
ATT_HEADS = 8
ATT_HEAD_DIM = 64
ATT_V_DIM = 128
ATT_WIDTH = 1024
CHUNK = 64
N_META = 16
EPS = 1e-6
ROPE_THETA = 10000.0
LOG2_E = math.log2(math.e)
N_EXPERTS = 32
TOP_K = 4
D_FF = 1024
SWIGLU_LIMIT = 7.0
SWIGLU_ALPHA = 1.702

COL_Z = 0
COL_X = COL_Z + D_INNER
COL_B = COL_X + D_INNER
COL_C = COL_B + SSD_GN
COL_Q = COL_C + SSD_GN
COL_K = COL_Q + ATT_WIDTH
COL_V = COL_K + ATT_WIDTH
COL_G = COL_V + ATT_WIDTH
N_MAIN = COL_G + 2 * D_MODEL

LANES = 128
SUBLANES = 8
GROUP_W = D_INNER // SSD_GROUPS
CONV_GW = GROUP_W + 2 * SSD_STATE
PROJ_TN = 1024
PROJ_TM = 1024
TOKEN_TM = 512
SSD_L = 256
SSD_GP = 8
ATT_TQ = 512
MOE_BLK = 512
SEG_PAD = 16
ONES_ROWS = 16
VMEM_LIMIT = 56 * 1024 * 1024
NEG_BIG = -1e30


def _cparams(sem):
    return pltpu.CompilerParams(dimension_semantics=sem, vmem_limit_bytes=VMEM_LIMIT)


def _sigmoid(x):
    return 1.0 / (1.0 + jnp.exp(-x))


def _inproj_kernel(x_ref, g_ref, w_ref, wdt_ref, cos_ref, sin_ref, o_ref, dt_ref, *rest, emit_vt,
                   seq_blocks):
    if emit_vt:
        vt_ref, kout_ref, vout_ref, u_scr, kbuf, vbuf, sems = rest
    else:
        (u_scr,) = rest
    j = pl.program_id(1)

    @pl.when(j == 0)
    def _():
        x = x_ref[...]
        ms = jnp.mean(x * x, axis=-1, keepdims=True)
        u = (x * lax.rsqrt(ms + EPS) * g_ref[...]).astype(BF16)
        u_scr[...] = u
        dt_ref[...] = jnp.dot(u, wdt_ref[...], preferred_element_type=F32)

    acc = jnp.dot(u_scr[...], w_ref[...], preferred_element_type=F32)
    is_q = j == COL_Q // PROJ_TN
    is_k = j == COL_K // PROJ_TN

    @pl.when(is_q | is_k)
    def _():
        cos = cos_ref[...]
        sin = sin_ref[...]
        lane = lax.broadcasted_iota(I32, cos.shape, 1)
        half = ATT_HEAD_DIM // 2
        first_half = (lane % ATT_HEAD_DIM) < half
        scale = jnp.where(is_q, ATT_HEAD_DIM ** -0.5 * LOG2_E, 1.0).astype(F32)
        for c in range(PROJ_TN // LANES):
            a = acc[:, c * LANES:(c + 1) * LANES]
            swapped = jnp.where(first_half, pltpu.roll(a, LANES - half, 1), pltpu.roll(a, half, 1))
            roped = (a * cos + swapped * sin) * scale
            o_ref[:, c * LANES:(c + 1) * LANES] = roped.astype(o_ref.dtype)
            if emit_vt:
                kbuf[:, c * LANES:(c + 1) * LANES] = roped

    @pl.when(jnp.logical_not(is_q | is_k))
    def _():
        o_ref[...] = acc.astype(o_ref.dtype)

    if emit_vt:
        i = pl.program_id(0)
        tm = x_ref.shape[0]
        row0 = (i // seq_blocks) * (N_META + seq_blocks * tm) + N_META + (i % seq_blocks) * tm

        def cache_copy(buf, out_ref, sem):
            per_row = buf.shape[0] // tm
            rows = out_ref.at[pl.ds(pl.multiple_of(row0 * per_row, SUBLANES), tm * per_row), :]
            return pltpu.make_async_copy(buf, rows, sem)

        @pl.when(is_k)
        def _():
            cache_copy(kbuf, kout_ref, sems.at[0]).start()

        @pl.when(j == COL_V // PROJ_TN)
        def _():
            for h in range(ATT_HEADS):
                vbuf[pl.ds(h, tm, stride=ATT_HEADS), :] = acc[:, h * ATT_V_DIM:(h + 1) * ATT_V_DIM]
            cache_copy(vbuf, vout_ref, sems.at[1]).start()
            tk = vt_ref.shape[-1]
            for h in range(ATT_HEADS):
                for s in range(vt_ref.shape[1]):
                    blk = acc[s * tk:(s + 1) * tk, h * ATT_V_DIM:(h + 1) * ATT_V_DIM]
                    vt_ref[h, s] = blk.T.astype(vt_ref.dtype)

        @pl.when(j == pl.num_programs(1) - 1)
        def _():
            cache_copy(kbuf, kout_ref, sems.at[0]).wait()
            cache_copy(vbuf, vout_ref, sems.at[1]).wait()


def _in_proj(x, g1, w_main, w_dt, cos_t, sin_t, tm, rope_blocks, vt_block=None):
    rows = x.shape[0]
    grid = (rows // tm, N_MAIN // PROJ_TN)
    scratch = [pltpu.VMEM((tm, D_MODEL), BF16)]
    dt_w = w_dt.shape[1]
    out_specs = [
        pl.BlockSpec((tm, PROJ_TN), lambda i, j: (i, j)),
        pl.BlockSpec((tm, dt_w), lambda i, j: (i, 0)),
    ]
    out_shape = [
        jax.ShapeDtypeStruct((rows, N_MAIN), BF16),
        jax.ShapeDtypeStruct((rows, dt_w), F32),
    ]
    if vt_block is not None:
        per = tm // vt_block
        out_specs.append(pl.BlockSpec((None, ATT_HEADS, per, ATT_V_DIM, vt_block),
                                      lambda i, j: (i // rope_blocks, 0, i % rope_blocks, 0, 0)))
        streams = rows // (rope_blocks * tm)
        out_shape.append(jax.ShapeDtypeStruct(
            (streams, ATT_HEADS, rope_blocks * per, ATT_V_DIM, vt_block), BF16))
        cache_rows = streams * (N_META + rope_blocks * tm)
        out_specs += [pl.BlockSpec(memory_space=pl.ANY), pl.BlockSpec(memory_space=pl.ANY)]
        out_shape += [jax.ShapeDtypeStruct((cache_rows, ATT_WIDTH), F32),
                      jax.ShapeDtypeStruct((cache_rows * ATT_HEADS, ATT_V_DIM), F32)]
        scratch += [pltpu.VMEM((tm, ATT_WIDTH), F32), pltpu.VMEM((tm * ATT_HEADS, ATT_V_DIM), F32),
                    pltpu.SemaphoreType.DMA((2,))]
    return pl.pallas_call(
        functools.partial(_inproj_kernel, emit_vt=vt_block is not None, seq_blocks=rope_blocks),
        grid=grid,
        in_specs=[
            pl.BlockSpec((tm, D_MODEL), lambda i, j: (i, 0)),
            pl.BlockSpec((1, D_MODEL), lambda i, j: (0, 0)),
            pl.BlockSpec((D_MODEL, PROJ_TN), lambda i, j: (0, j)),
            pl.BlockSpec((D_MODEL, dt_w), lambda i, j: (0, 0)),
            pl.BlockSpec((tm, LANES), lambda i, j: (i % rope_blocks, 0)),
            pl.BlockSpec((tm, LANES), lambda i, j: (i % rope_blocks, 0)),
        ],
        out_specs=out_specs,
        out_shape=out_shape,
        scratch_shapes=scratch,
        compiler_params=_cparams(("parallel", "arbitrary")),
        name="in_proj",
    )(x, g1, w_main, w_dt, cos_t, sin_t)


def _ssd_kernel(x_ref, b_ref, c_ref, z_ref, dt_ref, h0_ref, cp_ref, cw_ref, cb_ref, dtb_ref,
                aneg_ref, dsk_ref, ng_ref, y_ref, hf_ref, ct_ref, h_scr, f_scr):
    c = pl.program_id(2)
    L = x_ref.shape[0]

    @pl.when(c == 0)
    def _():
        h_scr[...] = h0_ref[...]
        f_scr[0:SUBLANES, :] = cp_ref[...]

    f_scr[SUBLANES:SUBLANES + L, 0:GROUP_W] = x_ref[...].astype(F32)
    f_scr[SUBLANES:SUBLANES + L, GROUP_W:GROUP_W + SSD_STATE] = b_ref[...].astype(F32)
    f_scr[SUBLANES:SUBLANES + L, GROUP_W + SSD_STATE:CONV_GW] = c_ref[...].astype(F32)
    w = cw_ref[...]
    acc = cb_ref[...]
    for i in range(CONV_W):
        lo = SUBLANES - (CONV_W - 1) + i
        acc = acc + w[i:i + 1, :] * f_scr[lo:lo + L, :]
    xc = acc * _sigmoid(acc)
    tail = f_scr[L:L + SUBLANES, :]
    f_scr[0:SUBLANES, :] = tail
    ct_ref[...] = tail

    xg = xc[:, 0:GROUP_W]
    bm = xc[:, GROUP_W:GROUP_W + SSD_STATE].astype(BF16)
    cm = xc[:, GROUP_W + SSD_STATE:CONV_GW].astype(BF16)

    dtr = dt_ref[...] + dtb_ref[...]
    dt = jnp.maximum(dtr, 0.0) + jnp.log(1.0 + jnp.exp(-jnp.abs(dtr)))
    da = dt * aneg_ref[...]
    ti = lax.broadcasted_iota(I32, (L, L), 0)
    si = lax.broadcasted_iota(I32, (L, L), 1)
    causal = si <= ti
    tril = causal.astype(F32)
    cum = jnp.dot(tril, da, preferred_element_type=F32, precision=lax.Precision.HIGHEST)
    sel = (lax.broadcasted_iota(I32, (SUBLANES, LANES), 0)
           == lax.broadcasted_iota(I32, (SUBLANES, LANES), 1)).astype(F32)
    cum_t = lax.dot_general(sel, cum, (((1,), (1,)), ((), ())), preferred_element_type=F32,
                            precision=lax.Precision.HIGHEST)

    cb = lax.dot_general(cm, bm, (((1,), (1,)), ((), ())), preferred_element_type=F32)
    dsk = dsk_ref[...]
    ys = []
    for r in range(SSD_HPG):
        col = cum[:, r:r + 1]
        row = cum_t[r:r + 1, :]
        dec = jnp.exp(jnp.where(causal, col - row, NEG_BIG))
        m = (cb * dec).astype(BF16)
        xh = xg[:, r * SSD_HEAD_DIM:(r + 1) * SSD_HEAD_DIM]
        xdt = xh * dt[:, r:r + 1]
        h_prev = h_scr[r]
        y = jnp.dot(m, xdt.astype(BF16), preferred_element_type=F32)
        y = y + jnp.exp(col) * lax.dot_general(cm, h_prev.astype(BF16), (((1,), (1,)), ((), ())),
                                               preferred_element_type=F32)
        y = y + dsk[:, r:r + 1] * xh
        ys.append(y)
        tot = cum[L - 1:L, r:r + 1]
        xw = (xdt * jnp.exp(tot - col)).astype(BF16)
        upd = lax.dot_general(xw, bm, (((0,), (0,)), ((), ())), preferred_element_type=F32)
        h_scr[r] = h_prev * jnp.exp(tot) + upd
    yg = jnp.concatenate(ys, axis=1)
    z = z_ref[...].astype(F32)
    yz = yg * (z * _sigmoid(z))
    ms = jnp.mean(yz * yz, axis=-1, keepdims=True)
    y_ref[...] = (yz * lax.rsqrt(ms + EPS) * ng_ref[...]).astype(y_ref.dtype)
    hf_ref[...] = h_scr[...]


def _ssd_long_kernel(x_ref, b_ref, c_ref, z_ref, dt_ref, h0_ref, cp_ref, cw_ref, cb_ref, dtb_ref,
                     aneg_ref, dsk_ref, ng_ref, tri_ref, shift_ref, y_ref, hf_ref, ct_ref, h_scr, f_scr,
                     dtt_scr):
    c = pl.program_id(2)
    L = x_ref.shape[0]
    reps = L // LANES

    @pl.when(c == 0)
    def _():
        h_scr[...] = h0_ref[...]
        f_scr[:, 0:SUBLANES, :] = cp_ref[...]
        f_scr[:, SUBLANES:2 * SUBLANES, :] = jnp.zeros((SSD_GP, SUBLANES, CONV_GW), F32)

    tri = tri_ref[...]
    tri_b = tri.astype(BF16)
    visible = tri > 0.5
    dtt_scr[...] = dt_ref[...].T
    for gi in range(SSD_GP):
        xs_ = slice(gi * GROUP_W, (gi + 1) * GROUP_W)
        ns_ = slice(gi * SSD_STATE, (gi + 1) * SSD_STATE)
        xb = jnp.concatenate([x_ref[:, xs_], b_ref[:, ns_], c_ref[:, ns_]], axis=1)
        xf = xb.astype(F32)
        w = cw_ref[gi]
        acc = cb_ref[gi] + w[CONV_W - 1:CONV_W, :] * xf
        for d in range(1, CONV_W):
            sh = jnp.dot(shift_ref[d - 1], xb, preferred_element_type=F32)
            acc = acc + w[CONV_W - 1 - d:CONV_W - d, :] * sh
        corr = jnp.zeros((SUBLANES, CONV_GW), F32)
        for i in range(CONV_W - 1):
            lo = SUBLANES - (CONV_W - 1) + i
            corr = corr + w[i:i + 1, :] * f_scr[gi, lo:lo + SUBLANES, :]
        acc = jnp.concatenate([acc[0:SUBLANES] + corr, acc[SUBLANES:]], axis=0)
        xc = acc * _sigmoid(acc)
        tail = xf[L - SUBLANES:L, :]
        f_scr[gi, 0:SUBLANES, :] = tail
        ct_ref[gi] = tail

        x_t = xc[:, 0:GROUP_W].T
        bm = xc[:, GROUP_W:GROUP_W + SSD_STATE].astype(BF16)
        cm = xc[:, GROUP_W + SSD_STATE:CONV_GW].astype(BF16)

        grp = pl.program_id(1) * SSD_GP + gi
        dt_rows = [dtt_scr[pl.ds(SUBLANES * r + grp, 1), :] for r in range(SSD_HPG)]
        dtr = jnp.concatenate(dt_rows + [jnp.zeros((SUBLANES - SSD_HPG, L), F32)], axis=0)
        dtr = dtr + jnp.tile(dtb_ref[gi], (1, reps))
        dt = jnp.maximum(dtr, 0.0) + jnp.log(1.0 + jnp.exp(-jnp.abs(dtr)))
        da = dt * jnp.tile(aneg_ref[gi], (1, reps))
        d1 = da.astype(BF16)
        r1 = da - d1.astype(F32)
        d2 = r1.astype(BF16)
        d3 = (r1 - d2.astype(F32)).astype(BF16)
        cum = (jnp.dot(d1, tri_b, preferred_element_type=F32) + jnp.dot(d2, tri_b, preferred_element_type=F32)
               + jnp.dot(d3, tri_b, preferred_element_type=F32))
        cum_col = cum.T

        cb_t = lax.dot_general(bm, cm, (((1,), (1,)), ((), ())), preferred_element_type=F32)
        dsk = dsk_ref[gi]
        ys = []
        for r in range(SSD_HPG):
            hd = gi * SSD_HPG + r
            row = cum[r:r + 1, :]
            dec = jnp.exp(jnp.where(visible, row - cum_col[:, r:r + 1], NEG_BIG))
            m = (cb_t * dec).astype(BF16)
            xh = x_t[r * SSD_HEAD_DIM:(r + 1) * SSD_HEAD_DIM, :]
            xdt = xh * dt[r:r + 1, :]
            h_prev = h_scr[hd]
            y = jnp.dot(xdt.astype(BF16), m, preferred_element_type=F32)
            y = y + jnp.exp(row) * lax.dot_general(h_prev.astype(BF16), cm, (((1,), (1,)), ((), ())),
                                                   preferred_element_type=F32)
            y = y + dsk[r:r + 1, 0:1] * xh
            ys.append(y)
            tot = row[:, L - 1:L]
            xw = (xdt * jnp.exp(tot - row)).astype(BF16)
            h_scr[hd] = h_prev * jnp.exp(tot) + jnp.dot(xw, bm, preferred_element_type=F32)
        yg = jnp.concatenate(ys, axis=0).T
        z = z_ref[:, xs_].astype(F32)
        yz = yg * (z * _sigmoid(z))
        ms = jnp.mean(yz * yz, axis=-1, keepdims=True)
        y_ref[:, xs_] = (yz * lax.rsqrt(ms + EPS) * ng_ref[:, xs_]).astype(y_ref.dtype)
    hf_ref[...] = h_scr[...]


def _per_group_lanes(v):
    out = jnp.zeros((SSD_GROUPS, 1, LANES), F32)
    return out.at[:, 0, :SSD_HPG].set(v.astype(F32).reshape(SSD_GROUPS, SSD_HPG))


def _per_group_rows(v):
    out = jnp.zeros((SSD_GROUPS, SUBLANES, LANES), F32)
    return out.at[:, :SSD_HPG, :].set(
        jnp.broadcast_to(v.astype(F32).reshape(SSD_GROUPS, SSD_HPG, 1), (SSD_GROUPS, SSD_HPG, LANES)))


def _ssd(proj, dt_raw, h0, conv_prev, cw_g, cb_g, dt_bias, a_neg, d_skip, norm_g, *, n_seq, t, l, row0,
         shared_state):
    nc = t // l
    rb0 = row0 // l
    long_chunks = l % LANES == 0

    def rows(s, g, c):
        return rb0 + s * nc + c

    def sidx(s):
        return 0 if shared_state else s

    if long_chunks:
        head_rows = SUBLANES
        head_par = [_per_group_rows(v) for v in (dt_bias, a_neg, d_skip)]
        step = jnp.arange(l, dtype=I32)
        tri = (step[:, None] <= step[None, :]).astype(F32)
        shift = jnp.stack([(step[None, :] == step[:, None] - d) for d in range(1, CONV_W)]).astype(BF16)
        extra_args = [tri, shift]
        extra_specs = [pl.BlockSpec((l, l), lambda s, g, c: (0, 0)),
                       pl.BlockSpec((CONV_W - 1, l, l), lambda s, g, c: (0, 0, 0))]
        body, gp = _ssd_long_kernel, SSD_GP
        lead = (gp,)
        scratch = [pltpu.VMEM((gp, 2 * SUBLANES, CONV_GW), F32), pltpu.VMEM((LANES, l), F32)]
        dt_spec = pl.BlockSpec((l, LANES), lambda s, g, c: (rows(s, g, c), 0))
    else:
        head_rows = 1
        head_par = [_per_group_lanes(v) for v in (dt_bias, a_neg, d_skip)]
        extra_args, extra_specs = [], []
        body, gp = _ssd_kernel, 1
        lead = (None,)
        scratch = [pltpu.VMEM((l + SUBLANES, CONV_GW), F32)]
        dt_spec = pl.BlockSpec((l, LANES), lambda s, g, c: (rows(s, g, c), g))

    def per_group(*shape):
        return pl.BlockSpec(lead + shape, lambda s, g, c: (g,) + (0,) * len(shape))

    def per_seq_group(*shape, shared):
        return pl.BlockSpec((None,) + lead + shape,
                            lambda s, g, c: ((sidx(s) if shared else s), g) + (0,) * len(shape))

    def state_spec(shared):
        return pl.BlockSpec((None, gp * SSD_HPG, SSD_HEAD_DIM, SSD_STATE),
                            lambda s, g, c: ((sidx(s) if shared else s), g, 0, 0))

    return pl.pallas_call(
        body,
        grid=(n_seq, SSD_GROUPS // gp, nc),
        in_specs=[
            pl.BlockSpec((l, gp * GROUP_W), lambda s, g, c: (rows(s, g, c), COL_X // (gp * GROUP_W) + g)),
            pl.BlockSpec((l, gp * SSD_STATE), lambda s, g, c: (rows(s, g, c), COL_B // (gp * SSD_STATE) + g)),
            pl.BlockSpec((l, gp * SSD_STATE), lambda s, g, c: (rows(s, g, c), COL_C // (gp * SSD_STATE) + g)),
            pl.BlockSpec((l, gp * GROUP_W), lambda s, g, c: (rows(s, g, c), COL_Z // (gp * GROUP_W) + g)),
            dt_spec,
            state_spec(True),
            per_seq_group(SUBLANES, CONV_GW, shared=True),
            per_group(CONV_W, CONV_GW),
            per_group(1, CONV_GW),
            per_group(head_rows, LANES), per_group(head_rows, LANES), per_group(head_rows, LANES),
            pl.BlockSpec((1, gp * GROUP_W), lambda s, g, c: (0, g)),
        ] + extra_specs,
        out_specs=[
            pl.BlockSpec((l, gp * GROUP_W), lambda s, g, c: (s * nc + c, g)),
            state_spec(False),
            per_seq_group(SUBLANES, CONV_GW, shared=False),
        ],
        out_shape=[
            jax.ShapeDtypeStruct((n_seq * t, D_INNER), BF16),
            jax.ShapeDtypeStruct((n_seq, SSD_HEADS, SSD_HEAD_DIM, SSD_STATE), F32),
            jax.ShapeDtypeStruct((n_seq, SSD_GROUPS, SUBLANES, CONV_GW), F32),
        ],
        scratch_shapes=[pltpu.VMEM((gp * SSD_HPG, SSD_HEAD_DIM, SSD_STATE), F32)] + scratch,
        compiler_params=_cparams(("parallel", "parallel", "arbitrary")),
        name="ssd_long" if long_chunks else "ssd",
    )(proj, proj, proj, proj, dt_raw, h0, conv_prev, cw_g, cb_g, *head_par, norm_g, *extra_args)


def _qk(q, k):
    return lax.dot_general(q, k, (((1,), (1,)), ((), ())), preferred_element_type=F32)


def _attn_prompt_kernel(lam_ref, q_ref, k_ref, vt_ref, mk_ref, mvt_ref, o_ref, sa_scr, sb_scr, acc_scr, m_scr):
    i = pl.program_id(2)
    tq = q_ref.shape[0]
    lam = lam_ref[0]
    q = q_ref[...]
    lane = lax.broadcasted_iota(I32, q.shape, 1)
    zero = jnp.zeros_like(q)
    qm = [jnp.where(lane < ATT_HEAD_DIM, q, zero), jnp.where(lane >= ATT_HEAD_DIM, q, zero)]

    def put_scores(s_ref, j):
        kblk = k_ref[pl.ds(pl.multiple_of(j * tq, tq), tq), :]
        for r in range(2):
            s_ref[r] = _qk(kblk, qm[r])

    def with_ones(vt):
        return jnp.concatenate([vt, jnp.ones((ONES_ROWS, vt.shape[1]), BF16)], axis=0)

    def update(s_ref, j, last=False):
        vt1 = with_ones(vt_ref[j])
        for r in range(2):
            s = s_ref[r]
            m_p = m_scr[r]
            if last:
                kpos = lax.broadcasted_iota(I32, (tq, tq), 0) // CHUNK
                qpos = lax.broadcasted_iota(I32, (tq, tq), 1) // CHUNK
                s = jnp.where(kpos <= qpos, s, NEG_BIG)
                s_meta = _qk(mk_ref[...], qm[r])
                m_p = jnp.maximum(m_p, jnp.max(s_meta, axis=0, keepdims=True))
            m_n = jnp.maximum(m_p, jnp.max(s, axis=0, keepdims=True))
            alpha = jnp.exp2(m_scr[r] - m_n)
            p = jnp.exp2((s - m_n).astype(BF16))
            acc = alpha * acc_scr[r] + jnp.dot(vt1, p, preferred_element_type=F32)
            if last:
                p_meta = jnp.exp2((s_meta - m_n).astype(BF16))
                acc = acc + jnp.dot(with_ones(mvt_ref[...]), p_meta, preferred_element_type=F32)
            acc_scr[r] = acc
            m_scr[r] = m_n

    m_scr[...] = jnp.full(m_scr.shape, NEG_BIG, F32)
    acc_scr[...] = jnp.zeros(acc_scr.shape, F32)

    put_scores(sa_scr, 0)

    def body(jj, _):
        j = 2 * jj
        put_scores(sb_scr, j + 1)
        update(sa_scr, j)
        put_scores(sa_scr, j + 2)
        update(sb_scr, j + 1)
        return 0

    lax.fori_loop(0, i // 2, body, 0)

    @pl.when(i % 2 == 0)
    def _():
        update(sa_scr, i, last=True)

    @pl.when(i % 2 == 1)
    def _():
        put_scores(sb_scr, i)
        update(sa_scr, i - 1)
        update(sb_scr, i, last=True)

    outs = [acc_scr[r, :ATT_V_DIM, :] / acc_scr[r, ATT_V_DIM:ATT_V_DIM + 1, :] for r in range(2)]
    o_ref[...] = (outs[0] - lam * outs[1]).T.astype(o_ref.dtype)


def _attn_prompt(lam, proj_p, vt_p, proj_s, mvt, *, batch, seq, meta_row0):
    nq = seq // ATT_TQ
    return pl.pallas_call(
        _attn_prompt_kernel,
        grid=(batch, ATT_HEADS, nq),
        in_specs=[
            pl.BlockSpec(memory_space=pltpu.SMEM),
            pl.BlockSpec((ATT_TQ, LANES), lambda b, h, i: (b * nq + i, COL_Q // LANES + h)),
            pl.BlockSpec((seq, LANES), lambda b, h, i: (b, COL_K // LANES + h)),
            pl.BlockSpec((None, None, nq, ATT_V_DIM, ATT_TQ), lambda b, h, i: (b, h, 0, 0, 0)),
            pl.BlockSpec((N_META, LANES), lambda b, h, i: (meta_row0 // N_META, COL_K // LANES + h)),
            pl.BlockSpec((None, ATT_V_DIM, N_META), lambda b, h, i: (h, 0, 0)),
        ],
        out_specs=pl.BlockSpec((ATT_TQ, LANES), lambda b, h, i: (b * nq + i, h)),
        out_shape=jax.ShapeDtypeStruct((batch * seq, ATT_WIDTH), BF16),
        scratch_shapes=[pltpu.VMEM((2, ATT_TQ, ATT_TQ), F32),
                        pltpu.VMEM((2, ATT_TQ, ATT_TQ), F32),
                        pltpu.VMEM((2, ATT_V_DIM + ONES_ROWS, ATT_TQ), F32),
                        pltpu.VMEM((2, 1, ATT_TQ), F32)],
        compiler_params=_cparams(("parallel", "parallel", "arbitrary")),
        name="attn_prompt",
    )(lam, proj_p, proj_p, vt_p, proj_s, mvt)


def _attn_short_kernel(lam_ref, q_ref, kn_ref, vn_ref, *rest, has_cache):
    if has_cache:
        kc_ref, vc_ref, o_ref = rest
    else:
        (o_ref,) = rest
    t = q_ref.shape[0]
    nh = 2 * ATT_HEADS
    lam = lam_ref[0]
    q = q_ref[...].astype(F32)
    qb = jnp.broadcast_to(q[None], (nh, t, ATT_WIDTH)).reshape(nh * t, ATT_WIDTH)
    row_head = lax.broadcasted_iota(I32, (nh * t, ATT_WIDTH), 0) // t
    col_head = lax.broadcasted_iota(I32, (nh * t, ATT_WIDTH), 1) // ATT_HEAD_DIM
    qbd = jnp.where(row_head == col_head, qb, 0.0).astype(BF16)
    s_new = _qk(qbd, kn_ref[...])
    m = jnp.max(s_new, axis=-1, keepdims=True)
    if has_cache:
        s_old = jnp.dot(qbd, kc_ref[...].astype(BF16), preferred_element_type=F32)
        m = jnp.maximum(m, jnp.max(s_old, axis=-1, keepdims=True))
    p_new = jnp.exp2(s_new - m)
    den = jnp.sum(p_new, axis=-1, keepdims=True)
    p_new = p_new.astype(BF16)
    if has_cache:
        p_old = jnp.exp2(s_old - m)
        den = den + jnp.sum(p_old, axis=-1, keepdims=True)
        p_old = p_old.astype(BF16)
        past = p_old.shape[1]
    for h in range(ATT_HEADS):
        cols = slice(h * ATT_V_DIM, (h + 1) * ATT_V_DIM)
        pair = slice(2 * h * t, (2 * h + 2) * t)
        acc = jnp.dot(p_new[pair], vn_ref[:, cols], preferred_element_type=F32)
        if has_cache:
            v_h = vc_ref[pl.ds(h, past, stride=ATT_HEADS), :].astype(BF16)
            acc = acc + jnp.dot(p_old[pair], v_h, preferred_element_type=F32)
        acc = acc / den[pair]
        o_ref[:, cols] = (acc[0:t] - lam * acc[t:2 * t]).astype(o_ref.dtype)


def _attn_short(lam, proj_s, cache_k, cache_v, *, n_seq, t, row0):
    has_cache = cache_k is not None
    rb0 = row0 // t
    in_specs = [
        pl.BlockSpec(memory_space=pltpu.SMEM),
        pl.BlockSpec((t, ATT_WIDTH), lambda s: (rb0 + s, COL_Q // ATT_WIDTH)),
        pl.BlockSpec((t, ATT_WIDTH), lambda s: (rb0 + s, COL_K // ATT_WIDTH)),
        pl.BlockSpec((t, ATT_WIDTH), lambda s: (rb0 + s, COL_V // ATT_WIDTH)),
    ]
    args = [lam, proj_s, proj_s, proj_s]
    if has_cache:
        past = cache_k.shape[2]
        in_specs += [pl.BlockSpec((None, ATT_WIDTH, past), lambda s: (s, 0, 0)),
                     pl.BlockSpec((None, past * ATT_HEADS, ATT_V_DIM), lambda s: (s, 0, 0))]
        args += [cache_k, cache_v]
    return pl.pallas_call(
        functools.partial(_attn_short_kernel, has_cache=has_cache),
        grid=(n_seq,),
        in_specs=in_specs,
        out_specs=pl.BlockSpec((t, ATT_WIDTH), lambda s: (s, 0)),
        out_shape=jax.ShapeDtypeStruct((n_seq * t, ATT_WIDTH), BF16),
        compiler_params=_cparams(("parallel",)),
        name="attn_cached" if has_cache else "attn_meta",
    )(*args)


def _merge_kernel(x_ref, yn_ref, o_ref, g1_ref, g2_ref, wso_ref, wao_ref, wo_ref, bg1_ref, bg2_ref,
                  sub_ref, n2_ref, wr_ref, br_ref,
                  h1_ref, u2_ref, eidx_ref, gate_ref, rank_ref, cnt_ref, *, sub_scale):
    tm = x_ref.shape[0]
    y_ssd = jnp.dot(yn_ref[...], wso_ref[...], preferred_element_type=F32)
    o = o_ref[...].astype(F32)
    parts = []
    for h in range(ATT_HEADS):
        oh = o[:, h * ATT_V_DIM:(h + 1) * ATT_V_DIM]
        ms = jnp.mean(oh * oh, axis=-1, keepdims=True)
        parts.append(oh * lax.rsqrt(ms + EPS) * sub_ref[...] * sub_scale)
    on = jnp.concatenate(parts, axis=1).astype(BF16)
    y_att = jnp.dot(on, wao_ref[...], preferred_element_type=F32)
    gs = _sigmoid(g1_ref[...].astype(F32) + bg1_ref[...])
    ga = _sigmoid(g2_ref[...].astype(F32) + bg2_ref[...])
    mix_in = (gs * y_ssd + ga * y_att).astype(BF16)
    h1 = x_ref[...] + jnp.dot(mix_in, wo_ref[...], preferred_element_type=F32)
    h1_ref[...] = h1
    ms = jnp.mean(h1 * h1, axis=-1, keepdims=True)
    u2 = h1 * lax.rsqrt(ms + EPS) * n2_ref[...]
    u2_ref[...] = u2.astype(u2_ref.dtype)

    logits = lax.dot_general(wr_ref[...], u2, (((1,), (1,)), ((), ())), preferred_element_type=F32,
                             precision=lax.Precision.HIGHEST)
    work = (logits + br_ref[...])[0:N_EXPERTS, :]
    expert = lax.broadcasted_iota(I32, (N_EXPERTS, tm), 0).astype(F32)
    vals, idxs, hots = [], [], []
    for _ in range(TOP_K):
        mx = jnp.max(work, axis=0, keepdims=True)
        ix = jnp.min(jnp.where(work == mx, expert, float(N_EXPERTS)), axis=0, keepdims=True)
        hot = expert == ix
        vals.append(mx)
        idxs.append(ix)
        hots.append(hot)
        work = jnp.where(hot, NEG_BIG, work)
    es = [jnp.exp(v - vals[0]) for v in vals]
    den = es[0] + es[1] + es[2] + es[3]
    hot_all = jnp.where(hots[0] | hots[1] | hots[2] | hots[3], 1.0, 0.0)
    si = lax.broadcasted_iota(I32, (tm, tm), 0)
    ti = lax.broadcasted_iota(I32, (tm, tm), 1)
    earlier = (si < ti).astype(BF16)
    prefix = jnp.dot(hot_all.astype(BF16), earlier, preferred_element_type=F32)
    row = lax.broadcasted_iota(I32, (SUBLANES, tm), 0)
    eidx = jnp.zeros((SUBLANES, tm), F32)
    gate = jnp.zeros((SUBLANES, tm), F32)
    rank = jnp.zeros((SUBLANES, tm), F32)
    for k in range(TOP_K):
        rk = jnp.sum(jnp.where(hots[k], prefix, 0.0), axis=0, keepdims=True)
        eidx = jnp.where(row == k, idxs[k], eidx)
        gate = jnp.where(row == k, es[k] / den, gate)
        rank = jnp.where(row == k, rk, rank)
    eidx_ref[...] = eidx.astype(I32)
    gate_ref[...] = gate
    rank_ref[...] = rank.astype(I32)
    cnt_ref[...] = jnp.sum(hot_all, axis=1, keepdims=True).astype(I32)


def _merge(x, yn, o, proj, wso, wao, wo, bg1, bg2, sub_g, n2_g, wr_t, br_c, *, tm, sub_scale):
    rows = x.shape[0]
    full = lambda shape: pl.BlockSpec(shape, lambda i: (0,) * len(shape))
    tok = lambda w: pl.BlockSpec((tm, w), lambda i: (i, 0))
    per_block = lambda a, b: pl.BlockSpec((None, a, b), lambda i: (i, 0, 0))
    return pl.pallas_call(
        functools.partial(_merge_kernel, sub_scale=sub_scale),
        grid=(rows // tm,),
        in_specs=[
            tok(D_MODEL), tok(D_INNER), tok(ATT_WIDTH),
            pl.BlockSpec((tm, D_MODEL), lambda i: (i, COL_G // D_MODEL)),
            pl.BlockSpec((tm, D_MODEL), lambda i: (i, COL_G // D_MODEL + 1)),
            full((D_INNER, D_MODEL)), full((ATT_WIDTH, D_MODEL)), full((D_MODEL, D_MODEL)),
            full((1, D_MODEL)), full((1, D_MODEL)), full((1, ATT_V_DIM)), full((1, D_MODEL)),
            full((LANES, D_MODEL)), full((LANES, 1)),
        ],
        out_specs=[tok(D_MODEL), tok(D_MODEL), per_block(SUBLANES, tm), per_block(SUBLANES, tm),
                   per_block(SUBLANES, tm), per_block(N_EXPERTS, 1)],
        out_shape=[
            jax.ShapeDtypeStruct((rows, D_MODEL), F32),
            jax.ShapeDtypeStruct((rows, D_MODEL), BF16),
            jax.ShapeDtypeStruct((rows // tm, SUBLANES, tm), I32),
            jax.ShapeDtypeStruct((rows // tm, SUBLANES, tm), F32),
            jax.ShapeDtypeStruct((rows // tm, SUBLANES, tm), I32),
            jax.ShapeDtypeStruct((rows // tm, N_EXPERTS, 1), I32),
        ],
        compiler_params=_cparams(("parallel",)),
        name="merge_router",
    )(x, yn, o, proj, proj, wso, wao, wo, bg1, bg2, sub_g, n2_g, wr_t, br_c)


def _seg_copy(local_ref, lo, hbm_ref, hi, sem, to_hbm):
    loc = local_ref.at[pl.ds(pl.multiple_of(lo, SEG_PAD), SEG_PAD), :]
    hbm = hbm_ref.at[pl.ds(pl.multiple_of(hi, SEG_PAD), SEG_PAD), :]
    return pltpu.make_async_copy(loc, hbm, sem) if to_hbm else pltpu.make_async_copy(hbm, loc, sem)


def _seg_copies_start(tab_ref, local_ref, hbm_ref, sem, to_hbm):
    def per_expert(e, total):
        n, lo, hi = tab_ref[0, e], tab_ref[1, e], tab_ref[2, e]

        def per_copy(c, _):
            _seg_copy(local_ref, lo + c * SEG_PAD, hbm_ref, hi + c * SEG_PAD, sem, to_hbm).start()
            return 0

        lax.fori_loop(0, n, per_copy, 0)
        return total + n

    return lax.fori_loop(0, N_EXPERTS, per_expert, 0)


def _seg_copies_wait(n, local_ref, hbm_ref, sem, to_hbm):
    def one(c, _):
        _seg_copy(local_ref, 0, hbm_ref, 0, sem, to_hbm).wait()
        return 0

    lax.fori_loop(0, n, one, 0)


def _dispatch_kernel(tab_ref, lpos_ref, u_ref, *rest, fill_tails):
    if fill_tails:
        tail_ref, _, xs_ref, loc_scr, zero_scr, sem = rest
    else:
        xs_ref, loc_scr, sem = rest
    lr, tm = loc_scr.shape[0], u_ref.shape[0]
    lpos = lpos_ref[...].astype(jnp.int16)
    p = lax.broadcasted_iota(I32, (lr, tm), 0).astype(jnp.int16)
    perm = jnp.zeros((lr, tm), BF16)
    for k in range(TOP_K):
        perm = jnp.where(p == lpos[k:k + 1, :], jnp.ones((lr, tm), BF16), perm)
    loc_scr[...] = jnp.dot(perm, u_ref[...], preferred_element_type=F32).astype(loc_scr.dtype)
    n = _seg_copies_start(tab_ref, loc_scr, xs_ref, sem, to_hbm=True)
    if fill_tails:
        zero_scr[...] = jnp.zeros(zero_scr.shape, zero_scr.dtype)

        def per_expert(e, total):
            cnt, hi = tail_ref[0, e], tail_ref[1, e]

            def per_copy(c, _):
                _seg_copy(zero_scr, 0, xs_ref, hi + c * SEG_PAD, sem, True).start()
                return 0

            lax.fori_loop(0, cnt, per_copy, 0)
            return total + cnt

        n = n + lax.fori_loop(0, N_EXPERTS, per_expert, 0)
    _seg_copies_wait(n, loc_scr, xs_ref, sem, to_hbm=True)


def _local_rows(tm):
    return tm * TOP_K + N_EXPERTS * SEG_PAD


def _dispatch(tab, lpos_t, u2, *, tm, xs_rows=None, xs=None, tail=None):
    rows = u2.shape[0]
    in_specs = [
        pl.BlockSpec((None, 3, N_EXPERTS), lambda i: (i, 0, 0), memory_space=pltpu.SMEM),
        pl.BlockSpec((None, TOP_K, tm), lambda i: (i, 0, 0)),
        pl.BlockSpec((tm, D_MODEL), lambda i: (i, 0)),
    ]
    args = [tab, lpos_t, u2]
    scratch = [pltpu.VMEM((_local_rows(tm), D_MODEL), BF16)]
    aliases = {}
    if tail is not None:
        assert xs is not None and rows == tm
        in_specs += [pl.BlockSpec(memory_space=pltpu.SMEM), pl.BlockSpec(memory_space=pl.ANY)]
        args += [tail, xs]
        scratch.append(pltpu.VMEM((SEG_PAD, D_MODEL), BF16))
        aliases = {4: 0}
        xs_rows = xs.shape[0]
    return pl.pallas_call(
        functools.partial(_dispatch_kernel, fill_tails=tail is not None),
        grid=(rows // tm,),
        in_specs=in_specs,
        out_specs=pl.BlockSpec(memory_space=pl.ANY),
        out_shape=jax.ShapeDtypeStruct((xs_rows, D_MODEL), BF16),
        scratch_shapes=scratch + [pltpu.SemaphoreType.DMA(())],
        input_output_aliases=aliases,
        compiler_params=_cparams(("arbitrary",)),
        name="moe_dispatch",
    )(*args)


def _ffn_kernel(be_ref, na_ref, x_ref, wgu_ref, bgu_ref, wd_ref, bd_ref, y_ref, wgu_scr, wd_scr):
    i = pl.program_id(0)
    active = i < na_ref[0]

    @pl.when(active & ((i == 0) | (be_ref[i] != be_ref[jnp.maximum(i - 1, 0)])))
    def _():
        wgu_scr[...] = wgu_ref[...].astype(BF16)
        wd_scr[...] = wd_ref[...].astype(BF16)

    @pl.when(active)
    def _():
        gu = jnp.dot(x_ref[...], wgu_scr[...], preferred_element_type=F32) + bgu_ref[...]
        gate = jnp.minimum(gu[:, :D_FF], SWIGLU_LIMIT)
        up = jnp.clip(gu[:, D_FF:], -SWIGLU_LIMIT, SWIGLU_LIMIT)
        hdn = (up + 1.0) * gate * _sigmoid(SWIGLU_ALPHA * gate)
        y = jnp.dot(hdn.astype(BF16), wd_scr[...], preferred_element_type=F32) + bd_ref[...]
        y_ref[...] = y.astype(y_ref.dtype)


def _ffn(block_exp, n_active, xs, wgu, bgu, wd, bd):
    n_blocks = xs.shape[0] // MOE_BLK

    def blk(i, be, na):
        return jnp.minimum(i, na[0] - 1)

    return pl.pallas_call(
        _ffn_kernel,
        grid_spec=pltpu.PrefetchScalarGridSpec(
            num_scalar_prefetch=2,
            grid=(n_blocks,),
            in_specs=[
                pl.BlockSpec((MOE_BLK, D_MODEL), lambda i, be, na: (blk(i, be, na), 0)),
                pl.BlockSpec((None, D_MODEL, 2 * D_FF), lambda i, be, na: (be[blk(i, be, na)], 0, 0)),
                pl.BlockSpec((None, 1, 2 * D_FF), lambda i, be, na: (be[blk(i, be, na)], 0, 0)),
                pl.BlockSpec((None, D_FF, D_MODEL), lambda i, be, na: (be[blk(i, be, na)], 0, 0)),
                pl.BlockSpec((None, 1, D_MODEL), lambda i, be, na: (be[blk(i, be, na)], 0, 0)),
            ],
            out_specs=pl.BlockSpec((MOE_BLK, D_MODEL), lambda i, be, na: (blk(i, be, na), 0)),
            scratch_shapes=[pltpu.VMEM((D_MODEL, 2 * D_FF), BF16), pltpu.VMEM((D_FF, D_MODEL), BF16)],
        ),
        out_shape=jax.ShapeDtypeStruct(xs.shape, xs.dtype),
        compiler_params=_cparams(("arbitrary",)),
        name="moe_ffn",
    )(block_exp, n_active, xs, wgu, bgu, wd, bd)


def _combine_kernel(tab_ref, lpos_ref, gate_ref, h1_ref, fg_ref, ys_ref, y_ref, loc_scr, sem):
    lr, tm = loc_scr.shape[0], h1_ref.shape[0]
    loc_scr[...] = jnp.zeros(loc_scr.shape, loc_scr.dtype)
    n = _seg_copies_start(tab_ref, loc_scr, ys_ref, sem, to_hbm=False)
    lpos = lpos_ref[...].astype(jnp.int16)
    gate = gate_ref[...].astype(BF16)
    p = lax.broadcasted_iota(I32, (tm, lr), 1).astype(jnp.int16)
    pick = jnp.zeros((tm, lr), BF16)
    for k in range(TOP_K):
        pick = jnp.where(p == lpos[:, k:k + 1], jnp.broadcast_to(gate[:, k:k + 1], (tm, lr)), pick)
    _seg_copies_wait(n, loc_scr, ys_ref, sem, to_hbm=False)
    h = h1_ref[...] + jnp.dot(pick, loc_scr[...], preferred_element_type=F32)
    ms = jnp.mean(h * h, axis=-1, keepdims=True)
    y_ref[...] = h * lax.rsqrt(ms + EPS) * fg_ref[...]


def _combine(tab, lpos, gate, h1, fg, ys, *, tm):
    rows = h1.shape[0]
    return pl.pallas_call(
        _combine_kernel,
        grid=(rows // tm,),
        in_specs=[
            pl.BlockSpec((None, 3, N_EXPERTS), lambda i: (i, 0, 0), memory_space=pltpu.SMEM),
            pl.BlockSpec((tm, TOP_K), lambda i: (i, 0)),
            pl.BlockSpec((tm, TOP_K), lambda i: (i, 0)),
            pl.BlockSpec((tm, D_MODEL), lambda i: (i, 0)),
            pl.BlockSpec((1, D_MODEL), lambda i: (0, 0)),
            pl.BlockSpec(memory_space=pl.ANY),
        ],
        out_specs=pl.BlockSpec((tm, D_MODEL), lambda i: (i, 0)),
        out_shape=jax.ShapeDtypeStruct((rows, D_MODEL), F32),
        scratch_shapes=[pltpu.VMEM((_local_rows(tm), D_MODEL), ys.dtype), pltpu.SemaphoreType.DMA(())],
        compiler_params=_cparams(("arbitrary",)),
        name="moe_combine",
    )(tab, lpos, gate, h1, fg, ys)


def _rope_tables(pos):
    d = ATT_HEAD_DIM
    inv = ROPE_THETA ** (-jnp.arange(0, d, 2, dtype=F32) / d)
    ang = pos.astype(F32)[:, None] * inv[None, :]
    cos = jnp.cos(ang)
    sin = jnp.sin(ang)
    cos_h = jnp.concatenate([cos, cos], axis=1)
    sin_h = jnp.concatenate([-sin, sin], axis=1)
    return jnp.tile(cos_h, (1, LANES // d)), jnp.tile(sin_h, (1, LANES // d))


def _conv_by_group(a):
    lead = a.shape[:-1]
    x = a[..., :D_INNER].reshape(lead + (SSD_GROUPS, GROUP_W))
    b = a[..., D_INNER:D_INNER + SSD_GN].reshape(lead + (SSD_GROUPS, SSD_STATE))
    c = a[..., D_INNER + SSD_GN:].reshape(lead + (SSD_GROUPS, SSD_STATE))
    return jnp.concatenate([x, b, c], axis=-1)


def _conv_from_group(a):
    lead = a.shape[:-2]
    x = a[..., :GROUP_W].reshape(lead + (D_INNER,))
    b = a[..., GROUP_W:GROUP_W + SSD_STATE].reshape(lead + (SSD_GN,))
    c = a[..., GROUP_W + SSD_STATE:].reshape(lead + (SSD_GN,))
    return jnp.concatenate([x, b, c], axis=-1)


def _conv_prev_blocks(prev):
    g = jnp.moveaxis(_conv_by_group(prev.astype(F32)), 1, 2)
    return jnp.pad(g, ((0, 0), (0, 0), (SUBLANES - (CONV_W - 1), 0), (0, 0)))


def _conv_tail_rows(ct):
    return _conv_from_group(jnp.moveaxis(ct[:, :, SUBLANES - (CONV_W - 1):, :], 1, 2))


def kernel(x_prompt, x_sample, cache_k, cache_v, state_ssm, state_conv, meta_tokens, norm1_g, w_in, conv_w, conv_b, dt_bias, a_log, d_skip, ssd_norm_g, lambda_q1, lambda_k1, lambda_q2, lambda_k2, subln_g, w_ssd_out, w_att_out, b_gate, w_o, norm2_g, w_router, b_router, w_gu, b_gu, w_down, b_down, final_norm_g):
    batch, seq, _ = x_prompt.shape
    dbatch, dseq, _ = x_sample.shape
    past = cache_k.shape[2]
    depth = norm1_g.shape[0]
    assert depth == 1 and dseq == N_META
    assert seq % SSD_L == 0 and seq % ATT_TQ == 0 and seq % TOKEN_TM == 0 and ATT_TQ % CHUNK == 0
    assert TOKEN_TM % ATT_TQ == 0 and PROJ_TM % ATT_TQ == 0
    n_p = batch * seq
    n_dec = dbatch * dseq
    n_s = n_dec + N_META
    lam_init = 0.8 - 0.6 * math.exp(-0.3 * 0)
    l = 0

    wi = w_in[l]
    o_z, o_xbc, o_dt = 0, D_INNER, D_INNER + CONV_DIM
    o_q = o_dt + SSD_HEADS
    w_main = jnp.concatenate([wi[:, o_z:o_xbc], wi[:, o_xbc:o_dt], wi[:, o_q:]], axis=1).astype(BF16)
    w_dt_heads = wi[:, o_dt:o_q].reshape(D_MODEL, SSD_GROUPS, SSD_HPG)
    w_dt = jnp.zeros((D_MODEL, SSD_GROUPS, LANES), F32).at[:, :, :SSD_HPG].set(
        w_dt_heads).reshape(D_MODEL, SSD_GROUPS * LANES).astype(BF16)
    w_dt_c = jnp.zeros((D_MODEL, LANES), F32).at[:, :SSD_HEADS].set(
        w_dt_heads.transpose(0, 2, 1).reshape(D_MODEL, SSD_HEADS)).astype(BF16)
    g1 = norm1_g[l].reshape(1, D_MODEL)
    cw_g = jnp.moveaxis(_conv_by_group(conv_w[l]), 0, 1)
    cb_g = _conv_by_group(conv_b[l])[:, None, :]
    a_neg = -jnp.exp(a_log[l].astype(F32))
    norm_g = ssd_norm_g[l].reshape(1, D_INNER)
    lam = (jnp.exp(jnp.sum(lambda_q1[l].astype(F32) * lambda_k1[l].astype(F32)))
           - jnp.exp(jnp.sum(lambda_q2[l].astype(F32) * lambda_k2[l].astype(F32))) + lam_init).reshape(1)
    wso = w_ssd_out[l].astype(BF16)
    wao = w_att_out[l].astype(BF16)
    wo = w_o[l].astype(BF16)
    bg1 = b_gate[l][:D_MODEL].reshape(1, D_MODEL)
    bg2 = b_gate[l][D_MODEL:].reshape(1, D_MODEL)
    sub_g = subln_g[l].reshape(1, ATT_V_DIM)
    n2_g = norm2_g[l].reshape(1, D_MODEL)
    wr = jnp.zeros((LANES, D_MODEL), F32).at[:N_EXPERTS, :].set(w_router[l].T)
    br = jnp.zeros((LANES, 1), F32).at[:N_EXPERTS, 0].set(b_router[l])
    wgu = w_gu[l]
    bgu = b_gu[l][:, None, :]
    wd = w_down[l]
    bd = b_down[l][:, None, :]
    fg = final_norm_g.reshape(1, D_MODEL)

    xp = x_prompt.reshape(n_p, D_MODEL)
    xs_rows = jnp.concatenate([x_sample.reshape(n_dec, D_MODEL), meta_tokens.astype(x_prompt.dtype)], axis=0)
    cos_p, sin_p = _rope_tables(N_META + jnp.arange(seq, dtype=I32))
    pos_s = jnp.concatenate([jnp.tile(past + jnp.arange(dseq, dtype=I32), dbatch), jnp.arange(N_META, dtype=I32)])
    cos_s, sin_s = _rope_tables(pos_s)

    tm_p = PROJ_TM if seq % PROJ_TM == 0 else TOKEN_TM
    proj_p, dt_p, vt_p, kc_p, vc_p = _in_proj(xp, g1, w_main, w_dt_c, cos_p, sin_p, tm_p, seq // tm_p, vt_block=ATT_TQ)
    proj_s, dt_s = _in_proj(xs_rows, g1, w_main, w_dt, cos_s, sin_s, n_s, 1)

    ssd_args = (cw_g, cb_g, dt_bias[l], a_neg, d_skip[l], norm_g)
    zero_h = jnp.zeros((1, SSD_HEADS, SSD_HEAD_DIM, SSD_STATE), F32)
    zero_c = jnp.zeros((1, SSD_GROUPS, SUBLANES, CONV_GW), F32)
    yn_m, h_m, ct_m = _ssd(proj_s, dt_s, zero_h, zero_c, *ssd_args, n_seq=1, t=N_META, l=N_META,
                           row0=n_dec, shared_state=True)
    yn_p, h_p, ct_p = _ssd(proj_p, dt_p, h_m, ct_m, *ssd_args, n_seq=batch, t=seq, l=SSD_L, row0=0,
                           shared_state=True)
    yn_d, h_d, ct_d = _ssd(proj_s, dt_s, state_ssm[l].astype(F32), _conv_prev_blocks(state_conv[l]), *ssd_args,
                           n_seq=dbatch, t=dseq, l=dseq, row0=0, shared_state=False)
    yn_s = jnp.concatenate([yn_d, yn_m], axis=0)

    mvt =proj_s[n_dec:n_s, COL_V:COL_V + ATT_WIDTH].reshape(N_META, ATT_HEADS, ATT_V_DIM).transpose(1, 2, 0)
    o_p = _attn_prompt(lam, proj_p, vt_p, proj_s, mvt, batch=batch, seq=seq, meta_row0=n_dec)
    o_d = _attn_short(lam, proj_s, cache_k[l].reshape(dbatch, past, ATT_WIDTH).transpose(0, 2, 1),
                      cache_v[l].reshape(dbatch, past * ATT_HEADS, ATT_V_DIM), n_seq=dbatch, t=dseq, row0=0)
    o_m = _attn_short(lam, proj_s, None, None, n_seq=1, t=N_META, row0=n_dec)
    o_s = jnp.concatenate([o_d, o_m], axis=0)

    merge_w = (wso, wao, wo, bg1, bg2, sub_g, n2_g, wr, br)
    h1_p, u2_p, e_p, gt_p, rk_p, cnt_p = _merge(xp, yn_p, o_p, proj_p, *merge_w, tm=TOKEN_TM,
                                                sub_scale=1.0 - lam_init)
    h1_s, u2_s, e_s, gt_s, rk_s, cnt_s = _merge(xs_rows, yn_s, o_s, proj_s, *merge_w, tm=n_s,
                                                sub_scale=1.0 - lam_init)

    nb_p = n_p // TOKEN_TM
    cnt = jnp.concatenate([cnt_p[:, :, 0], cnt_s[:, :, 0]], axis=0)
    seg = (cnt + SEG_PAD - 1) // SEG_PAD * SEG_PAD
    local_start = jnp.cumsum(seg, axis=1) - seg
    per_expert = jnp.sum(seg, axis=0)
    padded = (per_expert + MOE_BLK - 1) // MOE_BLK * MOE_BLK
    pend = jnp.cumsum(padded)
    pstart = pend - padded
    hbm_start = pstart[None, :] + jnp.cumsum(seg, axis=0) - seg
    tab = jnp.stack([seg // SEG_PAD, local_start, hbm_start], axis=1).astype(I32)
    n_rows_max = (n_p + n_s) * TOP_K + (nb_p + 1) * N_EXPERTS * (SEG_PAD - 1)
    n_blocks = -(-n_rows_max // MOE_BLK) + N_EXPERTS
    block_start = jnp.arange(n_blocks, dtype=I32) * MOE_BLK
    block_exp = jnp.minimum(jnp.sum((pend[None, :] <= block_start[:, None]).astype(I32), axis=1), N_EXPERTS - 1)
    n_active = (pend[-1:] // MOE_BLK).astype(I32)

    def local_rows_of(e, rk, starts):
        hot = e[:, :TOP_K, :, None] == jnp.arange(N_EXPERTS, dtype=I32)
        return (jnp.sum(jnp.where(hot, starts[:, None, None, :], 0), axis=-1) + rk[:, :TOP_K]).astype(I32)

    def by_token(a):
        return a[:, :TOP_K].transpose(0, 2, 1).reshape(-1, TOP_K)

    lpos_pt = local_rows_of(e_p, rk_p, local_start[:nb_p])
    lpos_st = local_rows_of(e_s, rk_s, local_start[nb_p:])
    lpos_p, lpos_s = by_token(lpos_pt), by_token(lpos_st)
    gt_p, gt_s = by_token(gt_p), by_token(gt_s)
    tail = jnp.stack([(padded - per_expert) // SEG_PAD, pstart + per_expert]).astype(I32)
    xs = _dispatch(tab[:nb_p], lpos_pt, u2_p, tm=TOKEN_TM, xs_rows=n_blocks * MOE_BLK)
    xs = _dispatch(tab[nb_p:], lpos_st, u2_s, tm=n_s, xs=xs, tail=tail)
    ys = _ffn(block_exp, n_active, xs, wgu, bgu, wd, bd)
    y_p = _combine(tab[:nb_p], lpos_p, gt_p, h1_p, fg, ys, tm=TOKEN_TM)
    y_s = _combine(tab[nb_p:], lpos_s, gt_s, h1_s, fg, ys, tm=n_s)

    def kv_rows(proj, col, lo, hi):
        return proj[lo:hi, col:col + ATT_WIDTH].astype(F32)

    def with_meta(cache, col, tail_shape):
        meta = kv_rows(proj_s, col, n_dec, n_s).reshape((1, N_META) + tail_shape)
        slab = cache.reshape((batch, N_META + seq) + tail_shape)
        return slab.at[:, :N_META].set(jnp.broadcast_to(meta, (batch, N_META) + tail_shape))

    new_k_p = with_meta(kc_p, COL_K, (ATT_WIDTH,)).reshape(1, batch, N_META + seq, 2 * ATT_HEADS, ATT_HEAD_DIM)
    new_v_p = with_meta(vc_p, COL_V, (ATT_HEADS, ATT_V_DIM))[None]
    new_k_s = kv_rows(proj_s, COL_K, 0, n_dec).reshape(1, dbatch, dseq, 2 * ATT_HEADS, ATT_HEAD_DIM)
    new_v_s = kv_rows(proj_s, COL_V, 0, n_dec).reshape(1, dbatch, dseq, ATT_HEADS, ATT_V_DIM)
    return (y_p.reshape(batch, seq, D_MODEL),
            y_s[:n_dec].reshape(dbatch, dseq, D_MODEL),
            new_k_p, new_v_p,
            h_p.astype(state_ssm.dtype)[None],
            _conv_tail_rows(ct_p).astype(x_prompt.dtype)[None],
            new_k_s, new_v_s,
            h_d.astype(state_ssm.dtype)[None],
            _conv_tail_rows(ct_d).astype(x_sample.dtype)[None])
```

```python
import functools
import math

import jax
import jax.numpy as jnp
from jax import lax
from jax.experimental import pallas as pl
from jax.experimental.pallas import tpu as pltpu

F32 = jnp.float32
BF16 = jnp.bfloat16
I32 = jnp.int32

D_MODEL = 1024
D_INNER = 2048
SSD_HEADS = 32
SSD_HEAD_DIM = 64
SSD_GROUPS = 8
SSD_HPG = SSD_HEADS // SSD_GROUPS
SSD_STATE = 128
SSD_GN = SSD_GROUPS * SSD_STATE
CONV_W = 4
CONV_DIM = D_INNER + 2 * SSD_GN
ATT_HEADS = 8
ATT_HEAD_DIM = 64
ATT_V_DIM = 128
ATT_WIDTH = 1024
CHUNK = 64
N_META = 16
EPS = 1e-6
ROPE_THETA = 10000.0
LOG2_E = math.log2(math.e)
N_EXPERTS = 32
TOP_K = 4
D_FF = 1024
SWIGLU_LIMIT = 7.0
SWIGLU_ALPHA = 1.702

COL_Z = 0
COL_X = COL_Z + D_INNER
COL_B = COL_X + D_INNER
COL_C = COL_B + SSD_GN
COL_Q = COL_C + SSD_GN
COL_K = COL_Q + ATT_WIDTH
COL_V = COL_K + ATT_WIDTH
COL_G = COL_V + ATT_WIDTH
N_MAIN = COL_G + 2 * D_MODEL

LANES = 128
SUBLANES = 8
GROUP_W = D_INNER // SSD_GROUPS
CONV_GW = GROUP_W + 2 * SSD_STATE
PROJ_TN = 1024
PROJ_TM = 1024
TOKEN_TM = 512
SSD_L = 256
SSD_GP = 4
ATT_TQ = 512
MOE_BLK = 512
SEG_PAD = 16
ONES_ROWS = 16
VMEM_LIMIT = 56 * 1024 * 1024
NEG_BIG = -1e30


def _cparams(sem):
    return pltpu.CompilerParams(dimension_semantics=sem, vmem_limit_bytes=VMEM_LIMIT)


def _sigmoid(x):
    return 1.0 / (1.0 + jnp.exp(-x))


def _inproj_kernel(x_ref, g_ref, w_ref, wdt_ref, cos_ref, sin_ref, *rest, emit_vt, seq_blocks):
    if emit_vt:
        mk_ref, mv_ref, o_ref, dt_ref, vt_ref, kout_ref, vout_ref, u_scr, kbuf, vbuf, sems = rest
    else:
        o_ref, dt_ref, u_scr = rest
    j = pl.program_id(1)

    @pl.when(j == 0)
    def _():
        x = x_ref[...]
        ms = jnp.mean(x * x, axis=-1, keepdims=True)
        u = (x * lax.rsqrt(ms + EPS) * g_ref[...]).astype(BF16)
        u_scr[...] = u
        dt_ref[...] = jnp.dot(u, wdt_ref[...], preferred_element_type=F32)

    acc = jnp.dot(u_scr[...], w_ref[...], preferred_element_type=F32)
    is_q = j == COL_Q // PROJ_TN
    is_k = j == COL_K // PROJ_TN

    @pl.when(is_q | is_k)
    def _():
        cos = cos_ref[...]
        sin = sin_ref[...]
        lane = lax.broadcasted_iota(I32, cos.shape, 1)
        half = ATT_HEAD_DIM // 2
        first_half = (lane % ATT_HEAD_DIM) < half
        scale = jnp.where(is_q, ATT_HEAD_DIM ** -0.5 * LOG2_E, 1.0).astype(F32)
        for c in range(PROJ_TN // LANES):
            a = acc[:, c * LANES:(c + 1) * LANES]
            swapped = jnp.where(first_half, pltpu.roll(a, LANES - half, 1), pltpu.roll(a, half, 1))
            roped = (a * cos + swapped * sin) * scale
            o_ref[:, c * LANES:(c + 1) * LANES] = roped.astype(o_ref.dtype)
            if emit_vt:
                kbuf[:, c * LANES:(c + 1) * LANES] = roped

    @pl.when(jnp.logical_not(is_q | is_k))
    def _():
        o_ref[...] = acc.astype(o_ref.dtype)

    if emit_vt:
        i = pl.program_id(0)
        tm = x_ref.shape[0]
        row0 = (i // seq_blocks) * (N_META + seq_blocks * tm) + N_META + (i % seq_blocks) * tm

        def cache_copy(buf, out_ref, sem):
            per_row = buf.shape[0] // tm
            rows = out_ref.at[pl.ds(pl.multiple_of(row0 * per_row, SUBLANES), tm * per_row), :]
            return pltpu.make_async_copy(buf, rows, sem)

        first = i % seq_blocks == 0
        slab0 = row0 - N_META

        def meta_copy(src_ref, out_ref, sem):
            per_row = src_ref.shape[0] // N_META
            rows = out_ref.at[pl.ds(pl.multiple_of(slab0 * per_row, SUBLANES), N_META * per_row), :]
            return pltpu.make_async_copy(src_ref, rows, sem)

        @pl.when(is_k)
        def _():
            cache_copy(kbuf, kout_ref, sems.at[0]).start()

        @pl.when(is_k & first)
        def _():
            meta_copy(mk_ref, kout_ref, sems.at[2]).start()
            meta_copy(mv_ref, vout_ref, sems.at[3]).start()

        @pl.when(j == COL_V // PROJ_TN)
        def _():
            for h in range(ATT_HEADS):
                vbuf[pl.ds(h, tm, stride=ATT_HEADS), :] = acc[:, h * ATT_V_DIM:(h + 1) * ATT_V_DIM]
            cache_copy(vbuf, vout_ref, sems.at[1]).start()
            tk = vt_ref.shape[-1]
            for h in range(ATT_HEADS):
                for s in range(vt_ref.shape[1]):
                    blk = acc[s * tk:(s + 1) * tk, h * ATT_V_DIM:(h + 1) * ATT_V_DIM]
                    vt_ref[h, s] = blk.T.astype(vt_ref.dtype)

        @pl.when(j == pl.num_programs(1) - 1)
        def _():
            cache_copy(kbuf, kout_ref, sems.at[0]).wait()
            cache_copy(vbuf, vout_ref, sems.at[1]).wait()

        @pl.when((j == pl.num_programs(1) - 1) & first)
        def _():
            meta_copy(mk_ref, kout_ref, sems.at[2]).wait()
            meta_copy(mv_ref, vout_ref, sems.at[3]).wait()


def _in_proj(x, g1, w_main, w_dt, cos_t, sin_t, tm, rope_blocks, vt_block=None, meta_kv=()):
    rows = x.shape[0]
    grid = (rows // tm, N_MAIN // PROJ_TN)
    scratch = [pltpu.VMEM((tm, D_MODEL), BF16)]
    dt_w = w_dt.shape[1]
    out_specs = [
        pl.BlockSpec((tm, PROJ_TN), lambda i, j: (i, j)),
        pl.BlockSpec((tm, dt_w), lambda i, j: (i, 0)),
    ]
    out_shape = [
        jax.ShapeDtypeStruct((rows, N_MAIN), BF16),
        jax.ShapeDtypeStruct((rows, dt_w), F32),
    ]
    if vt_block is not None:
        per = tm // vt_block
        out_specs.append(pl.BlockSpec((None, ATT_HEADS, per, ATT_V_DIM, vt_block),
                                      lambda i, j: (i // rope_blocks, 0, i % rope_blocks, 0, 0)))
        streams = rows // (rope_blocks * tm)
        out_shape.append(jax.ShapeDtypeStruct(
            (streams, ATT_HEADS, rope_blocks * per, ATT_V_DIM, vt_block), BF16))
        cache_rows = streams * (N_META + rope_blocks * tm)
        out_specs += [pl.BlockSpec(memory_space=pl.ANY), pl.BlockSpec(memory_space=pl.ANY)]
        out_shape += [jax.ShapeDtypeStruct((cache_rows, ATT_WIDTH), F32),
                      jax.ShapeDtypeStruct((cache_rows * ATT_HEADS, ATT_V_DIM), F32)]
        scratch += [pltpu.VMEM((tm, ATT_WIDTH), F32), pltpu.VMEM((tm * ATT_HEADS, ATT_V_DIM), F32),
                    pltpu.SemaphoreType.DMA((4,))]
    return pl.pallas_call(
        functools.partial(_inproj_kernel, emit_vt=vt_block is not None, seq_blocks=rope_blocks),
        grid=grid,
        in_specs=[
            pl.BlockSpec((tm, D_MODEL), lambda i, j: (i, 0)),
            pl.BlockSpec((1, D_MODEL), lambda i, j: (0, 0)),
            pl.BlockSpec((D_MODEL, PROJ_TN), lambda i, j: (0, j)),
            pl.BlockSpec((D_MODEL, dt_w), lambda i, j: (0, 0)),
            pl.BlockSpec((tm, LANES), lambda i, j: (i % rope_blocks, 0)),
            pl.BlockSpec((tm, LANES), lambda i, j: (i % rope_blocks, 0)),
        ] + [pl.BlockSpec(m.shape, lambda i, j: (0, 0)) for m in meta_kv],
        out_specs=out_specs,
        out_shape=out_shape,
        scratch_shapes=scratch,
        compiler_params=_cparams(("parallel", "arbitrary")),
        name="in_proj",
    )(x, g1, w_main, w_dt, cos_t, sin_t, *meta_kv)


def _ssd_kernel(x_ref, b_ref, c_ref, z_ref, dt_ref, h0_ref, cp_ref, cw_ref, cb_ref, dtb_ref,
                aneg_ref, dsk_ref, ng_ref, y_ref, hf_ref, ct_ref, h_scr, f_scr):
    c = pl.program_id(2)
    L = x_ref.shape[0]

    @pl.when(c == 0)
    def _():
        h_scr[...] = h0_ref[...]
        f_scr[0:SUBLANES, :] = cp_ref[...]

    f_scr[SUBLANES:SUBLANES + L, 0:GROUP_W] = x_ref[...].astype(F32)
    f_scr[SUBLANES:SUBLANES + L, GROUP_W:GROUP_W + SSD_STATE] = b_ref[...].astype(F32)
    f_scr[SUBLANES:SUBLANES + L, GROUP_W + SSD_STATE:CONV_GW] = c_ref[...].astype(F32)
    w = cw_ref[...]
    acc = cb_ref[...]
    for i in range(CONV_W):
        lo = SUBLANES - (CONV_W - 1) + i
        acc = acc + w[i:i + 1, :] * f_scr[lo:lo + L, :]
    xc = acc * _sigmoid(acc)
    tail = f_scr[L:L + SUBLANES, :]
    f_scr[0:SUBLANES, :] = tail
    ct_ref[...] = tail

    xg = xc[:, 0:GROUP_W]
    bm = xc[:, GROUP_W:GROUP_W + SSD_STATE].astype(BF16)
    cm = xc[:, GROUP_W + SSD_STATE:CONV_GW].astype(BF16)

    dtr = dt_ref[...] + dtb_ref[...]
    dt = jnp.maximum(dtr, 0.0) + jnp.log(1.0 + jnp.exp(-jnp.abs(dtr)))
    da = dt * aneg_ref[...]
    ti = lax.broadcasted_iota(I32, (L, L), 0)
    si = lax.broadcasted_iota(I32, (L, L), 1)
    causal = si <= ti
    tril = causal.astype(F32)
    cum = jnp.dot(tril, da, preferred_element_type=F32, precision=lax.Precision.HIGHEST)
    sel = (lax.broadcasted_iota(I32, (SUBLANES, LANES), 0)
           == lax.broadcasted_iota(I32, (SUBLANES, LANES), 1)).astype(F32)
    cum_t = lax.dot_general(sel, cum, (((1,), (1,)), ((), ())), preferred_element_type=F32,
                            precision=lax.Precision.HIGHEST)

    cb = lax.dot_general(cm, bm, (((1,), (1,)), ((), ())), preferred_element_type=F32)
    dsk = dsk_ref[...]
    ys = []
    for r in range(SSD_HPG):
        col = cum[:, r:r + 1]
        row = cum_t[r:r + 1, :]
        dec = jnp.exp(jnp.where(causal, col - row, NEG_BIG))
        m = (cb * dec).astype(BF16)
        xh = xg[:, r * SSD_HEAD_DIM:(r + 1) * SSD_HEAD_DIM]
        xdt = xh * dt[:, r:r + 1]
        h_prev = h_scr[r]
        y = jnp.dot(m, xdt.astype(BF16), preferred_element_type=F32)
        y = y + jnp.exp(col) * lax.dot_general(cm, h_prev.astype(BF16), (((1,), (1,)), ((), ())),
                                               preferred_element_type=F32)
        y = y + dsk[:, r:r + 1] * xh
        ys.append(y)
        tot = cum[L - 1:L, r:r + 1]
        xw = (xdt * jnp.exp(tot - col)).astype(BF16)
        upd = lax.dot_general(xw, bm, (((0,), (0,)), ((), ())), preferred_element_type=F32)
        h_scr[r] = h_prev * jnp.exp(tot) + upd
    yg = jnp.concatenate(ys, axis=1)
    z = z_ref[...].astype(F32)
    yz = yg * (z * _sigmoid(z))
    ms = jnp.mean(yz * yz, axis=-1, keepdims=True)
    y_ref[...] = (yz * lax.rsqrt(ms + EPS) * ng_ref[...]).astype(y_ref.dtype)
    hf_ref[...] = h_scr[...]


def _ssd_long_kernel(x_ref, b_ref, c_ref, z_ref, dt_ref, h0_ref, cp_ref, cw_ref, cb_ref, dtb_ref,
                     aneg_ref, dsk_ref, ng_ref, tri_ref, shift_ref, y_ref, hf_ref, ct_ref, h_scr, f_scr,
                     dtt_scr):
    c = pl.program_id(2)
    L = x_ref.shape[0]
    reps = L // LANES

    @pl.when(c == 0)
    def _():
        h_scr[...] = h0_ref[...]
        f_scr[:, 0:SUBLANES, :] = cp_ref[...]
        f_scr[:, SUBLANES:2 * SUBLANES, :] = jnp.zeros((SSD_GP, SUBLANES, CONV_GW), F32)

    tri = tri_ref[...]
    tri_b = tri.astype(BF16)
    visible = tri > 0.5
    dtt_scr[...] = dt_ref[...].T
    for gi in range(SSD_GP):
        xs_ = slice(gi * GROUP_W, (gi + 1) * GROUP_W)
        ns_ = slice(gi * SSD_STATE, (gi + 1) * SSD_STATE)
        xb = jnp.concatenate([x_ref[:, xs_], b_ref[:, ns_], c_ref[:, ns_]], axis=1)
        xf = xb.astype(F32)
        w = cw_ref[gi]
        acc = cb_ref[gi] + w[CONV_W - 1:CONV_W, :] * xf
        for d in range(1, CONV_W):
            sh = jnp.dot(shift_ref[d - 1], xb, preferred_element_type=F32)
            acc = acc + w[CONV_W - 1 - d:CONV_W - d, :] * sh
        corr = jnp.zeros((SUBLANES, CONV_GW), F32)
        for i in range(CONV_W - 1):
            lo = SUBLANES - (CONV_W - 1) + i
            corr = corr + w[i:i + 1, :] * f_scr[gi, lo:lo + SUBLANES, :]
        acc = jnp.concatenate([acc[0:SUBLANES] + corr, acc[SUBLANES:]], axis=0)
        xc = acc * _sigmoid(acc)
        tail = xf[L - SUBLANES:L, :]
        f_scr[gi, 0:SUBLANES, :] = tail
        ct_ref[gi] = tail

        x_t = xc[:, 0:GROUP_W].T
        bm = xc[:, GROUP_W:GROUP_W + SSD_STATE].astype(BF16)
        cm = xc[:, GROUP_W + SSD_STATE:CONV_GW].astype(BF16)

        grp = pl.program_id(1) * SSD_GP + gi
        dt_rows = [dtt_scr[pl.ds(SUBLANES * r + grp, 1), :] for r in range(SSD_HPG)]
        dtr = jnp.concatenate(dt_rows + [jnp.zeros((SUBLANES - SSD_HPG, L), F32)], axis=0)
        dtr = dtr + jnp.tile(dtb_ref[gi], (1, reps))
        dt = jnp.maximum(dtr, 0.0) + jnp.log(1.0 + jnp.exp(-jnp.abs(dtr)))
        da = dt * jnp.tile(aneg_ref[gi], (1, reps))
        d1 = da.astype(BF16)
        r1 = da - d1.astype(F32)
        d2 = r1.astype(BF16)
        d3 = (r1 - d2.astype(F32)).astype(BF16)
        cum = (jnp.dot(d1, tri_b, preferred_element_type=F32) + jnp.dot(d2, tri_b, preferred_element_type=F32)
               + jnp.dot(d3, tri_b, preferred_element_type=F32))
        cum_col = cum.T

        cb_t = lax.dot_general(bm, cm, (((1,), (1,)), ((), ())), preferred_element_type=F32)
        dsk = dsk_ref[gi]
        ys = []
        for r in range(SSD_HPG):
            hd = gi * SSD_HPG + r
            row = cum[r:r + 1, :]
            dec = jnp.exp(jnp.where(visible, row - cum_col[:, r:r + 1], NEG_BIG))
            m = (cb_t * dec).astype(BF16)
            xh = x_t[r * SSD_HEAD_DIM:(r + 1) * SSD_HEAD_DIM, :]
            xdt = xh * dt[r:r + 1, :]
            h_prev = h_scr[hd]
            y = jnp.dot(xdt.astype(BF16), m, preferred_element_type=F32)
            y = y + jnp.exp(row) * lax.dot_general(h_prev.astype(BF16), cm, (((1,), (1,)), ((), ())),
                                                   preferred_element_type=F32)
            y = y + dsk[r:r + 1, 0:1] * xh
            ys.append(y)
            tot = row[:, L - 1:L]
            xw = (xdt * jnp.exp(tot - row)).astype(BF16)
            h_scr[hd] = h_prev * jnp.exp(tot) + jnp.dot(xw, bm, preferred_element_type=F32)
        yg = jnp.concatenate(ys, axis=0).T
        z = z_ref[:, xs_].astype(F32)
        yz = yg * (z * _sigmoid(z))
        ms = jnp.mean(yz * yz, axis=-1, keepdims=True)
        y_ref[:, xs_] = (yz * lax.rsqrt(ms + EPS) * ng_ref[:, xs_]).astype(y_ref.dtype)
    hf_ref[...] = h_scr[...]


def _per_group_lanes(v):
    out = jnp.zeros((SSD_GROUPS, 1, LANES), F32)
    return out.at[:, 0, :SSD_HPG].set(v.astype(F32).reshape(SSD_GROUPS, SSD_HPG))


def _per_group_rows(v):
    out = jnp.zeros((SSD_GROUPS, SUBLANES, LANES), F32)
    return out.at[:, :SSD_HPG, :].set(
        jnp.broadcast_to(v.astype(F32).reshape(SSD_GROUPS, SSD_HPG, 1), (SSD_GROUPS, SSD_HPG, LANES)))


def _ssd(proj, dt_raw, h0, conv_prev, cw_g, cb_g, dt_bias, a_neg, d_skip, norm_g, *, n_seq, t, l, row0,
         shared_state):
    nc = t // l
    rb0 = row0 // l
    long_chunks = l % LANES == 0

    def rows(s, g, c):
        return rb0 + s * nc + c

    def sidx(s):
        return 0 if shared_state else s

    if long_chunks:
        head_rows = SUBLANES
        head_par = [_per_group_rows(v) for v in (dt_bias, a_neg, d_skip)]
        step = jnp.arange(l, dtype=I32)
        tri = (step[:, None] <= step[None, :]).astype(F32)
        shift = jnp.stack([(step[None, :] == step[:, None] - d) for d in range(1, CONV_W)]).astype(BF16)
        extra_args = [tri, shift]
        extra_specs = [pl.BlockSpec((l, l), lambda s, g, c: (0, 0)),
                       pl.BlockSpec((CONV_W - 1, l, l), lambda s, g, c: (0, 0, 0))]
        body, gp = _ssd_long_kernel, SSD_GP
        lead = (gp,)
        scratch = [pltpu.VMEM((gp, 2 * SUBLANES, CONV_GW), F32), pltpu.VMEM((LANES, l), F32)]
        dt_spec = pl.BlockSpec((l, LANES), lambda s, g, c: (rows(s, g, c), 0))
    else:
        head_rows = 1
        head_par = [_per_group_lanes(v) for v in (dt_bias, a_neg, d_skip)]
        extra_args, extra_specs = [], []
        body, gp = _ssd_kernel, 1
        lead = (None,)
        scratch = [pltpu.VMEM((l + SUBLANES, CONV_GW), F32)]
        dt_spec = pl.BlockSpec((l, LANES), lambda s, g, c: (rows(s, g, c), g))

    def per_group(*shape):
        return pl.BlockSpec(lead + shape, lambda s, g, c: (g,) + (0,) * len(shape))

    def per_seq_group(*shape, shared):
        return pl.BlockSpec((None,) + lead + shape,
                            lambda s, g, c: ((sidx(s) if shared else s), g) + (0,) * len(shape))

    def state_spec(shared):
        return pl.BlockSpec((None, gp * SSD_HPG, SSD_HEAD_DIM, SSD_STATE),
                            lambda s, g, c: ((sidx(s) if shared else s), g, 0, 0))

    return pl.pallas_call(
        body,
        grid=(n_seq, SSD_GROUPS // gp, nc),
        in_specs=[
            pl.BlockSpec((l, gp * GROUP_W), lambda s, g, c: (rows(s, g, c), COL_X // (gp * GROUP_W) + g)),
            pl.BlockSpec((l, gp * SSD_STATE), lambda s, g, c: (rows(s, g, c), COL_B // (gp * SSD_STATE) + g)),
            pl.BlockSpec((l, gp * SSD_STATE), lambda s, g, c: (rows(s, g, c), COL_C // (gp * SSD_STATE) + g)),
            pl.BlockSpec((l, gp * GROUP_W), lambda s, g, c: (rows(s, g, c), COL_Z // (gp * GROUP_W) + g)),
            dt_spec,
            state_spec(True),
            per_seq_group(SUBLANES, CONV_GW, shared=True),
            per_group(CONV_W, CONV_GW),
            per_group(1, CONV_GW),
            per_group(head_rows, LANES), per_group(head_rows, LANES), per_group(head_rows, LANES),
            pl.BlockSpec((1, gp * GROUP_W), lambda s, g, c: (0, g)),
        ] + extra_specs,
        out_specs=[
            pl.BlockSpec((l, gp * GROUP_W), lambda s, g, c: (s * nc + c, g)),
            state_spec(False),
            per_seq_group(SUBLANES, CONV_GW, shared=False),
        ],
        out_shape=[
            jax.ShapeDtypeStruct((n_seq * t, D_INNER), BF16),
            jax.ShapeDtypeStruct((n_seq, SSD_HEADS, SSD_HEAD_DIM, SSD_STATE), F32),
            jax.ShapeDtypeStruct((n_seq, SSD_GROUPS, SUBLANES, CONV_GW), F32),
        ],
        scratch_shapes=[pltpu.VMEM((gp * SSD_HPG, SSD_HEAD_DIM, SSD_STATE), F32)] + scratch,
        compiler_params=_cparams(("parallel", "parallel", "arbitrary")),
        name="ssd_long" if long_chunks else "ssd",
    )(proj, proj, proj, proj, dt_raw, h0, conv_prev, cw_g, cb_g, *head_par, norm_g, *extra_args)


def _qk(q, k):
    return lax.dot_general(q, k, (((1,), (1,)), ((), ())), preferred_element_type=F32)


def _attn_prompt_kernel(lam_ref, q_ref, k_ref, vt_ref, mk_ref, mvt_ref, o_ref, sa_scr, sb_scr, acc_scr, m_scr):
    i = pl.program_id(2)
    tq = q_ref.shape[0]
    lam = lam_ref[0]
    q = q_ref[...]
    lane = lax.broadcasted_iota(I32, q.shape, 1)
    zero = jnp.zeros_like(q)
    qm = [jnp.where(lane < ATT_HEAD_DIM, q, zero), jnp.where(lane >= ATT_HEAD_DIM, q, zero)]

    def put_scores(s_ref, j):
        kblk = k_ref[pl.ds(pl.multiple_of(j * tq, tq), tq), :]
        for r in range(2):
            s_ref[r] = _qk(kblk, qm[r])

    def with_ones(vt):
        return jnp.concatenate([vt, jnp.ones((ONES_ROWS, vt.shape[1]), BF16)], axis=0)

    def update(s_ref, j, last=False):
        vt1 = with_ones(vt_ref[j])
        for r in range(2):
            s = s_ref[r]
            m_p = m_scr[r]
            if last:
                kpos = lax.broadcasted_iota(I32, (tq, tq), 0) // CHUNK
                qpos = lax.broadcasted_iota(I32, (tq, tq), 1) // CHUNK
                s = jnp.where(kpos <= qpos, s, NEG_BIG)
                s_meta = _qk(mk_ref[...], qm[r])
                m_p = jnp.maximum(m_p, jnp.max(s_meta, axis=0, keepdims=True))
            m_n = jnp.maximum(m_p, jnp.max(s, axis=0, keepdims=True))
            alpha = jnp.exp2(m_scr[r] - m_n)
            p = jnp.exp2((s - m_n).astype(BF16))
            acc = alpha * acc_scr[r] + jnp.dot(vt1, p, preferred_element_type=F32)
            if last:
                p_meta = jnp.exp2((s_meta - m_n).astype(BF16))
                acc = acc + jnp.dot(with_ones(mvt_ref[...]), p_meta, preferred_element_type=F32)
            acc_scr[r] = acc
            m_scr[r] = m_n

    m_scr[...] = jnp.full(m_scr.shape, NEG_BIG, F32)
    acc_scr[...] = jnp.zeros(acc_scr.shape, F32)

    put_scores(sa_scr, 0)

    def body(jj, _):
        j = 2 * jj
        put_scores(sb_scr, j + 1)
        update(sa_scr, j)
        put_scores(sa_scr, j + 2)
        update(sb_scr, j + 1)
        return 0

    lax.fori_loop(0, i // 2, body, 0)

    @pl.when(i % 2 == 0)
    def _():
        update(sa_scr, i, last=True)

    @pl.when(i % 2 == 1)
    def _():
        put_scores(sb_scr, i)
        update(sa_scr, i - 1)
        update(sb_scr, i, last=True)

    outs = [acc_scr[r, :ATT_V_DIM, :] / acc_scr[r, ATT_V_DIM:ATT_V_DIM + 1, :] for r in range(2)]
    o_ref[...] = (outs[0] - lam * outs[1]).T.astype(o_ref.dtype)


def _attn_prompt(lam, proj_p, vt_p, proj_s, mvt, *, batch, seq, meta_row0):
    nq = seq // ATT_TQ
    return pl.pallas_call(
        _attn_prompt_kernel,
        grid=(batch, ATT_HEADS, nq),
        in_specs=[
            pl.BlockSpec(memory_space=pltpu.SMEM),
            pl.BlockSpec((ATT_TQ, LANES), lambda b, h, i: (b * nq + i, COL_Q // LANES + h)),
            pl.BlockSpec((seq, LANES), lambda b, h, i: (b, COL_K // LANES + h)),
            pl.BlockSpec((None, None, nq, ATT_V_DIM, ATT_TQ), lambda b, h, i: (b, h, 0, 0, 0)),
            pl.BlockSpec((N_META, LANES), lambda b, h, i: (meta_row0 // N_META, COL_K // LANES + h)),
            pl.BlockSpec((None, ATT_V_DIM, N_META), lambda b, h, i: (h, 0, 0)),
        ],
        out_specs=pl.BlockSpec((ATT_TQ, LANES), lambda b, h, i: (b * nq + i, h)),
        out_shape=jax.ShapeDtypeStruct((batch * seq, ATT_WIDTH), BF16),
        scratch_shapes=[pltpu.VMEM((2, ATT_TQ, ATT_TQ), F32),
                        pltpu.VMEM((2, ATT_TQ, ATT_TQ), F32),
                        pltpu.VMEM((2, ATT_V_DIM + ONES_ROWS, ATT_TQ), F32),
                        pltpu.VMEM((2, 1, ATT_TQ), F32)],
        compiler_params=_cparams(("parallel", "parallel", "arbitrary")),
        name="attn_prompt",
    )(lam, proj_p, proj_p, vt_p, proj_s, mvt)


def _attn_short_kernel(lam_ref, q_ref, kn_ref, vn_ref, *rest, has_cache):
    if has_cache:
        kc_ref, vc_ref, o_ref = rest
    else:
        (o_ref,) = rest
    t = q_ref.shape[0]
    nh = 2 * ATT_HEADS
    lam = lam_ref[0]
    q = q_ref[...].astype(F32)
    qb = jnp.broadcast_to(q[None], (nh, t, ATT_WIDTH)).reshape(nh * t, ATT_WIDTH)
    row_head = lax.broadcasted_iota(I32, (nh * t, ATT_WIDTH), 0) // t
    col_head = lax.broadcasted_iota(I32, (nh * t, ATT_WIDTH), 1) // ATT_HEAD_DIM
    qbd = jnp.where(row_head == col_head, qb, 0.0).astype(BF16)
    s_new = _qk(qbd, kn_ref[...])
    m = jnp.max(s_new, axis=-1, keepdims=True)
    if has_cache:
        s_old = jnp.dot(qbd, kc_ref[...].astype(BF16), preferred_element_type=F32)
        m = jnp.maximum(m, jnp.max(s_old, axis=-1, keepdims=True))
    p_new = jnp.exp2(s_new - m)
    den = jnp.sum(p_new, axis=-1, keepdims=True)
    p_new = p_new.astype(BF16)
    if has_cache:
        p_old = jnp.exp2(s_old - m)
        den = den + jnp.sum(p_old, axis=-1, keepdims=True)
        p_old = p_old.astype(BF16)
        past = p_old.shape[1]
    for h in range(ATT_HEADS):
        cols = slice(h * ATT_V_DIM, (h + 1) * ATT_V_DIM)
        pair = slice(2 * h * t, (2 * h + 2) * t)
        acc = jnp.dot(p_new[pair], vn_ref[:, cols], preferred_element_type=F32)
        if has_cache:
            v_h = vc_ref[pl.ds(h, past, stride=ATT_HEADS), :].astype(BF16)
            acc = acc + jnp.dot(p_old[pair], v_h, preferred_element_type=F32)
        acc = acc / den[pair]
        o_ref[:, cols] = (acc[0:t] - lam * acc[t:2 * t]).astype(o_ref.dtype)


def _attn_short(lam, proj_s, cache_k, cache_v, *, n_seq, t, row0):
    has_cache = cache_k is not None
    rb0 = row0 // t
    in_specs = [
        pl.BlockSpec(memory_space=pltpu.SMEM),
        pl.BlockSpec((t, ATT_WIDTH), lambda s: (rb0 + s, COL_Q // ATT_WIDTH)),
        pl.BlockSpec((t, ATT_WIDTH), lambda s: (rb0 + s, COL_K // ATT_WIDTH)),
        pl.BlockSpec((t, ATT_WIDTH), lambda s: (rb0 + s, COL_V // ATT_WIDTH)),
    ]
    args = [lam, proj_s, proj_s, proj_s]
    if has_cache:
        past = cache_k.shape[2]
        in_specs += [pl.BlockSpec((None, ATT_WIDTH, past), lambda s: (s, 0, 0)),
                     pl.BlockSpec((None, past * ATT_HEADS, ATT_V_DIM), lambda s: (s, 0, 0))]
        args += [cache_k, cache_v]
    return pl.pallas_call(
        functools.partial(_attn_short_kernel, has_cache=has_cache),
        grid=(n_seq,),
        in_specs=in_specs,
        out_specs=pl.BlockSpec((t, ATT_WIDTH), lambda s: (s, 0)),
        out_shape=jax.ShapeDtypeStruct((n_seq * t, ATT_WIDTH), BF16),
        compiler_params=_cparams(("parallel",)),
        name="attn_cached" if has_cache else "attn_meta",
    )(*args)


def _merge_kernel(x_ref, yn_ref, o_ref, g1_ref, g2_ref, wso_ref, wao_ref, wo_ref, bg1_ref, bg2_ref,
                  sub_ref, n2_ref, wr_ref, br_ref,
                  h1_ref, u2_ref, eidx_ref, gate_ref, rank_ref, cnt_ref, *, sub_scale):
    tm = x_ref.shape[0]
    y_ssd = jnp.dot(yn_ref[...], wso_ref[...], preferred_element_type=F32)
    o = o_ref[...].astype(F32)
    parts = []
    for h in range(ATT_HEADS):
        oh = o[:, h * ATT_V_DIM:(h + 1) * ATT_V_DIM]
        ms = jnp.mean(oh * oh, axis=-1, keepdims=True)
        parts.append(oh * lax.rsqrt(ms + EPS) * sub_ref[...] * sub_scale)
    on = jnp.concatenate(parts, axis=1).astype(BF16)
    y_att = jnp.dot(on, wao_ref[...], preferred_element_type=F32)
    gs = _sigmoid(g1_ref[...].astype(F32) + bg1_ref[...])
    ga = _sigmoid(g2_ref[...].astype(F32) + bg2_ref[...])
    mix_in = (gs * y_ssd + ga * y_att).astype(BF16)
    h1 = x_ref[...] + jnp.dot(mix_in, wo_ref[...], preferred_element_type=F32)
    h1_ref[...] = h1
    ms = jnp.mean(h1 * h1, axis=-1, keepdims=True)
    u2 = h1 * lax.rsqrt(ms + EPS) * n2_ref[...]
    u2_ref[...] = u2.astype(u2_ref.dtype)

    logits = lax.dot_general(wr_ref[...], u2, (((1,), (1,)), ((), ())), preferred_element_type=F32,
                             precision=lax.Precision.HIGHEST)
    work = (logits + br_ref[...])[0:N_EXPERTS, :]
    expert = lax.broadcasted_iota(I32, (N_EXPERTS, tm), 0).astype(F32)
    vals, idxs, hots = [], [], []
    for _ in range(TOP_K):
        mx = jnp.max(work, axis=0, keepdims=True)
        ix = jnp.min(jnp.where(work == mx, expert, float(N_EXPERTS)), axis=0, keepdims=True)
        hot = expert == ix
        vals.append(mx)
        idxs.append(ix)
        hots.append(hot)
        work = jnp.where(hot, NEG_BIG, work)
    es = [jnp.exp(v - vals[0]) for v in vals]
    den = es[0] + es[1] + es[2] + es[3]
    hot_all = jnp.where(hots[0] | hots[1] | hots[2] | hots[3], 1.0, 0.0)
    si = lax.broadcasted_iota(I32, (tm, tm), 0)
    ti = lax.broadcasted_iota(I32, (tm, tm), 1)
    earlier = (si < ti).astype(BF16)
    prefix = jnp.dot(hot_all.astype(BF16), earlier, preferred_element_type=F32)
    row = lax.broadcasted_iota(I32, (SUBLANES, tm), 0)
    eidx = jnp.zeros((SUBLANES, tm), F32)
    gate = jnp.zeros((SUBLANES, tm), F32)
    rank = jnp.zeros((SUBLANES, tm), F32)
    for k in range(TOP_K):
        rk = jnp.sum(jnp.where(hots[k], prefix, 0.0), axis=0, keepdims=True)
        eidx = jnp.where(row == k, idxs[k], eidx)
        gate = jnp.where(row == k, es[k] / den, gate)
        rank = jnp.where(row == k, rk, rank)
    eidx_ref[...] = eidx.astype(I32)
    gate_ref[...] = gate
    rank_ref[...] = rank.astype(I32)
    cnt_ref[...] = jnp.sum(hot_all, axis=1, keepdims=True).astype(I32)


def _merge(x, yn, o, proj, wso, wao, wo, bg1, bg2, sub_g, n2_g, wr_t, br_c, *, tm, sub_scale):
    rows = x.shape[0]
    full = lambda shape: pl.BlockSpec(shape, lambda i: (0,) * len(shape))
    tok = lambda w: pl.BlockSpec((tm, w), lambda i: (i, 0))
    per_block = lambda a, b: pl.BlockSpec((None, a, b), lambda i: (i, 0, 0))
    return pl.pallas_call(
        functools.partial(_merge_kernel, sub_scale=sub_scale),
        grid=(rows // tm,),
        in_specs=[
            tok(D_MODEL), tok(D_INNER), tok(ATT_WIDTH),
            pl.BlockSpec((tm, D_MODEL), lambda i: (i, COL_G // D_MODEL)),
            pl.BlockSpec((tm, D_MODEL), lambda i: (i, COL_G // D_MODEL + 1)),
            full((D_INNER, D_MODEL)), full((ATT_WIDTH, D_MODEL)), full((D_MODEL, D_MODEL)),
            full((1, D_MODEL)), full((1, D_MODEL)), full((1, ATT_V_DIM)), full((1, D_MODEL)),
            full((LANES, D_MODEL)), full((LANES, 1)),
        ],
        out_specs=[tok(D_MODEL), tok(D_MODEL), per_block(SUBLANES, tm), per_block(SUBLANES, tm),
                   per_block(SUBLANES, tm), per_block(N_EXPERTS, 1)],
        out_shape=[
            jax.ShapeDtypeStruct((rows, D_MODEL), F32),
            jax.ShapeDtypeStruct((rows, D_MODEL), BF16),
            jax.ShapeDtypeStruct((rows // tm, SUBLANES, tm), I32),
            jax.ShapeDtypeStruct((rows // tm, SUBLANES, tm), F32),
            jax.ShapeDtypeStruct((rows // tm, SUBLANES, tm), I32),
            jax.ShapeDtypeStruct((rows // tm, N_EXPERTS, 1), I32),
        ],
        compiler_params=_cparams(("parallel",)),
        name="merge_router",
    )(x, yn, o, proj, proj, wso, wao, wo, bg1, bg2, sub_g, n2_g, wr_t, br_c)


def _seg_copy(local_ref, lo, hbm_ref, hi, sem, to_hbm):
    loc = local_ref.at[pl.ds(pl.multiple_of(lo, SEG_PAD), SEG_PAD), :]
    hbm = hbm_ref.at[pl.ds(pl.multiple_of(hi, SEG_PAD), SEG_PAD), :]
    return pltpu.make_async_copy(loc, hbm, sem) if to_hbm else pltpu.make_async_copy(hbm, loc, sem)


def _seg_copies_start(tab_ref, local_ref, hbm_ref, sem, to_hbm):
    def per_expert(e, total):
        n, lo, hi = tab_ref[0, e], tab_ref[1, e], tab_ref[2, e]

        def per_copy(c, _):
            _seg_copy(local_ref, lo + c * SEG_PAD, hbm_ref, hi + c * SEG_PAD, sem, to_hbm).start()
            return 0

        lax.fori_loop(0, n, per_copy, 0)
        return total + n

    return lax.fori_loop(0, N_EXPERTS, per_expert, 0)


def _seg_copies_wait(n, local_ref, hbm_ref, sem, to_hbm):
    def one(c, _):
        _seg_copy(local_ref, 0, hbm_ref, 0, sem, to_hbm).wait()
        return 0

    lax.fori_loop(0, n, one, 0)


def _dispatch_kernel(tab_ref, lpos_ref, u_ref, *rest, fill_tails):
    if fill_tails:
        tail_ref, _, xs_ref, loc_scr, zero_scr, sem = rest
    else:
        xs_ref, loc_scr, sem = rest
    lr, tm = loc_scr.shape[0], u_ref.shape[0]
    lpos = lpos_ref[...].astype(jnp.int16)
    p = lax.broadcasted_iota(I32, (lr, tm), 0).astype(jnp.int16)
    perm = jnp.zeros((lr, tm), BF16)
    for k in range(TOP_K):
        perm = jnp.where(p == lpos[k:k + 1, :], jnp.ones((lr, tm), BF16), perm)
    loc_scr[...] = jnp.dot(perm, u_ref[...], preferred_element_type=F32).astype(loc_scr.dtype)
    n = _seg_copies_start(tab_ref, loc_scr, xs_ref, sem, to_hbm=True)
    if fill_tails:
        zero_scr[...] = jnp.zeros(zero_scr.shape, zero_scr.dtype)

        def per_expert(e, total):
            cnt, hi = tail_ref[0, e], tail_ref[1, e]

            def per_copy(c, _):
                _seg_copy(zero_scr, 0, xs_ref, hi + c * SEG_PAD, sem, True).start()
                return 0

            lax.fori_loop(0, cnt, per_copy, 0)
            return total + cnt

        n = n + lax.fori_loop(0, N_EXPERTS, per_expert, 0)
    _seg_copies_wait(n, loc_scr, xs_ref, sem, to_hbm=True)


def _local_rows(tm):
    return tm * TOP_K + N_EXPERTS * SEG_PAD


def _dispatch(tab, lpos_t, u2, *, tm, xs_rows=None, xs=None, tail=None):
    rows = u2.shape[0]
    in_specs = [
        pl.BlockSpec((None, 3, N_EXPERTS), lambda i: (i, 0, 0), memory_space=pltpu.SMEM),
        pl.BlockSpec((None, TOP_K, tm), lambda i: (i, 0, 0)),
        pl.BlockSpec((tm, D_MODEL), lambda i: (i, 0)),
    ]
    args = [tab, lpos_t, u2]
    scratch = [pltpu.VMEM((_local_rows(tm), D_MODEL), BF16)]
    aliases = {}
    if tail is not None:
        assert xs is not None and rows == tm
        in_specs += [pl.BlockSpec(memory_space=pltpu.SMEM), pl.BlockSpec(memory_space=pl.ANY)]
        args += [tail, xs]
        scratch.append(pltpu.VMEM((SEG_PAD, D_MODEL), BF16))
        aliases = {4: 0}
        xs_rows = xs.shape[0]
    return pl.pallas_call(
        functools.partial(_dispatch_kernel, fill_tails=tail is not None),
        grid=(rows // tm,),
        in_specs=in_specs,
        out_specs=pl.BlockSpec(memory_space=pl.ANY),
        out_shape=jax.ShapeDtypeStruct((xs_rows, D_MODEL), BF16),
        scratch_shapes=scratch + [pltpu.SemaphoreType.DMA(())],
        input_output_aliases=aliases,
        compiler_params=_cparams(("arbitrary",)),
        name="moe_dispatch",
    )(*args)


def _ffn_kernel(be_ref, na_ref, x_ref, wgu_ref, bgu_ref, wd_ref, bd_ref, y_ref, wgu_scr, wd_scr):
    i = pl.program_id(0)
    active = i < na_ref[0]

    @pl.when(active & ((i == 0) | (be_ref[i] != be_ref[jnp.maximum(i - 1, 0)])))
    def _():
        wgu_scr[...] = wgu_ref[...].astype(BF16)
        wd_scr[...] = wd_ref[...].astype(BF16)

    @pl.when(active)
    def _():
        gu = jnp.dot(x_ref[...], wgu_scr[...], preferred_element_type=F32) + bgu_ref[...]
        gate = jnp.minimum(gu[:, :D_FF], SWIGLU_LIMIT)
        up = jnp.clip(gu[:, D_FF:], -SWIGLU_LIMIT, SWIGLU_LIMIT)
        hdn = (up + 1.0) * gate * _sigmoid(SWIGLU_ALPHA * gate)
        y = jnp.dot(hdn.astype(BF16), wd_scr[...], preferred_element_type=F32) + bd_ref[...]
        y_ref[...] = y.astype(y_ref.dtype)


def _ffn(block_exp, n_active, xs, wgu, bgu, wd, bd):
    n_blocks = xs.shape[0] // MOE_BLK

    def blk(i, be, na):
        return jnp.minimum(i, na[0] - 1)

    return pl.pallas_call(
        _ffn_kernel,
        grid_spec=pltpu.PrefetchScalarGridSpec(
            num_scalar_prefetch=2,
            grid=(n_blocks,),
            in_specs=[
                pl.BlockSpec((MOE_BLK, D_MODEL), lambda i, be, na: (blk(i, be, na), 0)),
                pl.BlockSpec((None, D_MODEL, 2 * D_FF), lambda i, be, na: (be[blk(i, be, na)], 0, 0)),
                pl.BlockSpec((None, 1, 2 * D_FF), lambda i, be, na: (be[blk(i, be, na)], 0, 0)),
                pl.BlockSpec((None, D_FF, D_MODEL), lambda i, be, na: (be[blk(i, be, na)], 0, 0)),
                pl.BlockSpec((None, 1, D_MODEL), lambda i, be, na: (be[blk(i, be, na)], 0, 0)),
            ],
            out_specs=pl.BlockSpec((MOE_BLK, D_MODEL), lambda i, be, na: (blk(i, be, na), 0)),
            scratch_shapes=[pltpu.VMEM((D_MODEL, 2 * D_FF), BF16), pltpu.VMEM((D_FF, D_MODEL), BF16)],
        ),
        out_shape=jax.ShapeDtypeStruct(xs.shape, xs.dtype),
        compiler_params=_cparams(("arbitrary",)),
        name="moe_ffn",
    )(block_exp, n_active, xs, wgu, bgu, wd, bd)


def _combine_kernel(tab_ref, lpos_ref, gate_ref, h1_ref, fg_ref, ys_ref, y_ref, loc_scr, sem):
    lr, tm = loc_scr.shape[0], h1_ref.shape[0]
    loc_scr[...] = jnp.zeros(loc_scr.shape, loc_scr.dtype)
    n = _seg_copies_start(tab_ref, loc_scr, ys_ref, sem, to_hbm=False)
    lpos = lpos_ref[...].astype(jnp.int16)
    gate = gate_ref[...].astype(BF16)
    p = lax.broadcasted_iota(I32, (tm, lr), 1).astype(jnp.int16)
    pick = jnp.zeros((tm, lr), BF16)
    for k in range(TOP_K):
        pick = jnp.where(p == lpos[:, k:k + 1], jnp.broadcast_to(gate[:, k:k + 1], (tm, lr)), pick)
    _seg_copies_wait(n, loc_scr, ys_ref, sem, to_hbm=False)
    h = h1_ref[...] + jnp.dot(pick, loc_scr[...], preferred_element_type=F32)
    ms = jnp.mean(h * h, axis=-1, keepdims=True)
    y_ref[...] = h * lax.rsqrt(ms + EPS) * fg_ref[...]


def _combine(tab, lpos, gate, h1, fg, ys, *, tm):
    rows = h1.shape[0]
    return pl.pallas_call(
        _combine_kernel,
        grid=(rows // tm,),
        in_specs=[
            pl.BlockSpec((None, 3, N_EXPERTS), lambda i: (i, 0, 0), memory_space=pltpu.SMEM),
            pl.BlockSpec((tm, TOP_K), lambda i: (i, 0)),
            pl.BlockSpec((tm, TOP_K), lambda i: (i, 0)),
            pl.BlockSpec((tm, D_MODEL), lambda i: (i, 0)),
            pl.BlockSpec((1, D_MODEL), lambda i: (0, 0)),
            pl.BlockSpec(memory_space=pl.ANY),
        ],
        out_specs=pl.BlockSpec((tm, D_MODEL), lambda i: (i, 0)),
        out_shape=jax.ShapeDtypeStruct((rows, D_MODEL), F32),
        scratch_shapes=[pltpu.VMEM((_local_rows(tm), D_MODEL), ys.dtype), pltpu.SemaphoreType.DMA(())],
        compiler_params=_cparams(("arbitrary",)),
        name="moe_combine",
    )(tab, lpos, gate, h1, fg, ys)


def _rope_tables(pos):
    d = ATT_HEAD_DIM
    inv = ROPE_THETA ** (-jnp.arange(0, d, 2, dtype=F32) / d)
    ang = pos.astype(F32)[:, None] * inv[None, :]
    cos = jnp.cos(ang)
    sin = jnp.sin(ang)
    cos_h = jnp.concatenate([cos, cos], axis=1)
    sin_h = jnp.concatenate([-sin, sin], axis=1)
    return jnp.tile(cos_h, (1, LANES // d)), jnp.tile(sin_h, (1, LANES // d))


def _conv_by_group(a):
    lead = a.shape[:-1]
    x = a[..., :D_INNER].reshape(lead + (SSD_GROUPS, GROUP_W))
    b = a[..., D_INNER:D_INNER + SSD_GN].reshape(lead + (SSD_GROUPS, SSD_STATE))
    c = a[..., D_INNER + SSD_GN:].reshape(lead + (SSD_GROUPS, SSD_STATE))
    return jnp.concatenate([x, b, c], axis=-1)


def _conv_from_group(a):
    lead = a.shape[:-2]
    x = a[..., :GROUP_W].reshape(lead + (D_INNER,))
    b = a[..., GROUP_W:GROUP_W + SSD_STATE].reshape(lead + (SSD_GN,))
    c = a[..., GROUP_W + SSD_STATE:].reshape(lead + (SSD_GN,))
    return jnp.concatenate([x, b, c], axis=-1)


def _conv_prev_blocks(prev):
    g = jnp.moveaxis(_conv_by_group(prev.astype(F32)), 1, 2)
    return jnp.pad(g, ((0, 0), (0, 0), (SUBLANES - (CONV_W - 1), 0), (0, 0)))


def _conv_tail_rows(ct):
    return _conv_from_group(jnp.moveaxis(ct[:, :, SUBLANES - (CONV_W - 1):, :], 1, 2))


def kernel(x_prompt, x_sample, cache_k, cache_v, state_ssm, state_conv, meta_tokens, norm1_g, w_in, conv_w, conv_b, dt_bias, a_log, d_skip, ssd_norm_g, lambda_q1, lambda_k1, lambda_q2, lambda_k2, subln_g, w_ssd_out, w_att_out, b_gate, w_o, norm2_g, w_router, b_router, w_gu, b_gu, w_down, b_down, final_norm_g):
    batch, seq, _ = x_prompt.shape
    dbatch, dseq, _ = x_sample.shape
    past = cache_k.shape[2]
    depth = norm1_g.shape[0]
    assert depth == 1 and dseq == N_META
    assert seq % SSD_L == 0 and seq % ATT_TQ == 0 and seq % TOKEN_TM == 0 and ATT_TQ % CHUNK == 0
    assert TOKEN_TM % ATT_TQ == 0 and PROJ_TM % ATT_TQ == 0
    n_p = batch * seq
    n_dec = dbatch * dseq
    n_s = n_dec + N_META
    lam_init = 0.8 - 0.6 * math.exp(-0.3 * 0)
    l = 0

    wi = w_in[l]
    o_z, o_xbc, o_dt = 0, D_INNER, D_INNER + CONV_DIM
    o_q = o_dt + SSD_HEADS
    w_main = jnp.concatenate([wi[:, o_z:o_xbc], wi[:, o_xbc:o_dt], wi[:, o_q:]], axis=1).astype(BF16)
    w_dt_heads = wi[:, o_dt:o_q].reshape(D_MODEL, SSD_GROUPS, SSD_HPG)
    w_dt = jnp.zeros((D_MODEL, SSD_GROUPS, LANES), F32).at[:, :, :SSD_HPG].set(
        w_dt_heads).reshape(D_MODEL, SSD_GROUPS * LANES).astype(BF16)
    w_dt_c = jnp.zeros((D_MODEL, LANES), F32).at[:, :SSD_HEADS].set(
        w_dt_heads.transpose(0, 2, 1).reshape(D_MODEL, SSD_HEADS)).astype(BF16)
    g1 = norm1_g[l].reshape(1, D_MODEL)
    cw_g = jnp.moveaxis(_conv_by_group(conv_w[l]), 0, 1)
    cb_g = _conv_by_group(conv_b[l])[:, None, :]
    a_neg = -jnp.exp(a_log[l].astype(F32))
    norm_g = ssd_norm_g[l].reshape(1, D_INNER)
    lam = (jnp.exp(jnp.sum(lambda_q1[l].astype(F32) * lambda_k1[l].astype(F32)))
           - jnp.exp(jnp.sum(lambda_q2[l].astype(F32) * lambda_k2[l].astype(F32))) + lam_init).reshape(1)
    wso = w_ssd_out[l].astype(BF16)
    wao = w_att_out[l].astype(BF16)
    wo = w_o[l].astype(BF16)
    bg1 = b_gate[l][:D_MODEL].reshape(1, D_MODEL)
    bg2 = b_gate[l][D_MODEL:].reshape(1, D_MODEL)
    sub_g = subln_g[l].reshape(1, ATT_V_DIM)
    n2_g = norm2_g[l].reshape(1, D_MODEL)
    wr = jnp.zeros((LANES, D_MODEL), F32).at[:N_EXPERTS, :].set(w_router[l].T)
    br = jnp.zeros((LANES, 1), F32).at[:N_EXPERTS, 0].set(b_router[l])
    wgu = w_gu[l]
    bgu = b_gu[l][:, None, :]
    wd = w_down[l]
    bd = b_down[l][:, None, :]
    fg = final_norm_g.reshape(1, D_MODEL)

    xp = x_prompt.reshape(n_p, D_MODEL)
    xs_rows = jnp.concatenate([x_sample.reshape(n_dec, D_MODEL), meta_tokens.astype(x_prompt.dtype)], axis=0)
    cos_p, sin_p = _rope_tables(N_META + jnp.arange(seq, dtype=I32))
    pos_s = jnp.concatenate([jnp.tile(past + jnp.arange(dseq, dtype=I32), dbatch), jnp.arange(N_META, dtype=I32)])
    cos_s, sin_s = _rope_tables(pos_s)

    tm_p = PROJ_TM if seq % PROJ_TM == 0 else TOKEN_TM
    proj_s, dt_s = _in_proj(xs_rows, g1, w_main, w_dt, cos_s, sin_s, n_s, 1)

    def kv_rows(proj, col, lo, hi):
        return proj[lo:hi, col:col + ATT_WIDTH].astype(F32)

    meta_kv = (kv_rows(proj_s, COL_K, n_dec, n_s),
               kv_rows(proj_s, COL_V, n_dec, n_s).reshape(N_META * ATT_HEADS, ATT_V_DIM))
    proj_p, dt_p, vt_p, kc_p, vc_p = _in_proj(xp, g1, w_main, w_dt_c, cos_p, sin_p, tm_p, seq // tm_p,
                                              vt_block=ATT_TQ, meta_kv=meta_kv)

    ssd_args = (cw_g, cb_g, dt_bias[l], a_neg, d_skip[l], norm_g)
    zero_h = jnp.zeros((1, SSD_HEADS, SSD_HEAD_DIM, SSD_STATE), F32)
    zero_c = jnp.zeros((1, SSD_GROUPS, SUBLANES, CONV_GW), F32)
    yn_m, h_m, ct_m = _ssd(proj_s, dt_s, zero_h, zero_c, *ssd_args, n_seq=1, t=N_META, l=N_META,
                           row0=n_dec, shared_state=True)
    yn_p, h_p, ct_p = _ssd(proj_p, dt_p, h_m, ct_m, *ssd_args, n_seq=batch, t=seq, l=SSD_L, row0=0,
                           shared_state=True)
    yn_d, h_d, ct_d = _ssd(proj_s, dt_s, state_ssm[l].astype(F32), _conv_prev_blocks(state_conv[l]), *ssd_args,
                           n_seq=dbatch, t=dseq, l=dseq, row0=0, shared_state=False)
    yn_s = jnp.concatenate([yn_d, yn_m], axis=0)

    mvt =proj_s[n_dec:n_s, COL_V:COL_V + ATT_WIDTH].reshape(N_META, ATT_HEADS, ATT_V_DIM).transpose(1, 2, 0)
    o_p = _attn_prompt(lam, proj_p, vt_p, proj_s, mvt, batch=batch, seq=seq, meta_row0=n_dec)
    o_d = _attn_short(lam, proj_s, cache_k[l].reshape(dbatch, past, ATT_WIDTH).transpose(0, 2, 1),
                      cache_v[l].reshape(dbatch, past * ATT_HEADS, ATT_V_DIM), n_seq=dbatch, t=dseq, row0=0)
    o_m = _attn_short(lam, proj_s, None, None, n_seq=1, t=N_META, row0=n_dec)
    o_s = jnp.concatenate([o_d, o_m], axis=0)

    merge_w = (wso, wao, wo, bg1, bg2, sub_g, n2_g, wr, br)
    h1_p, u2_p, e_p, gt_p, rk_p, cnt_p = _merge(xp, yn_p, o_p, proj_p, *merge_w, tm=TOKEN_TM,
                                                sub_scale=1.0 - lam_init)
    h1_s, u2_s, e_s, gt_s, rk_s, cnt_s = _merge(xs_rows, yn_s, o_s, proj_s, *merge_w, tm=n_s,
                                                sub_scale=1.0 - lam_init)

    nb_p = n_p // TOKEN_TM
    cnt = jnp.concatenate([cnt_p[:, :, 0], cnt_s[:, :, 0]], axis=0)
    seg = (cnt + SEG_PAD - 1) // SEG_PAD * SEG_PAD
    local_start = jnp.cumsum(seg, axis=1) - seg
    per_expert = jnp.sum(seg, axis=0)
    padded = (per_expert + MOE_BLK - 1) // MOE_BLK * MOE_BLK
    pend = jnp.cumsum(padded)
    pstart = pend - padded
    hbm_start = pstart[None, :] + jnp.cumsum(seg, axis=0) - seg
    tab = jnp.stack([seg // SEG_PAD, local_start, hbm_start], axis=1).astype(I32)
    n_rows_max = (n_p + n_s) * TOP_K + (nb_p + 1) * N_EXPERTS * (SEG_PAD - 1)
    n_blocks = -(-n_rows_max // MOE_BLK) + N_EXPERTS
    block_start = jnp.arange(n_blocks, dtype=I32) * MOE_BLK
    block_exp = jnp.minimum(jnp.sum((pend[None, :] <= block_start[:, None]).astype(I32), axis=1), N_EXPERTS - 1)
    n_active = (pend[-1:] // MOE_BLK).astype(I32)

    def local_rows_of(e, rk, starts):
        hot = e[:, :TOP_K, :, None] == jnp.arange(N_EXPERTS, dtype=I32)
        return (jnp.sum(jnp.where(hot, starts[:, None, None, :], 0), axis=-1) + rk[:, :TOP_K]).astype(I32)

    def by_token(a):
        return a[:, :TOP_K].transpose(0, 2, 1).reshape(-1, TOP_K)

    lpos_pt = local_rows_of(e_p, rk_p, local_start[:nb_p])
    lpos_st = local_rows_of(e_s, rk_s, local_start[nb_p:])
    lpos_p, lpos_s = by_token(lpos_pt), by_token(lpos_st)
    gt_p, gt_s = by_token(gt_p), by_token(gt_s)
    tail = jnp.stack([(padded - per_expert) // SEG_PAD, pstart + per_expert]).astype(I32)
    xs = _dispatch(tab[:nb_p], lpos_pt, u2_p, tm=TOKEN_TM, xs_rows=n_blocks * MOE_BLK)
    xs = _dispatch(tab[nb_p:], lpos_st, u2_s, tm=n_s, xs=xs, tail=tail)
    ys = _ffn(block_exp, n_active, xs, wgu, bgu, wd, bd)
    y_p = _combine(tab[:nb_p], lpos_p, gt_p, h1_p, fg, ys, tm=TOKEN_TM)
    y_s = _combine(tab[nb_p:], lpos_s, gt_s, h1_s, fg, ys, tm=n_s)

    new_k_p = kc_p.reshape(1, batch, N_META + seq, 2 * ATT_HEADS, ATT_HEAD_DIM)
    new_v_p = vc_p.reshape(1, batch, N_META + seq, ATT_HEADS, ATT_V_DIM)
    new_k_s = kv_rows(proj_s, COL_K, 0, n_dec).reshape(1, dbatch, dseq, 2 * ATT_HEADS, ATT_HEAD_DIM)
    new_v_s = kv_rows(proj_s, COL_V, 0, n_dec).reshape(1, dbatch, dseq, ATT_HEADS, ATT_V_DIM)
    return (y_p.reshape(batch, seq, D_MODEL),
            y_s[:n_dec].reshape(dbatch, dseq, D_MODEL),
            new_k_p, new_v_p,
            h_p.astype(state_ssm.dtype)[None],
            _conv_tail_rows(ct_p).astype(x_prompt.dtype)[None],
            new_k_s, new_v_s,
            h_d.astype(state_ssm.dtype)[None],
            _conv_tail_rows(ct_d).astype(x_sample.dtype)[None])
```
